```python
import math
import jax
import jax.numpy as jnp
from jax import lax
import numpy as np

D_MODEL = 1024
BATCH = 16
SEQ = 256
DEPTH = 2
DEC_BATCH = 8
DEC_SEQ = 1024
PAST_LEN = 256

GRID_W = 64
ROPE_THETA = 10000.0
NORM_EPS = 1e-6
Q_BLOCK = 128

RET_HEADS = 4
RET_QK_DIM = 64
RET_V_DIM = 64
RET_CHUNK = 128
DIFF_HEADS = 4
DIFF_QK_DIM = 32
DIFF_V_DIM = 2 * DIFF_QK_DIM
MLA_HEADS = 8
MLA_Q_LORA = 256
MLA_KV_LORA = 128
MLA_NOPE_DIM = 64
MLA_ROPE_DIM = 32
MLA_V_DIM = 64

RET_QK_W = RET_HEADS * RET_QK_DIM
RET_V_W = RET_HEADS * RET_V_DIM
DIFF_QK_W = DIFF_HEADS * 2 * DIFF_QK_DIM
DIFF_V_W = DIFF_HEADS * DIFF_V_DIM
MLA_V_W = MLA_HEADS * MLA_V_DIM
MIX_W = RET_V_W + DIFF_V_W + MLA_V_W
IN_SPLITS = (RET_QK_W, RET_QK_W, RET_V_W, RET_V_W, DIFF_QK_W, DIFF_QK_W, DIFF_V_W, MLA_Q_LORA, MLA_KV_LORA, MLA_ROPE_DIM)
IN_W = sum(IN_SPLITS)

N_EXPERTS = 64
TOP_K = 6
N_GROUPS = 8
TOPK_GROUPS = 4
EXPERT_FF = 256
SHARED_FF = 256
ROUTED_SCALE = 2.5
EXPERT_BLOCK = 128

kernel_name = 'hybrid_ret_diff_mla_moe_dit_step'


def rmsnorm(x, gain=None):
    xf = x.astype(jnp.float32)
    y = xf * lax.rsqrt(jnp.mean(xf * xf, axis=-1, keepdims=True) + NORM_EPS)
    if gain is not None:
        y = y * gain.astype(jnp.float32)
    return y.astype(x.dtype)


def swiglu(x, w_gate, w_up, w_down):
    return (jax.nn.silu(x @ w_gate) * (x @ w_up)) @ w_down


def axial_rope_tables(n_tok, dim):
    n_rows = n_tok // GRID_W
    row = jnp.repeat(jnp.arange(n_rows, dtype=jnp.float32), GRID_W)
    col = jnp.tile(jnp.arange(GRID_W, dtype=jnp.float32), n_rows)
    half = dim // 2
    freqs = ROPE_THETA ** (-jnp.arange(0, half, 2, dtype=jnp.float32) / half)
    ar = row[:, None] * freqs[None, :]
    ac = col[:, None] * freqs[None, :]
    ang = jnp.concatenate([ar, ar, ac, ac], axis=-1)
    return jnp.cos(ang), jnp.sin(ang)


def apply_rope(x, cos, sin):
    a, b, c, d = jnp.split(x, 4, axis=-1)
    rot = jnp.concatenate([-b, a, -d, c], axis=-1)
    return x * cos.astype(x.dtype) + rot * sin.astype(x.dtype)


def sweep_queries(fn, qs):
    B, H, L, _ = qs[0].shape
    nb = L // Q_BLOCK
    blocks = tuple(jnp.moveaxis(q.reshape(B, H, nb, Q_BLOCK, q.shape[-1]), 2, 0) for q in qs)
    out = lax.map(lambda t: fn(*t), blocks)
    return jnp.moveaxis(out, 0, 2).reshape(B, H, L, out.shape[-1])


def retention_dir(q, k, v, log_gamma, s0, strict):
    B, H, L, dk = q.shape
    dv = v.shape[-1]
    C = RET_CHUNK
    nc = L // C
    dt = q.dtype
    idx = jnp.arange(C, dtype=jnp.float32)
    dist = idx[:, None] - idx[None, :]
    mask = (dist > 0) if strict else (dist >= 0)
    dmat = jnp.where(mask[None], jnp.exp(dist[None] * log_gamma[:, None, None]), 0.0).astype(dt)
    q_dec = jnp.exp((idx + 1.0)[None, :] * log_gamma[:, None]).astype(dt)
    k_dec = jnp.exp((C - 1.0 - idx)[None, :] * log_gamma[:, None]).astype(dt)
    c_dec = jnp.exp(C * log_gamma).astype(dt)
    qc = q.reshape(B, H, nc, C, dk)
    kc = k.reshape(B, H, nc, C, dk)
    vc = v.reshape(B, H, nc, C, dv)
    scores = jnp.einsum('bhnid,bhnjd->bhnij', qc, kc) * dmat[None, :, None]
    o_inner = jnp.einsum('bhnij,bhnje->bhnie', scores, vc)
    kv = jnp.einsum('bhnjd,bhnje->nbhde', kc * k_dec[None, :, None, :, None], vc)

    def step(s, kv_n):
        return s * c_dec[None, :, None, None] + kv_n, s

    s_fin, s_prev = lax.scan(step, s0.astype(dt), kv)
    o_cross = jnp.einsum('bhnid,nbhde->bhnie', qc * q_dec[None, :, None, :, None], s_prev)
    return (o_inner + o_cross).reshape(B, H, L, dv), s_fin


def retention(q, k, v, log_dec_f, log_dec_b, s0f, s0b):
    lgf = -jnp.exp(log_dec_f.astype(jnp.float32))
    lgb = -jnp.exp(log_dec_b.astype(jnp.float32))
    o_f, s_f = retention_dir(q, k, v, lgf, s0f, False)
    flip = lambda t: jnp.flip(t, axis=2)
    o_b, s_b = retention_dir(flip(q), flip(k), flip(v), lgb, s0b, True)
    return o_f + flip(o_b), s_f, s_b


def diff_attention(q1, q2, k1, k2, v, lam):
    scale = DIFF_QK_DIM ** -0.5

    def block(q1b, q2b):
        s1 = jnp.einsum('bhqd,bhkd->bhqk', q1b, k1).astype(jnp.float32) * scale
        s2 = jnp.einsum('bhqd,bhkd->bhqk', q2b, k2).astype(jnp.float32) * scale
        a = jax.nn.softmax(s1, axis=-1) - lam * jax.nn.softmax(s2, axis=-1)
        return jnp.einsum('bhqk,bhkd->bhqd', a.astype(v.dtype), v)

    return sweep_queries(block, (q1, q2))


def mla_attention(q_nope, q_rope, k_nope, k_rope, v):
    scale = (MLA_NOPE_DIM + MLA_ROPE_DIM) ** -0.5

    def block(qn, qr):
        s = jnp.einsum('bhqd,bhkd->bhqk', qn, k_nope) + jnp.einsum('bhqd,bkd->bhqk', qr, k_rope)
        p = jax.nn.softmax(s.astype(jnp.float32) * scale, axis=-1)
        return jnp.einsum('bhqk,bhkd->bhqd', p.astype(v.dtype), v)

    return sweep_queries(block, (q_nope, q_rope))


def token_mixers(h, lw, lam_init, ropes, ctx):
    B, L, _ = h.shape
    offsets = np.cumsum(IN_SPLITS)[:-1].tolist()
    rq, rk, rv, rg, dq, dk, dv, cq, ckv, kpe = jnp.split(h @ lw['w_in'], offsets, axis=-1)

    def heads(t, n):
        return t.reshape(B, L, n, -1).transpose(0, 2, 1, 3)

    rq = heads(rq, RET_HEADS)
    rk = heads(rk, RET_HEADS) * (RET_QK_DIM ** -0.5)
    rv = heads(rv, RET_HEADS)
    if ctx is None:
        s0f = jnp.zeros((B, RET_HEADS, RET_QK_DIM, RET_V_DIM), h.dtype)
        s0b = s0f
    else:
        s0f, s0b = ctx[4], ctx[5]
    o_r, s_f, s_b = retention(rq, rk, rv, lw['ret_decay_fwd'], lw['ret_decay_bwd'], s0f, s0b)
    ret_out = rmsnorm(o_r).transpose(0, 2, 1, 3).reshape(B, L, RET_V_W) * jax.nn.silu(rg)

    dq = dq.reshape(B, L, DIFF_HEADS, 2, DIFF_QK_DIM).transpose(0, 2, 1, 3, 4)
    dk = dk.reshape(B, L, DIFF_HEADS, 2, DIFF_QK_DIM).transpose(0, 2, 1, 3, 4)
    dv = heads(dv, DIFF_HEADS)
    q1, q2, k1, k2 = dq[..., 0, :], dq[..., 1, :], dk[..., 0, :], dk[..., 1, :]
    if ropes is not None:
        cos_d, sin_d = ropes[0]
        q1, q2, k1, k2 = (apply_rope(t, cos_d, sin_d) for t in (q1, q2, k1, k2))
    diff_k = jnp.concatenate([k1, k2], axis=-1)
    if ctx is None:
        diff_k_all, diff_v_all = diff_k, dv
    else:
        diff_k_all = jnp.concatenate([ctx[0], diff_k], axis=2)
        diff_v_all = jnp.concatenate([ctx[1], dv], axis=2)
    lq1, lk1, lq2, lk2 = lw['diff_lambda'].astype(jnp.float32)
    lam = jnp.exp(jnp.sum(lq1 * lk1)) - jnp.exp(jnp.sum(lq2 * lk2)) + lam_init
    o_d = diff_attention(q1, q2, diff_k_all[..., :DIFF_QK_DIM], diff_k_all[..., DIFF_QK_DIM:], diff_v_all, lam)
    diff_out = (rmsnorm(o_d, lw['diff_subln']) * (1.0 - lam_init)).transpose(0, 2, 1, 3).reshape(B, L, DIFF_V_W)

    qm = heads(rmsnorm(cq, lw['mla_q_norm']) @ lw['mla_w_uq'], MLA_HEADS)
    q_nope, q_rope = qm[..., :MLA_NOPE_DIM], qm[..., MLA_NOPE_DIM:]
    ckv = rmsnorm(ckv, lw['mla_kv_norm'])
    if ropes is not None:
        cos_m, sin_m = ropes[1]
        q_rope = apply_rope(q_rope, cos_m, sin_m)
        kpe = apply_rope(kpe, cos_m, sin_m)
    if ctx is None:
        ckv_all, kpe_all = ckv, kpe
    else:
        ckv_all = jnp.concatenate([ctx[2], ckv], axis=1)
        kpe_all = jnp.concatenate([ctx[3], kpe], axis=1)
    S = ckv_all.shape[1]
    kv = (ckv_all @ lw['mla_w_ukv']).reshape(B, S, MLA_HEADS, MLA_NOPE_DIM + MLA_V_DIM).transpose(0, 2, 1, 3)
    k_nope, v_m = kv[..., :MLA_NOPE_DIM], kv[..., MLA_NOPE_DIM:]
    o_m = mla_attention(q_nope, q_rope, k_nope, kpe_all, v_m)
    mla_out = o_m.transpose(0, 2, 1, 3).reshape(B, L, MLA_V_W)

    mix = jnp.concatenate([ret_out, diff_out, mla_out], axis=-1)
    new_ctx = (diff_k, dv, ckv, kpe, s_f, s_b) if ctx is None else None
    return mix, new_ctx


def moe_dispatch(x, eidx, gates, w_gate, w_up, w_down):
    T, D = x.shape
    TK = T * TOP_K
    flat_e = eidx.reshape(-1)
    flat_tok = jnp.repeat(jnp.arange(T, dtype=jnp.int32), TOP_K)
    flat_g = gates.reshape(-1)
    order = jnp.argsort(flat_e)
    se, stok, sg = flat_e[order], flat_tok[order], flat_g[order]
    counts = jnp.zeros((N_EXPERTS,), jnp.int32).at[flat_e].add(1)
    padded = (counts + EXPERT_BLOCK - 1) // EXPERT_BLOCK * EXPERT_BLOCK
    start = jnp.cumsum(counts) - counts
    pend = jnp.cumsum(padded)
    pstart = pend - padded
    dest = pstart[se] + jnp.arange(TK, dtype=jnp.int32) - start[se]
    P = TK + N_EXPERTS * EXPERT_BLOCK
    nblk = P // EXPERT_BLOCK
    buf_tok = jnp.full((P,), T, jnp.int32).at[dest].set(stok)
    buf_g = jnp.zeros((P,), x.dtype).at[dest].set(sg.astype(x.dtype))
    blk_e = jnp.minimum(jnp.searchsorted(pend, jnp.arange(nblk, dtype=jnp.int32) * EXPERT_BLOCK, side='right'), N_EXPERTS - 1)
    xb = jnp.concatenate([x, jnp.zeros((1, D), x.dtype)], axis=0)[buf_tok].reshape(nblk, EXPERT_BLOCK, D)

    def run(args):
        xe, e = args
        return swiglu(xe, w_gate[e], w_up[e], w_down[e])

    yb = lax.map(run, (xb, blk_e)).reshape(P, D)
    return jax.ops.segment_sum(yb * buf_g[:, None], buf_tok, num_segments=T + 1)[:T]


def moe_ffn(h, lw):
    B, L, D = h.shape
    T = B * L
    x = h.reshape(T, D)
    scores = jax.nn.sigmoid((x @ lw['router_w']).astype(jnp.float32))
    sel = scores + lw['router_bias'].astype(jnp.float32)
    grp_score = lax.top_k(sel.reshape(T, N_GROUPS, N_EXPERTS // N_GROUPS), 2)[0].sum(-1)
    _, gidx = lax.top_k(grp_score, TOPK_GROUPS)
    gmask = jnp.any(jnp.arange(N_GROUPS)[None, None, :] == gidx[:, :, None], axis=1)
    sel = jnp.where(jnp.repeat(gmask, N_EXPERTS // N_GROUPS, axis=1), sel, -jnp.inf)
    _, eidx = lax.top_k(sel, TOP_K)
    gates = jnp.take_along_axis(scores, eidx, axis=-1)
    gates = gates / jnp.sum(gates, axis=-1, keepdims=True) * ROUTED_SCALE
    routed = moe_dispatch(x, eidx, gates, lw['exp_w_gate'], lw['exp_w_up'], lw['exp_w_down'])
    shared = swiglu(x, lw['sh_w_gate'], lw['sh_w_up'], lw['sh_w_down'])
    return (routed + shared).reshape(B, L, D)


def trunk_layer(x, cond, lw, lam_init, ropes, ctx):
    mod = (jax.nn.silu(cond) @ lw['w_ada'] + lw['b_ada'])[:, None, :]
    shift1, scale1, gate1, shift2, scale2, gate2 = jnp.split(mod, 6, axis=-1)
    h = rmsnorm(x, lw['norm_mix']) * (1 + scale1) + shift1
    mix, new_ctx = token_mixers(h, lw, lam_init, ropes, ctx)
    x = x + gate1 * (mix @ lw['w_out'])
    h = rmsnorm(x, lw['norm_ffn']) * (1 + scale2) + shift2
    x = x + gate2 * moe_ffn(h, lw)
    return x, new_ctx


def setup_inputs(seed: int = 0) -> dict:
    key = jax.random.key(seed)
    keys = iter(jax.random.split(key, 40))

    def nrm(shape, s):
        return jax.random.normal(next(keys), shape, jnp.float32) * s

    D = D_MODEL
    ret_base = -(5.0 + jnp.arange(RET_HEADS, dtype=jnp.float32)) * math.log(2.0)
    return {
        'x_prompt': nrm((BATCH, SEQ, D), 1.0),
        'x_sample': nrm((DEC_BATCH, DEC_SEQ, D), 1.0),
        'cache_diff_k': nrm((DEC_BATCH, DEPTH, DIFF_HEADS, PAST_LEN, 2 * DIFF_QK_DIM), 1.0),
        'cache_diff_v': nrm((DEC_BATCH, DEPTH, DIFF_HEADS, PAST_LEN, DIFF_V_DIM), 1.0),
        'cache_mla_ckv': nrm((DEC_BATCH, DEPTH, PAST_LEN, MLA_KV_LORA), 1.0),
        'cache_mla_kpe': nrm((DEC_BATCH, DEPTH, PAST_LEN, MLA_ROPE_DIM), 1.0),
        'state_ret_fwd': nrm((DEC_BATCH, DEPTH, RET_HEADS, RET_QK_DIM, RET_V_DIM), 1.0),
        'state_ret_bwd': nrm((DEC_BATCH, DEPTH, RET_HEADS, RET_QK_DIM, RET_V_DIM), 1.0),
        'c': nrm((DEC_BATCH, D), 1.0),
        'c_ctx': nrm((D,), 1.0),
        'w_ada': nrm((DEPTH, D, 6 * D), 0.5 * D ** -0.5),
        'b_ada': nrm((DEPTH, 6 * D), 0.01),
        'norm_mix': 1.0 + nrm((DEPTH, D), 0.02),
        'norm_ffn': 1.0 + nrm((DEPTH, D), 0.02),
        'norm_final': 1.0 + nrm((D,), 0.02),
        'w_in': nrm((DEPTH, D, IN_W), D ** -0.5),
        'ret_decay_fwd': ret_base[None, :] + nrm((DEPTH, RET_HEADS), 0.1),
        'ret_decay_bwd': ret_base[None, :] + nrm((DEPTH, RET_HEADS), 0.1),
        'diff_lambda': nrm((DEPTH, 4, DIFF_QK_DIM), 0.1),
        'diff_subln': 1.0 + nrm((DEPTH, DIFF_V_DIM), 0.02),
        'mla_q_norm': 1.0 + nrm((DEPTH, MLA_Q_LORA), 0.02),
        'mla_w_uq': nrm((DEPTH, MLA_Q_LORA, MLA_HEADS * (MLA_NOPE_DIM + MLA_ROPE_DIM)), MLA_Q_LORA ** -0.5),
        'mla_kv_norm': 1.0 + nrm((DEPTH, MLA_KV_LORA), 0.02),
        'mla_w_ukv': nrm((DEPTH, MLA_KV_LORA, MLA_HEADS * (MLA_NOPE_DIM + MLA_V_DIM)), MLA_KV_LORA ** -0.5),
        'w_out': nrm((DEPTH, MIX_W, D), MIX_W ** -0.5),
        'router_w': nrm((DEPTH, D, N_EXPERTS), D ** -0.5),
        'router_bias': nrm((DEPTH, N_EXPERTS), 0.01),
        'exp_w_gate': nrm((DEPTH, N_EXPERTS, D, EXPERT_FF), D ** -0.5),
        'exp_w_up': nrm((DEPTH, N_EXPERTS, D, EXPERT_FF), D ** -0.5),
        'exp_w_down': nrm((DEPTH, N_EXPERTS, EXPERT_FF, D), EXPERT_FF ** -0.5),
        'sh_w_gate': nrm((DEPTH, D, SHARED_FF), D ** -0.5),
        'sh_w_up': nrm((DEPTH, D, SHARED_FF), D ** -0.5),
        'sh_w_down': nrm((DEPTH, SHARED_FF, D), SHARED_FF ** -0.5),
    }


def reference(x_prompt, x_sample, cache_diff_k, cache_diff_v, cache_mla_ckv, cache_mla_kpe, state_ret_fwd, state_ret_bwd, c, c_ctx, w_ada, b_ada, norm_mix, norm_ffn, norm_final, w_in, ret_decay_fwd, ret_decay_bwd, diff_lambda, diff_subln, mla_q_norm, mla_w_uq, mla_kv_norm, mla_w_ukv, w_out, router_w, router_bias, exp_w_gate, exp_w_up, exp_w_down, sh_w_gate, sh_w_up, sh_w_down):
    n_lat = x_sample.shape[1]
    rope_d = axial_rope_tables(n_lat, DIFF_QK_DIM)
    rope_m = axial_rope_tables(n_lat, MLA_ROPE_DIM)
    xp, xs = x_prompt, x_sample
    ctx_states = []
    for l in range(DEPTH):
        lw = {
            'w_ada': w_ada[l], 'b_ada': b_ada[l], 'norm_mix': norm_mix[l], 'norm_ffn': norm_ffn[l],
            'w_in': w_in[l], 'ret_decay_fwd': ret_decay_fwd[l], 'ret_decay_bwd': ret_decay_bwd[l],
            'diff_lambda': diff_lambda[l], 'diff_subln': diff_subln[l],
            'mla_q_norm': mla_q_norm[l], 'mla_w_uq': mla_w_uq[l], 'mla_kv_norm': mla_kv_norm[l], 'mla_w_ukv': mla_w_ukv[l],
            'w_out': w_out[l], 'router_w': router_w[l], 'router_bias': router_bias[l],
            'exp_w_gate': exp_w_gate[l], 'exp_w_up': exp_w_up[l], 'exp_w_down': exp_w_down[l],
            'sh_w_gate': sh_w_gate[l], 'sh_w_up': sh_w_up[l], 'sh_w_down': sh_w_down[l],
        }
        lam_init = 0.8 - 0.6 * math.exp(-0.3 * l)
        xp, st = trunk_layer(xp, c_ctx[None, :], lw, lam_init, None, None)
        ctx_states.append(st)
        cache_l = (cache_diff_k[:, l], cache_diff_v[:, l], cache_mla_ckv[:, l], cache_mla_kpe[:, l], state_ret_fwd[:, l], state_ret_bwd[:, l])
        xs, _ = trunk_layer(xs, c, lw, lam_init, (rope_d, rope_m), cache_l)
    y_prompt = rmsnorm(xp, norm_final)
    y_sample = rmsnorm(xs, norm_final)
    new_diff_k = jnp.stack([s[0] for s in ctx_states], axis=1)
    new_diff_v = jnp.stack([s[1] for s in ctx_states], axis=1)
    new_mla_ckv = jnp.stack([s[2] for s in ctx_states], axis=1)
    new_mla_kpe = jnp.stack([s[3] for s in ctx_states], axis=1)
    new_ret_fwd = jnp.stack([s[4] for s in ctx_states], axis=1)
    new_ret_bwd = jnp.stack([s[5] for s in ctx_states], axis=1)
    return (y_prompt, y_sample, new_diff_k, new_diff_v, new_mla_ckv, new_mla_kpe, new_ret_fwd, new_ret_bwd)
```

```python
import functools
import math

import numpy as np
import jax
import jax.numpy as jnp
from jax import lax
from jax.experimental import pallas as pl
from jax.experimental.pallas import tpu as pltpu

F32 = jnp.float32
BF16 = jnp.bfloat16
I32 = jnp.int32

D_MODEL = 1024
DEPTH = 2
GRID_W = 64
ROPE_THETA = 10000.0
NORM_EPS = 1e-6

RET_HEADS = 4
RET_DIM = 64
RET_CHUNK = 128
RET_W = RET_HEADS * RET_DIM
DIFF_HEADS = 4
DIFF_QK_DIM = 32
DIFF_V_DIM = 64
DIFF_V_W = DIFF_HEADS * DIFF_V_DIM
MLA_HEADS = 8
MLA_Q_LORA = 256
MLA_KV_LORA = 128
MLA_NOPE_DIM = 64
MLA_ROPE_DIM = 32
MLA_V_DIM = 64
MLA_V_W = MLA_HEADS * MLA_V_DIM
MIX_W = RET_W + DIFF_V_W + MLA_V_W

N_EXPERTS = 64
TOP_K = 6
N_GROUPS = 8
GROUP_SIZE = N_EXPERTS // N_GROUPS
TOPK_GROUPS = 4
EXPERT_FF = 256
ROUTED_SCALE = 2.5

LANES = 128
SLOT = LANES
TM = 256
SLOT_ROWS = 8
EB = 256
LOG_EB = 8
VMEM_LIMIT = 48 * 1024 * 1024

C_RQ, C_RK, C_RV, C_RG = 0, 256, 512, 768
C_DQ = 1024
C_DK = 1536
C_DV = 2048
C_CQ = 2304
C_CKV = 2560
C_KPE = 2688
N_PRE = 2816
QA_W = 4 * SLOT + MLA_HEADS * SLOT
KA_DK, KA_DV, KA_CKV, KA_KPE = 0, 512, 768, 896
KA_W = 1024
KR_LO, KR_HI = 64, 96


def _dot(a, b):
    return jnp.dot(a, b, preferred_element_type=F32)


def _dot_nt(a, b):
    return lax.dot_general(a, b, (((1,), (1,)), ((), ())), preferred_element_type=F32)


def _dot_tn(a, b):
    return lax.dot_general(a, b, (((0,), (0,)), ((), ())), preferred_element_type=F32)


def _split_dot(x, w_bf16):
    hi = x.astype(BF16)
    lo = (x - hi.astype(F32)).astype(BF16)
    return _dot(hi, w_bf16) + _dot(lo, w_bf16)


def _split_dot_nt(w, x):
    wh = w.astype(BF16)
    wl = (w - wh.astype(F32)).astype(BF16)
    xh = x.astype(BF16)
    xl = (x - xh.astype(F32)).astype(BF16)
    return _dot_nt(wh, xh) + _dot_nt(wh, xl) + _dot_nt(wl, xh)


def _silu(x):
    return x * jax.nn.sigmoid(x)


def _cparams(sem):
    return pltpu.CompilerParams(dimension_semantics=sem, vmem_limit_bytes=VMEM_LIMIT)


MOD_TN = 512


def _mod_kernel(c_ref, w_ref, b_ref, o_ref):
    s = _silu(c_ref[...])
    o_ref[...] = _split_dot3(s, w_ref[...]) + b_ref[...]


def _split_dot3(x, w):
    xh = x.astype(BF16)
    xl = (x - xh.astype(F32)).astype(BF16)
    wh = w.astype(BF16)
    wl = (w - wh.astype(F32)).astype(BF16)
    return _dot(xh, wh) + _dot(xh, wl) + _dot(xl, wh)


def _modulation(cond, w_ada, b_ada):
    n_rows = cond.shape[0]
    n_out = w_ada.shape[-1]
    return pl.pallas_call(
        _mod_kernel,
        grid=(DEPTH, n_out // MOD_TN),
        in_specs=[
            pl.BlockSpec((n_rows, D_MODEL), lambda l, j: (0, 0)),
            pl.BlockSpec((None, D_MODEL, MOD_TN), lambda l, j: (l, 0, j)),
            pl.BlockSpec((None, 1, MOD_TN), lambda l, j: (l, 0, j)),
        ],
        out_specs=pl.BlockSpec((None, n_rows, MOD_TN), lambda l, j: (l, 0, j)),
        out_shape=jax.ShapeDtypeStruct((DEPTH, n_rows, n_out), F32),
        compiler_params=_cparams(("arbitrary", "arbitrary")),
        name="adaln_mod",
    )(cond, w_ada, b_ada.reshape(DEPTH, 1, n_out))


def _rope_slot(x, cos, sa, sb):
    up = pltpu.roll(x, LANES - 8, 1)
    dn = pltpu.roll(x, 8, 1)
    return x * cos + up * sa + dn * sb


def _pre_kernel(tbl_ref, x_ref, mod_ref, g_ref, w_ref, qg_ref, wuq_ref, kvg_ref,
                cd_ref, sad_ref, sbd_ref, cm_ref, sam_ref, sbm_ref,
                ra_ref, qa_ref, ka_ref):
    del tbl_ref
    x = x_ref[...]
    mod = mod_ref[...]
    shift1 = mod[:, 0:D_MODEL]
    scale1 = mod[:, D_MODEL:2 * D_MODEL]
    ms = jnp.mean(x * x, axis=-1, keepdims=True)
    h = x * lax.rsqrt(ms + NORM_EPS) * g_ref[...]
    h = h * (1.0 + scale1) + shift1
    hb = h.astype(BF16)

    def proj(lo, hi):
        return _dot(hb, w_ref[:, lo:hi])

    ra_ref[:, 0:C_RG] = proj(C_RQ, C_RG)
    ra_ref[:, C_RG:C_DQ] = _silu(proj(C_RG, C_DQ))

    cd, sad, sbd = cd_ref[...], sad_ref[...], sbd_ref[...]
    cm, sam, sbm = cm_ref[...], sam_ref[...], sbm_ref[...]
    dq = proj(C_DQ, C_DK)
    dk = proj(C_DK, C_DV)
    for hd in range(DIFF_HEADS):
        sl = slice(hd * SLOT, (hd + 1) * SLOT)
        qa_ref[:, sl] = _rope_slot(dq[:, sl], cd, sad, sbd)
        ka_ref[:, KA_DK + hd * SLOT:KA_DK + (hd + 1) * SLOT] = _rope_slot(dk[:, sl], cd, sad, sbd)
    ka_ref[:, KA_DV:KA_CKV] = proj(C_DV, C_CQ)

    cq = proj(C_CQ, C_CKV)
    cqn = cq * lax.rsqrt(jnp.mean(cq * cq, axis=-1, keepdims=True) + NORM_EPS) * qg_ref[...]
    qm = _dot(cqn.astype(BF16), wuq_ref[...])
    for hd in range(MLA_HEADS):
        sl = slice(hd * SLOT, (hd + 1) * SLOT)
        qa_ref[:, 4 * SLOT + hd * SLOT:4 * SLOT + (hd + 1) * SLOT] = _rope_slot(qm[:, sl], cm, sam, sbm)

    ckv = proj(C_CKV, C_KPE)
    ka_ref[:, KA_CKV:KA_KPE] = ckv * lax.rsqrt(jnp.mean(ckv * ckv, axis=-1, keepdims=True) + NORM_EPS) * kvg_ref[...]
    ka_ref[:, KA_KPE:KA_W] = _rope_slot(proj(C_KPE, N_PRE), cm, sam, sbm)


def _pre_call(tbl, x, mod3, g, w_pre, qg, wuq, kvg, rope_d, rope_m):
    n_tok = x.shape[0]
    nt = n_tok // TM
    const = lambda i, t: (0, 0)
    tile = lambda i, t: (i, 0)
    rope = lambda i, t: (t[1, i], 0)
    gs = pltpu.PrefetchScalarGridSpec(
        num_scalar_prefetch=1,
        grid=(nt,),
        in_specs=[
            pl.BlockSpec((TM, D_MODEL), tile),
            pl.BlockSpec((None, 1, 6 * D_MODEL), lambda i, t: (t[0, i], 0, 0)),
            pl.BlockSpec((1, D_MODEL), const),
            pl.BlockSpec((D_MODEL, N_PRE), const),
            pl.BlockSpec((1, MLA_Q_LORA), const),
            pl.BlockSpec((MLA_Q_LORA, MLA_HEADS * SLOT), const),
            pl.BlockSpec((1, MLA_KV_LORA), const),
        ] + [pl.BlockSpec((TM, SLOT), rope)] * 6,
        out_specs=[
            pl.BlockSpec((TM, D_MODEL), tile),
            pl.BlockSpec((TM, QA_W), tile),
            pl.BlockSpec((TM, KA_W), tile),
        ],
    )
    return pl.pallas_call(
        _pre_kernel,
        grid_spec=gs,
        out_shape=[
            jax.ShapeDtypeStruct((n_tok, D_MODEL), F32),
            jax.ShapeDtypeStruct((n_tok, QA_W), F32),
            jax.ShapeDtypeStruct((n_tok, KA_W), F32),
        ],
        compiler_params=_cparams(("arbitrary",)),
        name="pre_proj",
    )(tbl, x, mod3, g, w_pre, qg, wuq, kvg, *rope_d, *rope_m)


def _lane_iota(shape):
    return lax.broadcasted_iota(I32, shape, len(shape) - 1)


def _head_mask(n_rows, width, head, head_w):
    lane = _lane_iota((n_rows, width))
    return (lane >= head * head_w) & (lane < (head + 1) * head_w)


def _seg_mean_sq(o, bd_ones):
    return _split_dot(o * o, bd_ones) * (1.0 / RET_DIM)


def _block_diag_ones(n, blk):
    r = lax.broadcasted_iota(I32, (n, n), 0) // blk
    c = lax.broadcasted_iota(I32, (n, n), 1) // blk
    return r == c


def _retention(ra_ref, seq_len, decf_ref, decb_ref, s0f, s0b):
    C = RET_CHUNK
    nc = seq_len // C
    lgf = -jnp.exp(decf_ref[...])
    lgb = -jnp.exp(decb_ref[...])
    pos = lax.broadcasted_iota(I32, (C, RET_W), 0).astype(F32)
    qdf = jnp.exp((pos + 1.0) * lgf)
    kdf = jnp.exp((C - 1.0 - pos) * lgf)
    cdf = jnp.exp(float(C) * lgf)
    qdb = jnp.exp((C - pos) * lgb)
    kdb = jnp.exp(pos * lgb)
    cdb = jnp.exp(float(C) * lgb)
    ii = lax.broadcasted_iota(I32, (C, C), 0).astype(F32)
    jj = lax.broadcasted_iota(I32, (C, C), 1).astype(F32)
    dist = ii - jj
    dmats = []
    for hd in range(RET_HEADS):
        lf = lgf[:, hd * RET_DIM:hd * RET_DIM + 1]
        lb = lgb[:, hd * RET_DIM:hd * RET_DIM + 1]
        dmats.append(jnp.where(dist >= 0, jnp.exp(dist * lf), jnp.exp(-dist * lb)))
    bd = _block_diag_ones(RET_W, RET_DIM)
    bd_ones = jnp.where(bd, 1.0, 0.0).astype(BF16)

    def chunk(n):
        rows = slice(n * C, (n + 1) * C)
        return (ra_ref[rows, C_RQ:C_RK], ra_ref[rows, C_RK:C_RV], ra_ref[rows, C_RV:C_RG])

    cross = [None] * nc
    sf = s0f
    for n in range(nc):
        q, k, v = chunk(n)
        cross[n] = _dot((q * qdf).astype(BF16), sf.astype(BF16))
        kv = _dot_tn((k * kdf).astype(BF16), v.astype(BF16))
        sf = sf * cdf + jnp.where(bd, kv, 0.0)
    sb = s0b
    for n in range(nc - 1, -1, -1):
        q, k, v = chunk(n)
        cross[n] = cross[n] + _dot((q * qdb).astype(BF16), sb.astype(BF16))
        kv = _dot_tn((k * kdb).astype(BF16), v.astype(BF16))
        sb = sb * cdb + jnp.where(bd, kv, 0.0)

    outs = []
    for n in range(nc):
        q, k, v = chunk(n)
        kb = k.astype(BF16)
        vb = v.astype(BF16)
        o = cross[n]
        for hd in range(RET_HEADS):
            hm = _head_mask(C, RET_W, hd, RET_DIM)
            sc = _dot_nt(jnp.where(hm, q, 0.0).astype(BF16), kb) * dmats[hd]
            o = o + jnp.where(hm, _dot(sc.astype(BF16), vb), 0.0)
        on = o * lax.rsqrt(_seg_mean_sq(o, bd_ones) + NORM_EPS)
        outs.append(on * ra_ref[n * C:(n + 1) * C, C_RG:C_DQ])
    return outs, sf, sb


def _softmax_pv(s_parts, v_parts, scale):
    m = None
    for s in s_parts:
        mm = jnp.max(s, axis=-1, keepdims=True)
        m = mm if m is None else jnp.maximum(m, mm)
    m = m * scale
    acc = None
    den = None
    for s, v in zip(s_parts, v_parts):
        e = jnp.exp(s * scale - m)
        ds = jnp.sum(e, axis=-1, keepdims=True)
        pv = _dot(e.astype(BF16), v)
        acc = pv if acc is None else acc + pv
        den = ds if den is None else den + ds
    return acc / den


def _diff_attention(dq, k_parts, v_parts, lam, subln, lam_init, bd_ones):
    lq = dq.shape[0]
    scale = DIFF_QK_DIM ** -0.5
    lane = _lane_iota((lq, SLOT))
    out = jnp.zeros((lq, DIFF_V_W), F32)
    for hd in range(DIFF_HEADS):
        qh = dq[:, hd * SLOT:(hd + 1) * SLOT]
        q1 = jnp.where(lane < DIFF_QK_DIM, qh, 0.0).astype(BF16)
        q2 = jnp.where(lane >= DIFF_QK_DIM, qh, 0.0).astype(BF16)
        s1 = [_dot_nt(q1[:, :kp[hd].shape[1]], kp[hd]) for kp in k_parts]
        s2 = [_dot_nt(q2[:, :kp[hd].shape[1]], kp[hd]) for kp in k_parts]
        o = _softmax_pv(s1, v_parts, scale) - lam * _softmax_pv(s2, v_parts, scale)
        out = jnp.where(_head_mask(lq, DIFF_V_W, hd, DIFF_V_DIM), o, out)
    on = out * lax.rsqrt(_seg_mean_sq(out, bd_ones) + NORM_EPS) * subln
    return on * (1.0 - lam_init)


def _mla_attention(qm, k_parts, v_parts):
    lq = qm.shape[0]
    scale = (MLA_NOPE_DIM + MLA_ROPE_DIM) ** -0.5
    halves = []
    for g in range(2):
        out = jnp.zeros((lq, 256), F32)
        for hh in range(4):
            hd = 4 * g + hh
            qh = qm[:, hd * SLOT:(hd + 1) * SLOT].astype(BF16)
            s = [_dot_nt(qh, kp[:, hd * SLOT:(hd + 1) * SLOT]) for kp in k_parts]
            o = _softmax_pv(s, [vp[:, 256 * g:256 * (g + 1)] for vp in v_parts], scale)
            out = jnp.where(_head_mask(lq, 256, hh, MLA_V_DIM), o, out)
        halves.append(out)
    return halves


def _mla_keys(ka_val_ckv, kr_slot, wk_ref, wv_ref):
    cb = ka_val_ckv.astype(BF16)
    kn = _dot(cb, wk_ref[...])
    ks = [(kn[:, hd * SLOT:(hd + 1) * SLOT] + kr_slot).astype(BF16) for hd in range(MLA_HEADS)]
    return jnp.concatenate(ks, axis=1), _dot(cb, wv_ref[...]).astype(BF16)


def _kr_only(kpe_slot):
    lane = _lane_iota(kpe_slot.shape)
    return jnp.where((lane >= KR_LO) & (lane < KR_HI), kpe_slot, 0.0)


def _diff_lambda(dl_ref, lam_init):
    dl = dl_ref[...]
    a = jnp.sum(dl[0:1] * dl[1:2], axis=-1, keepdims=True)
    b = jnp.sum(dl[2:3] * dl[3:4], axis=-1, keepdims=True)
    return jnp.exp(a) - jnp.exp(b) + lam_init


def _mix_prompt_kernel(lam_init, qa_ref, ka_ref, ra_ref, decf_ref, decb_ref, dl_ref, subln_ref,
                       wk_ref, wv_ref, mix_ref, sf_ref, sb_ref):
    seq = qa_ref.shape[0]
    zero_state = jnp.zeros((RET_W, RET_W), F32)
    outs, sf, sb = _retention(ra_ref, seq, decf_ref, decb_ref, zero_state, zero_state)
    for n, o in enumerate(outs):
        mix_ref[n * RET_CHUNK:(n + 1) * RET_CHUNK, 0:RET_W] = o
    sf_ref[...] = sf
    sb_ref[...] = sb

    bd_ones = jnp.where(_block_diag_ones(DIFF_V_W, DIFF_V_DIM), 1.0, 0.0).astype(BF16)
    lam = _diff_lambda(dl_ref, lam_init)
    kd = [ka_ref[:, KA_DK + hd * SLOT:KA_DK + (hd + 1) * SLOT].astype(BF16) for hd in range(DIFF_HEADS)]
    vd = ka_ref[:, KA_DV:KA_CKV].astype(BF16)
    mix_ref[:, RET_W:RET_W + DIFF_V_W] = _diff_attention(
        qa_ref[:, 0:4 * SLOT], [kd], [vd], lam, subln_ref[...], lam_init, bd_ones)

    km, vm = _mla_keys(ka_ref[:, KA_CKV:KA_KPE], _kr_only(ka_ref[:, KA_KPE:KA_W]), wk_ref, wv_ref)
    halves = _mla_attention(qa_ref[:, 4 * SLOT:QA_W], [km], [vm])
    mix_ref[:, 512:768] = halves[0]
    mix_ref[:, 768:1024] = halves[1]


def _mix_prompt_call(lam_init, qa, ka, ra, n_seq, seq_len, decf, decb, dl, subln, wk, wv):
    const = lambda b: (0, 0)
    seq = lambda b: (b, 0)
    return pl.pallas_call(
        functools.partial(_mix_prompt_kernel, lam_init),
        grid=(n_seq,),
        in_specs=[
            pl.BlockSpec((seq_len, QA_W), seq),
            pl.BlockSpec((seq_len, KA_W), seq),
            pl.BlockSpec((seq_len, D_MODEL), seq),
            pl.BlockSpec((1, RET_W), const),
            pl.BlockSpec((1, RET_W), const),
            pl.BlockSpec((4, DIFF_QK_DIM), const),
            pl.BlockSpec((1, DIFF_V_W), const),
            pl.BlockSpec((MLA_KV_LORA, MLA_HEADS * SLOT), const),
            pl.BlockSpec((MLA_KV_LORA, MLA_V_W), const),
        ],
        out_specs=[
            pl.BlockSpec((seq_len, MIX_W), seq),
            pl.BlockSpec((None, RET_W, RET_W), lambda b: (b, 0, 0)),
            pl.BlockSpec((None, RET_W, RET_W), lambda b: (b, 0, 0)),
        ],
        out_shape=[
            jax.ShapeDtypeStruct((n_seq * seq_len, MIX_W), F32),
            jax.ShapeDtypeStruct((n_seq, RET_W, RET_W), F32),
            jax.ShapeDtypeStruct((n_seq, RET_W, RET_W), F32),
        ],
        compiler_params=_cparams(("arbitrary",)),
        name="mix_prompt",
    )(qa, ka, ra, decf, decb, dl, subln, wk, wv)


def _mix_sample_kernel(lam_init, qa_ref, ka_ref, ra_ref, ckd_ref, cvd_ref, cckv_ref, ckpe_ref,
                       s0f_ref, s0b_ref, decf_ref, decb_ref, dl_ref, subln_ref, wk_ref, wv_ref,
                       place_ref, mix_ref,
                       ret_s, kdn_s, vdn_s, kdc_s, vdc_s, kmn_s, vmn_s, kmc_s, vmc_s):
    j = pl.program_id(1)
    seq = ka_ref.shape[0]

    @pl.when(j == 0)
    def _():
        outs, _, _ = _retention(ra_ref, seq, decf_ref, decb_ref, s0f_ref[...], s0b_ref[...])
        for n, o in enumerate(outs):
            ret_s[n * RET_CHUNK:(n + 1) * RET_CHUNK, :] = o
        kdn_s[...] = ka_ref[:, KA_DK:KA_DV].astype(BF16)
        vdn_s[...] = ka_ref[:, KA_DV:KA_CKV].astype(BF16)
        kdc_s[...] = ckd_ref[...].astype(BF16)
        vdc_s[...] = cvd_ref[...].astype(BF16)
        km, vm = _mla_keys(ka_ref[:, KA_CKV:KA_KPE], _kr_only(ka_ref[:, KA_KPE:KA_W]), wk_ref, wv_ref)
        kmn_s[...] = km
        vmn_s[...] = vm
        kr_ctx = _dot(ckpe_ref[...].astype(BF16), place_ref[...])
        km, vm = _mla_keys(cckv_ref[...], kr_ctx, wk_ref, wv_ref)
        kmc_s[...] = km
        vmc_s[...] = vm

    row0 = pl.multiple_of(j * TM, TM)
    mix_ref[:, 0:RET_W] = ret_s[pl.ds(row0, TM), :]

    bd_ones = jnp.where(_block_diag_ones(DIFF_V_W, DIFF_V_DIM), 1.0, 0.0).astype(BF16)
    lam = _diff_lambda(dl_ref, lam_init)
    kd_ctx = [kdc_s[hd] for hd in range(DIFF_HEADS)]
    kd_new = [kdn_s[:, hd * SLOT:(hd + 1) * SLOT] for hd in range(DIFF_HEADS)]
    mix_ref[:, RET_W:RET_W + DIFF_V_W] = _diff_attention(
        qa_ref[:, 0:4 * SLOT], [kd_ctx, kd_new], [vdc_s[...], vdn_s[...]], lam, subln_ref[...],
        lam_init, bd_ones)

    halves = _mla_attention(qa_ref[:, 4 * SLOT:QA_W], [kmc_s[...], kmn_s[...]], [vmc_s[...], vmn_s[...]])
    mix_ref[:, 512:768] = halves[0]
    mix_ref[:, 768:1024] = halves[1]


def _mix_sample_call(lam_init, layer, qa, ka, ra, tok0, n_seq, seq_len, past_len, cache_dk, cache_dv_t,
                     cache_ckv, cache_kpe, s0f_bd, s0b_bd, decf, decb, dl, subln, wk, wv, place):
    nq = seq_len // TM
    q0 = tok0 // TM
    s0 = tok0 // seq_len
    const = lambda b, j: (0, 0)
    return pl.pallas_call(
        functools.partial(_mix_sample_kernel, lam_init),
        grid=(n_seq, nq),
        in_specs=[
            pl.BlockSpec((TM, QA_W), lambda b, j: (q0 + b * nq + j, 0)),
            pl.BlockSpec((seq_len, KA_W), lambda b, j: (s0 + b, 0)),
            pl.BlockSpec((seq_len, D_MODEL), lambda b, j: (s0 + b, 0)),
            pl.BlockSpec((None, None, DIFF_HEADS, past_len, 2 * DIFF_QK_DIM), lambda b, j: (b, layer, 0, 0, 0)),
            pl.BlockSpec((None, None, past_len, DIFF_V_W), lambda b, j: (b, layer, 0, 0)),
            pl.BlockSpec((None, None, past_len, MLA_KV_LORA), lambda b, j: (b, layer, 0, 0)),
            pl.BlockSpec((None, None, past_len, MLA_ROPE_DIM), lambda b, j: (b, layer, 0, 0)),
            pl.BlockSpec((None, None, RET_W, RET_W), lambda b, j: (b, layer, 0, 0)),
            pl.BlockSpec((None, None, RET_W, RET_W), lambda b, j: (b, layer, 0, 0)),
            pl.BlockSpec((1, RET_W), const),
            pl.BlockSpec((1, RET_W), const),
            pl.BlockSpec((4, DIFF_QK_DIM), const),
            pl.BlockSpec((1, DIFF_V_W), const),
            pl.BlockSpec((MLA_KV_LORA, MLA_HEADS * SLOT), const),
            pl.BlockSpec((MLA_KV_LORA, MLA_V_W), const),
            pl.BlockSpec((MLA_ROPE_DIM, SLOT), const),
        ],
        out_specs=pl.BlockSpec((TM, MIX_W), lambda b, j: (b * nq + j, 0)),
        out_shape=jax.ShapeDtypeStruct((n_seq * seq_len, MIX_W), F32),
        scratch_shapes=[
            pltpu.VMEM((seq_len, RET_W), F32),
            pltpu.VMEM((seq_len, 4 * SLOT), BF16),
            pltpu.VMEM((seq_len, DIFF_V_W), BF16),
            pltpu.VMEM((DIFF_HEADS, past_len, 2 * DIFF_QK_DIM), BF16),
            pltpu.VMEM((past_len, DIFF_V_W), BF16),
            pltpu.VMEM((seq_len, MLA_HEADS * SLOT), BF16),
            pltpu.VMEM((seq_len, MLA_V_W), BF16),
            pltpu.VMEM((past_len, MLA_HEADS * SLOT), BF16),
            pltpu.VMEM((past_len, MLA_V_W), BF16),
        ],
        compiler_params=_cparams(("arbitrary", "arbitrary")),
        name="mix_sample",
    )(qa, ka, ra, cache_dk, cache_dv_t, cache_ckv, cache_kpe, s0f_bd, s0b_bd,
      decf, decb, dl, subln, wk, wv, place)


def _route(h2, rwt_ref, rb_ref, run_ref):
    tm = h2.shape[0]
    neg = -jnp.inf
    logits = _split_dot_nt(rwt_ref[...], h2)
    sc = jax.nn.sigmoid(logits)
    sel = sc + rb_ref[...]
    member = lax.broadcasted_iota(I32, (GROUP_SIZE, tm), 0).astype(F32)
    gscore = []
    for g in range(N_GROUPS):
        sg = sel[g * GROUP_SIZE:(g + 1) * GROUP_SIZE, :]
        m1 = jnp.max(sg, axis=0, keepdims=True)
        f1 = jnp.min(jnp.where(sg == m1, member, float(GROUP_SIZE)), axis=0, keepdims=True)
        m2 = jnp.max(jnp.where(member == f1, neg, sg), axis=0, keepdims=True)
        gscore.append(m1 + m2)
    gsel = [jnp.zeros((1, tm), F32) for _ in range(N_GROUPS)]
    for _ in range(TOPK_GROUPS):
        mx = gscore[0]
        for g in range(1, N_GROUPS):
            mx = jnp.maximum(mx, gscore[g])
        fi = jnp.full((1, tm), float(N_GROUPS), F32)
        for g in range(N_GROUPS - 1, -1, -1):
            fi = jnp.where(gscore[g] == mx, float(g), fi)
        for g in range(N_GROUPS):
            hit = fi == float(g)
            gsel[g] = jnp.where(hit, 1.0, gsel[g])
            gscore[g] = jnp.where(hit, neg, gscore[g])
    cand = jnp.concatenate(
        [jnp.where(gsel[g] > 0.0, sel[g * GROUP_SIZE:(g + 1) * GROUP_SIZE, :], neg) for g in range(N_GROUPS)],
        axis=0)
    flat = lax.broadcasted_iota(I32, (N_EXPERTS, tm), 0).astype(F32)
    hits, ids, gts = [], [], []
    chosen = jnp.zeros((N_EXPERTS, tm), F32)
    for _ in range(TOP_K):
        mx = jnp.max(cand, axis=0, keepdims=True)
        fk = jnp.min(jnp.where(cand == mx, flat, float(N_EXPERTS)), axis=0, keepdims=True)
        hit = flat == fk
        hits.append(hit)
        ids.append(fk)
        gts.append(jnp.sum(jnp.where(hit, sc, 0.0), axis=0, keepdims=True))
        chosen = jnp.where(hit, 1.0, chosen)
        cand = jnp.where(hit, neg, cand)
    gsum = gts[0]
    for g in gts[1:]:
        gsum = gsum + g
    gts = [g / gsum * ROUTED_SCALE for g in gts]

    before = (lax.broadcasted_iota(I32, (tm, tm), 0) < lax.broadcasted_iota(I32, (tm, tm), 1))
    rank_in = _dot(chosen.astype(BF16), jnp.where(before, 1.0, 0.0).astype(BF16))
    run = run_ref[...]
    rank = rank_in + run
    run_ref[...] = run + jnp.sum(chosen, axis=1, keepdims=True)
    rks = [jnp.sum(jnp.where(hit, rank, 0.0), axis=0, keepdims=True) for hit in hits]
    return ids, gts, rks


def _slot_rows(vals, n_rows):
    tm = vals[0].shape[1]
    row = lax.broadcasted_iota(I32, (n_rows, tm), 0)
    out = jnp.zeros((n_rows, tm), F32)
    for k, v in enumerate(vals):
        out = jnp.where(row == k, v, out)
    return out


def _post_kernel(tbl_ref, x_ref, mp_ref, ms_ref, mod_ref, wout_ref, g2_ref, shg_ref, shu_ref, shd_ref,
                 rwt_ref, rb_ref, base_ref, h2_ref, eidx_ref, gtok_ref, rank_ref, cnt_ref, run_ref):
    i = pl.program_id(0)
    n_prompt_tiles = tbl_ref[2, 0]

    @pl.when(i == 0)
    def _():
        run_ref[...] = jnp.zeros_like(run_ref)

    mod = mod_ref[...]
    gate1 = mod[:, 2 * D_MODEL:3 * D_MODEL]
    shift2 = mod[:, 3 * D_MODEL:4 * D_MODEL]
    scale2 = mod[:, 4 * D_MODEL:5 * D_MODEL]
    gate2 = mod[:, 5 * D_MODEL:6 * D_MODEL]
    mix = jnp.where(i < n_prompt_tiles, mp_ref[...], ms_ref[...])
    x1 = x_ref[...] + gate1 * _dot(mix.astype(BF16), wout_ref[...])
    ms = jnp.mean(x1 * x1, axis=-1, keepdims=True)
    h2 = x1 * lax.rsqrt(ms + NORM_EPS) * g2_ref[...]
    h2 = h2 * (1.0 + scale2) + shift2
    h2_ref[...] = h2
    hb = h2.astype(BF16)
    act = _silu(_dot(hb, shg_ref[...])) * _dot(hb, shu_ref[...])
    base_ref[...] = x1 + gate2 * _dot(act.astype(BF16), shd_ref[...])

    ids, gts, rks = _route(h2, rwt_ref, rb_ref, run_ref)
    eidx_ref[...] = _slot_rows(ids, SLOT_ROWS).astype(I32)
    rank_ref[...] = _slot_rows(rks, SLOT_ROWS).astype(I32)
    gtok_ref[...] = _slot_rows(gts, LANES).T
    cnt_ref[...] = jnp.broadcast_to(run_ref[...], cnt_ref.shape).astype(I32)


def _post_call(tbl, x, mix_p, mix_s, mod3, wout, g2, shg, shu, shd, rwt, rb):
    n_tok = x.shape[0]
    nt = n_tok // TM
    npt = mix_p.shape[0] // TM
    const = lambda i, t: (0, 0)
    tile = lambda i, t: (i, 0)
    gs = pltpu.PrefetchScalarGridSpec(
        num_scalar_prefetch=1,
        grid=(nt,),
        in_specs=[
            pl.BlockSpec((TM, D_MODEL), tile),
            pl.BlockSpec((TM, MIX_W), lambda i, t: (jnp.minimum(i, npt - 1), 0)),
            pl.BlockSpec((TM, MIX_W), lambda i, t: (jnp.maximum(i - npt, 0), 0)),
            pl.BlockSpec((None, 1, 6 * D_MODEL), lambda i, t: (t[0, i], 0, 0)),
            pl.BlockSpec((MIX_W, D_MODEL), const),
            pl.BlockSpec((1, D_MODEL), const),
            pl.BlockSpec((D_MODEL, EXPERT_FF), const),
            pl.BlockSpec((D_MODEL, EXPERT_FF), const),
            pl.BlockSpec((EXPERT_FF, D_MODEL), const),
            pl.BlockSpec((N_EXPERTS, D_MODEL), const),
            pl.BlockSpec((N_EXPERTS, 1), const),
        ],
        out_specs=[
            pl.BlockSpec((TM, D_MODEL), tile),
            pl.BlockSpec((TM, D_MODEL), tile),
            pl.BlockSpec((SLOT_ROWS, TM), lambda i, t: (0, i)),
            pl.BlockSpec((TM, LANES), tile),
            pl.BlockSpec((SLOT_ROWS, TM), lambda i, t: (0, i)),
            pl.BlockSpec((N_EXPERTS, LANES), const),
        ],
        scratch_shapes=[pltpu.VMEM((N_EXPERTS, 1), F32)],
    )
    return pl.pallas_call(
        _post_kernel,
        grid_spec=gs,
        out_shape=[
            jax.ShapeDtypeStruct((n_tok, D_MODEL), F32),
            jax.ShapeDtypeStruct((n_tok, D_MODEL), F32),
            jax.ShapeDtypeStruct((SLOT_ROWS, n_tok), I32),
            jax.ShapeDtypeStruct((n_tok, LANES), F32),
            jax.ShapeDtypeStruct((SLOT_ROWS, n_tok), I32),
            jax.ShapeDtypeStruct((N_EXPERTS, LANES), I32),
        ],
        compiler_params=_cparams(("arbitrary",)),
        name="post_route",
    )(tbl, x, mix_p, mix_s, mod3, wout, g2, shg, shu, shd, rwt, rb)


def _plan_kernel(cnt_ref, eidx_ref, rank_ref, dest_ref, blk_ref, nused_ref, seg_ref):
    n_blocks = blk_ref.shape[0]

    def per_expert(e, carry):
        run, last_e = carry
        c = cnt_ref[e, 0]
        nb = lax.shift_right_logical(c + (EB - 1), LOG_EB)
        b0 = lax.shift_right_logical(run, LOG_EB)
        seg_ref[0, e] = run
        seg_ref[1, e] = c
        seg_ref[2, e] = 0

        def fill(b, _):
            blk_ref[b0 + b] = e
            return 0

        lax.fori_loop(0, nb, fill, 0)
        return run + lax.shift_left(nb, LOG_EB), jnp.where(nb > 0, e, last_e)

    total, last_e = lax.fori_loop(0, N_EXPERTS, per_expert, (jnp.int32(0), jnp.int32(0)))
    n_used = lax.shift_right_logical(total, LOG_EB)
    nused_ref[0] = n_used
    seg_ref[2, 0] = n_used

    def fill_tail(b, _):
        blk_ref[b] = last_e
        return 0

    lax.fori_loop(n_used, n_blocks, fill_tail, 0)

    ei = eidx_ref[...]
    rk = rank_ref[...]

    def place(e, acc):
        return jnp.where(ei == e, seg_ref[0, e] + rk, acc)

    dest = lax.fori_loop(0, N_EXPERTS, place, jnp.zeros_like(rk))
    for t in range(dest_ref.shape[0]):
        dest_ref[t] = dest[:, t * TM:(t + 1) * TM]


def _plan_call(cnt, eidx, rank, n_blocks):
    n_tok = eidx.shape[1]
    nt = n_tok // TM
    smem = pl.BlockSpec(memory_space=pltpu.SMEM)
    vmem = pl.BlockSpec(memory_space=pltpu.VMEM)
    return pl.pallas_call(
        _plan_kernel,
        in_specs=[smem, vmem, vmem],
        out_specs=[vmem, smem, smem, smem],
        out_shape=[
            jax.ShapeDtypeStruct((nt, SLOT_ROWS, TM), I32),
            jax.ShapeDtypeStruct((n_blocks,), I32),
            jax.ShapeDtypeStruct((1,), I32),
            jax.ShapeDtypeStruct((3, N_EXPERTS), I32),
        ],
        compiler_params=pltpu.CompilerParams(vmem_limit_bytes=VMEM_LIMIT),
        name="moe_plan",
    )(cnt, eidx, rank)


def _row_copy(src_ref, src_row, dst_ref, dst_row, sem):
    return pltpu.make_async_copy(src_ref.at[pl.ds(src_row, 1)], dst_ref.at[pl.ds(dst_row, 1)], sem)


def _dispatch_kernel(dest_hbm, seg_ref, h_ref, xb_hbm, dsm, zrow, sem_t, sem_r):
    i = pl.program_id(0)
    tbl = pltpu.make_async_copy(dest_hbm.at[i], dsm, sem_t)
    tbl.start()
    tbl.wait()

    def issue(t, c):
        for k in range(TOP_K):
            _row_copy(h_ref, t, xb_hbm, dsm[k, t], sem_r).start()
        return c

    lax.fori_loop(0, TM, issue, 0)

    def drain(t, c):
        for k in range(TOP_K):
            _row_copy(h_ref, 0, xb_hbm, 0, sem_r).wait()
        return c

    lax.fori_loop(0, TM, drain, 0)

    @pl.when(i == pl.num_programs(0) - 1)
    def _():
        zrow[...] = jnp.zeros_like(zrow)

        def per_expert(e, c):
            first = seg_ref[0, e] + seg_ref[1, e]
            n_pad = jnp.bitwise_and(EB - jnp.bitwise_and(seg_ref[1, e], EB - 1), EB - 1)

            def z_issue(r, cc):
                _row_copy(zrow, 0, xb_hbm, first + r, sem_r).start()
                return cc

            lax.fori_loop(0, n_pad, z_issue, 0)

            def z_drain(r, cc):
                _row_copy(zrow, 0, xb_hbm, 0, sem_r).wait()
                return cc

            lax.fori_loop(0, n_pad, z_drain, 0)
            return c

        lax.fori_loop(0, N_EXPERTS, per_expert, 0)

        n_blocks = xb_hbm.shape[0] // EB

        def blk_copy(b):
            return pltpu.make_async_copy(zrow, xb_hbm.at[pl.ds(pl.multiple_of(b * EB, EB), EB)], sem_r)

        def t_issue(b, cc):
            blk_copy(b).start()
            return cc

        lax.fori_loop(seg_ref[2, 0], n_blocks, t_issue, 0)

        def t_drain(b, cc):
            blk_copy(0).wait()
            return cc

        lax.fori_loop(seg_ref[2, 0], n_blocks, t_drain, 0)


def _dispatch_call(dest3, seg, h2, n_rows):
    n_tok = h2.shape[0]
    nt = n_tok // TM
    return pl.pallas_call(
        _dispatch_kernel,
        grid=(nt,),
        in_specs=[
            pl.BlockSpec(memory_space=pl.ANY),
            pl.BlockSpec(memory_space=pltpu.SMEM),
            pl.BlockSpec((TM, D_MODEL), lambda i: (i, 0)),
        ],
        out_specs=pl.BlockSpec(memory_space=pl.ANY),
        out_shape=jax.ShapeDtypeStruct((n_rows, D_MODEL), F32),
        scratch_shapes=[
            pltpu.SMEM((SLOT_ROWS, TM), I32),
            pltpu.VMEM((EB, D_MODEL), F32),
            pltpu.SemaphoreType.DMA,
            pltpu.SemaphoreType.DMA,
        ],
        compiler_params=_cparams(("arbitrary",)),
        name="moe_dispatch",
    )(dest3, seg, h2)


def _experts_kernel(blk_ref, nused_ref, x_ref, wg_ref, wu_ref, wd_ref, y_ref, wg_s, wu_s, wd_s):
    i = pl.program_id(0)
    used = i < nused_ref[0]
    prev = blk_ref[jnp.maximum(i - 1, 0)]
    fresh = jnp.logical_or(i == 0, blk_ref[i] != prev)

    @pl.when(jnp.logical_and(used, fresh))
    def _():
        wg_s[...] = wg_ref[...].astype(BF16)
        wu_s[...] = wu_ref[...].astype(BF16)
        wd_s[...] = wd_ref[...].astype(BF16)

    @pl.when(used)
    def _():
        xb = x_ref[...].astype(BF16)
        act = _silu(_dot(xb, wg_s[...])) * _dot(xb, wu_s[...])
        y_ref[...] = _dot(act.astype(BF16), wd_s[...])

    @pl.when(jnp.logical_not(used))
    def _():
        y_ref[...] = jnp.zeros_like(y_ref)


def _experts_call(blk_e, n_used, xb, layer, wg, wu, wd):
    n_rows = xb.shape[0]
    nb = n_rows // EB
    gs = pltpu.PrefetchScalarGridSpec(
        num_scalar_prefetch=2,
        grid=(nb,),
        in_specs=[
            pl.BlockSpec((EB, D_MODEL), lambda i, be, nu: (jnp.minimum(i, nu[0] - 1), 0)),
            pl.BlockSpec((None, None, D_MODEL, EXPERT_FF), lambda i, be, nu: (layer, be[i], 0, 0)),
            pl.BlockSpec((None, None, D_MODEL, EXPERT_FF), lambda i, be, nu: (layer, be[i], 0, 0)),
            pl.BlockSpec((None, None, EXPERT_FF, D_MODEL), lambda i, be, nu: (layer, be[i], 0, 0)),
        ],
        out_specs=pl.BlockSpec((EB, D_MODEL), lambda i, be, nu: (i, 0)),
        scratch_shapes=[
            pltpu.VMEM((D_MODEL, EXPERT_FF), BF16),
            pltpu.VMEM((D_MODEL, EXPERT_FF), BF16),
            pltpu.VMEM((EXPERT_FF, D_MODEL), BF16),
        ],
    )
    return pl.pallas_call(
        _experts_kernel,
        grid_spec=gs,
        out_shape=jax.ShapeDtypeStruct((n_rows, D_MODEL), F32),
        compiler_params=_cparams(("arbitrary",)),
        name="moe_experts",
    )(blk_e, n_used, xb, wg, wu, wd)


def _combine_kernel(final, tbl_ref, dest_hbm, yb_hbm, base_ref, gtok_ref, mod_ref, gf_ref, out_ref,
                    dsm, rows, sem_t, sem_r):
    del tbl_ref
    i = pl.program_id(0)
    tbl = pltpu.make_async_copy(dest_hbm.at[i], dsm, sem_t)
    tbl.start()
    tbl.wait()

    def issue(t, c):
        for k in range(TOP_K):
            pltpu.make_async_copy(yb_hbm.at[pl.ds(dsm[k, t], 1)], rows.at[k, pl.ds(t, 1)], sem_r).start()
        return c

    lax.fori_loop(0, TM, issue, 0)

    def drain(t, c):
        for k in range(TOP_K):
            pltpu.make_async_copy(yb_hbm.at[pl.ds(0, 1)], rows.at[k, pl.ds(0, 1)], sem_r).wait()
        return c

    lax.fori_loop(0, TM, drain, 0)

    gt = gtok_ref[...]
    routed = gt[:, 0:1] * rows[0]
    for k in range(1, TOP_K):
        routed = routed + gt[:, k:k + 1] * rows[k]
    gate2 = mod_ref[...][:, 5 * D_MODEL:6 * D_MODEL]
    y = base_ref[...] + gate2 * routed
    if final:
        y = y * lax.rsqrt(jnp.mean(y * y, axis=-1, keepdims=True) + NORM_EPS) * gf_ref[...]
    out_ref[...] = y


def _combine_call(final, tbl, dest3, yb, base, gtok, mod3, gfinal):
    n_tok = base.shape[0]
    nt = n_tok // TM
    tile = lambda i, t: (i, 0)
    gs = pltpu.PrefetchScalarGridSpec(
        num_scalar_prefetch=1,
        grid=(nt,),
        in_specs=[
            pl.BlockSpec(memory_space=pl.ANY),
            pl.BlockSpec(memory_space=pl.ANY),
            pl.BlockSpec((TM, D_MODEL), tile),
            pl.BlockSpec((TM, LANES), tile),
            pl.BlockSpec((None, 1, 6 * D_MODEL), lambda i, t: (t[0, i], 0, 0)),
            pl.BlockSpec((1, D_MODEL), lambda i, t: (0, 0)),
        ],
        out_specs=pl.BlockSpec((TM, D_MODEL), tile),
        scratch_shapes=[
            pltpu.SMEM((SLOT_ROWS, TM), I32),
            pltpu.VMEM((TOP_K, TM, D_MODEL), F32),
            pltpu.SemaphoreType.DMA,
            pltpu.SemaphoreType.DMA,
        ],
    )
    return pl.pallas_call(
        functools.partial(_combine_kernel, final),
        grid_spec=gs,
        out_shape=jax.ShapeDtypeStruct((n_tok, D_MODEL), F32),
        compiler_params=_cparams(("arbitrary",)),
        name="moe_combine",
    )(tbl, dest3, yb, base, gtok, mod3, gfinal)


def _pad_cols(w, groups, width, slot):
    k = w.shape[0]
    w = w.reshape(k, groups, width)
    return jnp.pad(w, ((0, 0), (0, 0), (0, slot - width))).reshape(k, groups * slot)


def _prep_w_in(w):
    c = np.cumsum([0, 256, 256, 256, 256, 256, 256, 256, 256, 128, 32])
    rq, rk, rv, rg, dq, dk, dv, cq, ckv, kpe = [w[:, c[n]:c[n + 1]] for n in range(10)]
    kpe_slot = jnp.concatenate([kpe, jnp.zeros_like(kpe), kpe, jnp.zeros_like(kpe)], axis=1)
    cols = [rq, rk * (RET_DIM ** -0.5), rv, rg, _pad_cols(dq, DIFF_HEADS, 2 * DIFF_QK_DIM, SLOT),
            _pad_cols(dk, DIFF_HEADS, 2 * DIFF_QK_DIM, SLOT), dv, cq, ckv, kpe_slot]
    return jnp.concatenate(cols, axis=1).astype(BF16)


def _rope_tables(n_pos, dim, lane_offsets):
    n_rows = n_pos // GRID_W
    row = jnp.repeat(jnp.arange(n_rows, dtype=F32), GRID_W)
    col = jnp.tile(jnp.arange(GRID_W, dtype=F32), n_rows)
    half = dim // 2
    freqs = ROPE_THETA ** (-jnp.arange(0, half, 2, dtype=F32) / half)
    ar = row[:, None] * freqs[None, :]
    ac = col[:, None] * freqs[None, :]
    ang = jnp.concatenate([ar, ar, ac, ac], axis=-1)
    cos, sin = jnp.cos(ang), jnp.sin(ang)
    first = (np.arange(dim) % 16) < 8
    sa = jnp.where(first[None, :], -sin, 0.0)
    sb = jnp.where(first[None, :], 0.0, sin)
    c_t = jnp.ones((n_pos, SLOT), F32)
    a_t = jnp.zeros((n_pos, SLOT), F32)
    b_t = jnp.zeros((n_pos, SLOT), F32)
    for off in lane_offsets:
        c_t = c_t.at[:, off:off + dim].set(cos)
        a_t = a_t.at[:, off:off + dim].set(sa)
        b_t = b_t.at[:, off:off + dim].set(sb)
    ident = (jnp.ones((TM, SLOT), F32), jnp.zeros((TM, SLOT), F32), jnp.zeros((TM, SLOT), F32))
    return tuple(jnp.concatenate([i0, t], axis=0) for i0, t in zip(ident, (c_t, a_t, b_t)))


def _block_diag_states(s):
    b, l, h, dk, dv = s.shape
    eye = jnp.eye(h, dtype=s.dtype)
    return jnp.einsum('blhkv,hg->blhkgv', s, eye).reshape(b, l, h * dk, h * dv)


def kernel(x_prompt, x_sample, cache_diff_k, cache_diff_v, cache_mla_ckv, cache_mla_kpe, state_ret_fwd, state_ret_bwd, c, c_ctx, w_ada, b_ada, norm_mix, norm_ffn, norm_final, w_in, ret_decay_fwd, ret_decay_bwd, diff_lambda, diff_subln, mla_q_norm, mla_w_uq, mla_kv_norm, mla_w_ukv, w_out, router_w, router_bias, exp_w_gate, exp_w_up, exp_w_down, sh_w_gate, sh_w_up, sh_w_down):
    n_pb, p_len, _ = x_prompt.shape
    n_sb, s_len, _ = x_sample.shape
    past_len = cache_diff_k.shape[3]
    n_p = n_pb * p_len
    n_s = n_sb * s_len
    n_tok = n_p + n_s
    nt = n_tok // TM
    npt = n_p // TM
    assert p_len == TM and s_len % TM == 0 and n_p % s_len == 0 and past_len % 8 == 0

    tiles = np.arange(nt)
    mod_row = np.where(tiles < npt, n_sb, (tiles - npt) // (s_len // TM))
    rope_blk = np.where(tiles < npt, 0, 1 + (tiles - npt) % (s_len // TM))
    tbl = jnp.asarray(np.stack([mod_row, rope_blk, np.full(nt, npt)]).astype(np.int32))

    n_cond = 16
    cond = jnp.zeros((n_cond, D_MODEL), F32).at[:n_sb].set(c).at[n_sb].set(c_ctx)
    mod_all = _modulation(cond, w_ada, b_ada)

    rope_d = _rope_tables(s_len, DIFF_QK_DIM, (0, DIFF_QK_DIM))
    rope_m = _rope_tables(s_len, MLA_ROPE_DIM, (KR_LO,))
    place = jnp.zeros((MLA_ROPE_DIM, SLOT), F32).at[np.arange(MLA_ROPE_DIM), KR_LO + np.arange(MLA_ROPE_DIM)].set(1.0).astype(BF16)
    cache_dv_t = cache_diff_v.transpose(0, 1, 3, 2, 4).reshape(n_sb, DEPTH, past_len, DIFF_V_W)
    s0f_bd = _block_diag_states(state_ret_fwd)
    s0b_bd = _block_diag_states(state_ret_bwd)

    n_rows = n_tok * TOP_K + N_EXPERTS * EB
    n_blocks = n_rows // EB

    x = jnp.concatenate([x_prompt.reshape(n_p, D_MODEL), x_sample.reshape(n_s, D_MODEL)], axis=0)
    gfinal = norm_final.reshape(1, D_MODEL)
    caches = []
    for l in range(DEPTH):
        lam_init = 0.8 - 0.6 * math.exp(-0.3 * l)
        mod3 = mod_all[l].reshape(n_cond, 1, 6 * D_MODEL)
        w_pre = _prep_w_in(w_in[l])
        wuq = _pad_cols(mla_w_uq[l], MLA_HEADS, MLA_NOPE_DIM + MLA_ROPE_DIM, SLOT).astype(BF16)
        ukv = mla_w_ukv[l].reshape(MLA_KV_LORA, MLA_HEADS, MLA_NOPE_DIM + MLA_V_DIM)
        wk = _pad_cols(ukv[:, :, :MLA_NOPE_DIM].reshape(MLA_KV_LORA, -1), MLA_HEADS, MLA_NOPE_DIM, SLOT).astype(BF16)
        wv = ukv[:, :, MLA_NOPE_DIM:].reshape(MLA_KV_LORA, MLA_V_W).astype(BF16)
        decf = jnp.repeat(ret_decay_fwd[l], RET_DIM).reshape(1, RET_W)
        decb = jnp.repeat(ret_decay_bwd[l], RET_DIM).reshape(1, RET_W)
        subln = jnp.tile(diff_subln[l], DIFF_HEADS).reshape(1, DIFF_V_W)

        ra, qa, ka = _pre_call(tbl, x, mod3, norm_mix[l].reshape(1, D_MODEL), w_pre,
                               mla_q_norm[l].reshape(1, -1), wuq, mla_kv_norm[l].reshape(1, -1),
                               rope_d, rope_m)
        mix_p, sf, sb = _mix_prompt_call(lam_init, qa, ka, ra, n_pb, p_len, decf, decb,
                                         diff_lambda[l], subln, wk, wv)
        mix_s = _mix_sample_call(lam_init, l, qa, ka, ra, n_p, n_sb, s_len, past_len, cache_diff_k,
                                 cache_dv_t, cache_mla_ckv, cache_mla_kpe, s0f_bd, s0b_bd, decf, decb,
                                 diff_lambda[l], subln, wk, wv, place)
        base, h2, eidx, gtok, rank, cnt = _post_call(
            tbl, x, mix_p, mix_s, mod3, w_out[l].astype(BF16), norm_ffn[l].reshape(1, D_MODEL),
            sh_w_gate[l].astype(BF16), sh_w_up[l].astype(BF16), sh_w_down[l].astype(BF16),
            router_w[l].T, router_bias[l].reshape(N_EXPERTS, 1))
        dest3, blk_e, n_used, seg = _plan_call(cnt, eidx, rank, n_blocks)
        xb = _dispatch_call(dest3, seg, h2, n_rows)
        yb = _experts_call(blk_e, n_used, xb, l, exp_w_gate, exp_w_up, exp_w_down)
        x = _combine_call(l == DEPTH - 1, tbl, dest3, yb, base, gtok, mod3, gfinal)

        kp = ka[:n_p].reshape(n_pb, p_len, KA_W)
        dk = kp[:, :, KA_DK:KA_DV].reshape(n_pb, p_len, DIFF_HEADS, SLOT)[..., :2 * DIFF_QK_DIM]
        dv = kp[:, :, KA_DV:KA_CKV].reshape(n_pb, p_len, DIFF_HEADS, DIFF_V_DIM)
        diag = lambda s: jnp.stack([s[:, h * RET_DIM:(h + 1) * RET_DIM, h * RET_DIM:(h + 1) * RET_DIM]
                                    for h in range(RET_HEADS)], axis=1)
        caches.append((dk.transpose(0, 2, 1, 3), dv.transpose(0, 2, 1, 3), kp[:, :, KA_CKV:KA_KPE],
                       kp[:, :, KA_KPE:KA_KPE + MLA_ROPE_DIM], diag(sf), diag(sb)))

    y_prompt = x[:n_p].reshape(n_pb, p_len, D_MODEL)
    y_sample = x[n_p:].reshape(n_sb, s_len, D_MODEL)
    new = [jnp.stack([cs[n] for cs in caches], axis=1) for n in range(6)]
    return (y_prompt, y_sample, *new)
```

```python
import functools
import math

import numpy as np
import jax
import jax.numpy as jnp
from jax import lax
from jax.experimental import pallas as pl
from jax.experimental.pallas import tpu as pltpu

F32 = jnp.float32
BF16 = jnp.bfloat16
I32 = jnp.int32

D_MODEL = 1024
DEPTH = 2
GRID_W = 64
ROPE_THETA = 10000.0
NORM_EPS = 1e-6

RET_HEADS = 4
RET_DIM = 64
RET_CHUNK = 128
RET_W = RET_HEADS * RET_DIM
DIFF_HEADS = 4
DIFF_QK_DIM = 32
DIFF_V_DIM = 64
DIFF_V_W = DIFF_HEADS * DIFF_V_DIM
MLA_HEADS = 8
MLA_Q_LORA = 256
MLA_KV_LORA = 128
MLA_NOPE_DIM = 64
MLA_ROPE_DIM = 32
MLA_V_DIM = 64
MLA_V_W = MLA_HEADS * MLA_V_DIM
MIX_W = RET_W + DIFF_V_W + MLA_V_W

N_EXPERTS = 64
TOP_K = 6
N_GROUPS = 8
GROUP_SIZE = N_EXPERTS // N_GROUPS
TOPK_GROUPS = 4
EXPERT_FF = 256
ROUTED_SCALE = 2.5

LANES = 128
SLOT = LANES
TM = 256
SLOT_ROWS = 8
EB = 256
LOG_EB = 8
CH = 16
LOG_CH = 4
SORT_ROWS = 2560
VMEM_LIMIT = 48 * 1024 * 1024

C_RQ, C_RK, C_RV, C_RG = 0, 256, 512, 768
C_DQ = 1024
C_DK = 1536
C_DV = 2048
C_CQ = 2304
C_CKV = 2560
C_KPE = 2688
N_PRE = 2816
QA_W = 4 * SLOT + MLA_HEADS * SLOT
KA_DK, KA_DV, KA_CKV, KA_KPE = 0, 512, 768, 896
KA_W = 1024
KR_LO, KR_HI = 64, 96


def _dot(a, b):
    return jnp.dot(a, b, preferred_element_type=F32)


def _dot_nt(a, b):
    return lax.dot_general(a, b, (((1,), (1,)), ((), ())), preferred_element_type=F32)


def _dot_tn(a, b):
    return lax.dot_general(a, b, (((0,), (0,)), ((), ())), preferred_element_type=F32)


def _split_dot(x, w_bf16):
    hi = x.astype(BF16)
    lo = (x - hi.astype(F32)).astype(BF16)
    return _dot(hi, w_bf16) + _dot(lo, w_bf16)


def _split_dot_nt(w, x):
    wh = w.astype(BF16)
    wl = (w - wh.astype(F32)).astype(BF16)
    xh = x.astype(BF16)
    xl = (x - xh.astype(F32)).astype(BF16)
    return _dot_nt(wh, xh) + _dot_nt(wh, xl) + _dot_nt(wl, xh)


def _silu(x):
    return x * jax.nn.sigmoid(x)


def _cparams(sem):
    return pltpu.CompilerParams(dimension_semantics=sem, vmem_limit_bytes=VMEM_LIMIT)


MOD_TN = 512


def _mod_kernel(c_ref, w_ref, b_ref, o_ref):
    s = _silu(c_ref[...])
    o_ref[...] = _split_dot3(s, w_ref[...]) + b_ref[...]


def _split_dot3(x, w):
    xh = x.astype(BF16)
    xl = (x - xh.astype(F32)).astype(BF16)
    wh = w.astype(BF16)
    wl = (w - wh.astype(F32)).astype(BF16)
    return _dot(xh, wh) + _dot(xh, wl) + _dot(xl, wh)


def _modulation(cond, w_ada, b_ada):
    n_rows = cond.shape[0]
    n_out = w_ada.shape[-1]
    return pl.pallas_call(
        _mod_kernel,
        grid=(DEPTH, n_out // MOD_TN),
        in_specs=[
            pl.BlockSpec((n_rows, D_MODEL), lambda l, j: (0, 0)),
            pl.BlockSpec((None, D_MODEL, MOD_TN), lambda l, j: (l, 0, j)),
            pl.BlockSpec((None, 1, MOD_TN), lambda l, j: (l, 0, j)),
        ],
        out_specs=pl.BlockSpec((None, n_rows, MOD_TN), lambda l, j: (l, 0, j)),
        out_shape=jax.ShapeDtypeStruct((DEPTH, n_rows, n_out), F32),
        compiler_params=_cparams(("arbitrary", "arbitrary")),
        name="adaln_mod",
    )(cond, w_ada, b_ada.reshape(DEPTH, 1, n_out))


def _rope_slot(x, cos, sa, sb):
    up = pltpu.roll(x, LANES - 8, 1)
    dn = pltpu.roll(x, 8, 1)
    return x * cos + up * sa + dn * sb


def _pre_kernel(tbl_ref, x_ref, mod_ref, g_ref, w_ref, qg_ref, wuq_ref, kvg_ref,
                cd_ref, sad_ref, sbd_ref, cm_ref, sam_ref, sbm_ref,
                ra_ref, qa_ref, ka_ref):
    del tbl_ref
    x = x_ref[...]
    mod = mod_ref[...]
    shift1 = mod[:, 0:D_MODEL]
    scale1 = mod[:, D_MODEL:2 * D_MODEL]
    ms = jnp.mean(x * x, axis=-1, keepdims=True)
    h = x * lax.rsqrt(ms + NORM_EPS) * g_ref[...]
    h = h * (1.0 + scale1) + shift1
    hb = h.astype(BF16)

    def proj(lo, hi):
        return _dot(hb, w_ref[:, lo:hi])

    ra_ref[:, 0:C_RG] = proj(C_RQ, C_RG)
    ra_ref[:, C_RG:C_DQ] = _silu(proj(C_RG, C_DQ))

    cd, sad, sbd = cd_ref[...], sad_ref[...], sbd_ref[...]
    cm, sam, sbm = cm_ref[...], sam_ref[...], sbm_ref[...]
    dq = proj(C_DQ, C_DK)
    dk = proj(C_DK, C_DV)
    for hd in range(DIFF_HEADS):
        sl = slice(hd * SLOT, (hd + 1) * SLOT)
        qa_ref[:, sl] = _rope_slot(dq[:, sl], cd, sad, sbd)
        ka_ref[:, KA_DK + hd * SLOT:KA_DK + (hd + 1) * SLOT] = _rope_slot(dk[:, sl], cd, sad, sbd)
    ka_ref[:, KA_DV:KA_CKV] = proj(C_DV, C_CQ)

    cq = proj(C_CQ, C_CKV)
    cqn = cq * lax.rsqrt(jnp.mean(cq * cq, axis=-1, keepdims=True) + NORM_EPS) * qg_ref[...]
    qm = _dot(cqn.astype(BF16), wuq_ref[...])
    for hd in range(MLA_HEADS):
        sl = slice(hd * SLOT, (hd + 1) * SLOT)
        qa_ref[:, 4 * SLOT + hd * SLOT:4 * SLOT + (hd + 1) * SLOT] = _rope_slot(qm[:, sl], cm, sam, sbm)

    ckv = proj(C_CKV, C_KPE)
    ka_ref[:, KA_CKV:KA_KPE] = ckv * lax.rsqrt(jnp.mean(ckv * ckv, axis=-1, keepdims=True) + NORM_EPS) * kvg_ref[...]
    ka_ref[:, KA_KPE:KA_W] = _rope_slot(proj(C_KPE, N_PRE), cm, sam, sbm)


def _pre_call(tbl, x, mod3, g, w_pre, qg, wuq, kvg, rope_d, rope_m):
    n_tok = x.shape[0]
    nt = n_tok // TM
    const = lambda i, t: (0, 0)
    tile = lambda i, t: (i, 0)
    rope = lambda i, t: (t[1, i], 0)
    gs = pltpu.PrefetchScalarGridSpec(
        num_scalar_prefetch=1,
        grid=(nt,),
        in_specs=[
            pl.BlockSpec((TM, D_MODEL), tile),
            pl.BlockSpec((None, 1, 6 * D_MODEL), lambda i, t: (t[0, i], 0, 0)),
            pl.BlockSpec((1, D_MODEL), const),
            pl.BlockSpec((D_MODEL, N_PRE), const),
            pl.BlockSpec((1, MLA_Q_LORA), const),
            pl.BlockSpec((MLA_Q_LORA, MLA_HEADS * SLOT), const),
            pl.BlockSpec((1, MLA_KV_LORA), const),
        ] + [pl.BlockSpec((TM, SLOT), rope)] * 6,
        out_specs=[
            pl.BlockSpec((TM, D_MODEL), tile),
            pl.BlockSpec((TM, QA_W), tile),
            pl.BlockSpec((TM, KA_W), tile),
        ],
    )
    return pl.pallas_call(
        _pre_kernel,
        grid_spec=gs,
        out_shape=[
            jax.ShapeDtypeStruct((n_tok, D_MODEL), F32),
            jax.ShapeDtypeStruct((n_tok, QA_W), F32),
            jax.ShapeDtypeStruct((n_tok, KA_W), F32),
        ],
        compiler_params=_cparams(("arbitrary",)),
        name="pre_proj",
    )(tbl, x, mod3, g, w_pre, qg, wuq, kvg, *rope_d, *rope_m)


def _lane_iota(shape):
    return lax.broadcasted_iota(I32, shape, len(shape) - 1)


def _head_mask(n_rows, width, head, head_w):
    lane = _lane_iota((n_rows, width))
    return (lane >= head * head_w) & (lane < (head + 1) * head_w)


def _seg_mean_sq(o, bd_ones):
    return _split_dot(o * o, bd_ones) * (1.0 / RET_DIM)


def _block_diag_ones(n, blk):
    r = lax.broadcasted_iota(I32, (n, n), 0) // blk
    c = lax.broadcasted_iota(I32, (n, n), 1) // blk
    return r == c


def _retention(ra_ref, seq_len, decf_ref, decb_ref, s0f, s0b):
    C = RET_CHUNK
    nc = seq_len // C
    lgf = -jnp.exp(decf_ref[...])
    lgb = -jnp.exp(decb_ref[...])
    pos = lax.broadcasted_iota(I32, (C, RET_W), 0).astype(F32)
    qdf = jnp.exp((pos + 1.0) * lgf)
    kdf = jnp.exp((C - 1.0 - pos) * lgf)
    cdf = jnp.exp(float(C) * lgf)
    qdb = jnp.exp((C - pos) * lgb)
    kdb = jnp.exp(pos * lgb)
    cdb = jnp.exp(float(C) * lgb)
    ii = lax.broadcasted_iota(I32, (C, C), 0).astype(F32)
    jj = lax.broadcasted_iota(I32, (C, C), 1).astype(F32)
    dist = ii - jj
    dmats = []
    for hd in range(RET_HEADS):
        lf = lgf[:, hd * RET_DIM:hd * RET_DIM + 1]
        lb = lgb[:, hd * RET_DIM:hd * RET_DIM + 1]
        dmats.append(jnp.where(dist >= 0, jnp.exp(dist * lf), jnp.exp(-dist * lb)))
    bd = _block_diag_ones(RET_W, RET_DIM)
    bd_ones = jnp.where(bd, 1.0, 0.0).astype(BF16)

    def chunk(n):
        rows = slice(n * C, (n + 1) * C)
        return (ra_ref[rows, C_RQ:C_RK], ra_ref[rows, C_RK:C_RV], ra_ref[rows, C_RV:C_RG])

    cross = [None] * nc
    sf = s0f
    for n in range(nc):
        q, k, v = chunk(n)
        cross[n] = _dot((q * qdf).astype(BF16), sf.astype(BF16))
        kv = _dot_tn((k * kdf).astype(BF16), v.astype(BF16))
        sf = sf * cdf + jnp.where(bd, kv, 0.0)
    sb = s0b
    for n in range(nc - 1, -1, -1):
        q, k, v = chunk(n)
        cross[n] = cross[n] + _dot((q * qdb).astype(BF16), sb.astype(BF16))
        kv = _dot_tn((k * kdb).astype(BF16), v.astype(BF16))
        sb = sb * cdb + jnp.where(bd, kv, 0.0)

    outs = []
    for n in range(nc):
        q, k, v = chunk(n)
        kb = k.astype(BF16)
        vb = v.astype(BF16)
        o = cross[n]
        for hd in range(RET_HEADS):
            hm = _head_mask(C, RET_W, hd, RET_DIM)
            sc = _dot_nt(jnp.where(hm, q, 0.0).astype(BF16), kb) * dmats[hd]
            o = o + jnp.where(hm, _dot(sc.astype(BF16), vb), 0.0)
        on = o * lax.rsqrt(_seg_mean_sq(o, bd_ones) + NORM_EPS)
        outs.append(on * ra_ref[n * C:(n + 1) * C, C_RG:C_DQ])
    return outs, sf, sb


def _softmax_pv(s_parts, v_parts, scale):
    m = None
    for s in s_parts:
        mm = jnp.max(s, axis=-1, keepdims=True)
        m = mm if m is None else jnp.maximum(m, mm)
    m = m * scale
    acc = None
    den = None
    for s, v in zip(s_parts, v_parts):
        e = jnp.exp(s * scale - m)
        ds = jnp.sum(e, axis=-1, keepdims=True)
        pv = _dot(e.astype(BF16), v)
        acc = pv if acc is None else acc + pv
        den = ds if den is None else den + ds
    return acc / den


def _diff_attention(dq, k_parts, v_parts, lam, subln, lam_init, bd_ones):
    lq = dq.shape[0]
    scale = DIFF_QK_DIM ** -0.5
    lane = _lane_iota((lq, SLOT))
    out = jnp.zeros((lq, DIFF_V_W), F32)
    for hd in range(DIFF_HEADS):
        qh = dq[:, hd * SLOT:(hd + 1) * SLOT]
        q1 = jnp.where(lane < DIFF_QK_DIM, qh, 0.0).astype(BF16)
        q2 = jnp.where(lane >= DIFF_QK_DIM, qh, 0.0).astype(BF16)
        s1 = [_dot_nt(q1[:, :kp[hd].shape[1]], kp[hd]) for kp in k_parts]
        s2 = [_dot_nt(q2[:, :kp[hd].shape[1]], kp[hd]) for kp in k_parts]
        o = _softmax_pv(s1, v_parts, scale) - lam * _softmax_pv(s2, v_parts, scale)
        out = jnp.where(_head_mask(lq, DIFF_V_W, hd, DIFF_V_DIM), o, out)
    on = out * lax.rsqrt(_seg_mean_sq(out, bd_ones) + NORM_EPS) * subln
    return on * (1.0 - lam_init)


def _mla_attention(qm, k_parts, v_parts):
    lq = qm.shape[0]
    scale = (MLA_NOPE_DIM + MLA_ROPE_DIM) ** -0.5
    halves = []
    for g in range(2):
        out = jnp.zeros((lq, 256), F32)
        for hh in range(4):
            hd = 4 * g + hh
            qh = qm[:, hd * SLOT:(hd + 1) * SLOT].astype(BF16)
            s = [_dot_nt(qh, kp[:, hd * SLOT:(hd + 1) * SLOT]) for kp in k_parts]
            o = _softmax_pv(s, [vp[:, 256 * g:256 * (g + 1)] for vp in v_parts], scale)
            out = jnp.where(_head_mask(lq, 256, hh, MLA_V_DIM), o, out)
        halves.append(out)
    return halves


def _mla_keys(ka_val_ckv, kr_slot, wk_ref, wv_ref):
    cb = ka_val_ckv.astype(BF16)
    kn = _dot(cb, wk_ref[...])
    ks = [(kn[:, hd * SLOT:(hd + 1) * SLOT] + kr_slot).astype(BF16) for hd in range(MLA_HEADS)]
    return jnp.concatenate(ks, axis=1), _dot(cb, wv_ref[...]).astype(BF16)


def _kr_only(kpe_slot):
    lane = _lane_iota(kpe_slot.shape)
    return jnp.where((lane >= KR_LO) & (lane < KR_HI), kpe_slot, 0.0)


def _diff_lambda(dl_ref, lam_init):
    dl = dl_ref[...]
    a = jnp.sum(dl[0:1] * dl[1:2], axis=-1, keepdims=True)
    b = jnp.sum(dl[2:3] * dl[3:4], axis=-1, keepdims=True)
    return jnp.exp(a) - jnp.exp(b) + lam_init


def _mix_prompt_kernel(lam_init, qa_ref, ka_ref, ra_ref, decf_ref, decb_ref, dl_ref, subln_ref,
                       wk_ref, wv_ref, mix_ref, sf_ref, sb_ref):
    seq = qa_ref.shape[0]
    zero_state = jnp.zeros((RET_W, RET_W), F32)
    outs, sf, sb = _retention(ra_ref, seq, decf_ref, decb_ref, zero_state, zero_state)
    for n, o in enumerate(outs):
        mix_ref[n * RET_CHUNK:(n + 1) * RET_CHUNK, 0:RET_W] = o
    sf_ref[...] = sf
    sb_ref[...] = sb

    bd_ones = jnp.where(_block_diag_ones(DIFF_V_W, DIFF_V_DIM), 1.0, 0.0).astype(BF16)
    lam = _diff_lambda(dl_ref, lam_init)
    kd = [ka_ref[:, KA_DK + hd * SLOT:KA_DK + (hd + 1) * SLOT].astype(BF16) for hd in range(DIFF_HEADS)]
    vd = ka_ref[:, KA_DV:KA_CKV].astype(BF16)
    mix_ref[:, RET_W:RET_W + DIFF_V_W] = _diff_attention(
        qa_ref[:, 0:4 * SLOT], [kd], [vd], lam, subln_ref[...], lam_init, bd_ones)

    km, vm = _mla_keys(ka_ref[:, KA_CKV:KA_KPE], _kr_only(ka_ref[:, KA_KPE:KA_W]), wk_ref, wv_ref)
    halves = _mla_attention(qa_ref[:, 4 * SLOT:QA_W], [km], [vm])
    mix_ref[:, 512:768] = halves[0]
    mix_ref[:, 768:1024] = halves[1]


def _mix_prompt_call(lam_init, qa, ka, ra, n_seq, seq_len, decf, decb, dl, subln, wk, wv):
    const = lambda b: (0, 0)
    seq = lambda b: (b, 0)
    return pl.pallas_call(
        functools.partial(_mix_prompt_kernel, lam_init),
        grid=(n_seq,),
        in_specs=[
            pl.BlockSpec((seq_len, QA_W), seq),
            pl.BlockSpec((seq_len, KA_W), seq),
            pl.BlockSpec((seq_len, D_MODEL), seq),
            pl.BlockSpec((1, RET_W), const),
            pl.BlockSpec((1, RET_W), const),
            pl.BlockSpec((4, DIFF_QK_DIM), const),
            pl.BlockSpec((1, DIFF_V_W), const),
            pl.BlockSpec((MLA_KV_LORA, MLA_HEADS * SLOT), const),
            pl.BlockSpec((MLA_KV_LORA, MLA_V_W), const),
        ],
        out_specs=[
            pl.BlockSpec((seq_len, MIX_W), seq),
            pl.BlockSpec((None, RET_W, RET_W), lambda b: (b, 0, 0)),
            pl.BlockSpec((None, RET_W, RET_W), lambda b: (b, 0, 0)),
        ],
        out_shape=[
            jax.ShapeDtypeStruct((n_seq * seq_len, MIX_W), F32),
            jax.ShapeDtypeStruct((n_seq, RET_W, RET_W), F32),
            jax.ShapeDtypeStruct((n_seq, RET_W, RET_W), F32),
        ],
        compiler_params=_cparams(("arbitrary",)),
        name="mix_prompt",
    )(qa, ka, ra, decf, decb, dl, subln, wk, wv)


def _mix_sample_kernel(lam_init, qa_ref, ka_ref, ra_ref, ckd_ref, cvd_ref, cckv_ref, ckpe_ref,
                       s0f_ref, s0b_ref, decf_ref, decb_ref, dl_ref, subln_ref, wk_ref, wv_ref,
                       place_ref, mix_ref,
                       ret_s, kdn_s, vdn_s, kdc_s, vdc_s, kmn_s, vmn_s, kmc_s, vmc_s):
    j = pl.program_id(1)
    seq = ka_ref.shape[0]

    @pl.when(j == 0)
    def _():
        outs, _, _ = _retention(ra_ref, seq, decf_ref, decb_ref, s0f_ref[...], s0b_ref[...])
        for n, o in enumerate(outs):
            ret_s[n * RET_CHUNK:(n + 1) * RET_CHUNK, :] = o
        kdn_s[...] = ka_ref[:, KA_DK:KA_DV].astype(BF16)
        vdn_s[...] = ka_ref[:, KA_DV:KA_CKV].astype(BF16)
        kdc_s[...] = ckd_ref[...].astype(BF16)
        vdc_s[...] = cvd_ref[...].astype(BF16)
        km, vm = _mla_keys(ka_ref[:, KA_CKV:KA_KPE], _kr_only(ka_ref[:, KA_KPE:KA_W]), wk_ref, wv_ref)
        kmn_s[...] = km
        vmn_s[...] = vm
        kr_ctx = _dot(ckpe_ref[...].astype(BF16), place_ref[...])
        km, vm = _mla_keys(cckv_ref[...], kr_ctx, wk_ref, wv_ref)
        kmc_s[...] = km
        vmc_s[...] = vm

    row0 = pl.multiple_of(j * TM, TM)
    mix_ref[:, 0:RET_W] = ret_s[pl.ds(row0, TM), :]

    bd_ones = jnp.where(_block_diag_ones(DIFF_V_W, DIFF_V_DIM), 1.0, 0.0).astype(BF16)
    lam = _diff_lambda(dl_ref, lam_init)
    kd_ctx = [kdc_s[hd] for hd in range(DIFF_HEADS)]
    kd_new = [kdn_s[:, hd * SLOT:(hd + 1) * SLOT] for hd in range(DIFF_HEADS)]
    mix_ref[:, RET_W:RET_W + DIFF_V_W] = _diff_attention(
        qa_ref[:, 0:4 * SLOT], [kd_ctx, kd_new], [vdc_s[...], vdn_s[...]], lam, subln_ref[...],
        lam_init, bd_ones)

    halves = _mla_attention(qa_ref[:, 4 * SLOT:QA_W], [kmc_s[...], kmn_s[...]], [vmc_s[...], vmn_s[...]])
    mix_ref[:, 512:768] = halves[0]
    mix_ref[:, 768:1024] = halves[1]


def _mix_sample_call(lam_init, layer, qa, ka, ra, tok0, n_seq, seq_len, past_len, cache_dk, cache_dv_t,
                     cache_ckv, cache_kpe, s0f_bd, s0b_bd, decf, decb, dl, subln, wk, wv, place):
    nq = seq_len // TM
    q0 = tok0 // TM
    s0 = tok0 // seq_len
    const = lambda b, j: (0, 0)
    return pl.pallas_call(
        functools.partial(_mix_sample_kernel, lam_init),
        grid=(n_seq, nq),
        in_specs=[
            pl.BlockSpec((TM, QA_W), lambda b, j: (q0 + b * nq + j, 0)),
            pl.BlockSpec((seq_len, KA_W), lambda b, j: (s0 + b, 0)),
            pl.BlockSpec((seq_len, D_MODEL), lambda b, j: (s0 + b, 0)),
            pl.BlockSpec((None, None, DIFF_HEADS, past_len, 2 * DIFF_QK_DIM), lambda b, j: (b, layer, 0, 0, 0)),
            pl.BlockSpec((None, None, past_len, DIFF_V_W), lambda b, j: (b, layer, 0, 0)),
            pl.BlockSpec((None, None, past_len, MLA_KV_LORA), lambda b, j: (b, layer, 0, 0)),
            pl.BlockSpec((None, None, past_len, MLA_ROPE_DIM), lambda b, j: (b, layer, 0, 0)),
            pl.BlockSpec((None, None, RET_W, RET_W), lambda b, j: (b, layer, 0, 0)),
            pl.BlockSpec((None, None, RET_W, RET_W), lambda b, j: (b, layer, 0, 0)),
            pl.BlockSpec((1, RET_W), const),
            pl.BlockSpec((1, RET_W), const),
            pl.BlockSpec((4, DIFF_QK_DIM), const),
            pl.BlockSpec((1, DIFF_V_W), const),
            pl.BlockSpec((MLA_KV_LORA, MLA_HEADS * SLOT), const),
            pl.BlockSpec((MLA_KV_LORA, MLA_V_W), const),
            pl.BlockSpec((MLA_ROPE_DIM, SLOT), const),
        ],
        out_specs=pl.BlockSpec((TM, MIX_W), lambda b, j: (b * nq + j, 0)),
        out_shape=jax.ShapeDtypeStruct((n_seq * seq_len, MIX_W), F32),
        scratch_shapes=[
            pltpu.VMEM((seq_len, RET_W), F32),
            pltpu.VMEM((seq_len, 4 * SLOT), BF16),
            pltpu.VMEM((seq_len, DIFF_V_W), BF16),
            pltpu.VMEM((DIFF_HEADS, past_len, 2 * DIFF_QK_DIM), BF16),
            pltpu.VMEM((past_len, DIFF_V_W), BF16),
            pltpu.VMEM((seq_len, MLA_HEADS * SLOT), BF16),
            pltpu.VMEM((seq_len, MLA_V_W), BF16),
            pltpu.VMEM((past_len, MLA_HEADS * SLOT), BF16),
            pltpu.VMEM((past_len, MLA_V_W), BF16),
        ],
        compiler_params=_cparams(("arbitrary", "arbitrary")),
        name="mix_sample",
    )(qa, ka, ra, cache_dk, cache_dv_t, cache_ckv, cache_kpe, s0f_bd, s0b_bd,
      decf, decb, dl, subln, wk, wv, place)


def _route(h2, rwt_ref, rb_ref):
    tm = h2.shape[0]
    neg = -jnp.inf
    logits = _split_dot_nt(rwt_ref[...], h2)
    sc = jax.nn.sigmoid(logits)
    sel = sc + rb_ref[...]
    member = lax.broadcasted_iota(I32, (GROUP_SIZE, tm), 0).astype(F32)
    gscore = []
    for g in range(N_GROUPS):
        sg = sel[g * GROUP_SIZE:(g + 1) * GROUP_SIZE, :]
        m1 = jnp.max(sg, axis=0, keepdims=True)
        f1 = jnp.min(jnp.where(sg == m1, member, float(GROUP_SIZE)), axis=0, keepdims=True)
        m2 = jnp.max(jnp.where(member == f1, neg, sg), axis=0, keepdims=True)
        gscore.append(m1 + m2)
    gsel = [jnp.zeros((1, tm), F32) for _ in range(N_GROUPS)]
    for _ in range(TOPK_GROUPS):
        mx = gscore[0]
        for g in range(1, N_GROUPS):
            mx = jnp.maximum(mx, gscore[g])
        fi = jnp.full((1, tm), float(N_GROUPS), F32)
        for g in range(N_GROUPS - 1, -1, -1):
            fi = jnp.where(gscore[g] == mx, float(g), fi)
        for g in range(N_GROUPS):
            hit = fi == float(g)
            gsel[g] = jnp.where(hit, 1.0, gsel[g])
            gscore[g] = jnp.where(hit, neg, gscore[g])
    cand = jnp.concatenate(
        [jnp.where(gsel[g] > 0.0, sel[g * GROUP_SIZE:(g + 1) * GROUP_SIZE, :], neg) for g in range(N_GROUPS)],
        axis=0)
    flat = lax.broadcasted_iota(I32, (N_EXPERTS, tm), 0).astype(F32)
    hits, gts = [], []
    chosen = jnp.zeros((N_EXPERTS, tm), F32)
    for _ in range(TOP_K):
        mx = jnp.max(cand, axis=0, keepdims=True)
        fk = jnp.min(jnp.where(cand == mx, flat, float(N_EXPERTS)), axis=0, keepdims=True)
        hit = flat == fk
        hits.append(hit)
        gts.append(jnp.sum(jnp.where(hit, sc, 0.0), axis=0, keepdims=True))
        chosen = jnp.where(hit, 1.0, chosen)
        cand = jnp.where(hit, neg, cand)
    gsum = gts[0]
    for g in gts[1:]:
        gsum = gsum + g
    gts = [g / gsum * ROUTED_SCALE for g in gts]

    before = (lax.broadcasted_iota(I32, (tm, tm), 0) < lax.broadcasted_iota(I32, (tm, tm), 1))
    rank_in = _dot(chosen.astype(BF16), jnp.where(before, 1.0, 0.0).astype(BF16))
    cnt = jnp.sum(chosen, axis=1, keepdims=True)
    cnt_pad = jnp.floor((cnt + (CH - 1.0)) * (1.0 / CH)) * CH
    below = (lax.broadcasted_iota(I32, (N_EXPERTS, N_EXPERTS), 1) < lax.broadcasted_iota(I32, (N_EXPERTS, N_EXPERTS), 0))
    start = _dot(jnp.where(below, 1.0, 0.0).astype(BF16),
                 jnp.broadcast_to(cnt_pad, (N_EXPERTS, LANES)).astype(BF16))[:, 0:1]
    pos = rank_in + start
    lpos = [jnp.sum(jnp.where(hit, pos, 0.0), axis=0, keepdims=True) for hit in hits]
    return lpos, gts, cnt_pad


def _slot_rows(vals, n_rows):
    tm = vals[0].shape[1]
    row = lax.broadcasted_iota(I32, (n_rows, tm), 0)
    out = jnp.zeros((n_rows, tm), F32)
    for k, v in enumerate(vals):
        out = jnp.where(row == k, v, out)
    return out


def _post_kernel(tbl_ref, x_ref, mp_ref, ms_ref, mod_ref, wout_ref, g2_ref, shg_ref, shu_ref, shd_ref,
                 rwt_ref, rb_ref, base_ref, h2_ref, lpos_ref, ptok_ref, gtok_ref, cnt_ref):
    i = pl.program_id(0)
    n_prompt_tiles = tbl_ref[2, 0]

    @pl.when(i == 0)
    def _():
        cnt_ref[...] = jnp.zeros_like(cnt_ref)

    mod = mod_ref[...]
    gate1 = mod[:, 2 * D_MODEL:3 * D_MODEL]
    shift2 = mod[:, 3 * D_MODEL:4 * D_MODEL]
    scale2 = mod[:, 4 * D_MODEL:5 * D_MODEL]
    gate2 = mod[:, 5 * D_MODEL:6 * D_MODEL]
    mix = jnp.where(i < n_prompt_tiles, mp_ref[...], ms_ref[...])
    x1 = x_ref[...] + gate1 * _dot(mix.astype(BF16), wout_ref[...])
    ms = jnp.mean(x1 * x1, axis=-1, keepdims=True)
    h2 = x1 * lax.rsqrt(ms + NORM_EPS) * g2_ref[...]
    h2 = h2 * (1.0 + scale2) + shift2
    hb = h2.astype(BF16)
    h2_ref[...] = hb
    act = _silu(_dot(hb, shg_ref[...])) * _dot(hb, shu_ref[...])
    base_ref[...] = x1 + gate2 * _dot(act.astype(BF16), shd_ref[...])

    lpos, gts, cnt_pad = _route(h2, rwt_ref, rb_ref)
    lpos_ref[...] = _slot_rows(lpos, SLOT_ROWS)
    ptok_ref[...] = _slot_rows(lpos, LANES).T
    gtok_ref[...] = _slot_rows(gts, LANES).T
    col = lax.broadcasted_iota(I32, cnt_ref.shape, 1)
    cnt_ref[...] = jnp.where(col == i, cnt_pad.astype(I32), cnt_ref[...])


def _post_call(tbl, x, mix_p, mix_s, mod3, wout, g2, shg, shu, shd, rwt, rb):
    n_tok = x.shape[0]
    nt = n_tok // TM
    npt = mix_p.shape[0] // TM
    const = lambda i, t: (0, 0)
    tile = lambda i, t: (i, 0)
    gs = pltpu.PrefetchScalarGridSpec(
        num_scalar_prefetch=1,
        grid=(nt,),
        in_specs=[
            pl.BlockSpec((TM, D_MODEL), tile),
            pl.BlockSpec((TM, MIX_W), lambda i, t: (jnp.minimum(i, npt - 1), 0)),
            pl.BlockSpec((TM, MIX_W), lambda i, t: (jnp.maximum(i - npt, 0), 0)),
            pl.BlockSpec((None, 1, 6 * D_MODEL), lambda i, t: (t[0, i], 0, 0)),
            pl.BlockSpec((MIX_W, D_MODEL), const),
            pl.BlockSpec((1, D_MODEL), const),
            pl.BlockSpec((D_MODEL, EXPERT_FF), const),
            pl.BlockSpec((D_MODEL, EXPERT_FF), const),
            pl.BlockSpec((EXPERT_FF, D_MODEL), const),
            pl.BlockSpec((N_EXPERTS, D_MODEL), const),
            pl.BlockSpec((N_EXPERTS, 1), const),
        ],
        out_specs=[
            pl.BlockSpec((TM, D_MODEL), tile),
            pl.BlockSpec((TM, D_MODEL), tile),
            pl.BlockSpec((SLOT_ROWS, TM), lambda i, t: (0, i)),
            pl.BlockSpec((TM, LANES), tile),
            pl.BlockSpec((TM, LANES), tile),
            pl.BlockSpec((N_EXPERTS, LANES), const),
        ],
    )
    assert nt <= LANES
    return pl.pallas_call(
        _post_kernel,
        grid_spec=gs,
        out_shape=[
            jax.ShapeDtypeStruct((n_tok, D_MODEL), F32),
            jax.ShapeDtypeStruct((n_tok, D_MODEL), BF16),
            jax.ShapeDtypeStruct((SLOT_ROWS, n_tok), F32),
            jax.ShapeDtypeStruct((n_tok, LANES), F32),
            jax.ShapeDtypeStruct((n_tok, LANES), F32),
            jax.ShapeDtypeStruct((N_EXPERTS, LANES), I32),
        ],
        compiler_params=_cparams(("arbitrary",)),
        name="post_route",
    )(tbl, x, mix_p, mix_s, mod3, wout, g2, shg, shu, shd, rwt, rb)


def _plan_kernel(cnt_ref, src_ref, dst_ref, nch_ref, tot_ref, blk_ref, nused_ref, seg_ref):
    nt = src_ref.shape[0]
    n_blocks = blk_ref.shape[0]

    def per_tile(i, c):
        def per_e(e, lo):
            v = cnt_ref[e, i]
            src_ref[i, e] = lo
            nch_ref[i, e] = lax.shift_right_logical(v, LOG_CH)
            return lo + v

        tot_ref[i] = lax.shift_right_logical(lax.fori_loop(0, N_EXPERTS, per_e, jnp.int32(0)), LOG_CH)
        return c

    lax.fori_loop(0, nt, per_tile, 0)

    def per_expert(e, carry):
        start, last_e = carry

        def per_t(i, run):
            dst_ref[i, e] = run
            return run + cnt_ref[e, i]

        end = lax.fori_loop(0, nt, per_t, start)
        nb = lax.shift_right_logical(end - start + (EB - 1), LOG_EB)
        b0 = lax.shift_right_logical(start, LOG_EB)

        def fill(b, _):
            blk_ref[b0 + b] = e
            return 0

        lax.fori_loop(0, nb, fill, 0)
        nxt = start + lax.shift_left(nb, LOG_EB)
        seg_ref[0, e] = end
        seg_ref[1, e] = lax.shift_right_logical(nxt - end, LOG_CH)
        seg_ref[2, e] = 0
        return nxt, jnp.where(nb > 0, e, last_e)

    total, last_e = lax.fori_loop(0, N_EXPERTS, per_expert, (jnp.int32(0), jnp.int32(0)))
    n_used = lax.shift_right_logical(total, LOG_EB)
    nused_ref[0] = n_used
    seg_ref[2, 0] = n_used

    def fill_tail(b, _):
        blk_ref[b] = last_e
        return 0

    lax.fori_loop(n_used, n_blocks, fill_tail, 0)


def _plan_call(cnt, nt, n_blocks):
    smem = pl.BlockSpec(memory_space=pltpu.SMEM)
    return pl.pallas_call(
        _plan_kernel,
        in_specs=[smem],
        out_specs=[smem] * 7,
        out_shape=[
            jax.ShapeDtypeStruct((nt, N_EXPERTS), I32),
            jax.ShapeDtypeStruct((nt, N_EXPERTS), I32),
            jax.ShapeDtypeStruct((nt, N_EXPERTS), I32),
            jax.ShapeDtypeStruct((nt,), I32),
            jax.ShapeDtypeStruct((n_blocks,), I32),
            jax.ShapeDtypeStruct((1,), I32),
            jax.ShapeDtypeStruct((3, N_EXPERTS), I32),
        ],
        name="moe_plan",
    )(cnt)


def _chunk_copy(src_ref, src_row, dst_ref, dst_row, sem):
    return pltpu.make_async_copy(src_ref.at[pl.ds(pl.multiple_of(src_row, CH), CH)],
                                 dst_ref.at[pl.ds(pl.multiple_of(dst_row, CH), CH)], sem)


def _for_each_chunk(i, a_ref, b_ref, nch_ref, fn):
    def per_expert(e, c):
        a0 = a_ref[i, e]
        b0 = b_ref[i, e]

        def per_chunk(q, cc):
            fn(a0 + q * CH, b0 + q * CH)
            return cc

        lax.fori_loop(0, nch_ref[i, e], per_chunk, 0)
        return c

    lax.fori_loop(0, N_EXPERTS, per_expert, 0)


def _dispatch_kernel(src_ref, dst_ref, nch_ref, tot_ref, seg_ref, h_ref, lpos_ref, xb_hbm, sort_s, zero_s, sem_r):
    i = pl.program_id(0)
    lp = lpos_ref[...]
    hb = h_ref[...]
    blk = TM
    for r in range(SORT_ROWS // blk):
        srow = (lax.broadcasted_iota(I32, (blk, TM), 0) + r * blk).astype(F32)
        p = jnp.zeros((blk, TM), F32)
        for k in range(TOP_K):
            p = jnp.where(srow == lp[k:k + 1, :], 1.0, p)
        sort_s[r * blk:(r + 1) * blk, :] = _dot(p.astype(BF16), hb).astype(BF16)

    _for_each_chunk(i, src_ref, dst_ref, nch_ref,
                    lambda s, d: _chunk_copy(sort_s, s, xb_hbm, d, sem_r).start())

    def drain(q, c):
        _chunk_copy(sort_s, 0, xb_hbm, 0, sem_r).wait()
        return c

    lax.fori_loop(0, tot_ref[i], drain, 0)

    @pl.when(i == pl.num_programs(0) - 1)
    def _():
        zrow = zero_s
        zrow[...] = jnp.zeros_like(zrow)

        def per_expert(e, c):
            first = seg_ref[0, e]

            def z_issue(r, cc):
                _chunk_copy(zrow, 0, xb_hbm, first + r * CH, sem_r).start()
                return cc

            lax.fori_loop(0, seg_ref[1, e], z_issue, 0)

            def z_drain(r, cc):
                _chunk_copy(zrow, 0, xb_hbm, 0, sem_r).wait()
                return cc

            lax.fori_loop(0, seg_ref[1, e], z_drain, 0)
            return c

        lax.fori_loop(0, N_EXPERTS, per_expert, 0)

        n_blocks = xb_hbm.shape[0] // EB

        def blk_copy(b):
            return pltpu.make_async_copy(zrow, xb_hbm.at[pl.ds(pl.multiple_of(b * EB, EB), EB)], sem_r)

        def t_issue(b, cc):
            blk_copy(b).start()
            return cc

        lax.fori_loop(seg_ref[2, 0], n_blocks, t_issue, 0)

        def t_drain(b, cc):
            blk_copy(0).wait()
            return cc

        lax.fori_loop(seg_ref[2, 0], n_blocks, t_drain, 0)


def _dispatch_call(src, dst, nch, tot, seg, h2, lpos, n_rows):
    n_tok = h2.shape[0]
    nt = n_tok // TM
    smem = pl.BlockSpec(memory_space=pltpu.SMEM)
    return pl.pallas_call(
        _dispatch_kernel,
        grid=(nt,),
        in_specs=[
            smem, smem, smem, smem, smem,
            pl.BlockSpec((TM, D_MODEL), lambda i: (i, 0)),
            pl.BlockSpec((SLOT_ROWS, TM), lambda i: (0, i)),
        ],
        out_specs=pl.BlockSpec(memory_space=pl.ANY),
        out_shape=jax.ShapeDtypeStruct((n_rows, D_MODEL), BF16),
        scratch_shapes=[
            pltpu.VMEM((SORT_ROWS, D_MODEL), BF16),
            pltpu.VMEM((EB, D_MODEL), BF16),
            pltpu.SemaphoreType.DMA,
        ],
        compiler_params=_cparams(("arbitrary",)),
        name="moe_dispatch",
    )(src, dst, nch, tot, seg, h2, lpos)


def _experts_kernel(blk_ref, nused_ref, x_ref, wg_ref, wu_ref, wd_ref, y_ref, wg_s, wu_s, wd_s):
    i = pl.program_id(0)
    used = i < nused_ref[0]
    prev = blk_ref[jnp.maximum(i - 1, 0)]
    fresh = jnp.logical_or(i == 0, blk_ref[i] != prev)

    @pl.when(jnp.logical_and(used, fresh))
    def _():
        wg_s[...] = wg_ref[...].astype(BF16)
        wu_s[...] = wu_ref[...].astype(BF16)
        wd_s[...] = wd_ref[...].astype(BF16)

    @pl.when(used)
    def _():
        xb = x_ref[...]
        act = _silu(_dot(xb, wg_s[...])) * _dot(xb, wu_s[...])
        y_ref[...] = _dot(act.astype(BF16), wd_s[...]).astype(BF16)

    @pl.when(jnp.logical_not(used))
    def _():
        y_ref[...] = jnp.zeros_like(y_ref)


def _experts_call(blk_e, n_used, xb, layer, wg, wu, wd):
    n_rows = xb.shape[0]
    nb = n_rows // EB
    gs = pltpu.PrefetchScalarGridSpec(
        num_scalar_prefetch=2,
        grid=(nb,),
        in_specs=[
            pl.BlockSpec((EB, D_MODEL), lambda i, be, nu: (jnp.minimum(i, nu[0] - 1), 0)),
            pl.BlockSpec((None, None, D_MODEL, EXPERT_FF), lambda i, be, nu: (layer, be[i], 0, 0)),
            pl.BlockSpec((None, None, D_MODEL, EXPERT_FF), lambda i, be, nu: (layer, be[i], 0, 0)),
            pl.BlockSpec((None, None, EXPERT_FF, D_MODEL), lambda i, be, nu: (layer, be[i], 0, 0)),
        ],
        out_specs=pl.BlockSpec((EB, D_MODEL), lambda i, be, nu: (i, 0)),
        scratch_shapes=[
            pltpu.VMEM((D_MODEL, EXPERT_FF), BF16),
            pltpu.VMEM((D_MODEL, EXPERT_FF), BF16),
            pltpu.VMEM((EXPERT_FF, D_MODEL), BF16),
        ],
    )
    return pl.pallas_call(
        _experts_kernel,
        grid_spec=gs,
        out_shape=jax.ShapeDtypeStruct((n_rows, D_MODEL), BF16),
        compiler_params=_cparams(("arbitrary",)),
        name="moe_experts",
    )(blk_e, n_used, xb, wg, wu, wd)


def _combine_kernel(final, tbl_ref, src_ref, dst_ref, nch_ref, tot_ref, yb_hbm, base_ref, gtok_ref, ptok_ref,
                    mod_ref, gf_ref, out_ref, sort_s, sem_r):
    del tbl_ref
    i = pl.program_id(0)
    sort_s[...] = jnp.zeros_like(sort_s)
    _for_each_chunk(i, src_ref, dst_ref, nch_ref,
                    lambda s, d: _chunk_copy(yb_hbm, d, sort_s, s, sem_r).start())

    def drain(q, c):
        _chunk_copy(yb_hbm, 0, sort_s, 0, sem_r).wait()
        return c

    lax.fori_loop(0, tot_ref[i], drain, 0)

    gt = gtok_ref[...]
    pt = ptok_ref[...]
    col = lax.broadcasted_iota(I32, (TM, SORT_ROWS), 1).astype(F32)
    w = jnp.zeros((TM, SORT_ROWS), F32)
    for k in range(TOP_K):
        w = jnp.where(col == pt[:, k:k + 1], gt[:, k:k + 1], w)
    wh = w.astype(BF16)
    wl = (w - wh.astype(F32)).astype(BF16)
    ys = sort_s[...]
    routed = _dot(wh, ys) + _dot(wl, ys)
    gate2 = mod_ref[...][:, 5 * D_MODEL:6 * D_MODEL]
    y = base_ref[...] + gate2 * routed
    if final:
        y = y * lax.rsqrt(jnp.mean(y * y, axis=-1, keepdims=True) + NORM_EPS) * gf_ref[...]
    out_ref[...] = y


def _combine_call(final, tbl, src, dst, nch, tot, yb, base, gtok, ptok, mod3, gfinal):
    n_tok = base.shape[0]
    nt = n_tok // TM
    tile = lambda i, t: (i, 0)
    smem = pl.BlockSpec(memory_space=pltpu.SMEM)
    gs = pltpu.PrefetchScalarGridSpec(
        num_scalar_prefetch=1,
        grid=(nt,),
        in_specs=[
            smem, smem, smem, smem,
            pl.BlockSpec(memory_space=pl.ANY),
            pl.BlockSpec((TM, D_MODEL), tile),
            pl.BlockSpec((TM, LANES), tile),
            pl.BlockSpec((TM, LANES), tile),
            pl.BlockSpec((None, 1, 6 * D_MODEL), lambda i, t: (t[0, i], 0, 0)),
            pl.BlockSpec((1, D_MODEL), lambda i, t: (0, 0)),
        ],
        out_specs=pl.BlockSpec((TM, D_MODEL), tile),
        scratch_shapes=[
            pltpu.VMEM((SORT_ROWS, D_MODEL), BF16),
            pltpu.SemaphoreType.DMA,
        ],
    )
    return pl.pallas_call(
        functools.partial(_combine_kernel, final),
        grid_spec=gs,
        out_shape=jax.ShapeDtypeStruct((n_tok, D_MODEL), F32),
        compiler_params=_cparams(("arbitrary",)),
        name="moe_combine",
    )(tbl, src, dst, nch, tot, yb, base, gtok, ptok, mod3, gfinal)


def _pad_cols(w, groups, width, slot):
    k = w.shape[0]
    w = w.reshape(k, groups, width)
    return jnp.pad(w, ((0, 0), (0, 0), (0, slot - width))).reshape(k, groups * slot)


def _prep_w_in(w):
    c = np.cumsum([0, 256, 256, 256, 256, 256, 256, 256, 256, 128, 32])
    rq, rk, rv, rg, dq, dk, dv, cq, ckv, kpe = [w[:, c[n]:c[n + 1]] for n in range(10)]
    kpe_slot = jnp.concatenate([kpe, jnp.zeros_like(kpe), kpe, jnp.zeros_like(kpe)], axis=1)
    cols = [rq, rk * (RET_DIM ** -0.5), rv, rg, _pad_cols(dq, DIFF_HEADS, 2 * DIFF_QK_DIM, SLOT),
            _pad_cols(dk, DIFF_HEADS, 2 * DIFF_QK_DIM, SLOT), dv, cq, ckv, kpe_slot]
    return jnp.concatenate(cols, axis=1).astype(BF16)


def _rope_tables(n_pos, dim, lane_offsets):
    n_rows = n_pos // GRID_W
    row = jnp.repeat(jnp.arange(n_rows, dtype=F32), GRID_W)
    col = jnp.tile(jnp.arange(GRID_W, dtype=F32), n_rows)
    half = dim // 2
    freqs = ROPE_THETA ** (-jnp.arange(0, half, 2, dtype=F32) / half)
    ar = row[:, None] * freqs[None, :]
    ac = col[:, None] * freqs[None, :]
    ang = jnp.concatenate([ar, ar, ac, ac], axis=-1)
    cos, sin = jnp.cos(ang), jnp.sin(ang)
    first = (np.arange(dim) % 16) < 8
    sa = jnp.where(first[None, :], -sin, 0.0)
    sb = jnp.where(first[None, :], 0.0, sin)
    c_t = jnp.ones((n_pos, SLOT), F32)
    a_t = jnp.zeros((n_pos, SLOT), F32)
    b_t = jnp.zeros((n_pos, SLOT), F32)
    for off in lane_offsets:
        c_t = c_t.at[:, off:off + dim].set(cos)
        a_t = a_t.at[:, off:off + dim].set(sa)
        b_t = b_t.at[:, off:off + dim].set(sb)
    ident = (jnp.ones((TM, SLOT), F32), jnp.zeros((TM, SLOT), F32), jnp.zeros((TM, SLOT), F32))
    return tuple(jnp.concatenate([i0, t], axis=0) for i0, t in zip(ident, (c_t, a_t, b_t)))


def _block_diag_states(s):
    b, l, h, dk, dv = s.shape
    eye = jnp.eye(h, dtype=s.dtype)
    return jnp.einsum('blhkv,hg->blhkgv', s, eye).reshape(b, l, h * dk, h * dv)


def kernel(x_prompt, x_sample, cache_diff_k, cache_diff_v, cache_mla_ckv, cache_mla_kpe, state_ret_fwd, state_ret_bwd, c, c_ctx, w_ada, b_ada, norm_mix, norm_ffn, norm_final, w_in, ret_decay_fwd, ret_decay_bwd, diff_lambda, diff_subln, mla_q_norm, mla_w_uq, mla_kv_norm, mla_w_ukv, w_out, router_w, router_bias, exp_w_gate, exp_w_up, exp_w_down, sh_w_gate, sh_w_up, sh_w_down):
    n_pb, p_len, _ = x_prompt.shape
    n_sb, s_len, _ = x_sample.shape
    past_len = cache_diff_k.shape[3]
    n_p = n_pb * p_len
    n_s = n_sb * s_len
    n_tok = n_p + n_s
    nt = n_tok // TM
    npt = n_p // TM
    assert p_len == TM and s_len % TM == 0 and n_p % s_len == 0 and past_len % 8 == 0

    tiles = np.arange(nt)
    mod_row = np.where(tiles < npt, n_sb, (tiles - npt) // (s_len // TM))
    rope_blk = np.where(tiles < npt, 0, 1 + (tiles - npt) % (s_len // TM))
    tbl = jnp.asarray(np.stack([mod_row, rope_blk, np.full(nt, npt)]).astype(np.int32))

    n_cond = 16
    cond = jnp.zeros((n_cond, D_MODEL), F32).at[:n_sb].set(c).at[n_sb].set(c_ctx)
    mod_all = _modulation(cond, w_ada, b_ada)

    rope_d = _rope_tables(s_len, DIFF_QK_DIM, (0, DIFF_QK_DIM))
    rope_m = _rope_tables(s_len, MLA_ROPE_DIM, (KR_LO,))
    place = jnp.zeros((MLA_ROPE_DIM, SLOT), F32).at[np.arange(MLA_ROPE_DIM), KR_LO + np.arange(MLA_ROPE_DIM)].set(1.0).astype(BF16)
    cache_dv_t = cache_diff_v.transpose(0, 1, 3, 2, 4).reshape(n_sb, DEPTH, past_len, DIFF_V_W)
    s0f_bd = _block_diag_states(state_ret_fwd)
    s0b_bd = _block_diag_states(state_ret_bwd)

    n_blocks = pl.cdiv(n_tok * TOP_K + nt * N_EXPERTS * (CH - 1) + N_EXPERTS * (EB - CH), EB)
    n_rows = n_blocks * EB

    x = jnp.concatenate([x_prompt.reshape(n_p, D_MODEL), x_sample.reshape(n_s, D_MODEL)], axis=0)
    gfinal = norm_final.reshape(1, D_MODEL)
    caches = []
    for l in range(DEPTH):
        lam_init = 0.8 - 0.6 * math.exp(-0.3 * l)
        mod3 = mod_all[l].reshape(n_cond, 1, 6 * D_MODEL)
        w_pre = _prep_w_in(w_in[l])
        wuq = _pad_cols(mla_w_uq[l], MLA_HEADS, MLA_NOPE_DIM + MLA_ROPE_DIM, SLOT).astype(BF16)
        ukv = mla_w_ukv[l].reshape(MLA_KV_LORA, MLA_HEADS, MLA_NOPE_DIM + MLA_V_DIM)
        wk = _pad_cols(ukv[:, :, :MLA_NOPE_DIM].reshape(MLA_KV_LORA, -1), MLA_HEADS, MLA_NOPE_DIM, SLOT).astype(BF16)
        wv = ukv[:, :, MLA_NOPE_DIM:].reshape(MLA_KV_LORA, MLA_V_W).astype(BF16)
        decf = jnp.repeat(ret_decay_fwd[l], RET_DIM).reshape(1, RET_W)
        decb = jnp.repeat(ret_decay_bwd[l], RET_DIM).reshape(1, RET_W)
        subln = jnp.tile(diff_subln[l], DIFF_HEADS).reshape(1, DIFF_V_W)

        ra, qa, ka = _pre_call(tbl, x, mod3, norm_mix[l].reshape(1, D_MODEL), w_pre,
                               mla_q_norm[l].reshape(1, -1), wuq, mla_kv_norm[l].reshape(1, -1),
                               rope_d, rope_m)
        mix_p, sf, sb = _mix_prompt_call(lam_init, qa, ka, ra, n_pb, p_len, decf, decb,
                                         diff_lambda[l], subln, wk, wv)
        mix_s = _mix_sample_call(lam_init, l, qa, ka, ra, n_p, n_sb, s_len, past_len, cache_diff_k,
                                 cache_dv_t, cache_mla_ckv, cache_mla_kpe, s0f_bd, s0b_bd, decf, decb,
                                 diff_lambda[l], subln, wk, wv, place)
        base, h2, lpos, ptok, gtok, cnt = _post_call(
            tbl, x, mix_p, mix_s, mod3, w_out[l].astype(BF16), norm_ffn[l].reshape(1, D_MODEL),
            sh_w_gate[l].astype(BF16), sh_w_up[l].astype(BF16), sh_w_down[l].astype(BF16),
            router_w[l].T, router_bias[l].reshape(N_EXPERTS, 1))
        src, dst, nch, tot, blk_e, n_used, seg = _plan_call(cnt, nt, n_blocks)
        xb = _dispatch_call(src, dst, nch, tot, seg, h2, lpos, n_rows)
        yb = _experts_call(blk_e, n_used, xb, l, exp_w_gate, exp_w_up, exp_w_down)
        x = _combine_call(l == DEPTH - 1, tbl, src, dst, nch, tot, yb, base, gtok, ptok, mod3, gfinal)

        kp = ka[:n_p].reshape(n_pb, p_len, KA_W)
        dk = kp[:, :, KA_DK:KA_DV].reshape(n_pb, p_len, DIFF_HEADS, SLOT)[..., :2 * DIFF_QK_DIM]
        dv = kp[:, :, KA_DV:KA_CKV].reshape(n_pb, p_len, DIFF_HEADS, DIFF_V_DIM)
        diag = lambda s: jnp.stack([s[:, h * RET_DIM:(h + 1) * RET_DIM, h * RET_DIM:(h + 1) * RET_DIM]
                                    for h in range(RET_HEADS)], axis=1)
        caches.append((dk.transpose(0, 2, 1, 3), dv.transpose(0, 2, 1, 3), kp[:, :, KA_CKV:KA_KPE],
                       kp[:, :, KA_KPE:KA_KPE + MLA_ROPE_DIM], diag(sf), diag(sb)))

    y_prompt = x[:n_p].reshape(n_pb, p_len, D_MODEL)
    y_sample = x[n_p:].reshape(n_sb, s_len, D_MODEL)
    new = [jnp.stack([cs[n] for cs in caches], axis=1) for n in range(6)]
    return (y_prompt, y_sample, *new)
```

```python
import functools
import math

import numpy as np
import jax
import jax.numpy as jnp
from jax import lax
from jax.experimental import pallas as pl
from jax.experimental.pallas import tpu as pltpu

F32 = jnp.float32
BF16 = jnp.bfloat16
I32 = jnp.int32

D_MODEL = 1024
DEPTH = 2
GRID_W = 64
ROPE_THETA = 10000.0
NORM_EPS = 1e-6

RET_HEADS = 4
RET_DIM = 64
RET_CHUNK = 128
RET_W = RET_HEADS * RET_DIM
DIFF_HEADS = 4
DIFF_QK_DIM = 32
DIFF_V_DIM = 64
DIFF_V_W = DIFF_HEADS * DIFF_V_DIM
MLA_HEADS = 8
MLA_Q_LORA = 256
MLA_KV_LORA = 128
MLA_NOPE_DIM = 64
MLA_ROPE_DIM = 32
MLA_V_DIM = 64
MLA_V_W = MLA_HEADS * MLA_V_DIM
MIX_W = RET_W + DIFF_V_W + MLA_V_W

N_EXPERTS = 64
TOP_K = 6
N_GROUPS = 8
GROUP_SIZE = N_EXPERTS // N_GROUPS
TOPK_GROUPS = 4
EXPERT_FF = 256
ROUTED_SCALE = 2.5

LANES = 128
SLOT = LANES
TM = 256
SLOT_ROWS = 8
EB = 512
LOG_EB = 9
CH = 16
LOG_CH = 4
SORT_ROWS = 2560
VMEM_LIMIT = 48 * 1024 * 1024

C_RQ, C_RK, C_RV, C_RG = 0, 256, 512, 768
C_DQ = 1024
C_DK = 1536
C_DV = 2048
C_CQ = 2304
C_CKV = 2560
C_KPE = 2688
N_PRE = 2816
QA_W = 4 * SLOT + MLA_HEADS * SLOT
KA_DK, KA_DV, KA_CKV, KA_KPE = 0, 512, 768, 896
KA_W = 1024
KR_LO, KR_HI = 64, 96


def _dot(a, b):
    return jnp.dot(a, b, preferred_element_type=F32)


def _dot_nt(a, b):
    return lax.dot_general(a, b, (((1,), (1,)), ((), ())), preferred_element_type=F32)


def _dot_tn(a, b):
    return lax.dot_general(a, b, (((0,), (0,)), ((), ())), preferred_element_type=F32)


def _split_dot(x, w_bf16):
    hi = x.astype(BF16)
    lo = (x - hi.astype(F32)).astype(BF16)
    return _dot(hi, w_bf16) + _dot(lo, w_bf16)


def _split_dot_nt(w, x):
    wh = w.astype(BF16)
    wl = (w - wh.astype(F32)).astype(BF16)
    xh = x.astype(BF16)
    xl = (x - xh.astype(F32)).astype(BF16)
    return _dot_nt(wh, xh) + _dot_nt(wh, xl) + _dot_nt(wl, xh)


def _silu(x):
    return x * jax.nn.sigmoid(x)


def _cparams(sem):
    return pltpu.CompilerParams(dimension_semantics=sem, vmem_limit_bytes=VMEM_LIMIT)


MOD_TN = 512


def _mod_kernel(c_ref, w_ref, b_ref, o_ref):
    s = _silu(c_ref[...])
    o_ref[...] = _split_dot3(s, w_ref[...]) + b_ref[...]


def _split_dot3(x, w):
    xh = x.astype(BF16)
    xl = (x - xh.astype(F32)).astype(BF16)
    wh = w.astype(BF16)
    wl = (w - wh.astype(F32)).astype(BF16)
    return _dot(xh, wh) + _dot(xh, wl) + _dot(xl, wh)


def _modulation(cond, w_ada, b_ada):
    n_rows = cond.shape[0]
    n_out = w_ada.shape[-1]
    return pl.pallas_call(
        _mod_kernel,
        grid=(DEPTH, n_out // MOD_TN),
        in_specs=[
            pl.BlockSpec((n_rows, D_MODEL), lambda l, j: (0, 0)),
            pl.BlockSpec((None, D_MODEL, MOD_TN), lambda l, j: (l, 0, j)),
            pl.BlockSpec((None, 1, MOD_TN), lambda l, j: (l, 0, j)),
        ],
        out_specs=pl.BlockSpec((None, n_rows, MOD_TN), lambda l, j: (l, 0, j)),
        out_shape=jax.ShapeDtypeStruct((DEPTH, n_rows, n_out), F32),
        compiler_params=_cparams(("arbitrary", "arbitrary")),
        name="adaln_mod",
    )(cond, w_ada, b_ada.reshape(DEPTH, 1, n_out))


def _rope_slot(x, cos, sa, sb):
    up = pltpu.roll(x, LANES - 8, 1)
    dn = pltpu.roll(x, 8, 1)
    return x * cos + up * sa + dn * sb


def _pre_kernel(tbl_ref, x_ref, mod_ref, g_ref, w_ref, qg_ref, wuq_ref, kvg_ref,
                cd_ref, sad_ref, sbd_ref, cm_ref, sam_ref, sbm_ref,
                ra_ref, qa_ref, ka_ref):
    del tbl_ref
    x = x_ref[...]
    mod = mod_ref[...]
    shift1 = mod[:, 0:D_MODEL]
    scale1 = mod[:, D_MODEL:2 * D_MODEL]
    ms = jnp.mean(x * x, axis=-1, keepdims=True)
    h = x * lax.rsqrt(ms + NORM_EPS) * g_ref[...]
    h = h * (1.0 + scale1) + shift1
    hb = h.astype(BF16)

    def proj(lo, hi):
        return _dot(hb, w_ref[:, lo:hi])

    ra_ref[:, 0:C_RG] = proj(C_RQ, C_RG)
    ra_ref[:, C_RG:C_DQ] = _silu(proj(C_RG, C_DQ))

    cd, sad, sbd = cd_ref[...], sad_ref[...], sbd_ref[...]
    cm, sam, sbm = cm_ref[...], sam_ref[...], sbm_ref[...]
    dq = proj(C_DQ, C_DK)
    dk = proj(C_DK, C_DV)
    for hd in range(DIFF_HEADS):
        sl = slice(hd * SLOT, (hd + 1) * SLOT)
        qa_ref[:, sl] = _rope_slot(dq[:, sl], cd, sad, sbd)
        ka_ref[:, KA_DK + hd * SLOT:KA_DK + (hd + 1) * SLOT] = _rope_slot(dk[:, sl], cd, sad, sbd)
    ka_ref[:, KA_DV:KA_CKV] = proj(C_DV, C_CQ)

    cq = proj(C_CQ, C_CKV)
    cqn = cq * lax.rsqrt(jnp.mean(cq * cq, axis=-1, keepdims=True) + NORM_EPS) * qg_ref[...]
    qm = _dot(cqn.astype(BF16), wuq_ref[...])
    for hd in range(MLA_HEADS):
        sl = slice(hd * SLOT, (hd + 1) * SLOT)
        qa_ref[:, 4 * SLOT + hd * SLOT:4 * SLOT + (hd + 1) * SLOT] = _rope_slot(qm[:, sl], cm, sam, sbm)

    ckv = proj(C_CKV, C_KPE)
    ka_ref[:, KA_CKV:KA_KPE] = ckv * lax.rsqrt(jnp.mean(ckv * ckv, axis=-1, keepdims=True) + NORM_EPS) * kvg_ref[...]
    ka_ref[:, KA_KPE:KA_W] = _rope_slot(proj(C_KPE, N_PRE), cm, sam, sbm)


def _layer_spec(layer, rows, cols):
    return pl.BlockSpec((None, rows, cols), lambda *_: (layer, 0, 0))


def _mod_spec(layer):
    return pl.BlockSpec((None, None, 1, 6 * D_MODEL), lambda i, t: (layer, t[0, i], 0, 0))


def _pre_call(layer, tbl, x, mod4, g, w_pre, qg, wuq, kvg, rope_d, rope_m):
    n_tok = x.shape[0]
    nt = n_tok // TM
    tile = lambda i, t: (i, 0)
    rope = lambda i, t: (t[1, i], 0)
    gs = pltpu.PrefetchScalarGridSpec(
        num_scalar_prefetch=1,
        grid=(nt,),
        in_specs=[
            pl.BlockSpec((TM, D_MODEL), tile),
            _mod_spec(layer),
            _layer_spec(layer, 1, D_MODEL),
            _layer_spec(layer, D_MODEL, N_PRE),
            _layer_spec(layer, 1, MLA_Q_LORA),
            _layer_spec(layer, MLA_Q_LORA, MLA_HEADS * SLOT),
            _layer_spec(layer, 1, MLA_KV_LORA),
        ] + [pl.BlockSpec((TM, SLOT), rope)] * 6,
        out_specs=[
            pl.BlockSpec((TM, D_MODEL), tile),
            pl.BlockSpec((TM, QA_W), tile),
            pl.BlockSpec((TM, KA_W), tile),
        ],
    )
    return pl.pallas_call(
        _pre_kernel,
        grid_spec=gs,
        out_shape=[
            jax.ShapeDtypeStruct((n_tok, D_MODEL), F32),
            jax.ShapeDtypeStruct((n_tok, QA_W), F32),
            jax.ShapeDtypeStruct((n_tok, KA_W), F32),
        ],
        compiler_params=_cparams(("arbitrary",)),
        name="pre_proj",
    )(tbl, x, mod4, g, w_pre, qg, wuq, kvg, *rope_d, *rope_m)


def _lane_iota(shape):
    return lax.broadcasted_iota(I32, shape, len(shape) - 1)


def _head_mask(n_rows, width, head, head_w):
    lane = _lane_iota((n_rows, width))
    return (lane >= head * head_w) & (lane < (head + 1) * head_w)


def _seg_mean_sq(o, bd_ones):
    return _split_dot(o * o, bd_ones) * (1.0 / RET_DIM)


def _block_diag_ones(n, blk):
    r = lax.broadcasted_iota(I32, (n, n), 0) // blk
    c = lax.broadcasted_iota(I32, (n, n), 1) // blk
    return r == c


def _retention(ra_ref, seq_len, decf_ref, decb_ref, s0f, s0b):
    C = RET_CHUNK
    nc = seq_len // C
    lgf = -jnp.exp(decf_ref[...])
    lgb = -jnp.exp(decb_ref[...])
    pos = lax.broadcasted_iota(I32, (C, RET_W), 0).astype(F32)
    qdf = jnp.exp((pos + 1.0) * lgf)
    kdf = jnp.exp((C - 1.0 - pos) * lgf)
    cdf = jnp.exp(float(C) * lgf)
    qdb = jnp.exp((C - pos) * lgb)
    kdb = jnp.exp(pos * lgb)
    cdb = jnp.exp(float(C) * lgb)
    ii = lax.broadcasted_iota(I32, (C, C), 0).astype(F32)
    jj = lax.broadcasted_iota(I32, (C, C), 1).astype(F32)
    dist = ii - jj
    dmats = []
    for hd in range(RET_HEADS):
        lf = lgf[:, hd * RET_DIM:hd * RET_DIM + 1]
        lb = lgb[:, hd * RET_DIM:hd * RET_DIM + 1]
        dmats.append(jnp.where(dist >= 0, jnp.exp(dist * lf), jnp.exp(-dist * lb)))
    bd = _block_diag_ones(RET_W, RET_DIM)
    bd_ones = jnp.where(bd, 1.0, 0.0).astype(BF16)

    def chunk(n):
        rows = slice(n * C, (n + 1) * C)
        return (ra_ref[rows, C_RQ:C_RK], ra_ref[rows, C_RK:C_RV], ra_ref[rows, C_RV:C_RG])

    cross = [None] * nc
    sf = s0f
    for n in range(nc):
        q, k, v = chunk(n)
        cross[n] = _dot((q * qdf).astype(BF16), sf.astype(BF16))
        kv = _dot_tn((k * kdf).astype(BF16), v.astype(BF16))
        sf = sf * cdf + jnp.where(bd, kv, 0.0)
    sb = s0b
    for n in range(nc - 1, -1, -1):
        q, k, v = chunk(n)
        cross[n] = cross[n] + _dot((q * qdb).astype(BF16), sb.astype(BF16))
        kv = _dot_tn((k * kdb).astype(BF16), v.astype(BF16))
        sb = sb * cdb + jnp.where(bd, kv, 0.0)

    outs = []
    for n in range(nc):
        q, k, v = chunk(n)
        kb = k.astype(BF16)
        vb = v.astype(BF16)
        o = cross[n]
        for hd in range(RET_HEADS):
            hm = _head_mask(C, RET_W, hd, RET_DIM)
            sc = _dot_nt(jnp.where(hm, q, 0.0).astype(BF16), kb) * dmats[hd]
            o = o + jnp.where(hm, _dot(sc.astype(BF16), vb), 0.0)
        on = o * lax.rsqrt(_seg_mean_sq(o, bd_ones) + NORM_EPS)
        outs.append(on * ra_ref[n * C:(n + 1) * C, C_RG:C_DQ])
    return outs, sf, sb


def _softmax_pv(s_parts, v_parts, scale):
    m = None
    for s in s_parts:
        mm = jnp.max(s, axis=-1, keepdims=True)
        m = mm if m is None else jnp.maximum(m, mm)
    m = m * scale
    acc = None
    den = None
    for s, v in zip(s_parts, v_parts):
        e = jnp.exp(s * scale - m)
        ds = jnp.sum(e, axis=-1, keepdims=True)
        pv = _dot(e.astype(BF16), v)
        acc = pv if acc is None else acc + pv
        den = ds if den is None else den + ds
    return acc / den


def _diff_attention(dq, k_parts, v_parts, lam, subln, lam_init, bd_ones):
    lq = dq.shape[0]
    scale = DIFF_QK_DIM ** -0.5
    lane = _lane_iota((lq, SLOT))
    out = jnp.zeros((lq, DIFF_V_W), F32)
    for hd in range(DIFF_HEADS):
        qh = dq[:, hd * SLOT:(hd + 1) * SLOT]
        q1 = jnp.where(lane < DIFF_QK_DIM, qh, 0.0).astype(BF16)
        q2 = jnp.where(lane >= DIFF_QK_DIM, qh, 0.0).astype(BF16)
        s1 = [_dot_nt(q1[:, :kp[hd].shape[1]], kp[hd]) for kp in k_parts]
        s2 = [_dot_nt(q2[:, :kp[hd].shape[1]], kp[hd]) for kp in k_parts]
        o = _softmax_pv(s1, v_parts, scale) - lam * _softmax_pv(s2, v_parts, scale)
        out = jnp.where(_head_mask(lq, DIFF_V_W, hd, DIFF_V_DIM), o, out)
    on = out * lax.rsqrt(_seg_mean_sq(out, bd_ones) + NORM_EPS) * subln
    return on * (1.0 - lam_init)


def _mla_attention(qm, k_parts, v_parts):
    lq = qm.shape[0]
    scale = (MLA_NOPE_DIM + MLA_ROPE_DIM) ** -0.5
    halves = []
    for g in range(2):
        out = jnp.zeros((lq, 256), F32)
        for hh in range(4):
            hd = 4 * g + hh
            qh = qm[:, hd * SLOT:(hd + 1) * SLOT].astype(BF16)
            s = [_dot_nt(qh, kp[:, hd * SLOT:(hd + 1) * SLOT]) for kp in k_parts]
            o = _softmax_pv(s, [vp[:, 256 * g:256 * (g + 1)] for vp in v_parts], scale)
            out = jnp.where(_head_mask(lq, 256, hh, MLA_V_DIM), o, out)
        halves.append(out)
    return halves


def _mla_keys(ka_val_ckv, kr_slot, wk_ref, wv_ref):
    cb = ka_val_ckv.astype(BF16)
    kn = _dot(cb, wk_ref[...])
    ks = [(kn[:, hd * SLOT:(hd + 1) * SLOT] + kr_slot).astype(BF16) for hd in range(MLA_HEADS)]
    return jnp.concatenate(ks, axis=1), _dot(cb, wv_ref[...]).astype(BF16)


def _kr_only(kpe_slot):
    lane = _lane_iota(kpe_slot.shape)
    return jnp.where((lane >= KR_LO) & (lane < KR_HI), kpe_slot, 0.0)


def _diff_lambda(dl_ref, lam_init):
    dl = dl_ref[...]
    a = jnp.sum(dl[0:1] * dl[1:2], axis=-1, keepdims=True)
    b = jnp.sum(dl[2:3] * dl[3:4], axis=-1, keepdims=True)
    return jnp.exp(a) - jnp.exp(b) + lam_init


def _mix_prompt_kernel(lam_init, qa_ref, ka_ref, ra_ref, decf_ref, decb_ref, dl_ref, subln_ref,
                       wk_ref, wv_ref, mix_ref, sf_ref, sb_ref):
    seq = qa_ref.shape[0]
    zero_state = jnp.zeros((RET_W, RET_W), F32)
    outs, sf, sb = _retention(ra_ref, seq, decf_ref, decb_ref, zero_state, zero_state)
    for n, o in enumerate(outs):
        mix_ref[n * RET_CHUNK:(n + 1) * RET_CHUNK, 0:RET_W] = o
    sf_ref[...] = sf
    sb_ref[...] = sb

    bd_ones = jnp.where(_block_diag_ones(DIFF_V_W, DIFF_V_DIM), 1.0, 0.0).astype(BF16)
    lam = _diff_lambda(dl_ref, lam_init)
    kd = [ka_ref[:, KA_DK + hd * SLOT:KA_DK + (hd + 1) * SLOT].astype(BF16) for hd in range(DIFF_HEADS)]
    vd = ka_ref[:, KA_DV:KA_CKV].astype(BF16)
    mix_ref[:, RET_W:RET_W + DIFF_V_W] = _diff_attention(
        qa_ref[:, 0:4 * SLOT], [kd], [vd], lam, subln_ref[...], lam_init, bd_ones)

    km, vm = _mla_keys(ka_ref[:, KA_CKV:KA_KPE], _kr_only(ka_ref[:, KA_KPE:KA_W]), wk_ref, wv_ref)
    halves = _mla_attention(qa_ref[:, 4 * SLOT:QA_W], [km], [vm])
    mix_ref[:, 512:768] = halves[0]
    mix_ref[:, 768:1024] = halves[1]


def _mixer_param_specs(layer):
    return [
        _layer_spec(layer, 1, RET_W),
        _layer_spec(layer, 1, RET_W),
        _layer_spec(layer, 4, DIFF_QK_DIM),
        _layer_spec(layer, 1, DIFF_V_W),
        _layer_spec(layer, MLA_KV_LORA, MLA_HEADS * SLOT),
        _layer_spec(layer, MLA_KV_LORA, MLA_V_W),
    ]


def _mix_prompt_call(lam_init, layer, qa, ka, ra, n_seq, seq_len, decf, decb, dl, subln, wk, wv):
    seq = lambda b: (b, 0)
    return pl.pallas_call(
        functools.partial(_mix_prompt_kernel, lam_init),
        grid=(n_seq,),
        in_specs=[
            pl.BlockSpec((seq_len, QA_W), seq),
            pl.BlockSpec((seq_len, KA_W), seq),
            pl.BlockSpec((seq_len, D_MODEL), seq),
        ] + _mixer_param_specs(layer),
        out_specs=[
            pl.BlockSpec((seq_len, MIX_W), seq),
            pl.BlockSpec((None, RET_W, RET_W), lambda b: (b, 0, 0)),
            pl.BlockSpec((None, RET_W, RET_W), lambda b: (b, 0, 0)),
        ],
        out_shape=[
            jax.ShapeDtypeStruct((n_seq * seq_len, MIX_W), F32),
            jax.ShapeDtypeStruct((n_seq, RET_W, RET_W), F32),
            jax.ShapeDtypeStruct((n_seq, RET_W, RET_W), F32),
        ],
        compiler_params=_cparams(("arbitrary",)),
        name="mix_prompt",
    )(qa, ka, ra, decf, decb, dl, subln, wk, wv)


def _mix_sample_kernel(lam_init, qa_ref, ka_ref, ra_ref, ckd_ref, cvd_ref, cckv_ref, ckpe_ref,
                       s0f_ref, s0b_ref, decf_ref, decb_ref, dl_ref, subln_ref, wk_ref, wv_ref,
                       place_ref, mix_ref,
                       ret_s, kdn_s, vdn_s, kdc_s, vdc_s, kmn_s, vmn_s, kmc_s, vmc_s):
    j = pl.program_id(1)
    seq = ka_ref.shape[0]

    @pl.when(j == 0)
    def _():
        outs, _, _ = _retention(ra_ref, seq, decf_ref, decb_ref, s0f_ref[...], s0b_ref[...])
        for n, o in enumerate(outs):
            ret_s[n * RET_CHUNK:(n + 1) * RET_CHUNK, :] = o
        kdn_s[...] = ka_ref[:, KA_DK:KA_DV].astype(BF16)
        vdn_s[...] = ka_ref[:, KA_DV:KA_CKV].astype(BF16)
        kdc_s[...] = ckd_ref[...].astype(BF16)
        vdc_s[...] = cvd_ref[...].astype(BF16)
        km, vm = _mla_keys(ka_ref[:, KA_CKV:KA_KPE], _kr_only(ka_ref[:, KA_KPE:KA_W]), wk_ref, wv_ref)
        kmn_s[...] = km
        vmn_s[...] = vm
        kr_ctx = _dot(ckpe_ref[...].astype(BF16), place_ref[...])
        km, vm = _mla_keys(cckv_ref[...], kr_ctx, wk_ref, wv_ref)
        kmc_s[...] = km
        vmc_s[...] = vm

    row0 = pl.multiple_of(j * TM, TM)
    mix_ref[:, 0:RET_W] = ret_s[pl.ds(row0, TM), :]

    bd_ones = jnp.where(_block_diag_ones(DIFF_V_W, DIFF_V_DIM), 1.0, 0.0).astype(BF16)
    lam = _diff_lambda(dl_ref, lam_init)
    kd_ctx = [kdc_s[hd] for hd in range(DIFF_HEADS)]
    kd_new = [kdn_s[:, hd * SLOT:(hd + 1) * SLOT] for hd in range(DIFF_HEADS)]
    mix_ref[:, RET_W:RET_W + DIFF_V_W] = _diff_attention(
        qa_ref[:, 0:4 * SLOT], [kd_ctx, kd_new], [vdc_s[...], vdn_s[...]], lam, subln_ref[...],
        lam_init, bd_ones)

    halves = _mla_attention(qa_ref[:, 4 * SLOT:QA_W], [kmc_s[...], kmn_s[...]], [vmc_s[...], vmn_s[...]])
    mix_ref[:, 512:768] = halves[0]
    mix_ref[:, 768:1024] = halves[1]


def _mix_sample_call(lam_init, layer, qa, ka, ra, tok0, n_seq, seq_len, past_len, cache_dk, cache_dv_t,
                     cache_ckv, cache_kpe, s0f_bd, s0b_bd, decf, decb, dl, subln, wk, wv, place):
    nq = seq_len // TM
    q0 = tok0 // TM
    s0 = tok0 // seq_len
    const = lambda b, j: (0, 0)
    return pl.pallas_call(
        functools.partial(_mix_sample_kernel, lam_init),
        grid=(n_seq, nq),
        in_specs=[
            pl.BlockSpec((TM, QA_W), lambda b, j: (q0 + b * nq + j, 0)),
            pl.BlockSpec((seq_len, KA_W), lambda b, j: (s0 + b, 0)),
            pl.BlockSpec((seq_len, D_MODEL), lambda b, j: (s0 + b, 0)),
            pl.BlockSpec((None, None, DIFF_HEADS, past_len, 2 * DIFF_QK_DIM), lambda b, j: (b, layer, 0, 0, 0)),
            pl.BlockSpec((None, None, past_len, DIFF_V_W), lambda b, j: (b, layer, 0, 0)),
            pl.BlockSpec((None, None, past_len, MLA_KV_LORA), lambda b, j: (b, layer, 0, 0)),
            pl.BlockSpec((None, None, past_len, MLA_ROPE_DIM), lambda b, j: (b, layer, 0, 0)),
            pl.BlockSpec((None, None, RET_W, RET_W), lambda b, j: (b, layer, 0, 0)),
            pl.BlockSpec((None, None, RET_W, RET_W), lambda b, j: (b, layer, 0, 0)),
        ] + _mixer_param_specs(layer) + [
            pl.BlockSpec((MLA_ROPE_DIM, SLOT), const),
        ],
        out_specs=pl.BlockSpec((TM, MIX_W), lambda b, j: (b * nq + j, 0)),
        out_shape=jax.ShapeDtypeStruct((n_seq * seq_len, MIX_W), F32),
        scratch_shapes=[
            pltpu.VMEM((seq_len, RET_W), F32),
            pltpu.VMEM((seq_len, 4 * SLOT), BF16),
            pltpu.VMEM((seq_len, DIFF_V_W), BF16),
            pltpu.VMEM((DIFF_HEADS, past_len, 2 * DIFF_QK_DIM), BF16),
            pltpu.VMEM((past_len, DIFF_V_W), BF16),
            pltpu.VMEM((seq_len, MLA_HEADS * SLOT), BF16),
            pltpu.VMEM((seq_len, MLA_V_W), BF16),
            pltpu.VMEM((past_len, MLA_HEADS * SLOT), BF16),
            pltpu.VMEM((past_len, MLA_V_W), BF16),
        ],
        compiler_params=_cparams(("arbitrary", "arbitrary")),
        name="mix_sample",
    )(qa, ka, ra, cache_dk, cache_dv_t, cache_ckv, cache_kpe, s0f_bd, s0b_bd,
      decf, decb, dl, subln, wk, wv, place)


def _route(h2, rwt_ref, rb_ref):
    tm = h2.shape[0]
    neg = -jnp.inf
    logits = _split_dot_nt(rwt_ref[...], h2)
    sc = jax.nn.sigmoid(logits)
    sel = sc + rb_ref[...]
    member = lax.broadcasted_iota(I32, (GROUP_SIZE, tm), 0).astype(F32)
    gscore = []
    for g in range(N_GROUPS):
        sg = sel[g * GROUP_SIZE:(g + 1) * GROUP_SIZE, :]
        m1 = jnp.max(sg, axis=0, keepdims=True)
        f1 = jnp.min(jnp.where(sg == m1, member, float(GROUP_SIZE)), axis=0, keepdims=True)
        m2 = jnp.max(jnp.where(member == f1, neg, sg), axis=0, keepdims=True)
        gscore.append(m1 + m2)
    gsel = [jnp.zeros((1, tm), F32) for _ in range(N_GROUPS)]
    for _ in range(TOPK_GROUPS):
        mx = gscore[0]
        for g in range(1, N_GROUPS):
            mx = jnp.maximum(mx, gscore[g])
        fi = jnp.full((1, tm), float(N_GROUPS), F32)
        for g in range(N_GROUPS - 1, -1, -1):
            fi = jnp.where(gscore[g] == mx, float(g), fi)
        for g in range(N_GROUPS):
            hit = fi == float(g)
            gsel[g] = jnp.where(hit, 1.0, gsel[g])
            gscore[g] = jnp.where(hit, neg, gscore[g])
    cand = jnp.concatenate(
        [jnp.where(gsel[g] > 0.0, sel[g * GROUP_SIZE:(g + 1) * GROUP_SIZE, :], neg) for g in range(N_GROUPS)],
        axis=0)
    flat = lax.broadcasted_iota(I32, (N_EXPERTS, tm), 0).astype(F32)
    hits, gts = [], []
    chosen = jnp.zeros((N_EXPERTS, tm), F32)
    for _ in range(TOP_K):
        mx = jnp.max(cand, axis=0, keepdims=True)
        fk = jnp.min(jnp.where(cand == mx, flat, float(N_EXPERTS)), axis=0, keepdims=True)
        hit = flat == fk
        hits.append(hit)
        gts.append(jnp.sum(jnp.where(hit, sc, 0.0), axis=0, keepdims=True))
        chosen = jnp.where(hit, 1.0, chosen)
        cand = jnp.where(hit, neg, cand)
    gsum = gts[0]
    for g in gts[1:]:
        gsum = gsum + g
    gts = [g / gsum * ROUTED_SCALE for g in gts]

    before = (lax.broadcasted_iota(I32, (tm, tm), 0) < lax.broadcasted_iota(I32, (tm, tm), 1))
    rank_in = _dot(chosen.astype(BF16), jnp.where(before, 1.0, 0.0).astype(BF16))
    cnt = jnp.sum(chosen, axis=1, keepdims=True)
    cnt_pad = jnp.floor((cnt + (CH - 1.0)) * (1.0 / CH)) * CH
    below = (lax.broadcasted_iota(I32, (N_EXPERTS, N_EXPERTS), 1) < lax.broadcasted_iota(I32, (N_EXPERTS, N_EXPERTS), 0))
    start = _dot(jnp.where(below, 1.0, 0.0).astype(BF16),
                 jnp.broadcast_to(cnt_pad, (N_EXPERTS, LANES)).astype(BF16))[:, 0:1]
    pos = rank_in + start
    lpos = [jnp.sum(jnp.where(hit, pos, 0.0), axis=0, keepdims=True) for hit in hits]
    return lpos, gts, cnt_pad, start


def _slot_rows(vals, n_rows):
    tm = vals[0].shape[1]
    row = lax.broadcasted_iota(I32, (n_rows, tm), 0)
    out = jnp.zeros((n_rows, tm), F32)
    for k, v in enumerate(vals):
        out = jnp.where(row == k, v, out)
    return out


def _post_kernel(tbl_ref, x_ref, mp_ref, ms_ref, mod_ref, wout_ref, g2_ref, shg_ref, shu_ref, shd_ref,
                 rwt_ref, rb_ref, base_ref, h2_ref, lpos_ref, ptok_ref, gtok_ref, cnt_ref, start_ref, rel_ref,
                 run_ref):
    i = pl.program_id(0)
    n_prompt_tiles = tbl_ref[2, 0]

    @pl.when(i == 0)
    def _():
        cnt_ref[...] = jnp.zeros_like(cnt_ref)
        start_ref[...] = jnp.zeros_like(start_ref)
        rel_ref[...] = jnp.zeros_like(rel_ref)
        run_ref[...] = jnp.zeros_like(run_ref)

    mod = mod_ref[...]
    gate1 = mod[:, 2 * D_MODEL:3 * D_MODEL]
    shift2 = mod[:, 3 * D_MODEL:4 * D_MODEL]
    scale2 = mod[:, 4 * D_MODEL:5 * D_MODEL]
    gate2 = mod[:, 5 * D_MODEL:6 * D_MODEL]
    mix = jnp.where(i < n_prompt_tiles, mp_ref[...], ms_ref[...])
    x1 = x_ref[...] + gate1 * _dot(mix.astype(BF16), wout_ref[...])
    ms = jnp.mean(x1 * x1, axis=-1, keepdims=True)
    h2 = x1 * lax.rsqrt(ms + NORM_EPS) * g2_ref[...]
    h2 = h2 * (1.0 + scale2) + shift2
    hb = h2.astype(BF16)
    h2_ref[...] = hb
    act = _silu(_dot(hb, shg_ref[...])) * _dot(hb, shu_ref[...])
    base_ref[...] = x1 + gate2 * _dot(act.astype(BF16), shd_ref[...])

    lpos, gts, cnt_pad, start = _route(h2, rwt_ref, rb_ref)
    lpos_ref[...] = _slot_rows(lpos, SLOT_ROWS)
    ptok_ref[...] = _slot_rows(lpos, LANES).T
    gtok_ref[...] = _slot_rows(gts, LANES).T
    col = lax.broadcasted_iota(I32, cnt_ref.shape, 1)
    run = run_ref[...]
    cnt_ref[...] = jnp.where(col == i, cnt_pad.astype(I32), cnt_ref[...])
    start_ref[...] = jnp.where(col == i, start.astype(I32), start_ref[...])
    rel_ref[...] = jnp.where(col == i, run.astype(I32), rel_ref[...])
    run_ref[...] = run + cnt_pad


def _post_call(layer, tbl, x, mix_p, mix_s, mod4, wout, g2, shg, shu, shd, rwt, rb):
    n_tok = x.shape[0]
    nt = n_tok // TM
    npt = mix_p.shape[0] // TM
    const = lambda i, t: (0, 0)
    tile = lambda i, t: (i, 0)
    gs = pltpu.PrefetchScalarGridSpec(
        num_scalar_prefetch=1,
        grid=(nt,),
        in_specs=[
            pl.BlockSpec((TM, D_MODEL), tile),
            pl.BlockSpec((TM, MIX_W), lambda i, t: (jnp.minimum(i, npt - 1), 0)),
            pl.BlockSpec((TM, MIX_W), lambda i, t: (jnp.maximum(i - npt, 0), 0)),
            _mod_spec(layer),
            _layer_spec(layer, MIX_W, D_MODEL),
            _layer_spec(layer, 1, D_MODEL),
            _layer_spec(layer, D_MODEL, EXPERT_FF),
            _layer_spec(layer, D_MODEL, EXPERT_FF),
            _layer_spec(layer, EXPERT_FF, D_MODEL),
            _layer_spec(layer, N_EXPERTS, D_MODEL),
            _layer_spec(layer, N_EXPERTS, 1),
        ],
        out_specs=[
            pl.BlockSpec((TM, D_MODEL), tile),
            pl.BlockSpec((TM, D_MODEL), tile),
            pl.BlockSpec((SLOT_ROWS, TM), lambda i, t: (0, i)),
            pl.BlockSpec((TM, LANES), tile),
            pl.BlockSpec((TM, LANES), tile),
            pl.BlockSpec((N_EXPERTS, LANES), const),
            pl.BlockSpec((N_EXPERTS, LANES), const),
            pl.BlockSpec((N_EXPERTS, LANES), const),
        ],
        scratch_shapes=[pltpu.VMEM((N_EXPERTS, 1), F32)],
    )
    assert nt <= LANES
    return pl.pallas_call(
        _post_kernel,
        grid_spec=gs,
        out_shape=[
            jax.ShapeDtypeStruct((n_tok, D_MODEL), F32),
            jax.ShapeDtypeStruct((n_tok, D_MODEL), BF16),
            jax.ShapeDtypeStruct((SLOT_ROWS, n_tok), F32),
            jax.ShapeDtypeStruct((n_tok, LANES), F32),
            jax.ShapeDtypeStruct((n_tok, LANES), F32),
            jax.ShapeDtypeStruct((N_EXPERTS, LANES), I32),
            jax.ShapeDtypeStruct((N_EXPERTS, LANES), I32),
            jax.ShapeDtypeStruct((N_EXPERTS, LANES), I32),
        ],
        compiler_params=_cparams(("arbitrary",)),
        name="post_route",
    )(tbl, x, mix_p, mix_s, mod4, wout, g2, shg, shu, shd, rwt, rb)


def _plan_kernel(last_tile, cnt_ref, rel_ref, seg0_ref, blk_ref, nused_ref, seg_ref):
    n_blocks = blk_ref.shape[0]

    def per_expert(e, carry):
        start, last_e = carry
        seg0_ref[e] = start
        end = start + rel_ref[e, last_tile] + cnt_ref[e, last_tile]
        nb = lax.shift_right_logical(end - start + (EB - 1), LOG_EB)
        b0 = lax.shift_right_logical(start, LOG_EB)

        def fill(b, _):
            blk_ref[b0 + b] = e
            return 0

        lax.fori_loop(0, nb, fill, 0)
        nxt = start + lax.shift_left(nb, LOG_EB)
        seg_ref[0, e] = end
        seg_ref[1, e] = lax.shift_right_logical(nxt - end, LOG_CH)
        seg_ref[2, e] = 0
        return nxt, jnp.where(nb > 0, e, last_e)

    total, last_e = lax.fori_loop(0, N_EXPERTS, per_expert, (jnp.int32(0), jnp.int32(0)))
    n_used = lax.shift_right_logical(total, LOG_EB)
    nused_ref[0] = n_used
    seg_ref[2, 0] = n_used

    def fill_tail(b, _):
        blk_ref[b] = last_e
        return 0

    lax.fori_loop(n_used, n_blocks, fill_tail, 0)


def _plan_call(cnt, rel, nt, n_blocks):
    smem = pl.BlockSpec(memory_space=pltpu.SMEM)
    return pl.pallas_call(
        functools.partial(_plan_kernel, nt - 1),
        in_specs=[smem, smem],
        out_specs=[smem] * 4,
        out_shape=[
            jax.ShapeDtypeStruct((N_EXPERTS,), I32),
            jax.ShapeDtypeStruct((n_blocks,), I32),
            jax.ShapeDtypeStruct((1,), I32),
            jax.ShapeDtypeStruct((3, N_EXPERTS), I32),
        ],
        name="moe_plan",
    )(cnt, rel)


def _chunk_copy(src_ref, src_row, dst_ref, dst_row, sem):
    return pltpu.make_async_copy(src_ref.at[pl.ds(pl.multiple_of(src_row, CH), CH)],
                                 dst_ref.at[pl.ds(pl.multiple_of(dst_row, CH), CH)], sem)


class _Runs:
    def __init__(self, cnt_ref, start_ref, rel_ref, seg0_ref):
        self.cnt, self.start, self.rel, self.seg0 = cnt_ref, start_ref, rel_ref, seg0_ref

    def n_chunks(self, i):
        last = N_EXPERTS - 1
        return lax.shift_right_logical(self.start[last, i] + self.cnt[last, i], LOG_CH)

    def for_each_chunk(self, i, fn):
        def per_expert(e, c):
            a0 = self.start[e, i]
            b0 = self.seg0[e] + self.rel[e, i]

            def per_chunk(q, cc):
                fn(a0 + q * CH, b0 + q * CH)
                return cc

            lax.fori_loop(0, lax.shift_right_logical(self.cnt[e, i], LOG_CH), per_chunk, 0)
            return c

        lax.fori_loop(0, N_EXPERTS, per_expert, 0)


def _dispatch_kernel(cnt_ref, start_ref, rel_ref, seg0_ref, seg_ref, h_ref, lpos_ref, xb_hbm, sort_s, zero_s, sems):
    i = pl.program_id(0)
    last = pl.num_programs(0) - 1
    slot = lax.rem(i, 2)
    runs = _Runs(cnt_ref, start_ref, rel_ref, seg0_ref)
    lp = lpos_ref[...]
    hb = h_ref[...]
    blk = TM
    for r in range(SORT_ROWS // blk):
        srow = (lax.broadcasted_iota(I32, (blk, TM), 0) + r * blk).astype(F32)
        p = jnp.zeros((blk, TM), F32)
        for k in range(TOP_K):
            p = jnp.where(srow == lp[k:k + 1, :], 1.0, p)
        sort_s[slot, r * blk:(r + 1) * blk, :] = _dot(p.astype(BF16), hb).astype(BF16)

    runs.for_each_chunk(i, lambda s, d: _chunk_copy(sort_s.at[slot], s, xb_hbm, d, sems.at[slot]).start())

    def drain(tile, sl):
        def one(q, c):
            _chunk_copy(sort_s.at[sl], 0, xb_hbm, 0, sems.at[sl]).wait()
            return c

        lax.fori_loop(0, runs.n_chunks(tile), one, 0)

    @pl.when(i > 0)
    def _():
        drain(i - 1, 1 - slot)

    @pl.when(i == last)
    def _():
        drain(i, slot)
        sem_r = sems.at[0]
        zrow = zero_s
        zrow[...] = jnp.zeros_like(zrow)

        def per_expert(e, c):
            first = seg_ref[0, e]

            def z_issue(r, cc):
                _chunk_copy(zrow, 0, xb_hbm, first + r * CH, sem_r).start()
                return cc

            lax.fori_loop(0, seg_ref[1, e], z_issue, 0)

            def z_drain(r, cc):
                _chunk_copy(zrow, 0, xb_hbm, 0, sem_r).wait()
                return cc

            lax.fori_loop(0, seg_ref[1, e], z_drain, 0)
            return c

        lax.fori_loop(0, N_EXPERTS, per_expert, 0)

        n_blocks = xb_hbm.shape[0] // EB

        def blk_copy(b):
            return pltpu.make_async_copy(zrow, xb_hbm.at[pl.ds(pl.multiple_of(b * EB, EB), EB)], sem_r)

        def t_issue(b, cc):
            blk_copy(b).start()
            return cc

        lax.fori_loop(seg_ref[2, 0], n_blocks, t_issue, 0)

        def t_drain(b, cc):
            blk_copy(0).wait()
            return cc

        lax.fori_loop(seg_ref[2, 0], n_blocks, t_drain, 0)


def _dispatch_call(cnt, start, rel, seg0, seg, h2, lpos, n_rows):
    n_tok = h2.shape[0]
    nt = n_tok // TM
    smem = pl.BlockSpec(memory_space=pltpu.SMEM)
    return pl.pallas_call(
        _dispatch_kernel,
        grid=(nt,),
        in_specs=[
            smem, smem, smem, smem, smem,
            pl.BlockSpec((TM, D_MODEL), lambda i: (i, 0)),
            pl.BlockSpec((SLOT_ROWS, TM), lambda i: (0, i)),
        ],
        out_specs=pl.BlockSpec(memory_space=pl.ANY),
        out_shape=jax.ShapeDtypeStruct((n_rows, D_MODEL), BF16),
        scratch_shapes=[
            pltpu.VMEM((2, SORT_ROWS, D_MODEL), BF16),
            pltpu.VMEM((EB, D_MODEL), BF16),
            pltpu.SemaphoreType.DMA((2,)),
        ],
        compiler_params=_cparams(("arbitrary",)),
        name="moe_dispatch",
    )(cnt, start, rel, seg0, seg, h2, lpos)


def _experts_kernel(blk_ref, nused_ref, x_ref, wg_ref, wu_ref, wd_ref, y_ref, wg_s, wu_s, wd_s):
    i = pl.program_id(0)
    used = i < nused_ref[0]
    prev = blk_ref[jnp.maximum(i - 1, 0)]
    fresh = jnp.logical_or(i == 0, blk_ref[i] != prev)

    @pl.when(jnp.logical_and(used, fresh))
    def _():
        wg_s[...] = wg_ref[...].astype(BF16)
        wu_s[...] = wu_ref[...].astype(BF16)
        wd_s[...] = wd_ref[...].astype(BF16)

    @pl.when(used)
    def _():
        xb = x_ref[...]
        act = _silu(_dot(xb, wg_s[...])) * _dot(xb, wu_s[...])
        y_ref[...] = _dot(act.astype(BF16), wd_s[...]).astype(BF16)

    @pl.when(jnp.logical_not(used))
    def _():
        y_ref[...] = jnp.zeros_like(y_ref)


def _experts_call(blk_e, n_used, xb, layer, wg, wu, wd):
    n_rows = xb.shape[0]
    nb = n_rows // EB
    gs = pltpu.PrefetchScalarGridSpec(
        num_scalar_prefetch=2,
        grid=(nb,),
        in_specs=[
            pl.BlockSpec((EB, D_MODEL), lambda i, be, nu: (jnp.minimum(i, nu[0] - 1), 0)),
            pl.BlockSpec((None, None, D_MODEL, EXPERT_FF), lambda i, be, nu: (layer, be[i], 0, 0)),
            pl.BlockSpec((None, None, D_MODEL, EXPERT_FF), lambda i, be, nu: (layer, be[i], 0, 0)),
            pl.BlockSpec((None, None, EXPERT_FF, D_MODEL), lambda i, be, nu: (layer, be[i], 0, 0)),
        ],
        out_specs=pl.BlockSpec((EB, D_MODEL), lambda i, be, nu: (i, 0)),
        scratch_shapes=[
            pltpu.VMEM((D_MODEL, EXPERT_FF), BF16),
            pltpu.VMEM((D_MODEL, EXPERT_FF), BF16),
            pltpu.VMEM((EXPERT_FF, D_MODEL), BF16),
        ],
    )
    return pl.pallas_call(
        _experts_kernel,
        grid_spec=gs,
        out_shape=jax.ShapeDtypeStruct((n_rows, D_MODEL), BF16),
        compiler_params=_cparams(("arbitrary",)),
        name="moe_experts",
    )(blk_e, n_used, xb, wg, wu, wd)


def _combine_kernel(final, tbl_ref, cnt_ref, start_ref, rel_ref, seg0_ref, yb_hbm, base_ref, gtok_ref, ptok_ref,
                    mod_ref, gf_ref, out_ref, sort_s, sems):
    del tbl_ref
    i = pl.program_id(0)
    slot = lax.rem(i, 2)
    runs = _Runs(cnt_ref, start_ref, rel_ref, seg0_ref)

    def fetch(tile, sl):
        sort_s[sl] = jnp.zeros(sort_s.shape[1:], sort_s.dtype)
        runs.for_each_chunk(tile, lambda s, d: _chunk_copy(yb_hbm, d, sort_s.at[sl], s, sems.at[sl]).start())

    @pl.when(i == 0)
    def _():
        fetch(i, slot)

    @pl.when(i + 1 < pl.num_programs(0))
    def _():
        fetch(i + 1, 1 - slot)

    def drain(q, c):
        _chunk_copy(yb_hbm, 0, sort_s.at[slot], 0, sems.at[slot]).wait()
        return c

    lax.fori_loop(0, runs.n_chunks(i), drain, 0)

    gt = gtok_ref[...]
    pt = ptok_ref[...]
    col = lax.broadcasted_iota(I32, (TM, SORT_ROWS), 1).astype(F32)
    w = jnp.zeros((TM, SORT_ROWS), F32)
    for k in range(TOP_K):
        w = jnp.where(col == pt[:, k:k + 1], gt[:, k:k + 1], w)
    wh = w.astype(BF16)
    wl = (w - wh.astype(F32)).astype(BF16)
    ys = sort_s[slot]
    routed = _dot(wh, ys) + _dot(wl, ys)
    gate2 = mod_ref[...][:, 5 * D_MODEL:6 * D_MODEL]
    y = base_ref[...] + gate2 * routed
    if final:
        y = y * lax.rsqrt(jnp.mean(y * y, axis=-1, keepdims=True) + NORM_EPS) * gf_ref[...]
    out_ref[...] = y


def _combine_call(final, layer, tbl, cnt, start, rel, seg0, yb, base, gtok, ptok, mod4, gfinal):
    n_tok = base.shape[0]
    nt = n_tok // TM
    tile = lambda i, t: (i, 0)
    smem = pl.BlockSpec(memory_space=pltpu.SMEM)
    gs = pltpu.PrefetchScalarGridSpec(
        num_scalar_prefetch=1,
        grid=(nt,),
        in_specs=[
            smem, smem, smem, smem,
            pl.BlockSpec(memory_space=pl.ANY),
            pl.BlockSpec((TM, D_MODEL), tile),
            pl.BlockSpec((TM, LANES), tile),
            pl.BlockSpec((TM, LANES), tile),
            _mod_spec(layer),
            pl.BlockSpec((1, D_MODEL), lambda i, t: (0, 0)),
        ],
        out_specs=pl.BlockSpec((TM, D_MODEL), tile),
        scratch_shapes=[
            pltpu.VMEM((2, SORT_ROWS, D_MODEL), BF16),
            pltpu.SemaphoreType.DMA((2,)),
        ],
    )
    return pl.pallas_call(
        functools.partial(_combine_kernel, final),
        grid_spec=gs,
        out_shape=jax.ShapeDtypeStruct((n_tok, D_MODEL), F32),
        compiler_params=_cparams(("arbitrary",)),
        name="moe_combine",
    )(tbl, cnt, start, rel, seg0, yb, base, gtok, ptok, mod4, gfinal)


def _pad_cols(w, groups, width, slot):
    lead = w.shape[:-1]
    w = w.reshape(*lead, groups, width)
    pad = [(0, 0)] * (len(lead) + 1) + [(0, slot - width)]
    return jnp.pad(w, pad).reshape(*lead, groups * slot)


def _prep_w_in(w):
    c = np.cumsum([0, 256, 256, 256, 256, 256, 256, 256, 256, 128, 32])
    rq, rk, rv, rg, dq, dk, dv, cq, ckv, kpe = [w[..., c[n]:c[n + 1]] for n in range(10)]
    kpe_slot = jnp.concatenate([kpe, jnp.zeros_like(kpe), kpe, jnp.zeros_like(kpe)], axis=-1)
    cols = [rq, rk * (RET_DIM ** -0.5), rv, rg, _pad_cols(dq, DIFF_HEADS, 2 * DIFF_QK_DIM, SLOT),
            _pad_cols(dk, DIFF_HEADS, 2 * DIFF_QK_DIM, SLOT), dv, cq, ckv, kpe_slot]
    return jnp.concatenate(cols, axis=-1).astype(BF16)


def _rope_tables(n_pos, dim, lane_offsets):
    n_rows = n_pos // GRID_W
    row = jnp.repeat(jnp.arange(n_rows, dtype=F32), GRID_W)
    col = jnp.tile(jnp.arange(GRID_W, dtype=F32), n_rows)
    half = dim // 2
    freqs = ROPE_THETA ** (-jnp.arange(0, half, 2, dtype=F32) / half)
    ar = row[:, None] * freqs[None, :]
    ac = col[:, None] * freqs[None, :]
    ang = jnp.concatenate([ar, ar, ac, ac], axis=-1)
    cos, sin = jnp.cos(ang), jnp.sin(ang)
    first = (np.arange(dim) % 16) < 8
    sa = jnp.where(first[None, :], -sin, 0.0)
    sb = jnp.where(first[None, :], 0.0, sin)
    c_t = jnp.ones((n_pos, SLOT), F32)
    a_t = jnp.zeros((n_pos, SLOT), F32)
    b_t = jnp.zeros((n_pos, SLOT), F32)
    for off in lane_offsets:
        c_t = c_t.at[:, off:off + dim].set(cos)
        a_t = a_t.at[:, off:off + dim].set(sa)
        b_t = b_t.at[:, off:off + dim].set(sb)
    ident = (jnp.ones((TM, SLOT), F32), jnp.zeros((TM, SLOT), F32), jnp.zeros((TM, SLOT), F32))
    return tuple(jnp.concatenate([i0, t], axis=0) for i0, t in zip(ident, (c_t, a_t, b_t)))


def _block_diag_states(s):
    b, l, h, dk, dv = s.shape
    eye = jnp.eye(h, dtype=s.dtype)
    return jnp.einsum('blhkv,hg->blhkgv', s, eye).reshape(b, l, h * dk, h * dv)


def kernel(x_prompt, x_sample, cache_diff_k, cache_diff_v, cache_mla_ckv, cache_mla_kpe, state_ret_fwd, state_ret_bwd, c, c_ctx, w_ada, b_ada, norm_mix, norm_ffn, norm_final, w_in, ret_decay_fwd, ret_decay_bwd, diff_lambda, diff_subln, mla_q_norm, mla_w_uq, mla_kv_norm, mla_w_ukv, w_out, router_w, router_bias, exp_w_gate, exp_w_up, exp_w_down, sh_w_gate, sh_w_up, sh_w_down):
    n_pb, p_len, _ = x_prompt.shape
    n_sb, s_len, _ = x_sample.shape
    past_len = cache_diff_k.shape[3]
    n_p = n_pb * p_len
    n_s = n_sb * s_len
    n_tok = n_p + n_s
    nt = n_tok // TM
    npt = n_p // TM
    assert p_len == TM and s_len % TM == 0 and n_p % s_len == 0 and past_len % 8 == 0

    tiles = np.arange(nt)
    mod_row = np.where(tiles < npt, n_sb, (tiles - npt) // (s_len // TM))
    rope_blk = np.where(tiles < npt, 0, 1 + (tiles - npt) % (s_len // TM))
    tbl = jnp.asarray(np.stack([mod_row, rope_blk, np.full(nt, npt)]).astype(np.int32))

    n_cond = 16
    cond = jnp.zeros((n_cond, D_MODEL), F32).at[:n_sb].set(c).at[n_sb].set(c_ctx)
    mod_all = _modulation(cond, w_ada, b_ada)

    rope_d = _rope_tables(s_len, DIFF_QK_DIM, (0, DIFF_QK_DIM))
    rope_m = _rope_tables(s_len, MLA_ROPE_DIM, (KR_LO,))
    place = jnp.zeros((MLA_ROPE_DIM, SLOT), F32).at[np.arange(MLA_ROPE_DIM), KR_LO + np.arange(MLA_ROPE_DIM)].set(1.0).astype(BF16)
    cache_dv_t = cache_diff_v.transpose(0, 1, 3, 2, 4).reshape(n_sb, DEPTH, past_len, DIFF_V_W)
    s0f_bd = _block_diag_states(state_ret_fwd)
    s0b_bd = _block_diag_states(state_ret_bwd)

    n_blocks = pl.cdiv(n_tok * TOP_K + nt * N_EXPERTS * (CH - 1) + N_EXPERTS * (EB - CH), EB)
    n_rows = n_blocks * EB

    x = jnp.concatenate([x_prompt.reshape(n_p, D_MODEL), x_sample.reshape(n_s, D_MODEL)], axis=0)
    gfinal = norm_final.reshape(1, D_MODEL)

    mod4 = mod_all.reshape(DEPTH, n_cond, 1, 6 * D_MODEL)
    w_pre = _prep_w_in(w_in)
    wuq = _pad_cols(mla_w_uq, MLA_HEADS, MLA_NOPE_DIM + MLA_ROPE_DIM, SLOT).astype(BF16)
    ukv = mla_w_ukv.reshape(DEPTH, MLA_KV_LORA, MLA_HEADS, MLA_NOPE_DIM + MLA_V_DIM)
    wk = _pad_cols(ukv[..., :MLA_NOPE_DIM].reshape(DEPTH, MLA_KV_LORA, -1), MLA_HEADS, MLA_NOPE_DIM, SLOT).astype(BF16)
    wv = ukv[..., MLA_NOPE_DIM:].reshape(DEPTH, MLA_KV_LORA, MLA_V_W).astype(BF16)
    decf = jnp.repeat(ret_decay_fwd, RET_DIM, axis=-1).reshape(DEPTH, 1, RET_W)
    decb = jnp.repeat(ret_decay_bwd, RET_DIM, axis=-1).reshape(DEPTH, 1, RET_W)
    subln = jnp.tile(diff_subln, (1, DIFF_HEADS)).reshape(DEPTH, 1, DIFF_V_W)
    vec = lambda p: p.reshape(DEPTH, 1, -1)
    wout_b, shg_b, shu_b, shd_b = (w.astype(BF16) for w in (w_out, sh_w_gate, sh_w_up, sh_w_down))
    rwt = router_w.transpose(0, 2, 1)
    rb = router_bias.reshape(DEPTH, N_EXPERTS, 1)

    caches = []
    for l in range(DEPTH):
        lam_init = 0.8 - 0.6 * math.exp(-0.3 * l)
        ra, qa, ka = _pre_call(l, tbl, x, mod4, vec(norm_mix), w_pre, vec(mla_q_norm), wuq, vec(mla_kv_norm),
                               rope_d, rope_m)
        mix_p, sf, sb = _mix_prompt_call(lam_init, l, qa, ka, ra, n_pb, p_len, decf, decb,
                                         diff_lambda, subln, wk, wv)
        mix_s = _mix_sample_call(lam_init, l, qa, ka, ra, n_p, n_sb, s_len, past_len, cache_diff_k,
                                 cache_dv_t, cache_mla_ckv, cache_mla_kpe, s0f_bd, s0b_bd, decf, decb,
                                 diff_lambda, subln, wk, wv, place)
        base, h2, lpos, ptok, gtok, cnt, start, rel = _post_call(
            l, tbl, x, mix_p, mix_s, mod4, wout_b, vec(norm_ffn), shg_b, shu_b, shd_b, rwt, rb)
        seg0, blk_e, n_used, seg = _plan_call(cnt, rel, nt, n_blocks)
        xb = _dispatch_call(cnt, start, rel, seg0, seg, h2, lpos, n_rows)
        yb = _experts_call(blk_e, n_used, xb, l, exp_w_gate, exp_w_up, exp_w_down)
        x = _combine_call(l == DEPTH - 1, l, tbl, cnt, start, rel, seg0, yb, base, gtok, ptok, mod4, gfinal)

        kp = ka[:n_p].reshape(n_pb, p_len, KA_W)
        dk = kp[:, :, KA_DK:KA_DV].reshape(n_pb, p_len, DIFF_HEADS, SLOT)[..., :2 * DIFF_QK_DIM]
        dv = kp[:, :, KA_DV:KA_CKV].reshape(n_pb, p_len, DIFF_HEADS, DIFF_V_DIM)
        diag = lambda s: jnp.stack([s[:, h * RET_DIM:(h + 1) * RET_DIM, h * RET_DIM:(h + 1) * RET_DIM]
                                    for h in range(RET_HEADS)], axis=1)
        caches.append((dk.transpose(0, 2, 1, 3), dv.transpose(0, 2, 1, 3), kp[:, :, KA_CKV:KA_KPE],
                       kp[:, :, KA_KPE:KA_KPE + MLA_ROPE_DIM], diag(sf), diag(sb)))

    y_prompt = x[:n_p].reshape(n_pb, p_len, D_MODEL)
    y_sample = x[n_p:].reshape(n_sb, s_len, D_MODEL)
    new = [jnp.stack([cs[n] for cs in caches], axis=1) for n in range(6)]
    return (y_prompt, y_sample, *new)
```

```python
import functools
import math

import numpy as np
import jax
import jax.numpy as jnp
from jax import lax
from jax.experimental import pallas as pl
from jax.experimental.pallas import tpu as pltpu

F32 = jnp.float32
BF16 = jnp.bfloat16
I32 = jnp.int32

D_MODEL = 1024
DEPTH = 2
GRID_W = 64
ROPE_THETA = 10000.0
NORM_EPS = 1e-6

RET_HEADS = 4
RET_DIM = 64
RET_CHUNK = 128
RET_W = RET_HEADS * RET_DIM
DIFF_HEADS = 4
DIFF_QK_DIM = 32
DIFF_V_DIM = 64
DIFF_V_W = DIFF_HEADS * DIFF_V_DIM
MLA_HEADS = 8
MLA_Q_LORA = 256
MLA_KV_LORA = 128
MLA_NOPE_DIM = 64
MLA_ROPE_DIM = 32
MLA_V_DIM = 64
MLA_V_W = MLA_HEADS * MLA_V_DIM
MIX_W = RET_W + DIFF_V_W + MLA_V_W

N_EXPERTS = 64
TOP_K = 6
N_GROUPS = 8
GROUP_SIZE = N_EXPERTS // N_GROUPS
TOPK_GROUPS = 4
EXPERT_FF = 256
ROUTED_SCALE = 2.5

LANES = 128
SLOT = LANES
TM = 256
SLOT_ROWS = 8
EB = 512
LOG_EB = 9
CH = 16
LOG_CH = 4
SORT_ROWS = 2560
VMEM_LIMIT = 48 * 1024 * 1024

C_RQ, C_RK, C_RV, C_RG = 0, 256, 512, 768
C_DQ = 1024
C_DK = 1536
C_DV = 2048
C_CQ = 2304
C_CKV = 2560
C_KPE = 2688
N_PRE = 2816
QA_W = 4 * SLOT + MLA_HEADS * SLOT
KA_DK, KA_DV, KA_CKV, KA_KPE = 0, 512, 768, 896
KA_W = 1024
KR_LO, KR_HI = 64, 96


def _dot(a, b):
    return jnp.dot(a, b, preferred_element_type=F32)


def _dot_nt(a, b):
    return lax.dot_general(a, b, (((1,), (1,)), ((), ())), preferred_element_type=F32)


def _dot_tn(a, b):
    return lax.dot_general(a, b, (((0,), (0,)), ((), ())), preferred_element_type=F32)


def _split_dot(x, w_bf16):
    hi = x.astype(BF16)
    lo = (x - hi.astype(F32)).astype(BF16)
    return _dot(hi, w_bf16) + _dot(lo, w_bf16)


def _split_dot_nt(w, x):
    wh = w.astype(BF16)
    wl = (w - wh.astype(F32)).astype(BF16)
    xh = x.astype(BF16)
    xl = (x - xh.astype(F32)).astype(BF16)
    return _dot_nt(wh, xh) + _dot_nt(wh, xl) + _dot_nt(wl, xh)


def _silu(x):
    return x * jax.nn.sigmoid(x)


def _cparams(sem):
    return pltpu.CompilerParams(dimension_semantics=sem, vmem_limit_bytes=VMEM_LIMIT)


MOD_TN = 512


def _mod_kernel(c_ref, w_ref, b_ref, o_ref):
    s = _silu(c_ref[...])
    o_ref[...] = _split_dot3(s, w_ref[...]) + b_ref[...]


def _split_dot3(x, w):
    xh = x.astype(BF16)
    xl = (x - xh.astype(F32)).astype(BF16)
    wh = w.astype(BF16)
    wl = (w - wh.astype(F32)).astype(BF16)
    return _dot(xh, wh) + _dot(xh, wl) + _dot(xl, wh)


def _modulation(cond, w_ada, b_ada):
    n_rows = cond.shape[0]
    n_out = w_ada.shape[-1]
    return pl.pallas_call(
        _mod_kernel,
        grid=(DEPTH, n_out // MOD_TN),
        in_specs=[
            pl.BlockSpec((n_rows, D_MODEL), lambda l, j: (0, 0)),
            pl.BlockSpec((None, D_MODEL, MOD_TN), lambda l, j: (l, 0, j)),
            pl.BlockSpec((None, 1, MOD_TN), lambda l, j: (l, 0, j)),
        ],
        out_specs=pl.BlockSpec((None, n_rows, MOD_TN), lambda l, j: (l, 0, j)),
        out_shape=jax.ShapeDtypeStruct((DEPTH, n_rows, n_out), F32),
        compiler_params=_cparams(("arbitrary", "arbitrary")),
        name="adaln_mod",
    )(cond, w_ada, b_ada.reshape(DEPTH, 1, n_out))


def _rope_slot(x, cos, sa, sb):
    up = pltpu.roll(x, LANES - 8, 1)
    dn = pltpu.roll(x, 8, 1)
    return x * cos + up * sa + dn * sb


def _pre_kernel(tbl_ref, x_ref, mod_ref, g_ref, w_ref, qg_ref, wuq_ref, kvg_ref,
                cd_ref, sad_ref, sbd_ref, cm_ref, sam_ref, sbm_ref,
                ra_ref, qa_ref, ka_ref):
    del tbl_ref
    x = x_ref[...]
    mod = mod_ref[...]
    shift1 = mod[:, 0:D_MODEL]
    scale1 = mod[:, D_MODEL:2 * D_MODEL]
    ms = jnp.mean(x * x, axis=-1, keepdims=True)
    h = x * lax.rsqrt(ms + NORM_EPS) * g_ref[...]
    h = h * (1.0 + scale1) + shift1
    hb = h.astype(BF16)

    def proj(lo, hi):
        return _dot(hb, w_ref[:, lo:hi])

    ra_ref[:, 0:C_RG] = proj(C_RQ, C_RG)
    ra_ref[:, C_RG:C_DQ] = _silu(proj(C_RG, C_DQ))

    cd, sad, sbd = cd_ref[...], sad_ref[...], sbd_ref[...]
    cm, sam, sbm = cm_ref[...], sam_ref[...], sbm_ref[...]
    dq = proj(C_DQ, C_DK)
    dk = proj(C_DK, C_DV)
    for hd in range(DIFF_HEADS):
        sl = slice(hd * SLOT, (hd + 1) * SLOT)
        qa_ref[:, sl] = _rope_slot(dq[:, sl], cd, sad, sbd)
        ka_ref[:, KA_DK + hd * SLOT:KA_DK + (hd + 1) * SLOT] = _rope_slot(dk[:, sl], cd, sad, sbd)
    ka_ref[:, KA_DV:KA_CKV] = proj(C_DV, C_CQ)

    cq = proj(C_CQ, C_CKV)
    cqn = cq * lax.rsqrt(jnp.mean(cq * cq, axis=-1, keepdims=True) + NORM_EPS) * qg_ref[...]
    qm = _dot(cqn.astype(BF16), wuq_ref[...])
    for hd in range(MLA_HEADS):
        sl = slice(hd * SLOT, (hd + 1) * SLOT)
        qa_ref[:, 4 * SLOT + hd * SLOT:4 * SLOT + (hd + 1) * SLOT] = _rope_slot(qm[:, sl], cm, sam, sbm)

    ckv = proj(C_CKV, C_KPE)
    ka_ref[:, KA_CKV:KA_KPE] = ckv * lax.rsqrt(jnp.mean(ckv * ckv, axis=-1, keepdims=True) + NORM_EPS) * kvg_ref[...]
    ka_ref[:, KA_KPE:KA_W] = _rope_slot(proj(C_KPE, N_PRE), cm, sam, sbm)


def _layer_spec(layer, rows, cols):
    return pl.BlockSpec((None, rows, cols), lambda *_: (layer, 0, 0))


def _mod_spec(layer):
    return pl.BlockSpec((None, None, 1, 6 * D_MODEL), lambda i, t: (layer, t[0, i], 0, 0))


def _pre_call(layer, tbl, x, mod4, g, w_pre, qg, wuq, kvg, rope_d, rope_m):
    n_tok = x.shape[0]
    nt = n_tok // TM
    tile = lambda i, t: (i, 0)
    rope = lambda i, t: (t[1, i], 0)
    gs = pltpu.PrefetchScalarGridSpec(
        num_scalar_prefetch=1,
        grid=(nt,),
        in_specs=[
            pl.BlockSpec((TM, D_MODEL), tile),
            _mod_spec(layer),
            _layer_spec(layer, 1, D_MODEL),
            _layer_spec(layer, D_MODEL, N_PRE),
            _layer_spec(layer, 1, MLA_Q_LORA),
            _layer_spec(layer, MLA_Q_LORA, MLA_HEADS * SLOT),
            _layer_spec(layer, 1, MLA_KV_LORA),
        ] + [pl.BlockSpec((TM, SLOT), rope)] * 6,
        out_specs=[
            pl.BlockSpec((TM, D_MODEL), tile),
            pl.BlockSpec((TM, QA_W), tile),
            pl.BlockSpec((TM, KA_W), tile),
        ],
    )
    return pl.pallas_call(
        _pre_kernel,
        grid_spec=gs,
        out_shape=[
            jax.ShapeDtypeStruct((n_tok, D_MODEL), F32),
            jax.ShapeDtypeStruct((n_tok, QA_W), F32),
            jax.ShapeDtypeStruct((n_tok, KA_W), F32),
        ],
        compiler_params=_cparams(("arbitrary",)),
        name="pre_proj",
    )(tbl, x, mod4, g, w_pre, qg, wuq, kvg, *rope_d, *rope_m)


def _lane_iota(shape):
    return lax.broadcasted_iota(I32, shape, len(shape) - 1)


def _head_mask(n_rows, width, head, head_w):
    lane = _lane_iota((n_rows, width))
    return (lane >= head * head_w) & (lane < (head + 1) * head_w)


def _seg_mean_sq(o, bd_ones):
    return _split_dot(o * o, bd_ones) * (1.0 / RET_DIM)


def _block_diag_ones(n, blk):
    r = lax.broadcasted_iota(I32, (n, n), 0) // blk
    c = lax.broadcasted_iota(I32, (n, n), 1) // blk
    return r == c


def _retention(ra_ref, seq_len, decf_ref, decb_ref, s0f, s0b):
    C = RET_CHUNK
    nc = seq_len // C
    lgf = -jnp.exp(decf_ref[...])
    lgb = -jnp.exp(decb_ref[...])
    pos = lax.broadcasted_iota(I32, (C, RET_W), 0).astype(F32)
    qdf = jnp.exp((pos + 1.0) * lgf)
    kdf = jnp.exp((C - 1.0 - pos) * lgf)
    cdf = jnp.exp(float(C) * lgf)
    qdb = jnp.exp((C - pos) * lgb)
    kdb = jnp.exp(pos * lgb)
    cdb = jnp.exp(float(C) * lgb)
    ii = lax.broadcasted_iota(I32, (C, C), 0).astype(F32)
    jj = lax.broadcasted_iota(I32, (C, C), 1).astype(F32)
    dist = ii - jj
    dmats = []
    for hd in range(RET_HEADS):
        lf = lgf[:, hd * RET_DIM:hd * RET_DIM + 1]
        lb = lgb[:, hd * RET_DIM:hd * RET_DIM + 1]
        dmats.append(jnp.where(dist >= 0, jnp.exp(dist * lf), jnp.exp(-dist * lb)))
    bd = _block_diag_ones(RET_W, RET_DIM)
    bd_ones = jnp.where(bd, 1.0, 0.0).astype(BF16)

    def chunk(n):
        rows = slice(n * C, (n + 1) * C)
        return (ra_ref[rows, C_RQ:C_RK], ra_ref[rows, C_RK:C_RV], ra_ref[rows, C_RV:C_RG])

    cross = [None] * nc
    sf = s0f
    for n in range(nc):
        q, k, v = chunk(n)
        cross[n] = _dot((q * qdf).astype(BF16), sf.astype(BF16))
        kv = _dot_tn((k * kdf).astype(BF16), v.astype(BF16))
        sf = sf * cdf + jnp.where(bd, kv, 0.0)
    sb = s0b
    for n in range(nc - 1, -1, -1):
        q, k, v = chunk(n)
        cross[n] = cross[n] + _dot((q * qdb).astype(BF16), sb.astype(BF16))
        kv = _dot_tn((k * kdb).astype(BF16), v.astype(BF16))
        sb = sb * cdb + jnp.where(bd, kv, 0.0)

    outs = []
    for n in range(nc):
        q, k, v = chunk(n)
        kb = k.astype(BF16)
        vb = v.astype(BF16)
        o = cross[n]
        for hd in range(RET_HEADS):
            hm = _head_mask(C, RET_W, hd, RET_DIM)
            sc = _dot_nt(jnp.where(hm, q, 0.0).astype(BF16), kb) * dmats[hd]
            o = o + jnp.where(hm, _dot(sc.astype(BF16), vb), 0.0)
        on = o * lax.rsqrt(_seg_mean_sq(o, bd_ones) + NORM_EPS)
        outs.append(on * ra_ref[n * C:(n + 1) * C, C_RG:C_DQ])
    return outs, sf, sb


def _softmax_pv(s_parts, v_parts, scale):
    m = None
    for s in s_parts:
        mm = jnp.max(s, axis=-1, keepdims=True)
        m = mm if m is None else jnp.maximum(m, mm)
    m = m * scale
    acc = None
    den = None
    for s, v in zip(s_parts, v_parts):
        e = jnp.exp(s * scale - m)
        ds = jnp.sum(e, axis=-1, keepdims=True)
        pv = _dot(e.astype(BF16), v)
        acc = pv if acc is None else acc + pv
        den = ds if den is None else den + ds
    return acc / den


def _diff_attention(dq, k_parts, v_parts, lam, subln, lam_init, bd_ones):
    lq = dq.shape[0]
    scale = DIFF_QK_DIM ** -0.5
    lane = _lane_iota((lq, SLOT))
    out = jnp.zeros((lq, DIFF_V_W), F32)
    for hd in range(DIFF_HEADS):
        qh = dq[:, hd * SLOT:(hd + 1) * SLOT]
        q1 = jnp.where(lane < DIFF_QK_DIM, qh, 0.0).astype(BF16)
        q2 = jnp.where(lane >= DIFF_QK_DIM, qh, 0.0).astype(BF16)
        s1 = [_dot_nt(q1[:, :kp[hd].shape[1]], kp[hd]) for kp in k_parts]
        s2 = [_dot_nt(q2[:, :kp[hd].shape[1]], kp[hd]) for kp in k_parts]
        o = _softmax_pv(s1, v_parts, scale) - lam * _softmax_pv(s2, v_parts, scale)
        out = jnp.where(_head_mask(lq, DIFF_V_W, hd, DIFF_V_DIM), o, out)
    on = out * lax.rsqrt(_seg_mean_sq(out, bd_ones) + NORM_EPS) * subln
    return on * (1.0 - lam_init)


def _mla_attention(qm, k_parts, v_parts):
    lq = qm.shape[0]
    scale = (MLA_NOPE_DIM + MLA_ROPE_DIM) ** -0.5
    halves = []
    for g in range(2):
        out = jnp.zeros((lq, 256), F32)
        for hh in range(4):
            hd = 4 * g + hh
            qh = qm[:, hd * SLOT:(hd + 1) * SLOT].astype(BF16)
            s = [_dot_nt(qh, kp[:, hd * SLOT:(hd + 1) * SLOT]) for kp in k_parts]
            o = _softmax_pv(s, [vp[:, 256 * g:256 * (g + 1)] for vp in v_parts], scale)
            out = jnp.where(_head_mask(lq, 256, hh, MLA_V_DIM), o, out)
        halves.append(out)
    return halves


def _mla_keys(ka_val_ckv, kr_slot, wk_ref, wv_ref):
    cb = ka_val_ckv.astype(BF16)
    kn = _dot(cb, wk_ref[...])
    ks = [(kn[:, hd * SLOT:(hd + 1) * SLOT] + kr_slot).astype(BF16) for hd in range(MLA_HEADS)]
    return jnp.concatenate(ks, axis=1), _dot(cb, wv_ref[...]).astype(BF16)


def _kr_only(kpe_slot):
    lane = _lane_iota(kpe_slot.shape)
    return jnp.where((lane >= KR_LO) & (lane < KR_HI), kpe_slot, 0.0)


def _diff_lambda(dl_ref, lam_init):
    dl = dl_ref[...]
    a = jnp.sum(dl[0:1] * dl[1:2], axis=-1, keepdims=True)
    b = jnp.sum(dl[2:3] * dl[3:4], axis=-1, keepdims=True)
    return jnp.exp(a) - jnp.exp(b) + lam_init


def _mix_prompt_kernel(lam_init, qa_ref, ka_ref, ra_ref, decf_ref, decb_ref, dl_ref, subln_ref,
                       wk_ref, wv_ref, mix_ref, sf_ref, sb_ref):
    seq = qa_ref.shape[0]
    zero_state = jnp.zeros((RET_W, RET_W), F32)
    outs, sf, sb = _retention(ra_ref, seq, decf_ref, decb_ref, zero_state, zero_state)
    for n, o in enumerate(outs):
        mix_ref[n * RET_CHUNK:(n + 1) * RET_CHUNK, 0:RET_W] = o
    sf_ref[...] = sf
    sb_ref[...] = sb

    bd_ones = jnp.where(_block_diag_ones(DIFF_V_W, DIFF_V_DIM), 1.0, 0.0).astype(BF16)
    lam = _diff_lambda(dl_ref, lam_init)
    kd = [ka_ref[:, KA_DK + hd * SLOT:KA_DK + (hd + 1) * SLOT].astype(BF16) for hd in range(DIFF_HEADS)]
    vd = ka_ref[:, KA_DV:KA_CKV].astype(BF16)
    mix_ref[:, RET_W:RET_W + DIFF_V_W] = _diff_attention(
        qa_ref[:, 0:4 * SLOT], [kd], [vd], lam, subln_ref[...], lam_init, bd_ones)

    km, vm = _mla_keys(ka_ref[:, KA_CKV:KA_KPE], _kr_only(ka_ref[:, KA_KPE:KA_W]), wk_ref, wv_ref)
    halves = _mla_attention(qa_ref[:, 4 * SLOT:QA_W], [km], [vm])
    mix_ref[:, 512:768] = halves[0]
    mix_ref[:, 768:1024] = halves[1]


def _mixer_param_specs(layer):
    return [
        _layer_spec(layer, 1, RET_W),
        _layer_spec(layer, 1, RET_W),
        _layer_spec(layer, 4, DIFF_QK_DIM),
        _layer_spec(layer, 1, DIFF_V_W),
        _layer_spec(layer, MLA_KV_LORA, MLA_HEADS * SLOT),
        _layer_spec(layer, MLA_KV_LORA, MLA_V_W),
    ]


def _mix_prompt_call(lam_init, layer, qa, ka, ra, n_seq, seq_len, decf, decb, dl, subln, wk, wv):
    seq = lambda b: (b, 0)
    return pl.pallas_call(
        functools.partial(_mix_prompt_kernel, lam_init),
        grid=(n_seq,),
        in_specs=[
            pl.BlockSpec((seq_len, QA_W), seq),
            pl.BlockSpec((seq_len, KA_W), seq),
            pl.BlockSpec((seq_len, D_MODEL), seq),
        ] + _mixer_param_specs(layer),
        out_specs=[
            pl.BlockSpec((seq_len, MIX_W), seq),
            pl.BlockSpec((None, RET_W, RET_W), lambda b: (b, 0, 0)),
            pl.BlockSpec((None, RET_W, RET_W), lambda b: (b, 0, 0)),
        ],
        out_shape=[
            jax.ShapeDtypeStruct((n_seq * seq_len, MIX_W), F32),
            jax.ShapeDtypeStruct((n_seq, RET_W, RET_W), F32),
            jax.ShapeDtypeStruct((n_seq, RET_W, RET_W), F32),
        ],
        compiler_params=_cparams(("arbitrary",)),
        name="mix_prompt",
    )(qa, ka, ra, decf, decb, dl, subln, wk, wv)


def _mix_sample_kernel(lam_init, qa_ref, ka_ref, ra_ref, ckd_ref, cvd_ref, cckv_ref, ckpe_ref,
                       s0f_ref, s0b_ref, decf_ref, decb_ref, dl_ref, subln_ref, wk_ref, wv_ref,
                       place_ref, mix_ref,
                       ret_s, kdn_s, vdn_s, kdc_s, vdc_s, kmn_s, vmn_s, kmc_s, vmc_s):
    j = pl.program_id(1)
    seq = ka_ref.shape[0]

    @pl.when(j == 0)
    def _():
        outs, _, _ = _retention(ra_ref, seq, decf_ref, decb_ref, s0f_ref[...], s0b_ref[...])
        for n, o in enumerate(outs):
            ret_s[n * RET_CHUNK:(n + 1) * RET_CHUNK, :] = o
        kdn_s[...] = ka_ref[:, KA_DK:KA_DV].astype(BF16)
        vdn_s[...] = ka_ref[:, KA_DV:KA_CKV].astype(BF16)
        kdc_s[...] = ckd_ref[...].astype(BF16)
        vdc_s[...] = cvd_ref[...].astype(BF16)
        km, vm = _mla_keys(ka_ref[:, KA_CKV:KA_KPE], _kr_only(ka_ref[:, KA_KPE:KA_W]), wk_ref, wv_ref)
        kmn_s[...] = km
        vmn_s[...] = vm
        kr_ctx = _dot(ckpe_ref[...].astype(BF16), place_ref[...])
        km, vm = _mla_keys(cckv_ref[...], kr_ctx, wk_ref, wv_ref)
        kmc_s[...] = km
        vmc_s[...] = vm

    row0 = pl.multiple_of(j * TM, TM)
    mix_ref[:, 0:RET_W] = ret_s[pl.ds(row0, TM), :]

    bd_ones = jnp.where(_block_diag_ones(DIFF_V_W, DIFF_V_DIM), 1.0, 0.0).astype(BF16)
    lam = _diff_lambda(dl_ref, lam_init)
    kd_ctx = [kdc_s[hd] for hd in range(DIFF_HEADS)]
    kd_new = [kdn_s[:, hd * SLOT:(hd + 1) * SLOT] for hd in range(DIFF_HEADS)]
    mix_ref[:, RET_W:RET_W + DIFF_V_W] = _diff_attention(
        qa_ref[:, 0:4 * SLOT], [kd_ctx, kd_new], [vdc_s[...], vdn_s[...]], lam, subln_ref[...],
        lam_init, bd_ones)

    halves = _mla_attention(qa_ref[:, 4 * SLOT:QA_W], [kmc_s[...], kmn_s[...]], [vmc_s[...], vmn_s[...]])
    mix_ref[:, 512:768] = halves[0]
    mix_ref[:, 768:1024] = halves[1]


def _mix_sample_call(lam_init, layer, qa, ka, ra, tok0, n_seq, seq_len, past_len, cache_dk, cache_dv_t,
                     cache_ckv, cache_kpe, s0f_bd, s0b_bd, decf, decb, dl, subln, wk, wv, place):
    nq = seq_len // TM
    q0 = tok0 // TM
    s0 = tok0 // seq_len
    const = lambda b, j: (0, 0)
    return pl.pallas_call(
        functools.partial(_mix_sample_kernel, lam_init),
        grid=(n_seq, nq),
        in_specs=[
            pl.BlockSpec((TM, QA_W), lambda b, j: (q0 + b * nq + j, 0)),
            pl.BlockSpec((seq_len, KA_W), lambda b, j: (s0 + b, 0)),
            pl.BlockSpec((seq_len, D_MODEL), lambda b, j: (s0 + b, 0)),
            pl.BlockSpec((None, None, DIFF_HEADS, past_len, 2 * DIFF_QK_DIM), lambda b, j: (b, layer, 0, 0, 0)),
            pl.BlockSpec((None, None, past_len, DIFF_V_W), lambda b, j: (b, layer, 0, 0)),
            pl.BlockSpec((None, None, past_len, MLA_KV_LORA), lambda b, j: (b, layer, 0, 0)),
            pl.BlockSpec((None, None, past_len, MLA_ROPE_DIM), lambda b, j: (b, layer, 0, 0)),
            pl.BlockSpec((None, None, RET_W, RET_W), lambda b, j: (b, layer, 0, 0)),
            pl.BlockSpec((None, None, RET_W, RET_W), lambda b, j: (b, layer, 0, 0)),
        ] + _mixer_param_specs(layer) + [
            pl.BlockSpec((MLA_ROPE_DIM, SLOT), const),
        ],
        out_specs=pl.BlockSpec((TM, MIX_W), lambda b, j: (b * nq + j, 0)),
        out_shape=jax.ShapeDtypeStruct((n_seq * seq_len, MIX_W), F32),
        scratch_shapes=[
            pltpu.VMEM((seq_len, RET_W), F32),
            pltpu.VMEM((seq_len, 4 * SLOT), BF16),
            pltpu.VMEM((seq_len, DIFF_V_W), BF16),
            pltpu.VMEM((DIFF_HEADS, past_len, 2 * DIFF_QK_DIM), BF16),
            pltpu.VMEM((past_len, DIFF_V_W), BF16),
            pltpu.VMEM((seq_len, MLA_HEADS * SLOT), BF16),
            pltpu.VMEM((seq_len, MLA_V_W), BF16),
            pltpu.VMEM((past_len, MLA_HEADS * SLOT), BF16),
            pltpu.VMEM((past_len, MLA_V_W), BF16),
        ],
        compiler_params=_cparams(("arbitrary", "arbitrary")),
        name="mix_sample",
    )(qa, ka, ra, cache_dk, cache_dv_t, cache_ckv, cache_kpe, s0f_bd, s0b_bd,
      decf, decb, dl, subln, wk, wv, place)


def _route(h2, rwt_ref, rb_ref):
    tm = h2.shape[0]
    neg = -jnp.inf
    logits = _split_dot_nt(rwt_ref[...], h2)
    sc = jax.nn.sigmoid(logits)
    sel = sc + rb_ref[...]
    member = lax.broadcasted_iota(I32, (GROUP_SIZE, tm), 0).astype(F32)
    gscore = []
    for g in range(N_GROUPS):
        sg = sel[g * GROUP_SIZE:(g + 1) * GROUP_SIZE, :]
        m1 = jnp.max(sg, axis=0, keepdims=True)
        f1 = jnp.min(jnp.where(sg == m1, member, float(GROUP_SIZE)), axis=0, keepdims=True)
        m2 = jnp.max(jnp.where(member == f1, neg, sg), axis=0, keepdims=True)
        gscore.append(m1 + m2)
    gsel = [jnp.zeros((1, tm), F32) for _ in range(N_GROUPS)]
    for _ in range(TOPK_GROUPS):
        mx = gscore[0]
        for g in range(1, N_GROUPS):
            mx = jnp.maximum(mx, gscore[g])
        fi = jnp.full((1, tm), float(N_GROUPS), F32)
        for g in range(N_GROUPS - 1, -1, -1):
            fi = jnp.where(gscore[g] == mx, float(g), fi)
        for g in range(N_GROUPS):
            hit = fi == float(g)
            gsel[g] = jnp.where(hit, 1.0, gsel[g])
            gscore[g] = jnp.where(hit, neg, gscore[g])
    cand = jnp.concatenate(
        [jnp.where(gsel[g] > 0.0, sel[g * GROUP_SIZE:(g + 1) * GROUP_SIZE, :], neg) for g in range(N_GROUPS)],
        axis=0)
    flat = lax.broadcasted_iota(I32, (N_EXPERTS, tm), 0).astype(F32)
    hits, gts = [], []
    chosen = jnp.zeros((N_EXPERTS, tm), F32)
    for _ in range(TOP_K):
        mx = jnp.max(cand, axis=0, keepdims=True)
        fk = jnp.min(jnp.where(cand == mx, flat, float(N_EXPERTS)), axis=0, keepdims=True)
        hit = flat == fk
        hits.append(hit)
        gts.append(jnp.sum(jnp.where(hit, sc, 0.0), axis=0, keepdims=True))
        chosen = jnp.where(hit, 1.0, chosen)
        cand = jnp.where(hit, neg, cand)
    gsum = gts[0]
    for g in gts[1:]:
        gsum = gsum + g
    gts = [g / gsum * ROUTED_SCALE for g in gts]

    before = (lax.broadcasted_iota(I32, (tm, tm), 0) < lax.broadcasted_iota(I32, (tm, tm), 1))
    rank_in = _dot(chosen.astype(BF16), jnp.where(before, 1.0, 0.0).astype(BF16))
    cnt = jnp.sum(chosen, axis=1, keepdims=True)
    cnt_pad = jnp.floor((cnt + (CH - 1.0)) * (1.0 / CH)) * CH
    below = (lax.broadcasted_iota(I32, (N_EXPERTS, N_EXPERTS), 1) < lax.broadcasted_iota(I32, (N_EXPERTS, N_EXPERTS), 0))
    start = _dot(jnp.where(below, 1.0, 0.0).astype(BF16),
                 jnp.broadcast_to(cnt_pad, (N_EXPERTS, LANES)).astype(BF16))[:, 0:1]
    pos = rank_in + start
    lpos = [jnp.sum(jnp.where(hit, pos, 0.0), axis=0, keepdims=True) for hit in hits]
    return lpos, gts, cnt_pad, start


def _slot_rows(vals, n_rows):
    tm = vals[0].shape[1]
    row = lax.broadcasted_iota(I32, (n_rows, tm), 0)
    out = jnp.zeros((n_rows, tm), F32)
    for k, v in enumerate(vals):
        out = jnp.where(row == k, v, out)
    return out


def _post_kernel(tbl_ref, x_ref, mp_ref, ms_ref, mod_ref, wout_ref, g2_ref, shg_ref, shu_ref, shd_ref,
                 rwt_ref, rb_ref, base_ref, h2_ref, lpos_ref, ptok_ref, gtok_ref, cnt_ref, start_ref, rel_ref,
                 run_ref):
    i = pl.program_id(0)
    n_prompt_tiles = tbl_ref[2, 0]

    @pl.when(i == 0)
    def _():
        cnt_ref[...] = jnp.zeros_like(cnt_ref)
        start_ref[...] = jnp.zeros_like(start_ref)
        rel_ref[...] = jnp.zeros_like(rel_ref)
        run_ref[...] = jnp.zeros_like(run_ref)

    mod = mod_ref[...]
    gate1 = mod[:, 2 * D_MODEL:3 * D_MODEL]
    shift2 = mod[:, 3 * D_MODEL:4 * D_MODEL]
    scale2 = mod[:, 4 * D_MODEL:5 * D_MODEL]
    gate2 = mod[:, 5 * D_MODEL:6 * D_MODEL]
    mix = jnp.where(i < n_prompt_tiles, mp_ref[...], ms_ref[...])
    x1 = x_ref[...] + gate1 * _dot(mix.astype(BF16), wout_ref[...])
    ms = jnp.mean(x1 * x1, axis=-1, keepdims=True)
    h2 = x1 * lax.rsqrt(ms + NORM_EPS) * g2_ref[...]
    h2 = h2 * (1.0 + scale2) + shift2
    hb = h2.astype(BF16)
    h2_ref[...] = hb
    act = _silu(_dot(hb, shg_ref[...])) * _dot(hb, shu_ref[...])
    base_ref[...] = x1 + gate2 * _dot(act.astype(BF16), shd_ref[...])

    lpos, gts, cnt_pad, start = _route(h2, rwt_ref, rb_ref)
    lpos_ref[...] = _slot_rows(lpos, SLOT_ROWS)
    ptok_ref[...] = _slot_rows(lpos, LANES).T
    gtok_ref[...] = _slot_rows(gts, LANES).T
    col = lax.broadcasted_iota(I32, cnt_ref.shape, 1)
    run = run_ref[...]
    cnt_ref[...] = jnp.where(col == i, cnt_pad.astype(I32), cnt_ref[...])
    start_ref[...] = jnp.where(col == i, start.astype(I32), start_ref[...])
    rel_ref[...] = jnp.where(col == i, run.astype(I32), rel_ref[...])
    run_ref[...] = run + cnt_pad


def _post_call(layer, tbl, x, mix_p, mix_s, mod4, wout, g2, shg, shu, shd, rwt, rb):
    n_tok = x.shape[0]
    nt = n_tok // TM
    npt = mix_p.shape[0] // TM
    const = lambda i, t: (0, 0)
    tile = lambda i, t: (i, 0)
    gs = pltpu.PrefetchScalarGridSpec(
        num_scalar_prefetch=1,
        grid=(nt,),
        in_specs=[
            pl.BlockSpec((TM, D_MODEL), tile),
            pl.BlockSpec((TM, MIX_W), lambda i, t: (jnp.minimum(i, npt - 1), 0)),
            pl.BlockSpec((TM, MIX_W), lambda i, t: (jnp.maximum(i - npt, 0), 0)),
            _mod_spec(layer),
            _layer_spec(layer, MIX_W, D_MODEL),
            _layer_spec(layer, 1, D_MODEL),
            _layer_spec(layer, D_MODEL, EXPERT_FF),
            _layer_spec(layer, D_MODEL, EXPERT_FF),
            _layer_spec(layer, EXPERT_FF, D_MODEL),
            _layer_spec(layer, N_EXPERTS, D_MODEL),
            _layer_spec(layer, N_EXPERTS, 1),
        ],
        out_specs=[
            pl.BlockSpec((TM, D_MODEL), tile),
            pl.BlockSpec((TM, D_MODEL), tile),
            pl.BlockSpec((SLOT_ROWS, TM), lambda i, t: (0, i)),
            pl.BlockSpec((TM, LANES), tile),
            pl.BlockSpec((TM, LANES), tile),
            pl.BlockSpec((N_EXPERTS, LANES), const),
            pl.BlockSpec((N_EXPERTS, LANES), const),
            pl.BlockSpec((N_EXPERTS, LANES), const),
        ],
        scratch_shapes=[pltpu.VMEM((N_EXPERTS, 1), F32)],
    )
    assert nt <= LANES
    return pl.pallas_call(
        _post_kernel,
        grid_spec=gs,
        out_shape=[
            jax.ShapeDtypeStruct((n_tok, D_MODEL), F32),
            jax.ShapeDtypeStruct((n_tok, D_MODEL), BF16),
            jax.ShapeDtypeStruct((SLOT_ROWS, n_tok), F32),
            jax.ShapeDtypeStruct((n_tok, LANES), F32),
            jax.ShapeDtypeStruct((n_tok, LANES), F32),
            jax.ShapeDtypeStruct((N_EXPERTS, LANES), I32),
            jax.ShapeDtypeStruct((N_EXPERTS, LANES), I32),
            jax.ShapeDtypeStruct((N_EXPERTS, LANES), I32),
        ],
        compiler_params=_cparams(("arbitrary",)),
        name="post_route",
    )(tbl, x, mix_p, mix_s, mod4, wout, g2, shg, shu, shd, rwt, rb)


SEG_ROW0, SEG_NBLK, SEG_PAD0, SEG_NPAD, SEG_USED = range(5)


def _plan_kernel(last_tile, cnt_ref, rel_ref, seg_ref):
    def per_expert(e, start):
        end = start + rel_ref[e, last_tile] + cnt_ref[e, last_tile]
        nb = lax.shift_right_logical(end - start + (EB - 1), LOG_EB)
        nxt = start + lax.shift_left(nb, LOG_EB)
        seg_ref[SEG_ROW0, e] = start
        seg_ref[SEG_NBLK, e] = nb
        seg_ref[SEG_PAD0, e] = end
        seg_ref[SEG_NPAD, e] = lax.shift_right_logical(nxt - end, LOG_CH)
        seg_ref[SEG_USED, e] = 0
        return nxt

    total = lax.fori_loop(0, N_EXPERTS, per_expert, jnp.int32(0))
    seg_ref[SEG_USED, 0] = lax.shift_right_logical(total, LOG_EB)


def _plan_call(cnt, rel, nt):
    smem = pl.BlockSpec(memory_space=pltpu.SMEM)
    return pl.pallas_call(
        functools.partial(_plan_kernel, nt - 1),
        in_specs=[smem, smem],
        out_specs=smem,
        out_shape=jax.ShapeDtypeStruct((5, N_EXPERTS), I32),
        name="moe_plan",
    )(cnt, rel)


def _rows_copy(src_ref, src_row, dst_ref, dst_row, n_rows, sem):
    return pltpu.make_async_copy(src_ref.at[pl.ds(pl.multiple_of(src_row, CH), n_rows)],
                                 dst_ref.at[pl.ds(pl.multiple_of(dst_row, CH), n_rows)], sem)


class _Runs:
    def __init__(self, cnt_ref, start_ref, rel_ref, seg_ref):
        self.cnt, self.start, self.rel, self.seg = cnt_ref, start_ref, rel_ref, seg_ref

    def start_copies(self, i, copy, tot_ref, slot):
        def per_pair(e2, carry):
            n_big, n_small = carry
            for par in range(2):
                e = 2 * e2 + par
                c = self.cnt[e, i]
                a0 = self.start[e, i]
                b0 = self.seg[SEG_ROW0, e] + self.rel[e, i]
                nb = lax.shift_right_logical(c, LOG_CH + 1)
                odd = jnp.bitwise_and(lax.shift_right_logical(c, LOG_CH), 1)

                def big(q, cc, a0=a0, b0=b0, par=par):
                    copy(a0 + q * (2 * CH), b0 + q * (2 * CH), 2 * CH).start(priority=par)
                    return cc

                lax.fori_loop(0, nb, big, 0)

                @pl.when(odd == 1)
                def _(a0=a0, b0=b0, nb=nb, par=par):
                    copy(a0 + nb * (2 * CH), b0 + nb * (2 * CH), CH).start(priority=par)

                n_big, n_small = n_big + nb, n_small + odd
            return n_big, n_small

        n_big, n_small = lax.fori_loop(0, N_EXPERTS // 2, per_pair, (jnp.int32(0), jnp.int32(0)))
        tot_ref[slot, 0] = n_big
        tot_ref[slot, 1] = n_small

    @staticmethod
    def wait_copies(copy, tot_ref, slot):
        def big(q, c):
            copy(0, 0, 2 * CH).wait()
            return c

        lax.fori_loop(0, tot_ref[slot, 0], big, 0)

        def small(q, c):
            copy(0, 0, CH).wait()
            return c

        lax.fori_loop(0, tot_ref[slot, 1], small, 0)


def _dispatch_kernel(cnt_ref, start_ref, rel_ref, seg_ref, h_ref, lpos_ref, xb_hbm, sort_s, zero_s, tot_s, sems):
    i = pl.program_id(0)
    last = pl.num_programs(0) - 1
    slot = lax.rem(i, 2)
    runs = _Runs(cnt_ref, start_ref, rel_ref, seg_ref)
    lp = lpos_ref[...]
    hb = h_ref[...]
    blk = TM
    for r in range(SORT_ROWS // blk):
        srow = (lax.broadcasted_iota(I32, (blk, TM), 0) + r * blk).astype(F32)
        p = jnp.zeros((blk, TM), F32)
        for k in range(TOP_K):
            p = jnp.where(srow == lp[k:k + 1, :], 1.0, p)
        sort_s[slot, r * blk:(r + 1) * blk, :] = _dot(p.astype(BF16), hb).astype(BF16)

    def copy_from(sl):
        return lambda s, d, n: _rows_copy(sort_s.at[sl], s, xb_hbm, d, n, sems.at[sl])

    runs.start_copies(i, copy_from(slot), tot_s, slot)

    @pl.when(i > 0)
    def _():
        runs.wait_copies(copy_from(1 - slot), tot_s, 1 - slot)

    @pl.when(i == last)
    def _():
        runs.wait_copies(copy_from(slot), tot_s, slot)
        _zero_fill_unused(seg_ref, xb_hbm, zero_s, sems.at[0])


def _zero_fill_unused(seg_ref, buf_hbm, zero_s, sem):
    zero_s[...] = jnp.zeros_like(zero_s)

    def per_expert(e, c):
        first = seg_ref[SEG_PAD0, e]

        def z_issue(r, cc):
            _rows_copy(zero_s, 0, buf_hbm, first + r * CH, CH, sem).start()
            return cc

        lax.fori_loop(0, seg_ref[SEG_NPAD, e], z_issue, 0)

        def z_drain(r, cc):
            _rows_copy(zero_s, 0, buf_hbm, 0, CH, sem).wait()
            return cc

        lax.fori_loop(0, seg_ref[SEG_NPAD, e], z_drain, 0)
        return c

    lax.fori_loop(0, N_EXPERTS, per_expert, 0)
    _zero_fill_tail(seg_ref, buf_hbm, zero_s, sem)


def _zero_fill_tail(seg_ref, buf_hbm, zero_s, sem):
    n_blocks = buf_hbm.shape[0] // EB

    def blk_copy(b):
        return pltpu.make_async_copy(zero_s, buf_hbm.at[pl.ds(pl.multiple_of(b * EB, EB), EB)], sem)

    def t_issue(b, cc):
        blk_copy(b).start()
        return cc

    lax.fori_loop(seg_ref[SEG_USED, 0], n_blocks, t_issue, 0)

    def t_drain(b, cc):
        blk_copy(0).wait()
        return cc

    lax.fori_loop(seg_ref[SEG_USED, 0], n_blocks, t_drain, 0)


def _dispatch_call(cnt, start, rel, seg, h2, lpos, n_rows):
    n_tok = h2.shape[0]
    nt = n_tok // TM
    smem = pl.BlockSpec(memory_space=pltpu.SMEM)
    return pl.pallas_call(
        _dispatch_kernel,
        grid=(nt,),
        in_specs=[
            smem, smem, smem, smem,
            pl.BlockSpec((TM, D_MODEL), lambda i: (i, 0)),
            pl.BlockSpec((SLOT_ROWS, TM), lambda i: (0, i)),
        ],
        out_specs=pl.BlockSpec(memory_space=pl.ANY),
        out_shape=jax.ShapeDtypeStruct((n_rows, D_MODEL), BF16),
        scratch_shapes=[
            pltpu.VMEM((2, SORT_ROWS, D_MODEL), BF16),
            pltpu.VMEM((EB, D_MODEL), BF16),
            pltpu.SMEM((2, 2), I32),
            pltpu.SemaphoreType.DMA((2,)),
        ],
        compiler_params=_cparams(("arbitrary",)),
        name="moe_dispatch",
    )(cnt, start, rel, seg, h2, lpos)


def _experts_kernel(seg_ref, wg_ref, wu_ref, wd_ref, xb_hbm, yb_hbm, wg_s, wu_s, wd_s, x_s, y_s, sem_x, sem_y):
    e = pl.program_id(0)
    nb = seg_ref[SEG_NBLK, e]
    row0 = seg_ref[SEG_ROW0, e]

    def rows(b):
        return pl.ds(pl.multiple_of(row0 + b * EB, EB), EB)

    def x_copy(b, slot):
        return pltpu.make_async_copy(xb_hbm.at[rows(b)], x_s.at[slot], sem_x.at[slot])

    def y_copy(b, slot):
        return pltpu.make_async_copy(y_s.at[slot], yb_hbm.at[rows(b)], sem_y.at[slot])

    @pl.when(nb > 0)
    def _():
        x_copy(0, 0).start()
        wg_s[...] = wg_ref[...].astype(BF16)
        wu_s[...] = wu_ref[...].astype(BF16)
        wd_s[...] = wd_ref[...].astype(BF16)

        def block(b, c):
            slot = lax.rem(b, 2)

            @pl.when(b + 1 < nb)
            def _():
                x_copy(b + 1, 1 - slot).start()

            x_copy(b, slot).wait()

            @pl.when(b >= 2)
            def _():
                y_copy(b - 2, slot).wait()

            xb = x_s[slot]
            act = _silu(_dot(xb, wg_s[...])) * _dot(xb, wu_s[...])
            y_s[slot] = _dot(act.astype(BF16), wd_s[...]).astype(BF16)
            y_copy(b, slot).start()
            return c

        lax.fori_loop(0, nb, block, 0)

        @pl.when(nb >= 2)
        def _():
            y_copy(nb - 2, lax.rem(nb, 2)).wait()

        y_copy(nb - 1, lax.rem(nb - 1, 2)).wait()

    @pl.when(e == pl.num_programs(0) - 1)
    def _():
        y_s[0] = jnp.zeros(y_s.shape[1:], y_s.dtype)
        _zero_fill_tail(seg_ref, yb_hbm, y_s.at[0], sem_y.at[0])


def _experts_call(seg, xb, layer, wg, wu, wd):
    n_rows = xb.shape[0]
    return pl.pallas_call(
        _experts_kernel,
        grid=(N_EXPERTS,),
        in_specs=[
            pl.BlockSpec(memory_space=pltpu.SMEM),
            pl.BlockSpec((None, None, D_MODEL, EXPERT_FF), lambda e: (layer, e, 0, 0)),
            pl.BlockSpec((None, None, D_MODEL, EXPERT_FF), lambda e: (layer, e, 0, 0)),
            pl.BlockSpec((None, None, EXPERT_FF, D_MODEL), lambda e: (layer, e, 0, 0)),
            pl.BlockSpec(memory_space=pl.ANY),
        ],
        out_specs=pl.BlockSpec(memory_space=pl.ANY),
        out_shape=jax.ShapeDtypeStruct((n_rows, D_MODEL), BF16),
        scratch_shapes=[
            pltpu.VMEM((D_MODEL, EXPERT_FF), BF16),
            pltpu.VMEM((D_MODEL, EXPERT_FF), BF16),
            pltpu.VMEM((EXPERT_FF, D_MODEL), BF16),
            pltpu.VMEM((2, EB, D_MODEL), BF16),
            pltpu.VMEM((2, EB, D_MODEL), BF16),
            pltpu.SemaphoreType.DMA((2,)),
            pltpu.SemaphoreType.DMA((2,)),
        ],
        compiler_params=_cparams(("arbitrary",)),
        name="moe_experts",
    )(seg, wg, wu, wd, xb)


def _combine_kernel(final, tbl_ref, cnt_ref, start_ref, rel_ref, seg_ref, yb_hbm, base_ref, gtok_ref, ptok_ref,
                    mod_ref, gf_ref, out_ref, sort_s, tot_s, sems):
    del tbl_ref
    i = pl.program_id(0)
    slot = lax.rem(i, 2)
    runs = _Runs(cnt_ref, start_ref, rel_ref, seg_ref)

    def copy_to(sl):
        return lambda s, d, n: _rows_copy(yb_hbm, d, sort_s.at[sl], s, n, sems.at[sl])

    @pl.when(i == 0)
    def _():
        sort_s[...] = jnp.zeros_like(sort_s)
        runs.start_copies(i, copy_to(slot), tot_s, slot)

    @pl.when(i + 1 < pl.num_programs(0))
    def _():
        runs.start_copies(i + 1, copy_to(1 - slot), tot_s, 1 - slot)

    runs.wait_copies(copy_to(slot), tot_s, slot)

    gt = gtok_ref[...]
    pt = ptok_ref[...]
    col = lax.broadcasted_iota(I32, (TM, SORT_ROWS), 1).astype(F32)
    w = jnp.zeros((TM, SORT_ROWS), F32)
    for k in range(TOP_K):
        w = jnp.where(col == pt[:, k:k + 1], gt[:, k:k + 1], w)
    wh = w.astype(BF16)
    wl = (w - wh.astype(F32)).astype(BF16)
    ys = sort_s[slot]
    routed = _dot(wh, ys) + _dot(wl, ys)
    gate2 = mod_ref[...][:, 5 * D_MODEL:6 * D_MODEL]
    y = base_ref[...] + gate2 * routed
    if final:
        y = y * lax.rsqrt(jnp.mean(y * y, axis=-1, keepdims=True) + NORM_EPS) * gf_ref[...]
    out_ref[...] = y


def _combine_call(final, layer, tbl, cnt, start, rel, seg, yb, base, gtok, ptok, mod4, gfinal):
    n_tok = base.shape[0]
    nt = n_tok // TM
    tile = lambda i, t: (i, 0)
    smem = pl.BlockSpec(memory_space=pltpu.SMEM)
    gs = pltpu.PrefetchScalarGridSpec(
        num_scalar_prefetch=1,
        grid=(nt,),
        in_specs=[
            smem, smem, smem, smem,
            pl.BlockSpec(memory_space=pl.ANY),
            pl.BlockSpec((TM, D_MODEL), tile),
            pl.BlockSpec((TM, LANES), tile),
            pl.BlockSpec((TM, LANES), tile),
            _mod_spec(layer),
            pl.BlockSpec((1, D_MODEL), lambda i, t: (0, 0)),
        ],
        out_specs=pl.BlockSpec((TM, D_MODEL), tile),
        scratch_shapes=[
            pltpu.VMEM((2, SORT_ROWS, D_MODEL), BF16),
            pltpu.SMEM((2, 2), I32),
            pltpu.SemaphoreType.DMA((2,)),
        ],
    )
    return pl.pallas_call(
        functools.partial(_combine_kernel, final),
        grid_spec=gs,
        out_shape=jax.ShapeDtypeStruct((n_tok, D_MODEL), F32),
        compiler_params=_cparams(("arbitrary",)),
        name="moe_combine",
    )(tbl, cnt, start, rel, seg, yb, base, gtok, ptok, mod4, gfinal)


def _pad_cols(w, groups, width, slot):
    lead = w.shape[:-1]
    w = w.reshape(*lead, groups, width)
    pad = [(0, 0)] * (len(lead) + 1) + [(0, slot - width)]
    return jnp.pad(w, pad).reshape(*lead, groups * slot)


def _prep_w_in(w):
    c = np.cumsum([0, 256, 256, 256, 256, 256, 256, 256, 256, 128, 32])
    rq, rk, rv, rg, dq, dk, dv, cq, ckv, kpe = [w[..., c[n]:c[n + 1]] for n in range(10)]
    kpe_slot = jnp.concatenate([kpe, jnp.zeros_like(kpe), kpe, jnp.zeros_like(kpe)], axis=-1)
    cols = [rq, rk * (RET_DIM ** -0.5), rv, rg, _pad_cols(dq, DIFF_HEADS, 2 * DIFF_QK_DIM, SLOT),
            _pad_cols(dk, DIFF_HEADS, 2 * DIFF_QK_DIM, SLOT), dv, cq, ckv, kpe_slot]
    return jnp.concatenate(cols, axis=-1).astype(BF16)


def _rope_tables(n_pos, dim, lane_offsets):
    n_rows = n_pos // GRID_W
    row = jnp.repeat(jnp.arange(n_rows, dtype=F32), GRID_W)
    col = jnp.tile(jnp.arange(GRID_W, dtype=F32), n_rows)
    half = dim // 2
    freqs = ROPE_THETA ** (-jnp.arange(0, half, 2, dtype=F32) / half)
    ar = row[:, None] * freqs[None, :]
    ac = col[:, None] * freqs[None, :]
    ang = jnp.concatenate([ar, ar, ac, ac], axis=-1)
    cos, sin = jnp.cos(ang), jnp.sin(ang)
    first = (np.arange(dim) % 16) < 8
    sa = jnp.where(first[None, :], -sin, 0.0)
    sb = jnp.where(first[None, :], 0.0, sin)
    c_t = jnp.ones((n_pos, SLOT), F32)
    a_t = jnp.zeros((n_pos, SLOT), F32)
    b_t = jnp.zeros((n_pos, SLOT), F32)
    for off in lane_offsets:
        c_t = c_t.at[:, off:off + dim].set(cos)
        a_t = a_t.at[:, off:off + dim].set(sa)
        b_t = b_t.at[:, off:off + dim].set(sb)
    ident = (jnp.ones((TM, SLOT), F32), jnp.zeros((TM, SLOT), F32), jnp.zeros((TM, SLOT), F32))
    return tuple(jnp.concatenate([i0, t], axis=0) for i0, t in zip(ident, (c_t, a_t, b_t)))


def _block_diag_states(s):
    b, l, h, dk, dv = s.shape
    eye = jnp.eye(h, dtype=s.dtype)
    return jnp.einsum('blhkv,hg->blhkgv', s, eye).reshape(b, l, h * dk, h * dv)


def kernel(x_prompt, x_sample, cache_diff_k, cache_diff_v, cache_mla_ckv, cache_mla_kpe, state_ret_fwd, state_ret_bwd, c, c_ctx, w_ada, b_ada, norm_mix, norm_ffn, norm_final, w_in, ret_decay_fwd, ret_decay_bwd, diff_lambda, diff_subln, mla_q_norm, mla_w_uq, mla_kv_norm, mla_w_ukv, w_out, router_w, router_bias, exp_w_gate, exp_w_up, exp_w_down, sh_w_gate, sh_w_up, sh_w_down):
    n_pb, p_len, _ = x_prompt.shape
    n_sb, s_len, _ = x_sample.shape
    past_len = cache_diff_k.shape[3]
    n_p = n_pb * p_len
    n_s = n_sb * s_len
    n_tok = n_p + n_s
    nt = n_tok // TM
    npt = n_p // TM
    assert p_len == TM and s_len % TM == 0 and n_p % s_len == 0 and past_len % 8 == 0

    tiles = np.arange(nt)
    mod_row = np.where(tiles < npt, n_sb, (tiles - npt) // (s_len // TM))
    rope_blk = np.where(tiles < npt, 0, 1 + (tiles - npt) % (s_len // TM))
    tbl = jnp.asarray(np.stack([mod_row, rope_blk, np.full(nt, npt)]).astype(np.int32))

    n_cond = 16
    cond = jnp.zeros((n_cond, D_MODEL), F32).at[:n_sb].set(c).at[n_sb].set(c_ctx)
    mod_all = _modulation(cond, w_ada, b_ada)

    rope_d = _rope_tables(s_len, DIFF_QK_DIM, (0, DIFF_QK_DIM))
    rope_m = _rope_tables(s_len, MLA_ROPE_DIM, (KR_LO,))
    place = jnp.zeros((MLA_ROPE_DIM, SLOT), F32).at[np.arange(MLA_ROPE_DIM), KR_LO + np.arange(MLA_ROPE_DIM)].set(1.0).astype(BF16)
    cache_dv_t = cache_diff_v.transpose(0, 1, 3, 2, 4).reshape(n_sb, DEPTH, past_len, DIFF_V_W)
    s0f_bd = _block_diag_states(state_ret_fwd)
    s0b_bd = _block_diag_states(state_ret_bwd)

    n_blocks = pl.cdiv(n_tok * TOP_K + nt * N_EXPERTS * (CH - 1) + N_EXPERTS * (EB - CH), EB)
    n_rows = n_blocks * EB

    x = jnp.concatenate([x_prompt.reshape(n_p, D_MODEL), x_sample.reshape(n_s, D_MODEL)], axis=0)
    gfinal = norm_final.reshape(1, D_MODEL)

    mod4 = mod_all.reshape(DEPTH, n_cond, 1, 6 * D_MODEL)
    w_pre = _prep_w_in(w_in)
    wuq = _pad_cols(mla_w_uq, MLA_HEADS, MLA_NOPE_DIM + MLA_ROPE_DIM, SLOT).astype(BF16)
    ukv = mla_w_ukv.reshape(DEPTH, MLA_KV_LORA, MLA_HEADS, MLA_NOPE_DIM + MLA_V_DIM)
    wk = _pad_cols(ukv[..., :MLA_NOPE_DIM].reshape(DEPTH, MLA_KV_LORA, -1), MLA_HEADS, MLA_NOPE_DIM, SLOT).astype(BF16)
    wv = ukv[..., MLA_NOPE_DIM:].reshape(DEPTH, MLA_KV_LORA, MLA_V_W).astype(BF16)
    decf = jnp.repeat(ret_decay_fwd, RET_DIM, axis=-1).reshape(DEPTH, 1, RET_W)
    decb = jnp.repeat(ret_decay_bwd, RET_DIM, axis=-1).reshape(DEPTH, 1, RET_W)
    subln = jnp.tile(diff_subln, (1, DIFF_HEADS)).reshape(DEPTH, 1, DIFF_V_W)
    vec = lambda p: p.reshape(DEPTH, 1, -1)
    wout_b, shg_b, shu_b, shd_b = (w.astype(BF16) for w in (w_out, sh_w_gate, sh_w_up, sh_w_down))
    rwt = router_w.transpose(0, 2, 1)
    rb = router_bias.reshape(DEPTH, N_EXPERTS, 1)

    caches = []
    for l in range(DEPTH):
        lam_init = 0.8 - 0.6 * math.exp(-0.3 * l)
        ra, qa, ka = _pre_call(l, tbl, x, mod4, vec(norm_mix), w_pre, vec(mla_q_norm), wuq, vec(mla_kv_norm),
                               rope_d, rope_m)
        mix_p, sf, sb = _mix_prompt_call(lam_init, l, qa, ka, ra, n_pb, p_len, decf, decb,
                                         diff_lambda, subln, wk, wv)
        mix_s = _mix_sample_call(lam_init, l, qa, ka, ra, n_p, n_sb, s_len, past_len, cache_diff_k,
                                 cache_dv_t, cache_mla_ckv, cache_mla_kpe, s0f_bd, s0b_bd, decf, decb,
                                 diff_lambda, subln, wk, wv, place)
        base, h2, lpos, ptok, gtok, cnt, start, rel = _post_call(
            l, tbl, x, mix_p, mix_s, mod4, wout_b, vec(norm_ffn), shg_b, shu_b, shd_b, rwt, rb)
        seg = _plan_call(cnt, rel, nt)
        xb = _dispatch_call(cnt, start, rel, seg, h2, lpos, n_rows)
        yb = _experts_call(seg, xb, l, exp_w_gate, exp_w_up, exp_w_down)
        x = _combine_call(l == DEPTH - 1, l, tbl, cnt, start, rel, seg, yb, base, gtok, ptok, mod4, gfinal)

        kp = ka[:n_p].reshape(n_pb, p_len, KA_W)
        dk = kp[:, :, KA_DK:KA_DV].reshape(n_pb, p_len, DIFF_HEADS, SLOT)[..., :2 * DIFF_QK_DIM]
        dv = kp[:, :, KA_DV:KA_CKV].reshape(n_pb, p_len, DIFF_HEADS, DIFF_V_DIM)
        diag = lambda s: jnp.stack([s[:, h * RET_DIM:(h + 1) * RET_DIM, h * RET_DIM:(h + 1) * RET_DIM]
                                    for h in range(RET_HEADS)], axis=1)
        caches.append((dk.transpose(0, 2, 1, 3), dv.transpose(0, 2, 1, 3), kp[:, :, KA_CKV:KA_KPE],
                       kp[:, :, KA_KPE:KA_KPE + MLA_ROPE_DIM], diag(sf), diag(sb)))

    y_prompt = x[:n_p].reshape(n_pb, p_len, D_MODEL)
    y_sample = x[n_p:].reshape(n_sb, s_len, D_MODEL)
    new = [jnp.stack([cs[n] for cs in caches], axis=1) for n in range(6)]
    return (y_prompt, y_sample, *new)
```

```python
import functools
import math

import numpy as np
import jax
import jax.numpy as jnp
from jax import lax
from jax.experimental import pallas as pl
from jax.experimental.pallas import tpu as pltpu

F32 = jnp.float32
BF16 = jnp.bfloat16
I32 = jnp.int32

D_MODEL = 1024
DEPTH = 2
GRID_W = 64
ROPE_THETA = 10000.0
NORM_EPS = 1e-6

RET_HEADS = 4
RET_DIM = 64
RET_CHUNK = 128
RET_W = RET_HEADS * RET_DIM
DIFF_HEADS = 4
DIFF_QK_DIM = 32
DIFF_V_DIM = 64
DIFF_V_W = DIFF_HEADS * DIFF_V_DIM
MLA_HEADS = 8
MLA_Q_LORA = 256
MLA_KV_LORA = 128
MLA_NOPE_DIM = 64
MLA_ROPE_DIM = 32
MLA_V_DIM = 64
MLA_V_W = MLA_HEADS * MLA_V_DIM
MIX_W = RET_W + DIFF_V_W + MLA_V_W

N_EXPERTS = 64
TOP_K = 6
N_GROUPS = 8
GROUP_SIZE = N_EXPERTS // N_GROUPS
TOPK_GROUPS = 4
EXPERT_FF = 256
ROUTED_SCALE = 2.5

LANES = 128
SLOT = LANES
TM = 256
SLOT_ROWS = 8
EB = 512
LOG_EB = 9
CH = 16
LOG_CH = 4
SORT_ROWS = 2560
VMEM_LIMIT = 48 * 1024 * 1024

C_RQ, C_RK, C_RV, C_RG = 0, 256, 512, 768
C_DQ = 1024
C_DK = 1536
C_DV = 2048
C_CQ = 2304
C_CKV = 2560
C_KPE = 2688
N_PRE = 2816
QA_W = 4 * SLOT + MLA_HEADS * SLOT
KA_DK, KA_DV, KA_CKV, KA_KPE = 0, 512, 768, 896
KA_W = 1024
KR_LO, KR_HI = 64, 96


def _dot(a, b):
    return jnp.dot(a, b, preferred_element_type=F32)


def _dot_nt(a, b):
    return lax.dot_general(a, b, (((1,), (1,)), ((), ())), preferred_element_type=F32)


def _dot_tn(a, b):
    return lax.dot_general(a, b, (((0,), (0,)), ((), ())), preferred_element_type=F32)


def _split_dot(x, w_bf16):
    hi = x.astype(BF16)
    lo = (x - hi.astype(F32)).astype(BF16)
    return _dot(hi, w_bf16) + _dot(lo, w_bf16)


def _split_dot_nt(w, x):
    wh = w.astype(BF16)
    wl = (w - wh.astype(F32)).astype(BF16)
    xh = x.astype(BF16)
    xl = (x - xh.astype(F32)).astype(BF16)
    return _dot_nt(wh, xh) + _dot_nt(wh, xl) + _dot_nt(wl, xh)


def _silu(x):
    return x * jax.nn.sigmoid(x)


def _cparams(sem):
    return pltpu.CompilerParams(dimension_semantics=sem, vmem_limit_bytes=VMEM_LIMIT)


MOD_TN = 512


def _mod_kernel(c_ref, w_ref, b_ref, o_ref):
    s = _silu(c_ref[...])
    o_ref[...] = _split_dot3(s, w_ref[...]) + b_ref[...]


def _split_dot3(x, w):
    xh = x.astype(BF16)
    xl = (x - xh.astype(F32)).astype(BF16)
    wh = w.astype(BF16)
    wl = (w - wh.astype(F32)).astype(BF16)
    return _dot(xh, wh) + _dot(xh, wl) + _dot(xl, wh)


def _modulation(cond, w_ada, b_ada):
    n_rows = cond.shape[0]
    n_out = w_ada.shape[-1]
    return pl.pallas_call(
        _mod_kernel,
        grid=(DEPTH, n_out // MOD_TN),
        in_specs=[
            pl.BlockSpec((n_rows, D_MODEL), lambda l, j: (0, 0)),
            pl.BlockSpec((None, D_MODEL, MOD_TN), lambda l, j: (l, 0, j)),
            pl.BlockSpec((None, 1, MOD_TN), lambda l, j: (l, 0, j)),
        ],
        out_specs=pl.BlockSpec((None, n_rows, MOD_TN), lambda l, j: (l, 0, j)),
        out_shape=jax.ShapeDtypeStruct((DEPTH, n_rows, n_out), F32),
        compiler_params=_cparams(("arbitrary", "arbitrary")),
        name="adaln_mod",
    )(cond, w_ada, b_ada.reshape(DEPTH, 1, n_out))


def _rope_slot(x, cos, sa, sb):
    up = pltpu.roll(x, LANES - 8, 1)
    dn = pltpu.roll(x, 8, 1)
    return x * cos + up * sa + dn * sb


def _tile_x(tbl_ref, xa_ref, xb_ref):
    return jnp.where(pl.program_id(0) < tbl_ref[2, 0], xa_ref[...], xb_ref[...])


def _x_specs(npt, combined):
    off = 0 if combined else npt
    return [pl.BlockSpec((TM, D_MODEL), lambda i, t: (jnp.minimum(i, npt - 1), 0)),
            pl.BlockSpec((TM, D_MODEL), lambda i, t: (jnp.maximum(i, npt) - off, 0))]


def _pre_kernel(tbl_ref, xa_ref, xb_ref, mod_ref, g_ref, w_ref, qg_ref, wuq_ref, kvg_ref,
                cd_ref, sad_ref, sbd_ref, cm_ref, sam_ref, sbm_ref,
                ra_ref, qa_ref, ka_ref):
    x = _tile_x(tbl_ref, xa_ref, xb_ref)
    mod = mod_ref[...]
    shift1 = mod[:, 0:D_MODEL]
    scale1 = mod[:, D_MODEL:2 * D_MODEL]
    ms = jnp.mean(x * x, axis=-1, keepdims=True)
    h = x * lax.rsqrt(ms + NORM_EPS) * g_ref[...]
    h = h * (1.0 + scale1) + shift1
    hb = h.astype(BF16)

    def proj(lo, hi):
        return _dot(hb, w_ref[:, lo:hi])

    ra_ref[:, 0:C_RG] = proj(C_RQ, C_RG).astype(BF16)
    ra_ref[:, C_RG:C_DQ] = _silu(proj(C_RG, C_DQ)).astype(BF16)

    cd, sad, sbd = cd_ref[...], sad_ref[...], sbd_ref[...]
    cm, sam, sbm = cm_ref[...], sam_ref[...], sbm_ref[...]
    dq = proj(C_DQ, C_DK)
    dk = proj(C_DK, C_DV)
    for hd in range(DIFF_HEADS):
        sl = slice(hd * SLOT, (hd + 1) * SLOT)
        qa_ref[:, sl] = _rope_slot(dq[:, sl], cd, sad, sbd).astype(BF16)
        ka_ref[:, KA_DK + hd * SLOT:KA_DK + (hd + 1) * SLOT] = _rope_slot(dk[:, sl], cd, sad, sbd)
    ka_ref[:, KA_DV:KA_CKV] = proj(C_DV, C_CQ)

    cq = proj(C_CQ, C_CKV)
    cqn = cq * lax.rsqrt(jnp.mean(cq * cq, axis=-1, keepdims=True) + NORM_EPS) * qg_ref[...]
    qm = _dot(cqn.astype(BF16), wuq_ref[...])
    for hd in range(MLA_HEADS):
        sl = slice(hd * SLOT, (hd + 1) * SLOT)
        qa_ref[:, 4 * SLOT + hd * SLOT:4 * SLOT + (hd + 1) * SLOT] = _rope_slot(qm[:, sl], cm, sam, sbm).astype(BF16)

    ckv = proj(C_CKV, C_KPE)
    ka_ref[:, KA_CKV:KA_KPE] = ckv * lax.rsqrt(jnp.mean(ckv * ckv, axis=-1, keepdims=True) + NORM_EPS) * kvg_ref[...]
    ka_ref[:, KA_KPE:KA_W] = _rope_slot(proj(C_KPE, N_PRE), cm, sam, sbm)


def _layer_spec(layer, rows, cols):
    return pl.BlockSpec((None, rows, cols), lambda *_: (layer, 0, 0))


def _mod_spec(layer):
    return pl.BlockSpec((None, None, 1, 6 * D_MODEL), lambda i, t: (layer, t[0, i], 0, 0))


def _pre_call(layer, tbl, xa, xb, npt, n_tok, mod4, g, w_pre, qg, wuq, kvg, rope_d, rope_m):
    nt = n_tok // TM
    tile = lambda i, t: (i, 0)
    rope = lambda i, t: (t[1, i], 0)
    gs = pltpu.PrefetchScalarGridSpec(
        num_scalar_prefetch=1,
        grid=(nt,),
        in_specs=_x_specs(npt, xa is xb) + [
            _mod_spec(layer),
            _layer_spec(layer, 1, D_MODEL),
            _layer_spec(layer, D_MODEL, N_PRE),
            _layer_spec(layer, 1, MLA_Q_LORA),
            _layer_spec(layer, MLA_Q_LORA, MLA_HEADS * SLOT),
            _layer_spec(layer, 1, MLA_KV_LORA),
        ] + [pl.BlockSpec((TM, SLOT), rope)] * 6,
        out_specs=[
            pl.BlockSpec((TM, D_MODEL), tile),
            pl.BlockSpec((TM, QA_W), tile),
            pl.BlockSpec((TM, KA_W), tile),
        ],
    )
    return pl.pallas_call(
        _pre_kernel,
        grid_spec=gs,
        out_shape=[
            jax.ShapeDtypeStruct((n_tok, D_MODEL), BF16),
            jax.ShapeDtypeStruct((n_tok, QA_W), BF16),
            jax.ShapeDtypeStruct((n_tok, KA_W), F32),
        ],
        compiler_params=_cparams(("arbitrary",)),
        name="pre_proj",
    )(tbl, xa, xb, mod4, g, w_pre, qg, wuq, kvg, *rope_d, *rope_m)


def _lane_iota(shape):
    return lax.broadcasted_iota(I32, shape, len(shape) - 1)


def _head_mask(n_rows, width, head, head_w):
    lane = _lane_iota((n_rows, width))
    return (lane >= head * head_w) & (lane < (head + 1) * head_w)


def _seg_mean_sq(o, bd_ones):
    return _split_dot(o * o, bd_ones) * (1.0 / RET_DIM)


def _block_diag_ones(n, blk):
    r = lax.broadcasted_iota(I32, (n, n), 0) // blk
    c = lax.broadcasted_iota(I32, (n, n), 1) // blk
    return r == c


def _retention(ra_ref, seq_len, decf_ref, decb_ref, s0f, s0b):
    C = RET_CHUNK
    nc = seq_len // C
    lgf = -jnp.exp(decf_ref[...])
    lgb = -jnp.exp(decb_ref[...])
    pos = lax.broadcasted_iota(I32, (C, RET_W), 0).astype(F32)
    qdf = jnp.exp((pos + 1.0) * lgf)
    kdf = jnp.exp((C - 1.0 - pos) * lgf)
    cdf = jnp.exp(float(C) * lgf)
    qdb = jnp.exp((C - pos) * lgb)
    kdb = jnp.exp(pos * lgb)
    cdb = jnp.exp(float(C) * lgb)
    ii = lax.broadcasted_iota(I32, (C, C), 0).astype(F32)
    jj = lax.broadcasted_iota(I32, (C, C), 1).astype(F32)
    dist = ii - jj
    dmats = []
    for hd in range(RET_HEADS):
        lf = lgf[:, hd * RET_DIM:hd * RET_DIM + 1]
        lb = lgb[:, hd * RET_DIM:hd * RET_DIM + 1]
        dmats.append(jnp.where(dist >= 0, jnp.exp(dist * lf), jnp.exp(-dist * lb)))
    bd = _block_diag_ones(RET_W, RET_DIM)
    bd_ones = jnp.where(bd, 1.0, 0.0).astype(BF16)

    def chunk(n):
        rows = slice(n * C, (n + 1) * C)
        return (ra_ref[rows, C_RQ:C_RK], ra_ref[rows, C_RK:C_RV], ra_ref[rows, C_RV:C_RG])

    cross = [None] * nc
    sf = s0f
    for n in range(nc):
        q, k, v = chunk(n)
        cross[n] = _dot((q * qdf).astype(BF16), sf.astype(BF16))
        kv = _dot_tn((k * kdf).astype(BF16), v.astype(BF16))
        sf = sf * cdf + jnp.where(bd, kv, 0.0)
    sb = s0b
    for n in range(nc - 1, -1, -1):
        q, k, v = chunk(n)
        cross[n] = cross[n] + _dot((q * qdb).astype(BF16), sb.astype(BF16))
        kv = _dot_tn((k * kdb).astype(BF16), v.astype(BF16))
        sb = sb * cdb + jnp.where(bd, kv, 0.0)

    outs = []
    for n in range(nc):
        q, k, v = chunk(n)
        kb = k.astype(BF16)
        vb = v.astype(BF16)
        o = cross[n]
        for hd in range(RET_HEADS):
            hm = _head_mask(C, RET_W, hd, RET_DIM)
            sc = _dot_nt(jnp.where(hm, q, 0.0).astype(BF16), kb) * dmats[hd]
            o = o + jnp.where(hm, _dot(sc.astype(BF16), vb), 0.0)
        on = o * lax.rsqrt(_seg_mean_sq(o, bd_ones) + NORM_EPS)
        outs.append(on * ra_ref[n * C:(n + 1) * C, C_RG:C_DQ])
    return outs, sf, sb


def _softmax_pv(s_parts, v_parts, scale):
    m = None
    for s in s_parts:
        mm = jnp.max(s, axis=-1, keepdims=True)
        m = mm if m is None else jnp.maximum(m, mm)
    m = m * scale
    acc = None
    den = None
    for s, v in zip(s_parts, v_parts):
        e = jnp.exp(s * scale - m)
        ds = jnp.sum(e, axis=-1, keepdims=True)
        pv = _dot(e.astype(BF16), v)
        acc = pv if acc is None else acc + pv
        den = ds if den is None else den + ds
    return acc / den


def _diff_attention(dq, k_parts, v_parts, lam, subln, lam_init, bd_ones):
    lq = dq.shape[0]
    scale = DIFF_QK_DIM ** -0.5
    lane = _lane_iota((lq, SLOT))
    out = jnp.zeros((lq, DIFF_V_W), F32)
    for hd in range(DIFF_HEADS):
        qh = dq[:, hd * SLOT:(hd + 1) * SLOT]
        q1 = jnp.where(lane < DIFF_QK_DIM, qh, 0.0).astype(BF16)
        q2 = jnp.where(lane >= DIFF_QK_DIM, qh, 0.0).astype(BF16)
        s1 = [_dot_nt(q1[:, :kp[hd].shape[1]], kp[hd]) for kp in k_parts]
        s2 = [_dot_nt(q2[:, :kp[hd].shape[1]], kp[hd]) for kp in k_parts]
        o = _softmax_pv(s1, v_parts, scale) - lam * _softmax_pv(s2, v_parts, scale)
        out = jnp.where(_head_mask(lq, DIFF_V_W, hd, DIFF_V_DIM), o, out)
    on = out * lax.rsqrt(_seg_mean_sq(out, bd_ones) + NORM_EPS) * subln
    return on * (1.0 - lam_init)


def _mla_attention(qm, k_parts, v_parts):
    lq = qm.shape[0]
    scale = (MLA_NOPE_DIM + MLA_ROPE_DIM) ** -0.5
    halves = []
    for g in range(2):
        out = jnp.zeros((lq, 256), F32)
        for hh in range(4):
            hd = 4 * g + hh
            qh = qm[:, hd * SLOT:(hd + 1) * SLOT].astype(BF16)
            s = [_dot_nt(qh, kp[:, hd * SLOT:(hd + 1) * SLOT]) for kp in k_parts]
            o = _softmax_pv(s, [vp[:, 256 * g:256 * (g + 1)] for vp in v_parts], scale)
            out = jnp.where(_head_mask(lq, 256, hh, MLA_V_DIM), o, out)
        halves.append(out)
    return halves


def _mla_keys(ka_val_ckv, kr_slot, wk_ref, wv_ref):
    cb = ka_val_ckv.astype(BF16)
    kn = _dot(cb, wk_ref[...])
    ks = [(kn[:, hd * SLOT:(hd + 1) * SLOT] + kr_slot).astype(BF16) for hd in range(MLA_HEADS)]
    return jnp.concatenate(ks, axis=1), _dot(cb, wv_ref[...]).astype(BF16)


def _kr_only(kpe_slot):
    lane = _lane_iota(kpe_slot.shape)
    return jnp.where((lane >= KR_LO) & (lane < KR_HI), kpe_slot, 0.0)


def _diff_lambda(dl_ref, lam_init):
    dl = dl_ref[...]
    a = jnp.sum(dl[0:1] * dl[1:2], axis=-1, keepdims=True)
    b = jnp.sum(dl[2:3] * dl[3:4], axis=-1, keepdims=True)
    return jnp.exp(a) - jnp.exp(b) + lam_init


def _mix_prompt_kernel(lam_init, qa_ref, ka_ref, ra_ref, decf_ref, decb_ref, dl_ref, subln_ref,
                       wk_ref, wv_ref, mix_ref, sf_ref, sb_ref):
    seq = qa_ref.shape[0]
    zero_state = jnp.zeros((RET_W, RET_W), F32)
    outs, sf, sb = _retention(ra_ref, seq, decf_ref, decb_ref, zero_state, zero_state)
    for n, o in enumerate(outs):
        mix_ref[n * RET_CHUNK:(n + 1) * RET_CHUNK, 0:RET_W] = o.astype(BF16)
    sf_ref[...] = sf
    sb_ref[...] = sb

    bd_ones = jnp.where(_block_diag_ones(DIFF_V_W, DIFF_V_DIM), 1.0, 0.0).astype(BF16)
    lam = _diff_lambda(dl_ref, lam_init)
    kd = [ka_ref[:, KA_DK + hd * SLOT:KA_DK + (hd + 1) * SLOT].astype(BF16) for hd in range(DIFF_HEADS)]
    vd = ka_ref[:, KA_DV:KA_CKV].astype(BF16)
    mix_ref[:, RET_W:RET_W + DIFF_V_W] = _diff_attention(
        qa_ref[:, 0:4 * SLOT], [kd], [vd], lam, subln_ref[...], lam_init, bd_ones).astype(BF16)

    km, vm = _mla_keys(ka_ref[:, KA_CKV:KA_KPE], _kr_only(ka_ref[:, KA_KPE:KA_W]), wk_ref, wv_ref)
    halves = _mla_attention(qa_ref[:, 4 * SLOT:QA_W], [km], [vm])
    mix_ref[:, 512:768] = halves[0].astype(BF16)
    mix_ref[:, 768:1024] = halves[1].astype(BF16)


def _mixer_param_specs(layer):
    return [
        _layer_spec(layer, 1, RET_W),
        _layer_spec(layer, 1, RET_W),
        _layer_spec(layer, 4, DIFF_QK_DIM),
        _layer_spec(layer, 1, DIFF_V_W),
        _layer_spec(layer, MLA_KV_LORA, MLA_HEADS * SLOT),
        _layer_spec(layer, MLA_KV_LORA, MLA_V_W),
    ]


def _mix_prompt_call(lam_init, layer, qa, ka, ra, n_seq, seq_len, decf, decb, dl, subln, wk, wv):
    seq = lambda b: (b, 0)
    return pl.pallas_call(
        functools.partial(_mix_prompt_kernel, lam_init),
        grid=(n_seq,),
        in_specs=[
            pl.BlockSpec((seq_len, QA_W), seq),
            pl.BlockSpec((seq_len, KA_W), seq),
            pl.BlockSpec((seq_len, D_MODEL), seq),
        ] + _mixer_param_specs(layer),
        out_specs=[
            pl.BlockSpec((seq_len, MIX_W), seq),
            pl.BlockSpec((None, RET_W, RET_W), lambda b: (b, 0, 0)),
            pl.BlockSpec((None, RET_W, RET_W), lambda b: (b, 0, 0)),
        ],
        out_shape=[
            jax.ShapeDtypeStruct((n_seq * seq_len, MIX_W), BF16),
            jax.ShapeDtypeStruct((n_seq, RET_W, RET_W), F32),
            jax.ShapeDtypeStruct((n_seq, RET_W, RET_W), F32),
        ],
        compiler_params=_cparams(("arbitrary",)),
        name="mix_prompt",
    )(qa, ka, ra, decf, decb, dl, subln, wk, wv)


def _mix_sample_kernel(lam_init, qa_ref, ka_ref, ra_ref, ckd_ref, cvd_ref, cckv_ref, ckpe_ref,
                       s0f_ref, s0b_ref, decf_ref, decb_ref, dl_ref, subln_ref, wk_ref, wv_ref,
                       place_ref, mix_ref,
                       ret_s, kdn_s, vdn_s, kdc_s, vdc_s, kmn_s, vmn_s, kmc_s, vmc_s):
    j = pl.program_id(1)
    seq = ka_ref.shape[0]

    @pl.when(j == 0)
    def _():
        outs, _, _ = _retention(ra_ref, seq, decf_ref, decb_ref, s0f_ref[...], s0b_ref[...])
        for n, o in enumerate(outs):
            ret_s[n * RET_CHUNK:(n + 1) * RET_CHUNK, :] = o
        kdn_s[...] = ka_ref[:, KA_DK:KA_DV].astype(BF16)
        vdn_s[...] = ka_ref[:, KA_DV:KA_CKV].astype(BF16)
        kdc_s[...] = ckd_ref[...].astype(BF16)
        vdc_s[...] = cvd_ref[...].astype(BF16)
        km, vm = _mla_keys(ka_ref[:, KA_CKV:KA_KPE], _kr_only(ka_ref[:, KA_KPE:KA_W]), wk_ref, wv_ref)
        kmn_s[...] = km
        vmn_s[...] = vm
        kr_ctx = _dot(ckpe_ref[...].astype(BF16), place_ref[...])
        km, vm = _mla_keys(cckv_ref[...], kr_ctx, wk_ref, wv_ref)
        kmc_s[...] = km
        vmc_s[...] = vm

    row0 = pl.multiple_of(j * TM, TM)
    mix_ref[:, 0:RET_W] = ret_s[pl.ds(row0, TM), :].astype(BF16)

    bd_ones = jnp.where(_block_diag_ones(DIFF_V_W, DIFF_V_DIM), 1.0, 0.0).astype(BF16)
    lam = _diff_lambda(dl_ref, lam_init)
    kd_ctx = [kdc_s[hd] for hd in range(DIFF_HEADS)]
    kd_new = [kdn_s[:, hd * SLOT:(hd + 1) * SLOT] for hd in range(DIFF_HEADS)]
    mix_ref[:, RET_W:RET_W + DIFF_V_W] = _diff_attention(
        qa_ref[:, 0:4 * SLOT], [kd_ctx, kd_new], [vdc_s[...], vdn_s[...]], lam, subln_ref[...],
        lam_init, bd_ones).astype(BF16)

    halves = _mla_attention(qa_ref[:, 4 * SLOT:QA_W], [kmc_s[...], kmn_s[...]], [vmc_s[...], vmn_s[...]])
    mix_ref[:, 512:768] = halves[0].astype(BF16)
    mix_ref[:, 768:1024] = halves[1].astype(BF16)


def _mix_sample_call(lam_init, layer, qa, ka, ra, tok0, n_seq, seq_len, past_len, cache_dk, cache_dv_t,
                     cache_ckv, cache_kpe, s0f_bd, s0b_bd, decf, decb, dl, subln, wk, wv, place):
    nq = seq_len // TM
    q0 = tok0 // TM
    s0 = tok0 // seq_len
    const = lambda b, j: (0, 0)
    return pl.pallas_call(
        functools.partial(_mix_sample_kernel, lam_init),
        grid=(n_seq, nq),
        in_specs=[
            pl.BlockSpec((TM, QA_W), lambda b, j: (q0 + b * nq + j, 0)),
            pl.BlockSpec((seq_len, KA_W), lambda b, j: (s0 + b, 0)),
            pl.BlockSpec((seq_len, D_MODEL), lambda b, j: (s0 + b, 0)),
            pl.BlockSpec((None, None, DIFF_HEADS, past_len, 2 * DIFF_QK_DIM), lambda b, j: (b, layer, 0, 0, 0)),
            pl.BlockSpec((None, None, past_len, DIFF_V_W), lambda b, j: (b, layer, 0, 0)),
            pl.BlockSpec((None, None, past_len, MLA_KV_LORA), lambda b, j: (b, layer, 0, 0)),
            pl.BlockSpec((None, None, past_len, MLA_ROPE_DIM), lambda b, j: (b, layer, 0, 0)),
            pl.BlockSpec((None, None, RET_W, RET_W), lambda b, j: (b, layer, 0, 0)),
            pl.BlockSpec((None, None, RET_W, RET_W), lambda b, j: (b, layer, 0, 0)),
        ] + _mixer_param_specs(layer) + [
            pl.BlockSpec((MLA_ROPE_DIM, SLOT), const),
        ],
        out_specs=pl.BlockSpec((TM, MIX_W), lambda b, j: (b * nq + j, 0)),
        out_shape=jax.ShapeDtypeStruct((n_seq * seq_len, MIX_W), BF16),
        scratch_shapes=[
            pltpu.VMEM((seq_len, RET_W), F32),
            pltpu.VMEM((seq_len, 4 * SLOT), BF16),
            pltpu.VMEM((seq_len, DIFF_V_W), BF16),
            pltpu.VMEM((DIFF_HEADS, past_len, 2 * DIFF_QK_DIM), BF16),
            pltpu.VMEM((past_len, DIFF_V_W), BF16),
            pltpu.VMEM((seq_len, MLA_HEADS * SLOT), BF16),
            pltpu.VMEM((seq_len, MLA_V_W), BF16),
            pltpu.VMEM((past_len, MLA_HEADS * SLOT), BF16),
            pltpu.VMEM((past_len, MLA_V_W), BF16),
        ],
        compiler_params=_cparams(("arbitrary", "arbitrary")),
        name="mix_sample",
    )(qa, ka, ra, cache_dk, cache_dv_t, cache_ckv, cache_kpe, s0f_bd, s0b_bd,
      decf, decb, dl, subln, wk, wv, place)


def _route(h2, rwt_ref, rb_ref):
    tm = h2.shape[0]
    neg = -jnp.inf
    logits = _split_dot_nt(rwt_ref[...], h2)
    sc = jax.nn.sigmoid(logits)
    sel = sc + rb_ref[...]
    member = lax.broadcasted_iota(I32, (GROUP_SIZE, tm), 0).astype(F32)
    gscore = []
    for g in range(N_GROUPS):
        sg = sel[g * GROUP_SIZE:(g + 1) * GROUP_SIZE, :]
        m1 = jnp.max(sg, axis=0, keepdims=True)
        f1 = jnp.min(jnp.where(sg == m1, member, float(GROUP_SIZE)), axis=0, keepdims=True)
        m2 = jnp.max(jnp.where(member == f1, neg, sg), axis=0, keepdims=True)
        gscore.append(m1 + m2)
    gsel = [jnp.zeros((1, tm), F32) for _ in range(N_GROUPS)]
    for _ in range(TOPK_GROUPS):
        mx = gscore[0]
        for g in range(1, N_GROUPS):
            mx = jnp.maximum(mx, gscore[g])
        fi = jnp.full((1, tm), float(N_GROUPS), F32)
        for g in range(N_GROUPS - 1, -1, -1):
            fi = jnp.where(gscore[g] == mx, float(g), fi)
        for g in range(N_GROUPS):
            hit = fi == float(g)
            gsel[g] = jnp.where(hit, 1.0, gsel[g])
            gscore[g] = jnp.where(hit, neg, gscore[g])
    cand = jnp.concatenate(
        [jnp.where(gsel[g] > 0.0, sel[g * GROUP_SIZE:(g + 1) * GROUP_SIZE, :], neg) for g in range(N_GROUPS)],
        axis=0)
    flat = lax.broadcasted_iota(I32, (N_EXPERTS, tm), 0).astype(F32)
    hits, gts = [], []
    chosen = jnp.zeros((N_EXPERTS, tm), F32)
    for _ in range(TOP_K):
        mx = jnp.max(cand, axis=0, keepdims=True)
        fk = jnp.min(jnp.where(cand == mx, flat, float(N_EXPERTS)), axis=0, keepdims=True)
        hit = flat == fk
        hits.append(hit)
        gts.append(jnp.sum(jnp.where(hit, sc, 0.0), axis=0, keepdims=True))
        chosen = jnp.where(hit, 1.0, chosen)
        cand = jnp.where(hit, neg, cand)
    gsum = gts[0]
    for g in gts[1:]:
        gsum = gsum + g
    gts = [g / gsum * ROUTED_SCALE for g in gts]

    before = (lax.broadcasted_iota(I32, (tm, tm), 0) < lax.broadcasted_iota(I32, (tm, tm), 1))
    rank_in = _dot(chosen.astype(BF16), jnp.where(before, 1.0, 0.0).astype(BF16))
    cnt = jnp.sum(chosen, axis=1, keepdims=True)
    cnt_pad = jnp.floor((cnt + (CH - 1.0)) * (1.0 / CH)) * CH
    below = (lax.broadcasted_iota(I32, (N_EXPERTS, N_EXPERTS), 1) < lax.broadcasted_iota(I32, (N_EXPERTS, N_EXPERTS), 0))
    start = _dot(jnp.where(below, 1.0, 0.0).astype(BF16),
                 jnp.broadcast_to(cnt_pad, (N_EXPERTS, LANES)).astype(BF16))[:, 0:1]
    pos = rank_in + start
    lpos = [jnp.sum(jnp.where(hit, pos, 0.0), axis=0, keepdims=True) for hit in hits]
    return lpos, gts, cnt_pad, start


def _slot_rows(vals, n_rows):
    tm = vals[0].shape[1]
    row = lax.broadcasted_iota(I32, (n_rows, tm), 0)
    out = jnp.zeros((n_rows, tm), F32)
    for k, v in enumerate(vals):
        out = jnp.where(row == k, v, out)
    return out


def _post_kernel(tbl_ref, xa_ref, xb_ref, mp_ref, ms_ref, mod_ref, wout_ref, g2_ref, shg_ref, shu_ref, shd_ref,
                 rwt_ref, rb_ref, base_ref, h2_ref, lpos_ref, ptok_ref, gtok_ref, cnt_ref, start_ref, rel_ref,
                 run_ref):
    i = pl.program_id(0)
    n_prompt_tiles = tbl_ref[2, 0]

    @pl.when(i == 0)
    def _():
        cnt_ref[...] = jnp.zeros_like(cnt_ref)
        start_ref[...] = jnp.zeros_like(start_ref)
        rel_ref[...] = jnp.zeros_like(rel_ref)
        run_ref[...] = jnp.zeros_like(run_ref)

    mod = mod_ref[...]
    gate1 = mod[:, 2 * D_MODEL:3 * D_MODEL]
    shift2 = mod[:, 3 * D_MODEL:4 * D_MODEL]
    scale2 = mod[:, 4 * D_MODEL:5 * D_MODEL]
    gate2 = mod[:, 5 * D_MODEL:6 * D_MODEL]
    mix = jnp.where(i < n_prompt_tiles, mp_ref[...], ms_ref[...])
    x1 = _tile_x(tbl_ref, xa_ref, xb_ref) + gate1 * _dot(mix, wout_ref[...])
    ms = jnp.mean(x1 * x1, axis=-1, keepdims=True)
    h2 = x1 * lax.rsqrt(ms + NORM_EPS) * g2_ref[...]
    h2 = h2 * (1.0 + scale2) + shift2
    hb = h2.astype(BF16)
    h2_ref[...] = hb
    act = _silu(_dot(hb, shg_ref[...])) * _dot(hb, shu_ref[...])
    base_ref[...] = x1 + gate2 * _dot(act.astype(BF16), shd_ref[...])

    lpos, gts, cnt_pad, start = _route(h2, rwt_ref, rb_ref)
    lpos_ref[...] = _slot_rows(lpos, SLOT_ROWS)
    ptok_ref[...] = _slot_rows(lpos, LANES).T
    gtok_ref[...] = _slot_rows(gts, LANES).T
    col = lax.broadcasted_iota(I32, cnt_ref.shape, 1)
    run = run_ref[...]
    cnt_ref[...] = jnp.where(col == i, cnt_pad.astype(I32), cnt_ref[...])
    start_ref[...] = jnp.where(col == i, start.astype(I32), start_ref[...])
    rel_ref[...] = jnp.where(col == i, run.astype(I32), rel_ref[...])
    run_ref[...] = run + cnt_pad


def _post_call(layer, tbl, xa, xb, mix_p, mix_s, mod4, wout, g2, shg, shu, shd, rwt, rb):
    npt = mix_p.shape[0] // TM
    n_tok = mix_p.shape[0] + mix_s.shape[0]
    nt = n_tok // TM
    const = lambda i, t: (0, 0)
    tile = lambda i, t: (i, 0)
    gs = pltpu.PrefetchScalarGridSpec(
        num_scalar_prefetch=1,
        grid=(nt,),
        in_specs=_x_specs(npt, xa is xb) + [
            pl.BlockSpec((TM, MIX_W), lambda i, t: (jnp.minimum(i, npt - 1), 0)),
            pl.BlockSpec((TM, MIX_W), lambda i, t: (jnp.maximum(i - npt, 0), 0)),
            _mod_spec(layer),
            _layer_spec(layer, MIX_W, D_MODEL),
            _layer_spec(layer, 1, D_MODEL),
            _layer_spec(layer, D_MODEL, EXPERT_FF),
            _layer_spec(layer, D_MODEL, EXPERT_FF),
            _layer_spec(layer, EXPERT_FF, D_MODEL),
            _layer_spec(layer, N_EXPERTS, D_MODEL),
            _layer_spec(layer, N_EXPERTS, 1),
        ],
        out_specs=[
            pl.BlockSpec((TM, D_MODEL), tile),
            pl.BlockSpec((TM, D_MODEL), tile),
            pl.BlockSpec((SLOT_ROWS, TM), lambda i, t: (0, i)),
            pl.BlockSpec((TM, LANES), tile),
            pl.BlockSpec((TM, LANES), tile),
            pl.BlockSpec((N_EXPERTS, LANES), const),
            pl.BlockSpec((N_EXPERTS, LANES), const),
            pl.BlockSpec((N_EXPERTS, LANES), const),
        ],
        scratch_shapes=[pltpu.VMEM((N_EXPERTS, 1), F32)],
    )
    assert nt <= LANES
    return pl.pallas_call(
        _post_kernel,
        grid_spec=gs,
        out_shape=[
            jax.ShapeDtypeStruct((n_tok, D_MODEL), F32),
            jax.ShapeDtypeStruct((n_tok, D_MODEL), BF16),
            jax.ShapeDtypeStruct((SLOT_ROWS, n_tok), F32),
            jax.ShapeDtypeStruct((n_tok, LANES), F32),
            jax.ShapeDtypeStruct((n_tok, LANES), F32),
            jax.ShapeDtypeStruct((N_EXPERTS, LANES), I32),
            jax.ShapeDtypeStruct((N_EXPERTS, LANES), I32),
            jax.ShapeDtypeStruct((N_EXPERTS, LANES), I32),
        ],
        compiler_params=_cparams(("arbitrary",)),
        name="post_route",
    )(tbl, xa, xb, mix_p, mix_s, mod4, wout, g2, shg, shu, shd, rwt, rb)


SEG_ROW0, SEG_NBLK, SEG_PAD0, SEG_NPAD, SEG_USED = range(5)


def _plan_kernel(last_tile, cnt_ref, rel_ref, seg_ref):
    def per_expert(e, start):
        end = start + rel_ref[e, last_tile] + cnt_ref[e, last_tile]
        nb = lax.shift_right_logical(end - start + (EB - 1), LOG_EB)
        nxt = start + lax.shift_left(nb, LOG_EB)
        seg_ref[SEG_ROW0, e] = start
        seg_ref[SEG_NBLK, e] = nb
        seg_ref[SEG_PAD0, e] = end
        seg_ref[SEG_NPAD, e] = lax.shift_right_logical(nxt - end, LOG_CH)
        seg_ref[SEG_USED, e] = 0
        return nxt

    total = lax.fori_loop(0, N_EXPERTS, per_expert, jnp.int32(0))
    seg_ref[SEG_USED, 0] = lax.shift_right_logical(total, LOG_EB)


def _plan_call(cnt, rel, nt):
    smem = pl.BlockSpec(memory_space=pltpu.SMEM)
    return pl.pallas_call(
        functools.partial(_plan_kernel, nt - 1),
        in_specs=[smem, smem],
        out_specs=smem,
        out_shape=jax.ShapeDtypeStruct((5, N_EXPERTS), I32),
        name="moe_plan",
    )(cnt, rel)


def _rows_copy(src_ref, src_row, dst_ref, dst_row, n_rows, sem):
    return pltpu.make_async_copy(src_ref.at[pl.ds(pl.multiple_of(src_row, CH), n_rows)],
                                 dst_ref.at[pl.ds(pl.multiple_of(dst_row, CH), n_rows)], sem)


class _Runs:
    def __init__(self, cnt_ref, start_ref, rel_ref, seg_ref):
        self.cnt, self.start, self.rel, self.seg = cnt_ref, start_ref, rel_ref, seg_ref

    def start_copies(self, i, copy, tot_ref, slot):
        def per_pair(e2, carry):
            n_big, n_small = carry
            for par in range(2):
                e = 2 * e2 + par
                c = self.cnt[e, i]
                a0 = self.start[e, i]
                b0 = self.seg[SEG_ROW0, e] + self.rel[e, i]
                nb = lax.shift_right_logical(c, LOG_CH + 1)
                odd = jnp.bitwise_and(lax.shift_right_logical(c, LOG_CH), 1)

                def big(q, cc, a0=a0, b0=b0, par=par):
                    copy(a0 + q * (2 * CH), b0 + q * (2 * CH), 2 * CH).start(priority=par)
                    return cc

                lax.fori_loop(0, nb, big, 0)

                @pl.when(odd == 1)
                def _(a0=a0, b0=b0, nb=nb, par=par):
                    copy(a0 + nb * (2 * CH), b0 + nb * (2 * CH), CH).start(priority=par)

                n_big, n_small = n_big + nb, n_small + odd
            return n_big, n_small

        n_big, n_small = lax.fori_loop(0, N_EXPERTS // 2, per_pair, (jnp.int32(0), jnp.int32(0)))
        tot_ref[slot, 0] = n_big
        tot_ref[slot, 1] = n_small

    @staticmethod
    def wait_copies(copy, tot_ref, slot):
        def big(q, c):
            copy(0, 0, 2 * CH).wait()
            return c

        lax.fori_loop(0, tot_ref[slot, 0], big, 0)

        def small(q, c):
            copy(0, 0, CH).wait()
            return c

        lax.fori_loop(0, tot_ref[slot, 1], small, 0)


def _dispatch_kernel(reuse, cnt_ref, start_ref, rel_ref, seg_ref, h_ref, lpos_ref, *rest):
    xb_hbm, sort_s, zero_s, tot_s, sems = rest[1:] if reuse else rest
    i = pl.program_id(0)
    last = pl.num_programs(0) - 1
    slot = lax.rem(i, 2)
    runs = _Runs(cnt_ref, start_ref, rel_ref, seg_ref)
    lp = lpos_ref[...]
    hb = h_ref[...]
    blk = TM
    for r in range(SORT_ROWS // blk):
        srow = (lax.broadcasted_iota(I32, (blk, TM), 0) + r * blk).astype(F32)
        p = jnp.zeros((blk, TM), F32)
        for k in range(TOP_K):
            p = jnp.where(srow == lp[k:k + 1, :], 1.0, p)
        sort_s[slot, r * blk:(r + 1) * blk, :] = _dot(p.astype(BF16), hb).astype(BF16)

    def copy_from(sl):
        return lambda s, d, n: _rows_copy(sort_s.at[sl], s, xb_hbm, d, n, sems.at[sl])

    runs.start_copies(i, copy_from(slot), tot_s, slot)

    @pl.when(i > 0)
    def _():
        runs.wait_copies(copy_from(1 - slot), tot_s, 1 - slot)

    @pl.when(i == last)
    def _():
        runs.wait_copies(copy_from(slot), tot_s, slot)
        _zero_fill_unused(seg_ref, xb_hbm, zero_s, sems.at[0], tail=not reuse)


def _zero_fill_unused(seg_ref, buf_hbm, zero_s, sem, tail):
    zero_s[...] = jnp.zeros_like(zero_s)

    def per_expert(e, c):
        first = seg_ref[SEG_PAD0, e]

        def z_issue(r, cc):
            _rows_copy(zero_s, 0, buf_hbm, first + r * CH, CH, sem).start()
            return cc

        lax.fori_loop(0, seg_ref[SEG_NPAD, e], z_issue, 0)

        def z_drain(r, cc):
            _rows_copy(zero_s, 0, buf_hbm, 0, CH, sem).wait()
            return cc

        lax.fori_loop(0, seg_ref[SEG_NPAD, e], z_drain, 0)
        return c

    lax.fori_loop(0, N_EXPERTS, per_expert, 0)
    if tail:
        _zero_fill_tail(seg_ref, buf_hbm, zero_s, sem)


def _zero_fill_tail(seg_ref, buf_hbm, zero_s, sem):
    n_blocks = buf_hbm.shape[0] // EB

    def blk_copy(b):
        return pltpu.make_async_copy(zero_s, buf_hbm.at[pl.ds(pl.multiple_of(b * EB, EB), EB)], sem)

    def t_issue(b, cc):
        blk_copy(b).start()
        return cc

    lax.fori_loop(seg_ref[SEG_USED, 0], n_blocks, t_issue, 0)

    def t_drain(b, cc):
        blk_copy(0).wait()
        return cc

    lax.fori_loop(seg_ref[SEG_USED, 0], n_blocks, t_drain, 0)


def _dispatch_call(cnt, start, rel, seg, h2, lpos, n_rows, prev=None):
    n_tok = h2.shape[0]
    nt = n_tok // TM
    smem = pl.BlockSpec(memory_space=pltpu.SMEM)
    reuse = prev is not None
    return pl.pallas_call(
        functools.partial(_dispatch_kernel, reuse),
        grid=(nt,),
        in_specs=[
            smem, smem, smem, smem,
            pl.BlockSpec((TM, D_MODEL), lambda i: (i, 0)),
            pl.BlockSpec((SLOT_ROWS, TM), lambda i: (0, i)),
        ] + ([pl.BlockSpec(memory_space=pl.ANY)] if reuse else []),
        input_output_aliases={6: 0} if reuse else {},
        out_specs=pl.BlockSpec(memory_space=pl.ANY),
        out_shape=jax.ShapeDtypeStruct((n_rows, D_MODEL), BF16),
        scratch_shapes=[
            pltpu.VMEM((2, SORT_ROWS, D_MODEL), BF16),
            pltpu.VMEM((EB, D_MODEL), BF16),
            pltpu.SMEM((2, 2), I32),
            pltpu.SemaphoreType.DMA((2,)),
        ],
        compiler_params=_cparams(("arbitrary",)),
        name="moe_dispatch",
    )(cnt, start, rel, seg, h2, lpos, *([prev] if reuse else []))


def _experts_kernel(seg_ref, wg_ref, wu_ref, wd_ref, xb_hbm, yb_hbm, wg_s, wu_s, wd_s, x_s, y_s, sem_x, sem_y):
    e = pl.program_id(0)
    nb = seg_ref[SEG_NBLK, e]
    row0 = seg_ref[SEG_ROW0, e]

    def rows(b):
        return pl.ds(pl.multiple_of(row0 + b * EB, EB), EB)

    def x_copy(b, slot):
        return pltpu.make_async_copy(xb_hbm.at[rows(b)], x_s.at[slot], sem_x.at[slot])

    def y_copy(b, slot):
        return pltpu.make_async_copy(y_s.at[slot], yb_hbm.at[rows(b)], sem_y.at[slot])

    @pl.when(nb > 0)
    def _():
        x_copy(0, 0).start()
        wg_s[...] = wg_ref[...].astype(BF16)
        wu_s[...] = wu_ref[...].astype(BF16)
        wd_s[...] = wd_ref[...].astype(BF16)

        def block(b, c):
            slot = lax.rem(b, 2)

            @pl.when(b + 1 < nb)
            def _():
                x_copy(b + 1, 1 - slot).start()

            x_copy(b, slot).wait()

            @pl.when(b >= 2)
            def _():
                y_copy(b - 2, slot).wait()

            xb = x_s[slot]
            act = _silu(_dot(xb, wg_s[...])) * _dot(xb, wu_s[...])
            y_s[slot] = _dot(act.astype(BF16), wd_s[...]).astype(BF16)
            y_copy(b, slot).start()
            return c

        lax.fori_loop(0, nb, block, 0)

        @pl.when(nb >= 2)
        def _():
            y_copy(nb - 2, lax.rem(nb, 2)).wait()

        y_copy(nb - 1, lax.rem(nb - 1, 2)).wait()


def _experts_call(seg, xb, layer, wg, wu, wd):
    n_rows = xb.shape[0]
    return pl.pallas_call(
        _experts_kernel,
        grid=(N_EXPERTS,),
        in_specs=[
            pl.BlockSpec(memory_space=pltpu.SMEM),
            pl.BlockSpec((None, None, D_MODEL, EXPERT_FF), lambda e: (layer, e, 0, 0)),
            pl.BlockSpec((None, None, D_MODEL, EXPERT_FF), lambda e: (layer, e, 0, 0)),
            pl.BlockSpec((None, None, EXPERT_FF, D_MODEL), lambda e: (layer, e, 0, 0)),
            pl.BlockSpec(memory_space=pl.ANY),
        ],
        out_specs=pl.BlockSpec(memory_space=pl.ANY),
        out_shape=jax.ShapeDtypeStruct((n_rows, D_MODEL), BF16),
        input_output_aliases={4: 0},
        scratch_shapes=[
            pltpu.VMEM((D_MODEL, EXPERT_FF), BF16),
            pltpu.VMEM((D_MODEL, EXPERT_FF), BF16),
            pltpu.VMEM((EXPERT_FF, D_MODEL), BF16),
            pltpu.VMEM((2, EB, D_MODEL), BF16),
            pltpu.VMEM((2, EB, D_MODEL), BF16),
            pltpu.SemaphoreType.DMA((2,)),
            pltpu.SemaphoreType.DMA((2,)),
        ],
        compiler_params=_cparams(("arbitrary",)),
        name="moe_experts",
    )(seg, wg, wu, wd, xb)


def _combine_kernel(final, tbl_ref, cnt_ref, start_ref, rel_ref, seg_ref, yb_hbm, base_ref, gtok_ref, ptok_ref,
                    mod_ref, gf_ref, *rest):
    *out_refs, sort_s, tot_s, sems = rest
    i = pl.program_id(0)
    slot = lax.rem(i, 2)
    runs = _Runs(cnt_ref, start_ref, rel_ref, seg_ref)

    def copy_to(sl):
        return lambda s, d, n: _rows_copy(yb_hbm, d, sort_s.at[sl], s, n, sems.at[sl])

    @pl.when(i == 0)
    def _():
        sort_s[...] = jnp.zeros_like(sort_s)
        runs.start_copies(i, copy_to(slot), tot_s, slot)

    @pl.when(i + 1 < pl.num_programs(0))
    def _():
        runs.start_copies(i + 1, copy_to(1 - slot), tot_s, 1 - slot)

    runs.wait_copies(copy_to(slot), tot_s, slot)

    gt = gtok_ref[...]
    pt = ptok_ref[...]
    col = lax.broadcasted_iota(I32, (TM, SORT_ROWS), 1).astype(F32)
    w = jnp.zeros((TM, SORT_ROWS), F32)
    for k in range(TOP_K):
        w = jnp.where(col == pt[:, k:k + 1], gt[:, k:k + 1], w)
    wh = w.astype(BF16)
    wl = (w - wh.astype(F32)).astype(BF16)
    ys = sort_s[slot]
    routed = _dot(wh, ys) + _dot(wl, ys)
    gate2 = mod_ref[...][:, 5 * D_MODEL:6 * D_MODEL]
    y = base_ref[...] + gate2 * routed
    if final:
        y = y * lax.rsqrt(jnp.mean(y * y, axis=-1, keepdims=True) + NORM_EPS) * gf_ref[...]
        yp_ref, ys_ref = out_refs
        is_context = i < tbl_ref[2, 0]

        @pl.when(is_context)
        def _():
            yp_ref[...] = y

        @pl.when(jnp.logical_not(is_context))
        def _():
            ys_ref[...] = y
    else:
        out_refs[0][...] = y


def _combine_call(final, layer, npt, tbl, cnt, start, rel, seg, yb, base, gtok, ptok, mod4, gfinal):
    n_tok = base.shape[0]
    nt = n_tok // TM
    tile = lambda i, t: (i, 0)
    smem = pl.BlockSpec(memory_space=pltpu.SMEM)
    if final:
        out_specs = [pl.BlockSpec((TM, D_MODEL), lambda i, t: (jnp.minimum(i, npt - 1), 0)),
                     pl.BlockSpec((TM, D_MODEL), lambda i, t: (jnp.maximum(i - npt, 0), 0))]
        out_shape = [jax.ShapeDtypeStruct((npt * TM, D_MODEL), F32),
                     jax.ShapeDtypeStruct((n_tok - npt * TM, D_MODEL), F32)]
    else:
        out_specs = pl.BlockSpec((TM, D_MODEL), tile)
        out_shape = jax.ShapeDtypeStruct((n_tok, D_MODEL), F32)
    gs = pltpu.PrefetchScalarGridSpec(
        num_scalar_prefetch=1,
        grid=(nt,),
        in_specs=[
            smem, smem, smem, smem,
            pl.BlockSpec(memory_space=pl.ANY),
            pl.BlockSpec((TM, D_MODEL), tile),
            pl.BlockSpec((TM, LANES), tile),
            pl.BlockSpec((TM, LANES), tile),
            _mod_spec(layer),
            pl.BlockSpec((1, D_MODEL), lambda i, t: (0, 0)),
        ],
        out_specs=out_specs,
        scratch_shapes=[
            pltpu.VMEM((2, SORT_ROWS, D_MODEL), BF16),
            pltpu.SMEM((2, 2), I32),
            pltpu.SemaphoreType.DMA((2,)),
        ],
    )
    return pl.pallas_call(
        functools.partial(_combine_kernel, final),
        grid_spec=gs,
        out_shape=out_shape,
        compiler_params=_cparams(("arbitrary",)),
        name="moe_combine",
    )(tbl, cnt, start, rel, seg, yb, base, gtok, ptok, mod4, gfinal)


def _pad_cols(w, groups, width, slot):
    lead = w.shape[:-1]
    w = w.reshape(*lead, groups, width)
    pad = [(0, 0)] * (len(lead) + 1) + [(0, slot - width)]
    return jnp.pad(w, pad).reshape(*lead, groups * slot)


def _prep_w_in(w):
    c = np.cumsum([0, 256, 256, 256, 256, 256, 256, 256, 256, 128, 32])
    rq, rk, rv, rg, dq, dk, dv, cq, ckv, kpe = [w[..., c[n]:c[n + 1]] for n in range(10)]
    kpe_slot = jnp.concatenate([kpe, jnp.zeros_like(kpe), kpe, jnp.zeros_like(kpe)], axis=-1)
    cols = [rq, rk * (RET_DIM ** -0.5), rv, rg, _pad_cols(dq, DIFF_HEADS, 2 * DIFF_QK_DIM, SLOT),
            _pad_cols(dk, DIFF_HEADS, 2 * DIFF_QK_DIM, SLOT), dv, cq, ckv, kpe_slot]
    return jnp.concatenate(cols, axis=-1).astype(BF16)


def _rope_tables(n_pos, dim, lane_offsets):
    n_rows = n_pos // GRID_W
    row = jnp.repeat(jnp.arange(n_rows, dtype=F32), GRID_W)
    col = jnp.tile(jnp.arange(GRID_W, dtype=F32), n_rows)
    half = dim // 2
    freqs = ROPE_THETA ** (-jnp.arange(0, half, 2, dtype=F32) / half)
    ar = row[:, None] * freqs[None, :]
    ac = col[:, None] * freqs[None, :]
    ang = jnp.concatenate([ar, ar, ac, ac], axis=-1)
    cos, sin = jnp.cos(ang), jnp.sin(ang)
    first = (np.arange(dim) % 16) < 8
    sa = jnp.where(first[None, :], -sin, 0.0)
    sb = jnp.where(first[None, :], 0.0, sin)
    c_t = jnp.ones((n_pos, SLOT), F32)
    a_t = jnp.zeros((n_pos, SLOT), F32)
    b_t = jnp.zeros((n_pos, SLOT), F32)
    for off in lane_offsets:
        c_t = c_t.at[:, off:off + dim].set(cos)
        a_t = a_t.at[:, off:off + dim].set(sa)
        b_t = b_t.at[:, off:off + dim].set(sb)
    ident = (jnp.ones((TM, SLOT), F32), jnp.zeros((TM, SLOT), F32), jnp.zeros((TM, SLOT), F32))
    return tuple(jnp.concatenate([i0, t], axis=0) for i0, t in zip(ident, (c_t, a_t, b_t)))


def _block_diag_states(s):
    b, l, h, dk, dv = s.shape
    eye = jnp.eye(h, dtype=s.dtype)
    return jnp.einsum('blhkv,hg->blhkgv', s, eye).reshape(b, l, h * dk, h * dv)


def kernel(x_prompt, x_sample, cache_diff_k, cache_diff_v, cache_mla_ckv, cache_mla_kpe, state_ret_fwd, state_ret_bwd, c, c_ctx, w_ada, b_ada, norm_mix, norm_ffn, norm_final, w_in, ret_decay_fwd, ret_decay_bwd, diff_lambda, diff_subln, mla_q_norm, mla_w_uq, mla_kv_norm, mla_w_ukv, w_out, router_w, router_bias, exp_w_gate, exp_w_up, exp_w_down, sh_w_gate, sh_w_up, sh_w_down):
    n_pb, p_len, _ = x_prompt.shape
    n_sb, s_len, _ = x_sample.shape
    past_len = cache_diff_k.shape[3]
    n_p = n_pb * p_len
    n_s = n_sb * s_len
    n_tok = n_p + n_s
    nt = n_tok // TM
    npt = n_p // TM
    assert p_len == TM and s_len % TM == 0 and n_p % s_len == 0 and past_len % 8 == 0

    tiles = np.arange(nt)
    mod_row = np.where(tiles < npt, n_sb, (tiles - npt) // (s_len // TM))
    rope_blk = np.where(tiles < npt, 0, 1 + (tiles - npt) % (s_len // TM))
    tbl = jnp.asarray(np.stack([mod_row, rope_blk, np.full(nt, npt)]).astype(np.int32))

    n_cond = 16
    cond = jnp.zeros((n_cond, D_MODEL), F32).at[:n_sb].set(c).at[n_sb].set(c_ctx)
    mod_all = _modulation(cond, w_ada, b_ada)

    rope_d = _rope_tables(s_len, DIFF_QK_DIM, (0, DIFF_QK_DIM))
    rope_m = _rope_tables(s_len, MLA_ROPE_DIM, (KR_LO,))
    place = jnp.zeros((MLA_ROPE_DIM, SLOT), F32).at[np.arange(MLA_ROPE_DIM), KR_LO + np.arange(MLA_ROPE_DIM)].set(1.0).astype(BF16)
    cache_dv_t = cache_diff_v.transpose(0, 1, 3, 2, 4).reshape(n_sb, DEPTH, past_len, DIFF_V_W)
    s0f_bd = _block_diag_states(state_ret_fwd)
    s0b_bd = _block_diag_states(state_ret_bwd)

    n_blocks = pl.cdiv(n_tok * TOP_K + nt * N_EXPERTS * (CH - 1) + N_EXPERTS * (EB - CH), EB)
    n_rows = n_blocks * EB

    xa, xb = x_prompt.reshape(n_p, D_MODEL), x_sample.reshape(n_s, D_MODEL)
    gfinal = norm_final.reshape(1, D_MODEL)

    mod4 = mod_all.reshape(DEPTH, n_cond, 1, 6 * D_MODEL)
    w_pre = _prep_w_in(w_in)
    wuq = _pad_cols(mla_w_uq, MLA_HEADS, MLA_NOPE_DIM + MLA_ROPE_DIM, SLOT).astype(BF16)
    ukv = mla_w_ukv.reshape(DEPTH, MLA_KV_LORA, MLA_HEADS, MLA_NOPE_DIM + MLA_V_DIM)
    wk = _pad_cols(ukv[..., :MLA_NOPE_DIM].reshape(DEPTH, MLA_KV_LORA, -1), MLA_HEADS, MLA_NOPE_DIM, SLOT).astype(BF16)
    wv = ukv[..., MLA_NOPE_DIM:].reshape(DEPTH, MLA_KV_LORA, MLA_V_W).astype(BF16)
    decf = jnp.repeat(ret_decay_fwd, RET_DIM, axis=-1).reshape(DEPTH, 1, RET_W)
    decb = jnp.repeat(ret_decay_bwd, RET_DIM, axis=-1).reshape(DEPTH, 1, RET_W)
    subln = jnp.tile(diff_subln, (1, DIFF_HEADS)).reshape(DEPTH, 1, DIFF_V_W)
    vec = lambda p: p.reshape(DEPTH, 1, -1)
    wout_b, shg_b, shu_b, shd_b = (w.astype(BF16) for w in (w_out, sh_w_gate, sh_w_up, sh_w_down))
    rwt = router_w.transpose(0, 2, 1)
    rb = router_bias.reshape(DEPTH, N_EXPERTS, 1)

    caches = []
    sorted_buf = None
    for l in range(DEPTH):
        lam_init = 0.8 - 0.6 * math.exp(-0.3 * l)
        ra, qa, ka = _pre_call(l, tbl, xa, xb, npt, n_tok, mod4, vec(norm_mix), w_pre, vec(mla_q_norm), wuq,
                               vec(mla_kv_norm), rope_d, rope_m)
        mix_p, sf, sb = _mix_prompt_call(lam_init, l, qa, ka, ra, n_pb, p_len, decf, decb,
                                         diff_lambda, subln, wk, wv)
        mix_s = _mix_sample_call(lam_init, l, qa, ka, ra, n_p, n_sb, s_len, past_len, cache_diff_k,
                                 cache_dv_t, cache_mla_ckv, cache_mla_kpe, s0f_bd, s0b_bd, decf, decb,
                                 diff_lambda, subln, wk, wv, place)
        base, h2, lpos, ptok, gtok, cnt, start, rel = _post_call(
            l, tbl, xa, xb, mix_p, mix_s, mod4, wout_b, vec(norm_ffn), shg_b, shu_b, shd_b, rwt, rb)
        seg = _plan_call(cnt, rel, nt)
        sorted_buf = _dispatch_call(cnt, start, rel, seg, h2, lpos, n_rows, prev=sorted_buf)
        sorted_buf = _experts_call(seg, sorted_buf, l, exp_w_gate, exp_w_up, exp_w_down)
        final = l == DEPTH - 1
        out = _combine_call(final, l, npt, tbl, cnt, start, rel, seg, sorted_buf, base, gtok, ptok, mod4, gfinal)
        xa, xb = out if final else (out, out)

        kp = ka[:n_p].reshape(n_pb, p_len, KA_W)
        dk = kp[:, :, KA_DK:KA_DV].reshape(n_pb, p_len, DIFF_HEADS, SLOT)[..., :2 * DIFF_QK_DIM]
        dv = kp[:, :, KA_DV:KA_CKV].reshape(n_pb, p_len, DIFF_HEADS, DIFF_V_DIM)
        diag = lambda s: jnp.stack([s[:, h * RET_DIM:(h + 1) * RET_DIM, h * RET_DIM:(h + 1) * RET_DIM]
                                    for h in range(RET_HEADS)], axis=1)
        caches.append((dk.transpose(0, 2, 1, 3), dv.transpose(0, 2, 1, 3), kp[:, :, KA_CKV:KA_KPE],
                       kp[:, :, KA_KPE:KA_KPE + MLA_ROPE_DIM], diag(sf), diag(sb)))

    y_prompt = xa.reshape(n_pb, p_len, D_MODEL)
    y_sample = xb.reshape(n_sb, s_len, D_MODEL)
    new = [jnp.stack([cs[n] for cs in caches], axis=1) for n in range(6)]
    return (y_prompt, y_sample, *new)
```

```python
import functools
import math

import numpy as np
import jax
import jax.numpy as jnp
from jax import lax
from jax.experimental import pallas as pl
from jax.experimental.pallas import tpu as pltpu

F32 = jnp.float32
BF16 = jnp.bfloat16
I32 = jnp.int32

D_MODEL = 1024
DEPTH = 2
GRID_W = 64
ROPE_THETA = 10000.0
NORM_EPS = 1e-6

RET_HEADS = 4
RET_DIM = 64
RET_CHUNK = 128
RET_W = RET_HEADS * RET_DIM
DIFF_HEADS = 4
DIFF_QK_DIM = 32
DIFF_V_DIM = 64
DIFF_V_W = DIFF_HEADS * DIFF_V_DIM
MLA_HEADS = 8
MLA_Q_LORA = 256
MLA_KV_LORA = 128
MLA_NOPE_DIM = 64
MLA_ROPE_DIM = 32
MLA_V_DIM = 64
MLA_V_W = MLA_HEADS * MLA_V_DIM
MIX_W = RET_W + DIFF_V_W + MLA_V_W

N_EXPERTS = 64
TOP_K = 6
N_GROUPS = 8
GROUP_SIZE = N_EXPERTS // N_GROUPS
TOPK_GROUPS = 4
EXPERT_FF = 256
ROUTED_SCALE = 2.5

LANES = 128
SLOT = LANES
TM = 256
SLOT_ROWS = 8
EB = 512
LOG_EB = 9
CH = 16
LOG_CH = 4
SORT_ROWS = 2560
VMEM_LIMIT = 48 * 1024 * 1024

C_RQ, C_RK, C_RV, C_RG = 0, 256, 512, 768
C_DQ = 1024
C_DK = 1536
C_DV = 2048
C_CQ = 2304
C_CKV = 2560
C_KPE = 2688
N_PRE = 2816
QA_W = 4 * SLOT + MLA_HEADS * SLOT
KA_DK, KA_DV, KA_CKV, KA_KPE = 0, 512, 768, 896
KA_W = 1024
KR_LO, KR_HI = 64, 96


def _dot(a, b):
    return jnp.dot(a, b, preferred_element_type=F32)


def _dot_nt(a, b):
    return lax.dot_general(a, b, (((1,), (1,)), ((), ())), preferred_element_type=F32)


def _dot_tn(a, b):
    return lax.dot_general(a, b, (((0,), (0,)), ((), ())), preferred_element_type=F32)


def _split_dot(x, w_bf16):
    hi = x.astype(BF16)
    lo = (x - hi.astype(F32)).astype(BF16)
    return _dot(hi, w_bf16) + _dot(lo, w_bf16)


def _split_dot_nt(w, x):
    wh = w.astype(BF16)
    wl = (w - wh.astype(F32)).astype(BF16)
    xh = x.astype(BF16)
    xl = (x - xh.astype(F32)).astype(BF16)
    return _dot_nt(wh, xh) + _dot_nt(wh, xl) + _dot_nt(wl, xh)


def _silu(x):
    return x * jax.nn.sigmoid(x)


def _cparams(sem):
    return pltpu.CompilerParams(dimension_semantics=sem, vmem_limit_bytes=VMEM_LIMIT)


MOD_TN = 512


def _mod_kernel(c_ref, w_ref, b_ref, o_ref):
    s = _silu(c_ref[...])
    o_ref[...] = _split_dot3(s, w_ref[...]) + b_ref[...]


def _split_dot3(x, w):
    xh = x.astype(BF16)
    xl = (x - xh.astype(F32)).astype(BF16)
    wh = w.astype(BF16)
    wl = (w - wh.astype(F32)).astype(BF16)
    return _dot(xh, wh) + _dot(xh, wl) + _dot(xl, wh)


def _modulation(cond, w_ada, b_ada):
    n_rows = cond.shape[0]
    n_out = w_ada.shape[-1]
    return pl.pallas_call(
        _mod_kernel,
        grid=(DEPTH, n_out // MOD_TN),
        in_specs=[
            pl.BlockSpec((n_rows, D_MODEL), lambda l, j: (0, 0)),
            pl.BlockSpec((None, D_MODEL, MOD_TN), lambda l, j: (l, 0, j)),
            pl.BlockSpec((None, 1, MOD_TN), lambda l, j: (l, 0, j)),
        ],
        out_specs=pl.BlockSpec((None, n_rows, MOD_TN), lambda l, j: (l, 0, j)),
        out_shape=jax.ShapeDtypeStruct((DEPTH, n_rows, n_out), F32),
        compiler_params=_cparams(("arbitrary", "arbitrary")),
        name="adaln_mod",
    )(cond, w_ada, b_ada.reshape(DEPTH, 1, n_out))


def _rope_slot(x, cos, sa, sb):
    up = pltpu.roll(x, LANES - 8, 1)
    dn = pltpu.roll(x, 8, 1)
    return x * cos + up * sa + dn * sb


def _tile_x(tbl_ref, xa_ref, xb_ref):
    return jnp.where(pl.program_id(0) < tbl_ref[2, 0], xa_ref[...], xb_ref[...])


def _x_specs(npt, combined):
    off = 0 if combined else npt
    return [pl.BlockSpec((TM, D_MODEL), lambda i, t: (jnp.minimum(i, npt - 1), 0)),
            pl.BlockSpec((TM, D_MODEL), lambda i, t: (jnp.maximum(i, npt) - off, 0))]


def _pre_kernel(tbl_ref, xa_ref, xb_ref, mod_ref, g_ref, w_ref, qg_ref, wuq_ref, kvg_ref,
                cd_ref, sad_ref, sbd_ref, cm_ref, sam_ref, sbm_ref,
                ra_ref, qa_ref, ka_ref):
    x = _tile_x(tbl_ref, xa_ref, xb_ref)
    mod = mod_ref[...]
    shift1 = mod[:, 0:D_MODEL]
    scale1 = mod[:, D_MODEL:2 * D_MODEL]
    ms = jnp.mean(x * x, axis=-1, keepdims=True)
    h = x * lax.rsqrt(ms + NORM_EPS) * g_ref[...]
    h = h * (1.0 + scale1) + shift1
    hb = h.astype(BF16)

    def proj(lo, hi):
        return _dot(hb, w_ref[:, lo:hi])

    ra_ref[:, 0:C_RG] = proj(C_RQ, C_RG).astype(BF16)
    ra_ref[:, C_RG:C_DQ] = _silu(proj(C_RG, C_DQ)).astype(BF16)

    cd, sad, sbd = cd_ref[...], sad_ref[...], sbd_ref[...]
    cm, sam, sbm = cm_ref[...], sam_ref[...], sbm_ref[...]
    dq = proj(C_DQ, C_DK)
    dk = proj(C_DK, C_DV)
    for hd in range(DIFF_HEADS):
        sl = slice(hd * SLOT, (hd + 1) * SLOT)
        qa_ref[:, sl] = _rope_slot(dq[:, sl], cd, sad, sbd).astype(BF16)
        ka_ref[:, KA_DK + hd * SLOT:KA_DK + (hd + 1) * SLOT] = _rope_slot(dk[:, sl], cd, sad, sbd)
    ka_ref[:, KA_DV:KA_CKV] = proj(C_DV, C_CQ)

    cq = proj(C_CQ, C_CKV)
    cqn = cq * lax.rsqrt(jnp.mean(cq * cq, axis=-1, keepdims=True) + NORM_EPS) * qg_ref[...]
    qm = _dot(cqn.astype(BF16), wuq_ref[...])
    for hd in range(MLA_HEADS):
        sl = slice(hd * SLOT, (hd + 1) * SLOT)
        qa_ref[:, 4 * SLOT + hd * SLOT:4 * SLOT + (hd + 1) * SLOT] = _rope_slot(qm[:, sl], cm, sam, sbm).astype(BF16)

    ckv = proj(C_CKV, C_KPE)
    ka_ref[:, KA_CKV:KA_KPE] = ckv * lax.rsqrt(jnp.mean(ckv * ckv, axis=-1, keepdims=True) + NORM_EPS) * kvg_ref[...]
    ka_ref[:, KA_KPE:KA_W] = _rope_slot(proj(C_KPE, N_PRE), cm, sam, sbm)


def _layer_spec(layer, rows, cols):
    return pl.BlockSpec((None, rows, cols), lambda *_: (layer, 0, 0))


def _mod_spec(layer):
    return pl.BlockSpec((None, None, 1, 6 * D_MODEL), lambda i, t: (layer, t[0, i], 0, 0))


def _pre_call(layer, tbl, xa, xb, npt, n_tok, mod4, g, w_pre, qg, wuq, kvg, rope_d, rope_m):
    nt = n_tok // TM
    tile = lambda i, t: (i, 0)
    rope = lambda i, t: (t[1, i], 0)
    gs = pltpu.PrefetchScalarGridSpec(
        num_scalar_prefetch=1,
        grid=(nt,),
        in_specs=_x_specs(npt, xa is xb) + [
            _mod_spec(layer),
            _layer_spec(layer, 1, D_MODEL),
            _layer_spec(layer, D_MODEL, N_PRE),
            _layer_spec(layer, 1, MLA_Q_LORA),
            _layer_spec(layer, MLA_Q_LORA, MLA_HEADS * SLOT),
            _layer_spec(layer, 1, MLA_KV_LORA),
        ] + [pl.BlockSpec((TM, SLOT), rope)] * 6,
        out_specs=[
            pl.BlockSpec((TM, D_MODEL), tile),
            pl.BlockSpec((TM, QA_W), tile),
            pl.BlockSpec((TM, KA_W), tile),
        ],
    )
    return pl.pallas_call(
        _pre_kernel,
        grid_spec=gs,
        out_shape=[
            jax.ShapeDtypeStruct((n_tok, D_MODEL), BF16),
            jax.ShapeDtypeStruct((n_tok, QA_W), BF16),
            jax.ShapeDtypeStruct((n_tok, KA_W), F32),
        ],
        compiler_params=_cparams(("arbitrary",)),
        name="pre_proj",
    )(tbl, xa, xb, mod4, g, w_pre, qg, wuq, kvg, *rope_d, *rope_m)


def _lane_iota(shape):
    return lax.broadcasted_iota(I32, shape, len(shape) - 1)


def _head_mask(n_rows, width, head, head_w):
    lane = _lane_iota((n_rows, width))
    return (lane >= head * head_w) & (lane < (head + 1) * head_w)


def _seg_mean_sq(o, bd_ones):
    return _split_dot(o * o, bd_ones) * (1.0 / RET_DIM)


def _block_diag_ones(n, blk):
    r = lax.broadcasted_iota(I32, (n, n), 0) // blk
    c = lax.broadcasted_iota(I32, (n, n), 1) // blk
    return r == c


def _retention(ra_ref, seq_len, decf_ref, decb_ref, s0f, s0b):
    C = RET_CHUNK
    nc = seq_len // C
    lgf = -jnp.exp(decf_ref[...])
    lgb = -jnp.exp(decb_ref[...])
    pos = lax.broadcasted_iota(I32, (C, RET_W), 0).astype(F32)
    qdf = jnp.exp((pos + 1.0) * lgf)
    kdf = jnp.exp((C - 1.0 - pos) * lgf)
    cdf = jnp.exp(float(C) * lgf)
    qdb = jnp.exp((C - pos) * lgb)
    kdb = jnp.exp(pos * lgb)
    cdb = jnp.exp(float(C) * lgb)
    ii = lax.broadcasted_iota(I32, (C, C), 0).astype(F32)
    jj = lax.broadcasted_iota(I32, (C, C), 1).astype(F32)
    dist = ii - jj
    dmats = []
    for hd in range(RET_HEADS):
        lf = lgf[:, hd * RET_DIM:hd * RET_DIM + 1]
        lb = lgb[:, hd * RET_DIM:hd * RET_DIM + 1]
        dmats.append(jnp.where(dist >= 0, jnp.exp(dist * lf), jnp.exp(-dist * lb)))
    bd = _block_diag_ones(RET_W, RET_DIM)
    bd_ones = jnp.where(bd, 1.0, 0.0).astype(BF16)

    def chunk(n):
        rows = slice(n * C, (n + 1) * C)
        return (ra_ref[rows, C_RQ:C_RK], ra_ref[rows, C_RK:C_RV], ra_ref[rows, C_RV:C_RG])

    cross = [None] * nc
    sf = s0f
    for n in range(nc):
        q, k, v = chunk(n)
        cross[n] = _dot((q * qdf).astype(BF16), sf.astype(BF16))
        kv = _dot_tn((k * kdf).astype(BF16), v.astype(BF16))
        sf = sf * cdf + jnp.where(bd, kv, 0.0)
    sb = s0b
    for n in range(nc - 1, -1, -1):
        q, k, v = chunk(n)
        cross[n] = cross[n] + _dot((q * qdb).astype(BF16), sb.astype(BF16))
        kv = _dot_tn((k * kdb).astype(BF16), v.astype(BF16))
        sb = sb * cdb + jnp.where(bd, kv, 0.0)

    outs = []
    for n in range(nc):
        q, k, v = chunk(n)
        kb = k.astype(BF16)
        vb = v.astype(BF16)
        o = cross[n]
        for hd in range(RET_HEADS):
            hm = _head_mask(C, RET_W, hd, RET_DIM)
            sc = _dot_nt(jnp.where(hm, q, 0.0).astype(BF16), kb) * dmats[hd]
            o = o + jnp.where(hm, _dot(sc.astype(BF16), vb), 0.0)
        on = o * lax.rsqrt(_seg_mean_sq(o, bd_ones) + NORM_EPS)
        outs.append(on * ra_ref[n * C:(n + 1) * C, C_RG:C_DQ])
    return outs, sf, sb


def _softmax_pv(s_parts, v_parts, scale):
    m = None
    for s in s_parts:
        mm = jnp.max(s, axis=-1, keepdims=True)
        m = mm if m is None else jnp.maximum(m, mm)
    m = m * scale
    acc = None
    den = None
    for s, v in zip(s_parts, v_parts):
        e = jnp.exp(s * scale - m)
        ds = jnp.sum(e, axis=-1, keepdims=True)
        pv = _dot(e.astype(BF16), v)
        acc = pv if acc is None else acc + pv
        den = ds if den is None else den + ds
    return acc / den


def _diff_attention(dq, k_parts, v_parts, lam, subln, lam_init, bd_ones):
    lq = dq.shape[0]
    scale = DIFF_QK_DIM ** -0.5
    lane = _lane_iota((lq, SLOT))
    out = jnp.zeros((lq, DIFF_V_W), F32)
    for hd in range(DIFF_HEADS):
        qh = dq[:, hd * SLOT:(hd + 1) * SLOT]
        q1 = jnp.where(lane < DIFF_QK_DIM, qh, 0.0).astype(BF16)
        q2 = jnp.where(lane >= DIFF_QK_DIM, qh, 0.0).astype(BF16)
        s1 = [_dot_nt(q1[:, :kp[hd].shape[1]], kp[hd]) for kp in k_parts]
        s2 = [_dot_nt(q2[:, :kp[hd].shape[1]], kp[hd]) for kp in k_parts]
        o = _softmax_pv(s1, v_parts, scale) - lam * _softmax_pv(s2, v_parts, scale)
        out = jnp.where(_head_mask(lq, DIFF_V_W, hd, DIFF_V_DIM), o, out)
    on = out * lax.rsqrt(_seg_mean_sq(out, bd_ones) + NORM_EPS) * subln
    return on * (1.0 - lam_init)


def _mla_attention(qm, k_parts, v_parts):
    lq = qm.shape[0]
    scale = (MLA_NOPE_DIM + MLA_ROPE_DIM) ** -0.5
    halves = []
    for g in range(2):
        out = jnp.zeros((lq, 256), F32)
        for hh in range(4):
            hd = 4 * g + hh
            qh = qm[:, hd * SLOT:(hd + 1) * SLOT].astype(BF16)
            s = [_dot_nt(qh, kp[:, hd * SLOT:(hd + 1) * SLOT]) for kp in k_parts]
            o = _softmax_pv(s, [vp[:, 256 * g:256 * (g + 1)] for vp in v_parts], scale)
            out = jnp.where(_head_mask(lq, 256, hh, MLA_V_DIM), o, out)
        halves.append(out)
    return halves


def _mla_keys(ka_val_ckv, kr_slot, wk_ref, wv_ref):
    cb = ka_val_ckv.astype(BF16)
    kn = _dot(cb, wk_ref[...])
    ks = [(kn[:, hd * SLOT:(hd + 1) * SLOT] + kr_slot).astype(BF16) for hd in range(MLA_HEADS)]
    return jnp.concatenate(ks, axis=1), _dot(cb, wv_ref[...]).astype(BF16)


def _kr_only(kpe_slot):
    lane = _lane_iota(kpe_slot.shape)
    return jnp.where((lane >= KR_LO) & (lane < KR_HI), kpe_slot, 0.0)


def _diff_lambda(dl_ref, lam_init):
    dl = dl_ref[...]
    a = jnp.sum(dl[0:1] * dl[1:2], axis=-1, keepdims=True)
    b = jnp.sum(dl[2:3] * dl[3:4], axis=-1, keepdims=True)
    return jnp.exp(a) - jnp.exp(b) + lam_init


def _mix_prompt_kernel(lam_init, qa_ref, ka_ref, ra_ref, decf_ref, decb_ref, dl_ref, subln_ref,
                       wk_ref, wv_ref, mix_ref, sf_ref, sb_ref):
    seq = qa_ref.shape[0]
    zero_state = jnp.zeros((RET_W, RET_W), F32)
    outs, sf, sb = _retention(ra_ref, seq, decf_ref, decb_ref, zero_state, zero_state)
    for n, o in enumerate(outs):
        mix_ref[n * RET_CHUNK:(n + 1) * RET_CHUNK, 0:RET_W] = o.astype(BF16)
    sf_ref[...] = sf
    sb_ref[...] = sb

    bd_ones = jnp.where(_block_diag_ones(DIFF_V_W, DIFF_V_DIM), 1.0, 0.0).astype(BF16)
    lam = _diff_lambda(dl_ref, lam_init)
    kd = [ka_ref[:, KA_DK + hd * SLOT:KA_DK + (hd + 1) * SLOT].astype(BF16) for hd in range(DIFF_HEADS)]
    vd = ka_ref[:, KA_DV:KA_CKV].astype(BF16)
    mix_ref[:, RET_W:RET_W + DIFF_V_W] = _diff_attention(
        qa_ref[:, 0:4 * SLOT], [kd], [vd], lam, subln_ref[...], lam_init, bd_ones).astype(BF16)

    km, vm = _mla_keys(ka_ref[:, KA_CKV:KA_KPE], _kr_only(ka_ref[:, KA_KPE:KA_W]), wk_ref, wv_ref)
    halves = _mla_attention(qa_ref[:, 4 * SLOT:QA_W], [km], [vm])
    mix_ref[:, 512:768] = halves[0].astype(BF16)
    mix_ref[:, 768:1024] = halves[1].astype(BF16)


def _mixer_param_specs(layer):
    return [
        _layer_spec(layer, 1, RET_W),
        _layer_spec(layer, 1, RET_W),
        _layer_spec(layer, 4, DIFF_QK_DIM),
        _layer_spec(layer, 1, DIFF_V_W),
        _layer_spec(layer, MLA_KV_LORA, MLA_HEADS * SLOT),
        _layer_spec(layer, MLA_KV_LORA, MLA_V_W),
    ]


def _mix_prompt_call(lam_init, layer, qa, ka, ra, n_seq, seq_len, decf, decb, dl, subln, wk, wv):
    seq = lambda b: (b, 0)
    return pl.pallas_call(
        functools.partial(_mix_prompt_kernel, lam_init),
        grid=(n_seq,),
        in_specs=[
            pl.BlockSpec((seq_len, QA_W), seq),
            pl.BlockSpec((seq_len, KA_W), seq),
            pl.BlockSpec((seq_len, D_MODEL), seq),
        ] + _mixer_param_specs(layer),
        out_specs=[
            pl.BlockSpec((seq_len, MIX_W), seq),
            pl.BlockSpec((None, RET_W, RET_W), lambda b: (b, 0, 0)),
            pl.BlockSpec((None, RET_W, RET_W), lambda b: (b, 0, 0)),
        ],
        out_shape=[
            jax.ShapeDtypeStruct((n_seq * seq_len, MIX_W), BF16),
            jax.ShapeDtypeStruct((n_seq, RET_W, RET_W), F32),
            jax.ShapeDtypeStruct((n_seq, RET_W, RET_W), F32),
        ],
        compiler_params=_cparams(("arbitrary",)),
        name="mix_prompt",
    )(qa, ka, ra, decf, decb, dl, subln, wk, wv)


def _mix_sample_kernel(lam_init, qa_ref, ka_ref, ra_ref, ckd_ref, cvd_ref, cckv_ref, ckpe_ref,
                       s0f_ref, s0b_ref, decf_ref, decb_ref, dl_ref, subln_ref, wk_ref, wv_ref,
                       place_ref, mix_ref,
                       ret_s, kdn_s, vdn_s, kdc_s, vdc_s, kmn_s, vmn_s, kmc_s, vmc_s):
    j = pl.program_id(1)
    seq = ka_ref.shape[0]

    @pl.when(j == 0)
    def _():
        outs, _, _ = _retention(ra_ref, seq, decf_ref, decb_ref, s0f_ref[...], s0b_ref[...])
        for n, o in enumerate(outs):
            ret_s[n * RET_CHUNK:(n + 1) * RET_CHUNK, :] = o
        kdn_s[...] = ka_ref[:, KA_DK:KA_DV].astype(BF16)
        vdn_s[...] = ka_ref[:, KA_DV:KA_CKV].astype(BF16)
        kdc_s[...] = ckd_ref[...].astype(BF16)
        vdc_s[...] = cvd_ref[...].astype(BF16)
        km, vm = _mla_keys(ka_ref[:, KA_CKV:KA_KPE], _kr_only(ka_ref[:, KA_KPE:KA_W]), wk_ref, wv_ref)
        kmn_s[...] = km
        vmn_s[...] = vm
        kr_ctx = _dot(ckpe_ref[...].astype(BF16), place_ref[...])
        km, vm = _mla_keys(cckv_ref[...], kr_ctx, wk_ref, wv_ref)
        kmc_s[...] = km
        vmc_s[...] = vm

    row0 = pl.multiple_of(j * TM, TM)
    mix_ref[:, 0:RET_W] = ret_s[pl.ds(row0, TM), :].astype(BF16)

    bd_ones = jnp.where(_block_diag_ones(DIFF_V_W, DIFF_V_DIM), 1.0, 0.0).astype(BF16)
    lam = _diff_lambda(dl_ref, lam_init)
    kd_ctx = [kdc_s[hd] for hd in range(DIFF_HEADS)]
    kd_new = [kdn_s[:, hd * SLOT:(hd + 1) * SLOT] for hd in range(DIFF_HEADS)]
    mix_ref[:, RET_W:RET_W + DIFF_V_W] = _diff_attention(
        qa_ref[:, 0:4 * SLOT], [kd_ctx, kd_new], [vdc_s[...], vdn_s[...]], lam, subln_ref[...],
        lam_init, bd_ones).astype(BF16)

    halves = _mla_attention(qa_ref[:, 4 * SLOT:QA_W], [kmc_s[...], kmn_s[...]], [vmc_s[...], vmn_s[...]])
    mix_ref[:, 512:768] = halves[0].astype(BF16)
    mix_ref[:, 768:1024] = halves[1].astype(BF16)


def _mix_sample_call(lam_init, layer, qa, ka, ra, tok0, n_seq, seq_len, past_len, cache_dk, cache_dv_t,
                     cache_ckv, cache_kpe, s0f_bd, s0b_bd, decf, decb, dl, subln, wk, wv, place):
    nq = seq_len // TM
    q0 = tok0 // TM
    s0 = tok0 // seq_len
    const = lambda b, j: (0, 0)
    return pl.pallas_call(
        functools.partial(_mix_sample_kernel, lam_init),
        grid=(n_seq, nq),
        in_specs=[
            pl.BlockSpec((TM, QA_W), lambda b, j: (q0 + b * nq + j, 0)),
            pl.BlockSpec((seq_len, KA_W), lambda b, j: (s0 + b, 0)),
            pl.BlockSpec((seq_len, D_MODEL), lambda b, j: (s0 + b, 0)),
            pl.BlockSpec((None, None, DIFF_HEADS, past_len, 2 * DIFF_QK_DIM), lambda b, j: (b, layer, 0, 0, 0)),
            pl.BlockSpec((None, None, past_len, DIFF_V_W), lambda b, j: (b, layer, 0, 0)),
            pl.BlockSpec((None, None, past_len, MLA_KV_LORA), lambda b, j: (b, layer, 0, 0)),
            pl.BlockSpec((None, None, past_len, MLA_ROPE_DIM), lambda b, j: (b, layer, 0, 0)),
            pl.BlockSpec((None, None, RET_W, RET_W), lambda b, j: (b, layer, 0, 0)),
            pl.BlockSpec((None, None, RET_W, RET_W), lambda b, j: (b, layer, 0, 0)),
        ] + _mixer_param_specs(layer) + [
            pl.BlockSpec((MLA_ROPE_DIM, SLOT), const),
        ],
        out_specs=pl.BlockSpec((TM, MIX_W), lambda b, j: (b * nq + j, 0)),
        out_shape=jax.ShapeDtypeStruct((n_seq * seq_len, MIX_W), BF16),
        scratch_shapes=[
            pltpu.VMEM((seq_len, RET_W), F32),
            pltpu.VMEM((seq_len, 4 * SLOT), BF16),
            pltpu.VMEM((seq_len, DIFF_V_W), BF16),
            pltpu.VMEM((DIFF_HEADS, past_len, 2 * DIFF_QK_DIM), BF16),
            pltpu.VMEM((past_len, DIFF_V_W), BF16),
            pltpu.VMEM((seq_len, MLA_HEADS * SLOT), BF16),
            pltpu.VMEM((seq_len, MLA_V_W), BF16),
            pltpu.VMEM((past_len, MLA_HEADS * SLOT), BF16),
            pltpu.VMEM((past_len, MLA_V_W), BF16),
        ],
        compiler_params=_cparams(("arbitrary", "arbitrary")),
        name="mix_sample",
    )(qa, ka, ra, cache_dk, cache_dv_t, cache_ckv, cache_kpe, s0f_bd, s0b_bd,
      decf, decb, dl, subln, wk, wv, place)


def _route(h2, rwt_ref, rb_ref):
    tm = h2.shape[0]
    neg = -jnp.inf
    logits = _split_dot_nt(rwt_ref[...], h2)
    sc = jax.nn.sigmoid(logits)
    sel = sc + rb_ref[...]
    member = lax.broadcasted_iota(I32, (GROUP_SIZE, tm), 0).astype(F32)
    gscore = []
    for g in range(N_GROUPS):
        sg = sel[g * GROUP_SIZE:(g + 1) * GROUP_SIZE, :]
        m1 = jnp.max(sg, axis=0, keepdims=True)
        f1 = jnp.min(jnp.where(sg == m1, member, float(GROUP_SIZE)), axis=0, keepdims=True)
        m2 = jnp.max(jnp.where(member == f1, neg, sg), axis=0, keepdims=True)
        gscore.append(m1 + m2)
    gsel = [jnp.zeros((1, tm), F32) for _ in range(N_GROUPS)]
    for _ in range(TOPK_GROUPS):
        mx = gscore[0]
        for g in range(1, N_GROUPS):
            mx = jnp.maximum(mx, gscore[g])
        fi = jnp.full((1, tm), float(N_GROUPS), F32)
        for g in range(N_GROUPS - 1, -1, -1):
            fi = jnp.where(gscore[g] == mx, float(g), fi)
        for g in range(N_GROUPS):
            hit = fi == float(g)
            gsel[g] = jnp.where(hit, 1.0, gsel[g])
            gscore[g] = jnp.where(hit, neg, gscore[g])
    cand = jnp.concatenate(
        [jnp.where(gsel[g] > 0.0, sel[g * GROUP_SIZE:(g + 1) * GROUP_SIZE, :], neg) for g in range(N_GROUPS)],
        axis=0)
    flat = lax.broadcasted_iota(I32, (N_EXPERTS, tm), 0).astype(F32)
    hits, gts = [], []
    chosen = jnp.zeros((N_EXPERTS, tm), F32)
    for _ in range(TOP_K):
        mx = jnp.max(cand, axis=0, keepdims=True)
        fk = jnp.min(jnp.where(cand == mx, flat, float(N_EXPERTS)), axis=0, keepdims=True)
        hit = flat == fk
        hits.append(hit)
        gts.append(jnp.sum(jnp.where(hit, sc, 0.0), axis=0, keepdims=True))
        chosen = jnp.where(hit, 1.0, chosen)
        cand = jnp.where(hit, neg, cand)
    gsum = gts[0]
    for g in gts[1:]:
        gsum = gsum + g
    gts = [g / gsum * ROUTED_SCALE for g in gts]

    before = (lax.broadcasted_iota(I32, (tm, tm), 0) < lax.broadcasted_iota(I32, (tm, tm), 1))
    rank_in = _dot(chosen.astype(BF16), jnp.where(before, 1.0, 0.0).astype(BF16))
    cnt = jnp.sum(chosen, axis=1, keepdims=True)
    cnt_pad = jnp.floor((cnt + (CH - 1.0)) * (1.0 / CH)) * CH
    below = (lax.broadcasted_iota(I32, (N_EXPERTS, N_EXPERTS), 1) < lax.broadcasted_iota(I32, (N_EXPERTS, N_EXPERTS), 0))
    start = _dot(jnp.where(below, 1.0, 0.0).astype(BF16),
                 jnp.broadcast_to(cnt_pad, (N_EXPERTS, LANES)).astype(BF16))[:, 0:1]
    pos = rank_in + start
    lpos = [jnp.sum(jnp.where(hit, pos, 0.0), axis=0, keepdims=True) for hit in hits]
    return lpos, gts, cnt_pad, start


def _slot_rows(vals, n_rows):
    tm = vals[0].shape[1]
    row = lax.broadcasted_iota(I32, (n_rows, tm), 0)
    out = jnp.zeros((n_rows, tm), F32)
    for k, v in enumerate(vals):
        out = jnp.where(row == k, v, out)
    return out


def _post_kernel(tbl_ref, xa_ref, xb_ref, mp_ref, ms_ref, mod_ref, wout_ref, g2_ref, shg_ref, shu_ref, shd_ref,
                 rwt_ref, rb_ref, base_ref, h2_ref, lpos_ref, ptok_ref, gtok_ref, cnt_ref, start_ref, rel_ref,
                 run_ref):
    i = pl.program_id(0)
    n_prompt_tiles = tbl_ref[2, 0]

    @pl.when(i == 0)
    def _():
        cnt_ref[...] = jnp.zeros_like(cnt_ref)
        start_ref[...] = jnp.zeros_like(start_ref)
        rel_ref[...] = jnp.zeros_like(rel_ref)
        run_ref[...] = jnp.zeros_like(run_ref)

    mod = mod_ref[...]
    gate1 = mod[:, 2 * D_MODEL:3 * D_MODEL]
    shift2 = mod[:, 3 * D_MODEL:4 * D_MODEL]
    scale2 = mod[:, 4 * D_MODEL:5 * D_MODEL]
    gate2 = mod[:, 5 * D_MODEL:6 * D_MODEL]
    mix = jnp.where(i < n_prompt_tiles, mp_ref[...], ms_ref[...])
    x1 = _tile_x(tbl_ref, xa_ref, xb_ref) + gate1 * _dot(mix, wout_ref[...])
    ms = jnp.mean(x1 * x1, axis=-1, keepdims=True)
    h2 = x1 * lax.rsqrt(ms + NORM_EPS) * g2_ref[...]
    h2 = h2 * (1.0 + scale2) + shift2
    hb = h2.astype(BF16)
    h2_ref[...] = hb
    act = _silu(_dot(hb, shg_ref[...])) * _dot(hb, shu_ref[...])
    base_ref[...] = x1 + gate2 * _dot(act.astype(BF16), shd_ref[...])

    lpos, gts, cnt_pad, start = _route(h2, rwt_ref, rb_ref)
    lpos_ref[...] = _slot_rows(lpos, SLOT_ROWS)
    ptok_ref[...] = _slot_rows(lpos, LANES).T
    gtok_ref[...] = _slot_rows(gts, LANES).T
    col = lax.broadcasted_iota(I32, cnt_ref.shape, 1)
    run = run_ref[...]
    cnt_ref[...] = jnp.where(col == i, cnt_pad.astype(I32), cnt_ref[...])
    start_ref[...] = jnp.where(col == i, start.astype(I32), start_ref[...])
    rel_ref[...] = jnp.where(col == i, run.astype(I32), rel_ref[...])
    run_ref[...] = run + cnt_pad


def _post_call(layer, tbl, xa, xb, mix_p, mix_s, mod4, wout, g2, shg, shu, shd, rwt, rb):
    npt = mix_p.shape[0] // TM
    n_tok = mix_p.shape[0] + mix_s.shape[0]
    nt = n_tok // TM
    const = lambda i, t: (0, 0)
    tile = lambda i, t: (i, 0)
    gs = pltpu.PrefetchScalarGridSpec(
        num_scalar_prefetch=1,
        grid=(nt,),
        in_specs=_x_specs(npt, xa is xb) + [
            pl.BlockSpec((TM, MIX_W), lambda i, t: (jnp.minimum(i, npt - 1), 0)),
            pl.BlockSpec((TM, MIX_W), lambda i, t: (jnp.maximum(i - npt, 0), 0)),
            _mod_spec(layer),
            _layer_spec(layer, MIX_W, D_MODEL),
            _layer_spec(layer, 1, D_MODEL),
            _layer_spec(layer, D_MODEL, EXPERT_FF),
            _layer_spec(layer, D_MODEL, EXPERT_FF),
            _layer_spec(layer, EXPERT_FF, D_MODEL),
            _layer_spec(layer, N_EXPERTS, D_MODEL),
            _layer_spec(layer, N_EXPERTS, 1),
        ],
        out_specs=[
            pl.BlockSpec((TM, D_MODEL), tile),
            pl.BlockSpec((TM, D_MODEL), tile),
            pl.BlockSpec((SLOT_ROWS, TM), lambda i, t: (0, i)),
            pl.BlockSpec((TM, LANES), tile),
            pl.BlockSpec((TM, LANES), tile),
            pl.BlockSpec((N_EXPERTS, LANES), const),
            pl.BlockSpec((N_EXPERTS, LANES), const),
            pl.BlockSpec((N_EXPERTS, LANES), const),
        ],
        scratch_shapes=[pltpu.VMEM((N_EXPERTS, 1), F32)],
    )
    assert nt <= LANES
    return pl.pallas_call(
        _post_kernel,
        grid_spec=gs,
        out_shape=[
            jax.ShapeDtypeStruct((n_tok, D_MODEL), F32),
            jax.ShapeDtypeStruct((n_tok, D_MODEL), BF16),
            jax.ShapeDtypeStruct((SLOT_ROWS, n_tok), F32),
            jax.ShapeDtypeStruct((n_tok, LANES), F32),
            jax.ShapeDtypeStruct((n_tok, LANES), F32),
            jax.ShapeDtypeStruct((N_EXPERTS, LANES), I32),
            jax.ShapeDtypeStruct((N_EXPERTS, LANES), I32),
            jax.ShapeDtypeStruct((N_EXPERTS, LANES), I32),
        ],
        compiler_params=_cparams(("arbitrary",)),
        name="post_route",
    )(tbl, xa, xb, mix_p, mix_s, mod4, wout, g2, shg, shu, shd, rwt, rb)


SEG_ROW0, SEG_NBLK, SEG_PAD0, SEG_NPAD, SEG_NEXT, SEG_USED = range(6)


def _plan_kernel(last_tile, cnt_ref, rel_ref, seg_ref):
    def per_expert(e, start):
        end = start + rel_ref[e, last_tile] + cnt_ref[e, last_tile]
        nb = lax.shift_right_logical(end - start + (EB - 1), LOG_EB)
        nxt = start + lax.shift_left(nb, LOG_EB)
        seg_ref[SEG_ROW0, e] = start
        seg_ref[SEG_NBLK, e] = nb
        seg_ref[SEG_PAD0, e] = end
        seg_ref[SEG_NPAD, e] = lax.shift_right_logical(nxt - end, LOG_CH)
        seg_ref[SEG_USED, e] = 0
        return nxt

    total = lax.fori_loop(0, N_EXPERTS, per_expert, jnp.int32(0))

    def link(k, nxt):
        e = N_EXPERTS - 1 - k
        seg_ref[SEG_NEXT, e] = nxt
        return jnp.where(seg_ref[SEG_NBLK, e] > 0, e, nxt)

    first = lax.fori_loop(0, N_EXPERTS, link, jnp.int32(N_EXPERTS))
    seg_ref[SEG_USED, 0] = lax.shift_right_logical(total, LOG_EB)
    seg_ref[SEG_USED, 1] = first


def _plan_call(cnt, rel, nt):
    smem = pl.BlockSpec(memory_space=pltpu.SMEM)
    return pl.pallas_call(
        functools.partial(_plan_kernel, nt - 1),
        in_specs=[smem, smem],
        out_specs=smem,
        out_shape=jax.ShapeDtypeStruct((6, N_EXPERTS), I32),
        name="moe_plan",
    )(cnt, rel)


def _rows_copy(src_ref, src_row, dst_ref, dst_row, n_rows, sem):
    return pltpu.make_async_copy(src_ref.at[pl.ds(pl.multiple_of(src_row, CH), n_rows)],
                                 dst_ref.at[pl.ds(pl.multiple_of(dst_row, CH), n_rows)], sem)


class _Runs:
    def __init__(self, cnt_ref, start_ref, rel_ref, seg_ref):
        self.cnt, self.start, self.rel, self.seg = cnt_ref, start_ref, rel_ref, seg_ref

    def start_copies(self, i, copy, tot_ref, slot):
        def per_pair(e2, carry):
            n_big, n_small = carry
            for par in range(2):
                e = 2 * e2 + par
                c = self.cnt[e, i]
                a0 = self.start[e, i]
                b0 = self.seg[SEG_ROW0, e] + self.rel[e, i]
                nb = lax.shift_right_logical(c, LOG_CH + 1)
                odd = jnp.bitwise_and(lax.shift_right_logical(c, LOG_CH), 1)

                def big(q, cc, a0=a0, b0=b0, par=par):
                    copy(a0 + q * (2 * CH), b0 + q * (2 * CH), 2 * CH).start(priority=par)
                    return cc

                lax.fori_loop(0, nb, big, 0)

                @pl.when(odd == 1)
                def _(a0=a0, b0=b0, nb=nb, par=par):
                    copy(a0 + nb * (2 * CH), b0 + nb * (2 * CH), CH).start(priority=par)

                n_big, n_small = n_big + nb, n_small + odd
            return n_big, n_small

        n_big, n_small = lax.fori_loop(0, N_EXPERTS // 2, per_pair, (jnp.int32(0), jnp.int32(0)))
        tot_ref[slot, 0] = n_big
        tot_ref[slot, 1] = n_small

    @staticmethod
    def wait_copies(copy, tot_ref, slot):
        def big(q, c):
            copy(0, 0, 2 * CH).wait()
            return c

        lax.fori_loop(0, tot_ref[slot, 0], big, 0)

        def small(q, c):
            copy(0, 0, CH).wait()
            return c

        lax.fori_loop(0, tot_ref[slot, 1], small, 0)


def _dispatch_kernel(reuse, cnt_ref, start_ref, rel_ref, seg_ref, h_ref, lpos_ref, *rest):
    xb_hbm, sort_s, zero_s, tot_s, sems = rest[1:] if reuse else rest
    i = pl.program_id(0)
    last = pl.num_programs(0) - 1
    slot = lax.rem(i, 2)
    runs = _Runs(cnt_ref, start_ref, rel_ref, seg_ref)
    lp = lpos_ref[...]
    hb = h_ref[...]
    blk = TM
    for r in range(SORT_ROWS // blk):
        srow = (lax.broadcasted_iota(I32, (blk, TM), 0) + r * blk).astype(F32)
        p = jnp.zeros((blk, TM), F32)
        for k in range(TOP_K):
            p = jnp.where(srow == lp[k:k + 1, :], 1.0, p)
        sort_s[slot, r * blk:(r + 1) * blk, :] = _dot(p.astype(BF16), hb).astype(BF16)

    def copy_from(sl):
        return lambda s, d, n: _rows_copy(sort_s.at[sl], s, xb_hbm, d, n, sems.at[sl])

    runs.start_copies(i, copy_from(slot), tot_s, slot)

    @pl.when(i > 0)
    def _():
        runs.wait_copies(copy_from(1 - slot), tot_s, 1 - slot)

    @pl.when(i == last)
    def _():
        runs.wait_copies(copy_from(slot), tot_s, slot)
        _zero_fill_unused(seg_ref, xb_hbm, zero_s, sems.at[0], tail=not reuse)


def _zero_fill_unused(seg_ref, buf_hbm, zero_s, sem, tail):
    zero_s[...] = jnp.zeros_like(zero_s)

    def per_expert(e, c):
        first = seg_ref[SEG_PAD0, e]

        def z_issue(r, cc):
            _rows_copy(zero_s, 0, buf_hbm, first + r * CH, CH, sem).start()
            return cc

        lax.fori_loop(0, seg_ref[SEG_NPAD, e], z_issue, 0)

        def z_drain(r, cc):
            _rows_copy(zero_s, 0, buf_hbm, 0, CH, sem).wait()
            return cc

        lax.fori_loop(0, seg_ref[SEG_NPAD, e], z_drain, 0)
        return c

    lax.fori_loop(0, N_EXPERTS, per_expert, 0)
    if tail:
        _zero_fill_tail(seg_ref, buf_hbm, zero_s, sem)


def _zero_fill_tail(seg_ref, buf_hbm, zero_s, sem):
    n_blocks = buf_hbm.shape[0] // EB

    def blk_copy(b):
        return pltpu.make_async_copy(zero_s, buf_hbm.at[pl.ds(pl.multiple_of(b * EB, EB), EB)], sem)

    def t_issue(b, cc):
        blk_copy(b).start()
        return cc

    lax.fori_loop(seg_ref[SEG_USED, 0], n_blocks, t_issue, 0)

    def t_drain(b, cc):
        blk_copy(0).wait()
        return cc

    lax.fori_loop(seg_ref[SEG_USED, 0], n_blocks, t_drain, 0)


def _dispatch_call(cnt, start, rel, seg, h2, lpos, n_rows, prev=None):
    n_tok = h2.shape[0]
    nt = n_tok // TM
    smem = pl.BlockSpec(memory_space=pltpu.SMEM)
    reuse = prev is not None
    return pl.pallas_call(
        functools.partial(_dispatch_kernel, reuse),
        grid=(nt,),
        in_specs=[
            smem, smem, smem, smem,
            pl.BlockSpec((TM, D_MODEL), lambda i: (i, 0)),
            pl.BlockSpec((SLOT_ROWS, TM), lambda i: (0, i)),
        ] + ([pl.BlockSpec(memory_space=pl.ANY)] if reuse else []),
        input_output_aliases={6: 0} if reuse else {},
        out_specs=pl.BlockSpec(memory_space=pl.ANY),
        out_shape=jax.ShapeDtypeStruct((n_rows, D_MODEL), BF16),
        scratch_shapes=[
            pltpu.VMEM((2, SORT_ROWS, D_MODEL), BF16),
            pltpu.VMEM((EB, D_MODEL), BF16),
            pltpu.SMEM((2, 2), I32),
            pltpu.SemaphoreType.DMA((2,)),
        ],
        compiler_params=_cparams(("arbitrary",)),
        name="moe_dispatch",
    )(cnt, start, rel, seg, h2, lpos, *([prev] if reuse else []))


X_SLOTS = 4


def _experts_kernel(layer, seg_ref, wg_hbm, wu_hbm, wd_hbm, xb_hbm, yb_hbm,
                    wgf_s, wuf_s, wdf_s, wg_s, wu_s, wd_s, x_s, y_s, sem_w, sem_x, sem_y):
    n_used = seg_ref[SEG_USED, 0]
    first = seg_ref[SEG_USED, 1]

    def rows(g):
        return pl.ds(pl.multiple_of(g * EB, EB), EB)

    def x_copy(g, slot):
        return pltpu.make_async_copy(xb_hbm.at[rows(g)], x_s.at[slot], sem_x.at[slot])

    def y_copy(g, slot):
        return pltpu.make_async_copy(y_s.at[slot], yb_hbm.at[rows(g)], sem_y.at[slot])

    def w_copies(e, slot):
        return [pltpu.make_async_copy(hbm.at[layer, e], buf.at[slot], sem_w.at[slot, n])
                for n, (hbm, buf) in enumerate(((wg_hbm, wgf_s), (wu_hbm, wuf_s), (wd_hbm, wdf_s)))]

    def fetch_weights(e, slot):
        for c in w_copies(e, slot):
            c.start()

    def take_weights(slot):
        for c in w_copies(0, slot):
            c.wait()
        wg_s[...] = wgf_s[slot].astype(BF16)
        wu_s[...] = wuf_s[slot].astype(BF16)
        wd_s[...] = wdf_s[slot].astype(BF16)

    def next_expert(e):
        return seg_ref[SEG_NEXT, jnp.minimum(e, N_EXPERTS - 1)]

    @pl.when(n_used > 0)
    def _():
        for p in range(X_SLOTS - 1):
            @pl.when(p < n_used)
            def _(p=p):
                x_copy(p, p).start()

        fetch_weights(first, 0)

        @pl.when(next_expert(first) < N_EXPERTS)
        def _():
            fetch_weights(next_expert(first), 1)

        take_weights(0)

        def block(g, carry):
            e, left, wslot = carry
            ahead = g + (X_SLOTS - 1)

            @pl.when(ahead < n_used)
            def _():
                x_copy(ahead, lax.rem(ahead, X_SLOTS)).start()

            xslot = lax.rem(g, X_SLOTS)
            yslot = lax.rem(g, 2)
            x_copy(g, xslot).wait()

            @pl.when(g >= 2)
            def _():
                y_copy(g - 2, yslot).wait()

            xb = x_s[xslot]
            act = _silu(_dot(xb, wg_s[...])) * _dot(xb, wu_s[...])
            y_s[yslot] = _dot(act.astype(BF16), wd_s[...]).astype(BF16)
            y_copy(g, yslot).start()

            switch = jnp.logical_and(left == 1, g + 1 < n_used)
            nxt = next_expert(e)

            @pl.when(switch)
            def _():
                take_weights(1 - wslot)

                @pl.when(next_expert(nxt) < N_EXPERTS)
                def _():
                    fetch_weights(next_expert(nxt), wslot)

            nxt_c = jnp.minimum(nxt, N_EXPERTS - 1)
            return (jnp.where(switch, nxt_c, e), jnp.where(switch, seg_ref[SEG_NBLK, nxt_c], left - 1),
                    jnp.where(switch, 1 - wslot, wslot))

        lax.fori_loop(0, n_used, block,
                      (first, seg_ref[SEG_NBLK, jnp.minimum(first, N_EXPERTS - 1)], jnp.int32(0)))

        @pl.when(n_used >= 2)
        def _():
            y_copy(n_used - 2, lax.rem(n_used, 2)).wait()

        y_copy(n_used - 1, lax.rem(n_used - 1, 2)).wait()


def _experts_call(seg, xb, layer, wg, wu, wd):
    n_rows = xb.shape[0]
    hbm = pl.BlockSpec(memory_space=pl.ANY)
    return pl.pallas_call(
        functools.partial(_experts_kernel, layer),
        in_specs=[pl.BlockSpec(memory_space=pltpu.SMEM), hbm, hbm, hbm, hbm],
        out_specs=hbm,
        out_shape=jax.ShapeDtypeStruct((n_rows, D_MODEL), BF16),
        input_output_aliases={4: 0},
        scratch_shapes=[
            pltpu.VMEM((2, D_MODEL, EXPERT_FF), F32),
            pltpu.VMEM((2, D_MODEL, EXPERT_FF), F32),
            pltpu.VMEM((2, EXPERT_FF, D_MODEL), F32),
            pltpu.VMEM((D_MODEL, EXPERT_FF), BF16),
            pltpu.VMEM((D_MODEL, EXPERT_FF), BF16),
            pltpu.VMEM((EXPERT_FF, D_MODEL), BF16),
            pltpu.VMEM((X_SLOTS, EB, D_MODEL), BF16),
            pltpu.VMEM((2, EB, D_MODEL), BF16),
            pltpu.SemaphoreType.DMA((2, 3)),
            pltpu.SemaphoreType.DMA((X_SLOTS,)),
            pltpu.SemaphoreType.DMA((2,)),
        ],
        compiler_params=pltpu.CompilerParams(vmem_limit_bytes=VMEM_LIMIT),
        name="moe_experts",
    )(seg, wg, wu, wd, xb)


def _combine_kernel(final, tbl_ref, cnt_ref, start_ref, rel_ref, seg_ref, yb_hbm, base_ref, gtok_ref, ptok_ref,
                    mod_ref, gf_ref, *rest):
    *out_refs, sort_s, tot_s, sems = rest
    i = pl.program_id(0)
    slot = lax.rem(i, 2)
    runs = _Runs(cnt_ref, start_ref, rel_ref, seg_ref)

    def copy_to(sl):
        return lambda s, d, n: _rows_copy(yb_hbm, d, sort_s.at[sl], s, n, sems.at[sl])

    @pl.when(i == 0)
    def _():
        sort_s[...] = jnp.zeros_like(sort_s)
        runs.start_copies(i, copy_to(slot), tot_s, slot)

    @pl.when(i + 1 < pl.num_programs(0))
    def _():
        runs.start_copies(i + 1, copy_to(1 - slot), tot_s, 1 - slot)

    runs.wait_copies(copy_to(slot), tot_s, slot)

    gt = gtok_ref[...]
    pt = ptok_ref[...]
    col = lax.broadcasted_iota(I32, (TM, SORT_ROWS), 1).astype(F32)
    w = jnp.zeros((TM, SORT_ROWS), F32)
    for k in range(TOP_K):
        w = jnp.where(col == pt[:, k:k + 1], gt[:, k:k + 1], w)
    wh = w.astype(BF16)
    wl = (w - wh.astype(F32)).astype(BF16)
    ys = sort_s[slot]
    routed = _dot(wh, ys) + _dot(wl, ys)
    gate2 = mod_ref[...][:, 5 * D_MODEL:6 * D_MODEL]
    y = base_ref[...] + gate2 * routed
    if final:
        y = y * lax.rsqrt(jnp.mean(y * y, axis=-1, keepdims=True) + NORM_EPS) * gf_ref[...]
        yp_ref, ys_ref = out_refs
        is_context = i < tbl_ref[2, 0]

        @pl.when(is_context)
        def _():
            yp_ref[...] = y

        @pl.when(jnp.logical_not(is_context))
        def _():
            ys_ref[...] = y
    else:
        out_refs[0][...] = y


def _combine_call(final, layer, npt, tbl, cnt, start, rel, seg, yb, base, gtok, ptok, mod4, gfinal):
    n_tok = base.shape[0]
    nt = n_tok // TM
    tile = lambda i, t: (i, 0)
    smem = pl.BlockSpec(memory_space=pltpu.SMEM)
    if final:
        out_specs = [pl.BlockSpec((TM, D_MODEL), lambda i, t: (jnp.minimum(i, npt - 1), 0)),
                     pl.BlockSpec((TM, D_MODEL), lambda i, t: (jnp.maximum(i - npt, 0), 0))]
        out_shape = [jax.ShapeDtypeStruct((npt * TM, D_MODEL), F32),
                     jax.ShapeDtypeStruct((n_tok - npt * TM, D_MODEL), F32)]
    else:
        out_specs = pl.BlockSpec((TM, D_MODEL), tile)
        out_shape = jax.ShapeDtypeStruct((n_tok, D_MODEL), F32)
    gs = pltpu.PrefetchScalarGridSpec(
        num_scalar_prefetch=1,
        grid=(nt,),
        in_specs=[
            smem, smem, smem, smem,
            pl.BlockSpec(memory_space=pl.ANY),
            pl.BlockSpec((TM, D_MODEL), tile),
            pl.BlockSpec((TM, LANES), tile),
            pl.BlockSpec((TM, LANES), tile),
            _mod_spec(layer),
            pl.BlockSpec((1, D_MODEL), lambda i, t: (0, 0)),
        ],
        out_specs=out_specs,
        scratch_shapes=[
            pltpu.VMEM((2, SORT_ROWS, D_MODEL), BF16),
            pltpu.SMEM((2, 2), I32),
            pltpu.SemaphoreType.DMA((2,)),
        ],
    )
    return pl.pallas_call(
        functools.partial(_combine_kernel, final),
        grid_spec=gs,
        out_shape=out_shape,
        compiler_params=_cparams(("arbitrary",)),
        name="moe_combine",
    )(tbl, cnt, start, rel, seg, yb, base, gtok, ptok, mod4, gfinal)


def _pad_cols(w, groups, width, slot):
    lead = w.shape[:-1]
    w = w.reshape(*lead, groups, width)
    pad = [(0, 0)] * (len(lead) + 1) + [(0, slot - width)]
    return jnp.pad(w, pad).reshape(*lead, groups * slot)


def _prep_w_in(w):
    c = np.cumsum([0, 256, 256, 256, 256, 256, 256, 256, 256, 128, 32])
    rq, rk, rv, rg, dq, dk, dv, cq, ckv, kpe = [w[..., c[n]:c[n + 1]] for n in range(10)]
    kpe_slot = jnp.concatenate([kpe, jnp.zeros_like(kpe), kpe, jnp.zeros_like(kpe)], axis=-1)
    cols = [rq, rk * (RET_DIM ** -0.5), rv, rg, _pad_cols(dq, DIFF_HEADS, 2 * DIFF_QK_DIM, SLOT),
            _pad_cols(dk, DIFF_HEADS, 2 * DIFF_QK_DIM, SLOT), dv, cq, ckv, kpe_slot]
    return jnp.concatenate(cols, axis=-1).astype(BF16)


def _rope_tables(n_pos, dim, lane_offsets):
    n_rows = n_pos // GRID_W
    row = jnp.repeat(jnp.arange(n_rows, dtype=F32), GRID_W)
    col = jnp.tile(jnp.arange(GRID_W, dtype=F32), n_rows)
    half = dim // 2
    freqs = ROPE_THETA ** (-jnp.arange(0, half, 2, dtype=F32) / half)
    ar = row[:, None] * freqs[None, :]
    ac = col[:, None] * freqs[None, :]
    ang = jnp.concatenate([ar, ar, ac, ac], axis=-1)
    cos, sin = jnp.cos(ang), jnp.sin(ang)
    first = (np.arange(dim) % 16) < 8
    sa = jnp.where(first[None, :], -sin, 0.0)
    sb = jnp.where(first[None, :], 0.0, sin)
    c_t = jnp.ones((n_pos, SLOT), F32)
    a_t = jnp.zeros((n_pos, SLOT), F32)
    b_t = jnp.zeros((n_pos, SLOT), F32)
    for off in lane_offsets:
        c_t = c_t.at[:, off:off + dim].set(cos)
        a_t = a_t.at[:, off:off + dim].set(sa)
        b_t = b_t.at[:, off:off + dim].set(sb)
    ident = (jnp.ones((TM, SLOT), F32), jnp.zeros((TM, SLOT), F32), jnp.zeros((TM, SLOT), F32))
    return tuple(jnp.concatenate([i0, t], axis=0) for i0, t in zip(ident, (c_t, a_t, b_t)))


def _block_diag_states(s):
    b, l, h, dk, dv = s.shape
    eye = jnp.eye(h, dtype=s.dtype)
    return jnp.einsum('blhkv,hg->blhkgv', s, eye).reshape(b, l, h * dk, h * dv)


def kernel(x_prompt, x_sample, cache_diff_k, cache_diff_v, cache_mla_ckv, cache_mla_kpe, state_ret_fwd, state_ret_bwd, c, c_ctx, w_ada, b_ada, norm_mix, norm_ffn, norm_final, w_in, ret_decay_fwd, ret_decay_bwd, diff_lambda, diff_subln, mla_q_norm, mla_w_uq, mla_kv_norm, mla_w_ukv, w_out, router_w, router_bias, exp_w_gate, exp_w_up, exp_w_down, sh_w_gate, sh_w_up, sh_w_down):
    n_pb, p_len, _ = x_prompt.shape
    n_sb, s_len, _ = x_sample.shape
    past_len = cache_diff_k.shape[3]
    n_p = n_pb * p_len
    n_s = n_sb * s_len
    n_tok = n_p + n_s
    nt = n_tok // TM
    npt = n_p // TM
    assert p_len == TM and s_len % TM == 0 and n_p % s_len == 0 and past_len % 8 == 0

    tiles = np.arange(nt)
    mod_row = np.where(tiles < npt, n_sb, (tiles - npt) // (s_len // TM))
    rope_blk = np.where(tiles < npt, 0, 1 + (tiles - npt) % (s_len // TM))
    tbl = jnp.asarray(np.stack([mod_row, rope_blk, np.full(nt, npt)]).astype(np.int32))

    n_cond = 16
    cond = jnp.zeros((n_cond, D_MODEL), F32).at[:n_sb].set(c).at[n_sb].set(c_ctx)
    mod_all = _modulation(cond, w_ada, b_ada)

    rope_d = _rope_tables(s_len, DIFF_QK_DIM, (0, DIFF_QK_DIM))
    rope_m = _rope_tables(s_len, MLA_ROPE_DIM, (KR_LO,))
    place = jnp.zeros((MLA_ROPE_DIM, SLOT), F32).at[np.arange(MLA_ROPE_DIM), KR_LO + np.arange(MLA_ROPE_DIM)].set(1.0).astype(BF16)
    cache_dv_t = cache_diff_v.transpose(0, 1, 3, 2, 4).reshape(n_sb, DEPTH, past_len, DIFF_V_W)
    s0f_bd = _block_diag_states(state_ret_fwd)
    s0b_bd = _block_diag_states(state_ret_bwd)

    n_blocks = pl.cdiv(n_tok * TOP_K + nt * N_EXPERTS * (CH - 1) + N_EXPERTS * (EB - CH), EB)
    n_rows = n_blocks * EB

    xa, xb = x_prompt.reshape(n_p, D_MODEL), x_sample.reshape(n_s, D_MODEL)
    gfinal = norm_final.reshape(1, D_MODEL)

    mod4 = mod_all.reshape(DEPTH, n_cond, 1, 6 * D_MODEL)
    w_pre = _prep_w_in(w_in)
    wuq = _pad_cols(mla_w_uq, MLA_HEADS, MLA_NOPE_DIM + MLA_ROPE_DIM, SLOT).astype(BF16)
    ukv = mla_w_ukv.reshape(DEPTH, MLA_KV_LORA, MLA_HEADS, MLA_NOPE_DIM + MLA_V_DIM)
    wk = _pad_cols(ukv[..., :MLA_NOPE_DIM].reshape(DEPTH, MLA_KV_LORA, -1), MLA_HEADS, MLA_NOPE_DIM, SLOT).astype(BF16)
    wv = ukv[..., MLA_NOPE_DIM:].reshape(DEPTH, MLA_KV_LORA, MLA_V_W).astype(BF16)
    decf = jnp.repeat(ret_decay_fwd, RET_DIM, axis=-1).reshape(DEPTH, 1, RET_W)
    decb = jnp.repeat(ret_decay_bwd, RET_DIM, axis=-1).reshape(DEPTH, 1, RET_W)
    subln = jnp.tile(diff_subln, (1, DIFF_HEADS)).reshape(DEPTH, 1, DIFF_V_W)
    vec = lambda p: p.reshape(DEPTH, 1, -1)
    wout_b, shg_b, shu_b, shd_b = (w.astype(BF16) for w in (w_out, sh_w_gate, sh_w_up, sh_w_down))
    rwt = router_w.transpose(0, 2, 1)
    rb = router_bias.reshape(DEPTH, N_EXPERTS, 1)

    caches = []
    sorted_buf = None
    for l in range(DEPTH):
        lam_init = 0.8 - 0.6 * math.exp(-0.3 * l)
        ra, qa, ka = _pre_call(l, tbl, xa, xb, npt, n_tok, mod4, vec(norm_mix), w_pre, vec(mla_q_norm), wuq,
                               vec(mla_kv_norm), rope_d, rope_m)
        mix_p, sf, sb = _mix_prompt_call(lam_init, l, qa, ka, ra, n_pb, p_len, decf, decb,
                                         diff_lambda, subln, wk, wv)
        mix_s = _mix_sample_call(lam_init, l, qa, ka, ra, n_p, n_sb, s_len, past_len, cache_diff_k,
                                 cache_dv_t, cache_mla_ckv, cache_mla_kpe, s0f_bd, s0b_bd, decf, decb,
                                 diff_lambda, subln, wk, wv, place)
        base, h2, lpos, ptok, gtok, cnt, start, rel = _post_call(
            l, tbl, xa, xb, mix_p, mix_s, mod4, wout_b, vec(norm_ffn), shg_b, shu_b, shd_b, rwt, rb)
        seg = _plan_call(cnt, rel, nt)
        sorted_buf = _dispatch_call(cnt, start, rel, seg, h2, lpos, n_rows, prev=sorted_buf)
        sorted_buf = _experts_call(seg, sorted_buf, l, exp_w_gate, exp_w_up, exp_w_down)
        final = l == DEPTH - 1
        out = _combine_call(final, l, npt, tbl, cnt, start, rel, seg, sorted_buf, base, gtok, ptok, mod4, gfinal)
        xa, xb = out if final else (out, out)

        kp = ka[:n_p].reshape(n_pb, p_len, KA_W)
        dk = kp[:, :, KA_DK:KA_DV].reshape(n_pb, p_len, DIFF_HEADS, SLOT)[..., :2 * DIFF_QK_DIM]
        dv = kp[:, :, KA_DV:KA_CKV].reshape(n_pb, p_len, DIFF_HEADS, DIFF_V_DIM)
        diag = lambda s: jnp.stack([s[:, h * RET_DIM:(h + 1) * RET_DIM, h * RET_DIM:(h + 1) * RET_DIM]
                                    for h in range(RET_HEADS)], axis=1)
        caches.append((dk.transpose(0, 2, 1, 3), dv.transpose(0, 2, 1, 3), kp[:, :, KA_CKV:KA_KPE],
                       kp[:, :, KA_KPE:KA_KPE + MLA_ROPE_DIM], diag(sf), diag(sb)))

    y_prompt = xa.reshape(n_pb, p_len, D_MODEL)
    y_sample = xb.reshape(n_sb, s_len, D_MODEL)
    new = [jnp.stack([cs[n] for cs in caches], axis=1) for n in range(6)]
    return (y_prompt, y_sample, *new)
```

```python
import functools
import math

import numpy as np
import jax
import jax.numpy as jnp
from jax import lax
from jax.experimental import pallas as pl
from jax.experimental.pallas import tpu as pltpu

F32 = jnp.float32
BF16 = jnp.bfloat16
I32 = jnp.int32

D_MODEL = 1024
DEPTH = 2
GRID_W = 64
ROPE_THETA = 10000.0
NORM_EPS = 1e-6

RET_HEADS = 4
RET_DIM = 64
RET_CHUNK = 128
RET_W = RET_HEADS * RET_DIM
DIFF_HEADS = 4
DIFF_QK_DIM = 32
DIFF_V_DIM = 64
DIFF_V_W = DIFF_HEADS * DIFF_V_DIM
MLA_HEADS = 8
MLA_Q_LORA = 256
MLA_KV_LORA = 128
MLA_NOPE_DIM = 64
MLA_ROPE_DIM = 32
MLA_V_DIM = 64
MLA_V_W = MLA_HEADS * MLA_V_DIM
MIX_W = RET_W + DIFF_V_W + MLA_V_W

N_EXPERTS = 64
TOP_K = 6
N_GROUPS = 8
GROUP_SIZE = N_EXPERTS // N_GROUPS
TOPK_GROUPS = 4
EXPERT_FF = 256
ROUTED_SCALE = 2.5

LANES = 128
SLOT = LANES
TM = 256
SLOT_ROWS = 8
EB = 512
LOG_EB = 9
CH = 16
LOG_CH = 4
SORT_ROWS = 2560
VMEM_LIMIT = 48 * 1024 * 1024

C_RQ, C_RK, C_RV, C_RG = 0, 256, 512, 768
C_DQ = 1024
C_DK = 1536
C_DV = 2048
C_CQ = 2304
C_CKV = 2560
C_KPE = 2688
N_PRE = 2816
QA_W = 4 * SLOT + MLA_HEADS * SLOT
KA_DK, KA_DV, KA_CKV, KA_KPE = 0, 512, 768, 896
KA_W = 1024
KR_LO, KR_HI = 64, 96


def _dot(a, b):
    return jnp.dot(a, b, preferred_element_type=F32)


def _dot_nt(a, b):
    return lax.dot_general(a, b, (((1,), (1,)), ((), ())), preferred_element_type=F32)


def _dot_tn(a, b):
    return lax.dot_general(a, b, (((0,), (0,)), ((), ())), preferred_element_type=F32)


def _split_dot(x, w_bf16):
    hi = x.astype(BF16)
    lo = (x - hi.astype(F32)).astype(BF16)
    return _dot(hi, w_bf16) + _dot(lo, w_bf16)


def _split_dot_nt(w, x):
    wh = w.astype(BF16)
    wl = (w - wh.astype(F32)).astype(BF16)
    xh = x.astype(BF16)
    xl = (x - xh.astype(F32)).astype(BF16)
    return _dot_nt(wh, xh) + _dot_nt(wh, xl) + _dot_nt(wl, xh)


def _silu(x):
    return x * jax.nn.sigmoid(x)


def _cparams(sem):
    return pltpu.CompilerParams(dimension_semantics=sem, vmem_limit_bytes=VMEM_LIMIT)


MOD_TN = 512


def _mod_kernel(c_ref, w_ref, b_ref, o_ref):
    s = _silu(c_ref[...])
    o_ref[...] = _split_dot3(s, w_ref[...]) + b_ref[...]


def _split_dot3(x, w):
    xh = x.astype(BF16)
    xl = (x - xh.astype(F32)).astype(BF16)
    wh = w.astype(BF16)
    wl = (w - wh.astype(F32)).astype(BF16)
    return _dot(xh, wh) + _dot(xh, wl) + _dot(xl, wh)


def _modulation(cond, w_ada, b_ada):
    n_rows = cond.shape[0]
    n_out = w_ada.shape[-1]
    return pl.pallas_call(
        _mod_kernel,
        grid=(DEPTH, n_out // MOD_TN),
        in_specs=[
            pl.BlockSpec((n_rows, D_MODEL), lambda l, j: (0, 0)),
            pl.BlockSpec((None, D_MODEL, MOD_TN), lambda l, j: (l, 0, j)),
            pl.BlockSpec((None, 1, MOD_TN), lambda l, j: (l, 0, j)),
        ],
        out_specs=pl.BlockSpec((None, n_rows, MOD_TN), lambda l, j: (l, 0, j)),
        out_shape=jax.ShapeDtypeStruct((DEPTH, n_rows, n_out), F32),
        compiler_params=_cparams(("arbitrary", "arbitrary")),
        name="adaln_mod",
    )(cond, w_ada, b_ada.reshape(DEPTH, 1, n_out))


def _rope_slot(x, cos, sa, sb):
    up = pltpu.roll(x, LANES - 8, 1)
    dn = pltpu.roll(x, 8, 1)
    return x * cos + up * sa + dn * sb


def _tile_x(tbl_ref, xa_ref, xb_ref):
    return jnp.where(pl.program_id(0) < tbl_ref[2, 0], xa_ref[...], xb_ref[...])


def _x_specs(npt, combined):
    off = 0 if combined else npt
    return [pl.BlockSpec((TM, D_MODEL), lambda i, t: (jnp.minimum(i, npt - 1), 0)),
            pl.BlockSpec((TM, D_MODEL), lambda i, t: (jnp.maximum(i, npt) - off, 0))]


def _pre_kernel(tbl_ref, xa_ref, xb_ref, mod_ref, g_ref, w_ref, qg_ref, wuq_ref, kvg_ref,
                cd_ref, sad_ref, sbd_ref, cm_ref, sam_ref, sbm_ref,
                ra_ref, qa_ref, ka_ref):
    x = _tile_x(tbl_ref, xa_ref, xb_ref)
    mod = mod_ref[...]
    shift1 = mod[:, 0:D_MODEL]
    scale1 = mod[:, D_MODEL:2 * D_MODEL]
    ms = jnp.mean(x * x, axis=-1, keepdims=True)
    h = x * lax.rsqrt(ms + NORM_EPS) * g_ref[...]
    h = h * (1.0 + scale1) + shift1
    hb = h.astype(BF16)

    def proj(lo, hi):
        return _dot(hb, w_ref[:, lo:hi])

    ra_ref[:, 0:C_RG] = proj(C_RQ, C_RG).astype(BF16)
    ra_ref[:, C_RG:C_DQ] = _silu(proj(C_RG, C_DQ)).astype(BF16)

    cd, sad, sbd = cd_ref[...], sad_ref[...], sbd_ref[...]
    cm, sam, sbm = cm_ref[...], sam_ref[...], sbm_ref[...]
    dq = proj(C_DQ, C_DK)
    dk = proj(C_DK, C_DV)
    for hd in range(DIFF_HEADS):
        sl = slice(hd * SLOT, (hd + 1) * SLOT)
        qa_ref[:, sl] = _rope_slot(dq[:, sl], cd, sad, sbd).astype(BF16)
        ka_ref[:, KA_DK + hd * SLOT:KA_DK + (hd + 1) * SLOT] = _rope_slot(dk[:, sl], cd, sad, sbd)
    ka_ref[:, KA_DV:KA_CKV] = proj(C_DV, C_CQ)

    cq = proj(C_CQ, C_CKV)
    cqn = cq * lax.rsqrt(jnp.mean(cq * cq, axis=-1, keepdims=True) + NORM_EPS) * qg_ref[...]
    qm = _dot(cqn.astype(BF16), wuq_ref[...])
    for hd in range(MLA_HEADS):
        sl = slice(hd * SLOT, (hd + 1) * SLOT)
        qa_ref[:, 4 * SLOT + hd * SLOT:4 * SLOT + (hd + 1) * SLOT] = _rope_slot(qm[:, sl], cm, sam, sbm).astype(BF16)

    ckv = proj(C_CKV, C_KPE)
    ka_ref[:, KA_CKV:KA_KPE] = ckv * lax.rsqrt(jnp.mean(ckv * ckv, axis=-1, keepdims=True) + NORM_EPS) * kvg_ref[...]
    ka_ref[:, KA_KPE:KA_W] = _rope_slot(proj(C_KPE, N_PRE), cm, sam, sbm)


def _layer_spec(layer, rows, cols):
    return pl.BlockSpec((None, rows, cols), lambda *_: (layer, 0, 0))


def _mod_spec(layer):
    return pl.BlockSpec((None, None, 1, 6 * D_MODEL), lambda i, t: (layer, t[0, i], 0, 0))


def _pre_call(layer, tbl, xa, xb, npt, n_tok, mod4, g, w_pre, qg, wuq, kvg, rope_d, rope_m):
    nt = n_tok // TM
    tile = lambda i, t: (i, 0)
    rope = lambda i, t: (t[1, i], 0)
    gs = pltpu.PrefetchScalarGridSpec(
        num_scalar_prefetch=1,
        grid=(nt,),
        in_specs=_x_specs(npt, xa is xb) + [
            _mod_spec(layer),
            _layer_spec(layer, 1, D_MODEL),
            _layer_spec(layer, D_MODEL, N_PRE),
            _layer_spec(layer, 1, MLA_Q_LORA),
            _layer_spec(layer, MLA_Q_LORA, MLA_HEADS * SLOT),
            _layer_spec(layer, 1, MLA_KV_LORA),
        ] + [pl.BlockSpec((TM, SLOT), rope)] * 6,
        out_specs=[
            pl.BlockSpec((TM, D_MODEL), tile),
            pl.BlockSpec((TM, QA_W), tile),
            pl.BlockSpec((TM, KA_W), tile),
        ],
    )
    return pl.pallas_call(
        _pre_kernel,
        grid_spec=gs,
        out_shape=[
            jax.ShapeDtypeStruct((n_tok, D_MODEL), BF16),
            jax.ShapeDtypeStruct((n_tok, QA_W), BF16),
            jax.ShapeDtypeStruct((n_tok, KA_W), F32),
        ],
        compiler_params=_cparams(("arbitrary",)),
        name="pre_proj",
    )(tbl, xa, xb, mod4, g, w_pre, qg, wuq, kvg, *rope_d, *rope_m)


def _lane_iota(shape):
    return lax.broadcasted_iota(I32, shape, len(shape) - 1)


def _head_mask(n_rows, width, head, head_w):
    lane = _lane_iota((n_rows, width))
    return (lane >= head * head_w) & (lane < (head + 1) * head_w)


def _seg_mean_sq(o, bd_ones):
    return _split_dot(o * o, bd_ones) * (1.0 / RET_DIM)


def _block_diag_ones(n, blk):
    r = lax.broadcasted_iota(I32, (n, n), 0) // blk
    c = lax.broadcasted_iota(I32, (n, n), 1) // blk
    return r == c


def _retention(ra_ref, seq_len, decf_ref, decb_ref, s0f, s0b):
    C = RET_CHUNK
    nc = seq_len // C
    lgf = -jnp.exp(decf_ref[...])
    lgb = -jnp.exp(decb_ref[...])
    pos = lax.broadcasted_iota(I32, (C, RET_W), 0).astype(F32)
    qdf = jnp.exp((pos + 1.0) * lgf)
    kdf = jnp.exp((C - 1.0 - pos) * lgf)
    cdf = jnp.exp(float(C) * lgf)
    qdb = jnp.exp((C - pos) * lgb)
    kdb = jnp.exp(pos * lgb)
    cdb = jnp.exp(float(C) * lgb)
    ii = lax.broadcasted_iota(I32, (C, C), 0).astype(F32)
    jj = lax.broadcasted_iota(I32, (C, C), 1).astype(F32)
    dist = ii - jj
    dmats = []
    for hd in range(RET_HEADS):
        lf = lgf[:, hd * RET_DIM:hd * RET_DIM + 1]
        lb = lgb[:, hd * RET_DIM:hd * RET_DIM + 1]
        dmats.append(jnp.where(dist >= 0, jnp.exp(dist * lf), jnp.exp(-dist * lb)))
    bd = _block_diag_ones(RET_W, RET_DIM)
    bd_ones = jnp.where(bd, 1.0, 0.0).astype(BF16)

    def chunk(n):
        rows = slice(n * C, (n + 1) * C)
        return (ra_ref[rows, C_RQ:C_RK], ra_ref[rows, C_RK:C_RV], ra_ref[rows, C_RV:C_RG])

    cross = [None] * nc
    sf = s0f
    for n in range(nc):
        q, k, v = chunk(n)
        cross[n] = _dot((q * qdf).astype(BF16), sf.astype(BF16))
        kv = _dot_tn((k * kdf).astype(BF16), v.astype(BF16))
        sf = sf * cdf + jnp.where(bd, kv, 0.0)
    sb = s0b
    for n in range(nc - 1, -1, -1):
        q, k, v = chunk(n)
        cross[n] = cross[n] + _dot((q * qdb).astype(BF16), sb.astype(BF16))
        kv = _dot_tn((k * kdb).astype(BF16), v.astype(BF16))
        sb = sb * cdb + jnp.where(bd, kv, 0.0)

    outs = []
    for n in range(nc):
        q, k, v = chunk(n)
        kb = k.astype(BF16)
        vb = v.astype(BF16)
        o = cross[n]
        for hd in range(RET_HEADS):
            hm = _head_mask(C, RET_W, hd, RET_DIM)
            sc = _dot_nt(jnp.where(hm, q, 0.0).astype(BF16), kb) * dmats[hd]
            o = o + jnp.where(hm, _dot(sc.astype(BF16), vb), 0.0)
        on = o * lax.rsqrt(_seg_mean_sq(o, bd_ones) + NORM_EPS)
        outs.append(on * ra_ref[n * C:(n + 1) * C, C_RG:C_DQ])
    return outs, sf, sb


def _softmax_pv(s_parts, v_parts, scale):
    m = None
    for s in s_parts:
        mm = jnp.max(s, axis=-1, keepdims=True)
        m = mm if m is None else jnp.maximum(m, mm)
    m = m * scale
    acc = None
    den = None
    for s, v in zip(s_parts, v_parts):
        e = jnp.exp(s * scale - m)
        ds = jnp.sum(e, axis=-1, keepdims=True)
        pv = _dot(e.astype(BF16), v)
        acc = pv if acc is None else acc + pv
        den = ds if den is None else den + ds
    return acc / den


def _diff_attention(dq, k_parts, v_parts, lam, subln, lam_init, bd_ones):
    lq = dq.shape[0]
    scale = DIFF_QK_DIM ** -0.5
    lane = _lane_iota((lq, SLOT))
    out = jnp.zeros((lq, DIFF_V_W), F32)
    for hd in range(DIFF_HEADS):
        qh = dq[:, hd * SLOT:(hd + 1) * SLOT]
        q1 = jnp.where(lane < DIFF_QK_DIM, qh, 0.0).astype(BF16)
        q2 = jnp.where(lane >= DIFF_QK_DIM, qh, 0.0).astype(BF16)
        s1 = [_dot_nt(q1[:, :kp[hd].shape[1]], kp[hd]) for kp in k_parts]
        s2 = [_dot_nt(q2[:, :kp[hd].shape[1]], kp[hd]) for kp in k_parts]
        o = _softmax_pv(s1, v_parts, scale) - lam * _softmax_pv(s2, v_parts, scale)
        out = jnp.where(_head_mask(lq, DIFF_V_W, hd, DIFF_V_DIM), o, out)
    on = out * lax.rsqrt(_seg_mean_sq(out, bd_ones) + NORM_EPS) * subln
    return on * (1.0 - lam_init)


def _mla_attention(qm, k_parts, v_parts):
    lq = qm.shape[0]
    scale = (MLA_NOPE_DIM + MLA_ROPE_DIM) ** -0.5
    halves = []
    for g in range(2):
        out = jnp.zeros((lq, 256), F32)
        for hh in range(4):
            hd = 4 * g + hh
            qh = qm[:, hd * SLOT:(hd + 1) * SLOT].astype(BF16)
            s = [_dot_nt(qh, kp[:, hd * SLOT:(hd + 1) * SLOT]) for kp in k_parts]
            o = _softmax_pv(s, [vp[:, 256 * g:256 * (g + 1)] for vp in v_parts], scale)
            out = jnp.where(_head_mask(lq, 256, hh, MLA_V_DIM), o, out)
        halves.append(out)
    return halves


def _mla_keys(ka_val_ckv, kr_slot, wk_ref, wv_ref):
    cb = ka_val_ckv.astype(BF16)
    kn = _dot(cb, wk_ref[...])
    ks = [(kn[:, hd * SLOT:(hd + 1) * SLOT] + kr_slot).astype(BF16) for hd in range(MLA_HEADS)]
    return jnp.concatenate(ks, axis=1), _dot(cb, wv_ref[...]).astype(BF16)


def _kr_only(kpe_slot):
    lane = _lane_iota(kpe_slot.shape)
    return jnp.where((lane >= KR_LO) & (lane < KR_HI), kpe_slot, 0.0)


def _diff_lambda(dl_ref, lam_init):
    dl = dl_ref[...]
    a = jnp.sum(dl[0:1] * dl[1:2], axis=-1, keepdims=True)
    b = jnp.sum(dl[2:3] * dl[3:4], axis=-1, keepdims=True)
    return jnp.exp(a) - jnp.exp(b) + lam_init


def _mix_prompt_kernel(lam_init, qa_ref, ka_ref, ra_ref, decf_ref, decb_ref, dl_ref, subln_ref,
                       wk_ref, wv_ref, mix_ref, sf_ref, sb_ref):
    seq = qa_ref.shape[0]
    zero_state = jnp.zeros((RET_W, RET_W), F32)
    outs, sf, sb = _retention(ra_ref, seq, decf_ref, decb_ref, zero_state, zero_state)
    for n, o in enumerate(outs):
        mix_ref[n * RET_CHUNK:(n + 1) * RET_CHUNK, 0:RET_W] = o.astype(BF16)
    sf_ref[...] = sf
    sb_ref[...] = sb

    bd_ones = jnp.where(_block_diag_ones(DIFF_V_W, DIFF_V_DIM), 1.0, 0.0).astype(BF16)
    lam = _diff_lambda(dl_ref, lam_init)
    kd = [ka_ref[:, KA_DK + hd * SLOT:KA_DK + (hd + 1) * SLOT].astype(BF16) for hd in range(DIFF_HEADS)]
    vd = ka_ref[:, KA_DV:KA_CKV].astype(BF16)
    mix_ref[:, RET_W:RET_W + DIFF_V_W] = _diff_attention(
        qa_ref[:, 0:4 * SLOT], [kd], [vd], lam, subln_ref[...], lam_init, bd_ones).astype(BF16)

    km, vm = _mla_keys(ka_ref[:, KA_CKV:KA_KPE], _kr_only(ka_ref[:, KA_KPE:KA_W]), wk_ref, wv_ref)
    halves = _mla_attention(qa_ref[:, 4 * SLOT:QA_W], [km], [vm])
    mix_ref[:, 512:768] = halves[0].astype(BF16)
    mix_ref[:, 768:1024] = halves[1].astype(BF16)


def _mixer_param_specs(layer):
    return [
        _layer_spec(layer, 1, RET_W),
        _layer_spec(layer, 1, RET_W),
        _layer_spec(layer, 4, DIFF_QK_DIM),
        _layer_spec(layer, 1, DIFF_V_W),
        _layer_spec(layer, MLA_KV_LORA, MLA_HEADS * SLOT),
        _layer_spec(layer, MLA_KV_LORA, MLA_V_W),
    ]


def _mix_prompt_call(lam_init, layer, qa, ka, ra, n_seq, seq_len, decf, decb, dl, subln, wk, wv):
    seq = lambda b: (b, 0)
    return pl.pallas_call(
        functools.partial(_mix_prompt_kernel, lam_init),
        grid=(n_seq,),
        in_specs=[
            pl.BlockSpec((seq_len, QA_W), seq),
            pl.BlockSpec((seq_len, KA_W), seq),
            pl.BlockSpec((seq_len, D_MODEL), seq),
        ] + _mixer_param_specs(layer),
        out_specs=[
            pl.BlockSpec((seq_len, MIX_W), seq),
            pl.BlockSpec((None, RET_W, RET_W), lambda b: (b, 0, 0)),
            pl.BlockSpec((None, RET_W, RET_W), lambda b: (b, 0, 0)),
        ],
        out_shape=[
            jax.ShapeDtypeStruct((n_seq * seq_len, MIX_W), BF16),
            jax.ShapeDtypeStruct((n_seq, RET_W, RET_W), F32),
            jax.ShapeDtypeStruct((n_seq, RET_W, RET_W), F32),
        ],
        compiler_params=_cparams(("arbitrary",)),
        name="mix_prompt",
    )(qa, ka, ra, decf, decb, dl, subln, wk, wv)


def _mix_sample_kernel(lam_init, qa_ref, ka_ref, ra_ref, ckd_ref, cvd_ref, cckv_ref, ckpe_ref,
                       s0f_ref, s0b_ref, decf_ref, decb_ref, dl_ref, subln_ref, wk_ref, wv_ref,
                       place_ref, mix_ref,
                       ret_s, kdn_s, vdn_s, kdc_s, vdc_s, kmn_s, vmn_s, kmc_s, vmc_s):
    j = pl.program_id(1)
    seq = ka_ref.shape[0]

    @pl.when(j == 0)
    def _():
        outs, _, _ = _retention(ra_ref, seq, decf_ref, decb_ref, s0f_ref[...], s0b_ref[...])
        for n, o in enumerate(outs):
            ret_s[n * RET_CHUNK:(n + 1) * RET_CHUNK, :] = o
        kdn_s[...] = ka_ref[:, KA_DK:KA_DV].astype(BF16)
        vdn_s[...] = ka_ref[:, KA_DV:KA_CKV].astype(BF16)
        kdc_s[...] = ckd_ref[...].astype(BF16)
        vdc_s[...] = cvd_ref[...].astype(BF16)
        km, vm = _mla_keys(ka_ref[:, KA_CKV:KA_KPE], _kr_only(ka_ref[:, KA_KPE:KA_W]), wk_ref, wv_ref)
        kmn_s[...] = km
        vmn_s[...] = vm
        kr_ctx = _dot(ckpe_ref[...].astype(BF16), place_ref[...])
        km, vm = _mla_keys(cckv_ref[...], kr_ctx, wk_ref, wv_ref)
        kmc_s[...] = km
        vmc_s[...] = vm

    row0 = pl.multiple_of(j * TM, TM)
    mix_ref[:, 0:RET_W] = ret_s[pl.ds(row0, TM), :].astype(BF16)

    bd_ones = jnp.where(_block_diag_ones(DIFF_V_W, DIFF_V_DIM), 1.0, 0.0).astype(BF16)
    lam = _diff_lambda(dl_ref, lam_init)
    kd_ctx = [kdc_s[hd] for hd in range(DIFF_HEADS)]
    kd_new = [kdn_s[:, hd * SLOT:(hd + 1) * SLOT] for hd in range(DIFF_HEADS)]
    mix_ref[:, RET_W:RET_W + DIFF_V_W] = _diff_attention(
        qa_ref[:, 0:4 * SLOT], [kd_ctx, kd_new], [vdc_s[...], vdn_s[...]], lam, subln_ref[...],
        lam_init, bd_ones).astype(BF16)

    halves = _mla_attention(qa_ref[:, 4 * SLOT:QA_W], [kmc_s[...], kmn_s[...]], [vmc_s[...], vmn_s[...]])
    mix_ref[:, 512:768] = halves[0].astype(BF16)
    mix_ref[:, 768:1024] = halves[1].astype(BF16)


def _mix_sample_call(lam_init, layer, qa, ka, ra, tok0, n_seq, seq_len, past_len, cache_dk, cache_dv_t,
                     cache_ckv, cache_kpe, s0f_bd, s0b_bd, decf, decb, dl, subln, wk, wv, place):
    nq = seq_len // TM
    q0 = tok0 // TM
    s0 = tok0 // seq_len
    const = lambda b, j: (0, 0)
    return pl.pallas_call(
        functools.partial(_mix_sample_kernel, lam_init),
        grid=(n_seq, nq),
        in_specs=[
            pl.BlockSpec((TM, QA_W), lambda b, j: (q0 + b * nq + j, 0)),
            pl.BlockSpec((seq_len, KA_W), lambda b, j: (s0 + b, 0)),
            pl.BlockSpec((seq_len, D_MODEL), lambda b, j: (s0 + b, 0)),
            pl.BlockSpec((None, None, DIFF_HEADS, past_len, 2 * DIFF_QK_DIM), lambda b, j: (b, layer, 0, 0, 0)),
            pl.BlockSpec((None, None, past_len, DIFF_V_W), lambda b, j: (b, layer, 0, 0)),
            pl.BlockSpec((None, None, past_len, MLA_KV_LORA), lambda b, j: (b, layer, 0, 0)),
            pl.BlockSpec((None, None, past_len, MLA_ROPE_DIM), lambda b, j: (b, layer, 0, 0)),
            pl.BlockSpec((None, None, RET_W, RET_W), lambda b, j: (b, layer, 0, 0)),
            pl.BlockSpec((None, None, RET_W, RET_W), lambda b, j: (b, layer, 0, 0)),
        ] + _mixer_param_specs(layer) + [
            pl.BlockSpec((MLA_ROPE_DIM, SLOT), const),
        ],
        out_specs=pl.BlockSpec((TM, MIX_W), lambda b, j: (b * nq + j, 0)),
        out_shape=jax.ShapeDtypeStruct((n_seq * seq_len, MIX_W), BF16),
        scratch_shapes=[
            pltpu.VMEM((seq_len, RET_W), F32),
            pltpu.VMEM((seq_len, 4 * SLOT), BF16),
            pltpu.VMEM((seq_len, DIFF_V_W), BF16),
            pltpu.VMEM((DIFF_HEADS, past_len, 2 * DIFF_QK_DIM), BF16),
            pltpu.VMEM((past_len, DIFF_V_W), BF16),
            pltpu.VMEM((seq_len, MLA_HEADS * SLOT), BF16),
            pltpu.VMEM((seq_len, MLA_V_W), BF16),
            pltpu.VMEM((past_len, MLA_HEADS * SLOT), BF16),
            pltpu.VMEM((past_len, MLA_V_W), BF16),
        ],
        compiler_params=_cparams(("arbitrary", "arbitrary")),
        name="mix_sample",
    )(qa, ka, ra, cache_dk, cache_dv_t, cache_ckv, cache_kpe, s0f_bd, s0b_bd,
      decf, decb, dl, subln, wk, wv, place)


def _route(h2, rwt_ref, rb_ref):
    tm = h2.shape[0]
    neg = -jnp.inf
    logits = _split_dot_nt(rwt_ref[...], h2)
    sc = jax.nn.sigmoid(logits)
    sel = sc + rb_ref[...]
    member = lax.broadcasted_iota(I32, (GROUP_SIZE, tm), 0).astype(F32)
    gscore = []
    for g in range(N_GROUPS):
        sg = sel[g * GROUP_SIZE:(g + 1) * GROUP_SIZE, :]
        m1 = jnp.max(sg, axis=0, keepdims=True)
        f1 = jnp.min(jnp.where(sg == m1, member, float(GROUP_SIZE)), axis=0, keepdims=True)
        m2 = jnp.max(jnp.where(member == f1, neg, sg), axis=0, keepdims=True)
        gscore.append(m1 + m2)
    gsel = [jnp.zeros((1, tm), F32) for _ in range(N_GROUPS)]
    for _ in range(TOPK_GROUPS):
        mx = gscore[0]
        for g in range(1, N_GROUPS):
            mx = jnp.maximum(mx, gscore[g])
        fi = jnp.full((1, tm), float(N_GROUPS), F32)
        for g in range(N_GROUPS - 1, -1, -1):
            fi = jnp.where(gscore[g] == mx, float(g), fi)
        for g in range(N_GROUPS):
            hit = fi == float(g)
            gsel[g] = jnp.where(hit, 1.0, gsel[g])
            gscore[g] = jnp.where(hit, neg, gscore[g])
    cand = jnp.concatenate(
        [jnp.where(gsel[g] > 0.0, sel[g * GROUP_SIZE:(g + 1) * GROUP_SIZE, :], neg) for g in range(N_GROUPS)],
        axis=0)
    flat = lax.broadcasted_iota(I32, (N_EXPERTS, tm), 0).astype(F32)
    hits, gts = [], []
    chosen = jnp.zeros((N_EXPERTS, tm), F32)
    for _ in range(TOP_K):
        mx = jnp.max(cand, axis=0, keepdims=True)
        fk = jnp.min(jnp.where(cand == mx, flat, float(N_EXPERTS)), axis=0, keepdims=True)
        hit = flat == fk
        hits.append(hit)
        gts.append(jnp.sum(jnp.where(hit, sc, 0.0), axis=0, keepdims=True))
        chosen = jnp.where(hit, 1.0, chosen)
        cand = jnp.where(hit, neg, cand)
    gsum = gts[0]
    for g in gts[1:]:
        gsum = gsum + g
    gts = [g / gsum * ROUTED_SCALE for g in gts]

    before = (lax.broadcasted_iota(I32, (tm, tm), 0) < lax.broadcasted_iota(I32, (tm, tm), 1))
    rank_in = _dot(chosen.astype(BF16), jnp.where(before, 1.0, 0.0).astype(BF16))
    cnt = jnp.sum(chosen, axis=1, keepdims=True)
    cnt_pad = jnp.floor((cnt + (CH - 1.0)) * (1.0 / CH)) * CH
    below = (lax.broadcasted_iota(I32, (N_EXPERTS, N_EXPERTS), 1) < lax.broadcasted_iota(I32, (N_EXPERTS, N_EXPERTS), 0))
    start = _dot(jnp.where(below, 1.0, 0.0).astype(BF16),
                 jnp.broadcast_to(cnt_pad, (N_EXPERTS, LANES)).astype(BF16))[:, 0:1]
    pos = rank_in + start
    lpos = [jnp.sum(jnp.where(hit, pos, 0.0), axis=0, keepdims=True) for hit in hits]
    return lpos, gts, cnt_pad, start


def _slot_rows(vals, n_rows):
    tm = vals[0].shape[1]
    row = lax.broadcasted_iota(I32, (n_rows, tm), 0)
    out = jnp.zeros((n_rows, tm), F32)
    for k, v in enumerate(vals):
        out = jnp.where(row == k, v, out)
    return out


def _post_kernel(tbl_ref, xa_ref, xb_ref, mp_ref, ms_ref, mod_ref, wout_ref, g2_ref, shg_ref, shu_ref, shd_ref,
                 rwt_ref, rb_ref, base_ref, h2_ref, lpos_ref, ptok_ref, gtok_ref, cnt_ref, start_ref, rel_ref,
                 run_ref):
    i = pl.program_id(0)
    n_prompt_tiles = tbl_ref[2, 0]

    @pl.when(i == 0)
    def _():
        cnt_ref[...] = jnp.zeros_like(cnt_ref)
        start_ref[...] = jnp.zeros_like(start_ref)
        rel_ref[...] = jnp.zeros_like(rel_ref)
        run_ref[...] = jnp.zeros_like(run_ref)

    mod = mod_ref[...]
    gate1 = mod[:, 2 * D_MODEL:3 * D_MODEL]
    shift2 = mod[:, 3 * D_MODEL:4 * D_MODEL]
    scale2 = mod[:, 4 * D_MODEL:5 * D_MODEL]
    gate2 = mod[:, 5 * D_MODEL:6 * D_MODEL]
    mix = jnp.where(i < n_prompt_tiles, mp_ref[...], ms_ref[...])
    x1 = _tile_x(tbl_ref, xa_ref, xb_ref) + gate1 * _dot(mix, wout_ref[...])
    ms = jnp.mean(x1 * x1, axis=-1, keepdims=True)
    h2 = x1 * lax.rsqrt(ms + NORM_EPS) * g2_ref[...]
    h2 = h2 * (1.0 + scale2) + shift2
    hb = h2.astype(BF16)
    h2_ref[...] = hb
    act = _silu(_dot(hb, shg_ref[...])) * _dot(hb, shu_ref[...])
    base_ref[...] = x1 + gate2 * _dot(act.astype(BF16), shd_ref[...])

    lpos, gts, cnt_pad, start = _route(h2, rwt_ref, rb_ref)
    lpos_ref[...] = _slot_rows(lpos, SLOT_ROWS)
    ptok_ref[...] = _slot_rows(lpos, LANES).T
    gtok_ref[...] = _slot_rows(gts, LANES).T
    col = lax.broadcasted_iota(I32, cnt_ref.shape, 1)
    run = run_ref[...]
    cnt_ref[...] = jnp.where(col == i, cnt_pad.astype(I32), cnt_ref[...])
    start_ref[...] = jnp.where(col == i, start.astype(I32), start_ref[...])
    rel_ref[...] = jnp.where(col == i, run.astype(I32), rel_ref[...])
    run_ref[...] = run + cnt_pad


def _post_call(layer, tbl, xa, xb, mix_p, mix_s, mod4, wout, g2, shg, shu, shd, rwt, rb):
    npt = mix_p.shape[0] // TM
    n_tok = mix_p.shape[0] + mix_s.shape[0]
    nt = n_tok // TM
    const = lambda i, t: (0, 0)
    tile = lambda i, t: (i, 0)
    gs = pltpu.PrefetchScalarGridSpec(
        num_scalar_prefetch=1,
        grid=(nt,),
        in_specs=_x_specs(npt, xa is xb) + [
            pl.BlockSpec((TM, MIX_W), lambda i, t: (jnp.minimum(i, npt - 1), 0)),
            pl.BlockSpec((TM, MIX_W), lambda i, t: (jnp.maximum(i - npt, 0), 0)),
            _mod_spec(layer),
            _layer_spec(layer, MIX_W, D_MODEL),
            _layer_spec(layer, 1, D_MODEL),
            _layer_spec(layer, D_MODEL, EXPERT_FF),
            _layer_spec(layer, D_MODEL, EXPERT_FF),
            _layer_spec(layer, EXPERT_FF, D_MODEL),
            _layer_spec(layer, N_EXPERTS, D_MODEL),
            _layer_spec(layer, N_EXPERTS, 1),
        ],
        out_specs=[
            pl.BlockSpec((TM, D_MODEL), tile),
            pl.BlockSpec((TM, D_MODEL), tile),
            pl.BlockSpec((SLOT_ROWS, TM), lambda i, t: (0, i)),
            pl.BlockSpec((TM, LANES), tile),
            pl.BlockSpec((TM, LANES), tile),
            pl.BlockSpec((N_EXPERTS, LANES), const),
            pl.BlockSpec((N_EXPERTS, LANES), const),
            pl.BlockSpec((N_EXPERTS, LANES), const),
        ],
        scratch_shapes=[pltpu.VMEM((N_EXPERTS, 1), F32)],
    )
    assert nt <= LANES
    return pl.pallas_call(
        _post_kernel,
        grid_spec=gs,
        out_shape=[
            jax.ShapeDtypeStruct((n_tok, D_MODEL), F32),
            jax.ShapeDtypeStruct((n_tok, D_MODEL), BF16),
            jax.ShapeDtypeStruct((SLOT_ROWS, n_tok), F32),
            jax.ShapeDtypeStruct((n_tok, LANES), F32),
            jax.ShapeDtypeStruct((n_tok, LANES), F32),
            jax.ShapeDtypeStruct((N_EXPERTS, LANES), I32),
            jax.ShapeDtypeStruct((N_EXPERTS, LANES), I32),
            jax.ShapeDtypeStruct((N_EXPERTS, LANES), I32),
        ],
        compiler_params=_cparams(("arbitrary",)),
        name="post_route",
    )(tbl, xa, xb, mix_p, mix_s, mod4, wout, g2, shg, shu, shd, rwt, rb)


SEG_ROW0, SEG_NBLK, SEG_PAD0, SEG_NPAD, SEG_NEXT, SEG_USED = range(6)


def _plan_kernel(last_tile, cnt_ref, rel_ref, seg_ref):
    def per_expert(e, start):
        end = start + rel_ref[e, last_tile] + cnt_ref[e, last_tile]
        nb = lax.shift_right_logical(end - start + (EB - 1), LOG_EB)
        nxt = start + lax.shift_left(nb, LOG_EB)
        seg_ref[SEG_ROW0, e] = start
        seg_ref[SEG_NBLK, e] = nb
        seg_ref[SEG_PAD0, e] = end
        seg_ref[SEG_NPAD, e] = lax.shift_right_logical(nxt - end, LOG_CH)
        seg_ref[SEG_USED, e] = 0
        return nxt

    total = lax.fori_loop(0, N_EXPERTS, per_expert, jnp.int32(0))

    def link(k, nxt):
        e = N_EXPERTS - 1 - k
        seg_ref[SEG_NEXT, e] = nxt
        return jnp.where(seg_ref[SEG_NBLK, e] > 0, e, nxt)

    first = lax.fori_loop(0, N_EXPERTS, link, jnp.int32(N_EXPERTS))
    seg_ref[SEG_USED, 0] = lax.shift_right_logical(total, LOG_EB)
    seg_ref[SEG_USED, 1] = first


def _plan_call(cnt, rel, nt):
    smem = pl.BlockSpec(memory_space=pltpu.SMEM)
    return pl.pallas_call(
        functools.partial(_plan_kernel, nt - 1),
        in_specs=[smem, smem],
        out_specs=smem,
        out_shape=jax.ShapeDtypeStruct((6, N_EXPERTS), I32),
        name="moe_plan",
    )(cnt, rel)


def _rows_copy(src_ref, src_row, dst_ref, dst_row, n_rows, sem):
    return pltpu.make_async_copy(src_ref.at[pl.ds(pl.multiple_of(src_row, CH), n_rows)],
                                 dst_ref.at[pl.ds(pl.multiple_of(dst_row, CH), n_rows)], sem)


class _Runs:
    def __init__(self, cnt_ref, start_ref, rel_ref, seg_ref):
        self.cnt, self.start, self.rel, self.seg = cnt_ref, start_ref, rel_ref, seg_ref

    def n_rows(self, i):
        return self.start[N_EXPERTS - 1, i] + self.cnt[N_EXPERTS - 1, i]

    def start_copies(self, i, copy, tot_ref, slot):
        def per_pair(e2, carry):
            n_big, n_small = carry
            for par in range(2):
                e = 2 * e2 + par
                c = self.cnt[e, i]
                a0 = self.start[e, i]
                b0 = self.seg[SEG_ROW0, e] + self.rel[e, i]
                nb = lax.shift_right_logical(c, LOG_CH + 1)
                odd = jnp.bitwise_and(lax.shift_right_logical(c, LOG_CH), 1)

                def big(q, cc, a0=a0, b0=b0, par=par):
                    copy(a0 + q * (2 * CH), b0 + q * (2 * CH), 2 * CH).start(priority=par)
                    return cc

                lax.fori_loop(0, nb, big, 0)

                @pl.when(odd == 1)
                def _(a0=a0, b0=b0, nb=nb, par=par):
                    copy(a0 + nb * (2 * CH), b0 + nb * (2 * CH), CH).start(priority=par)

                n_big, n_small = n_big + nb, n_small + odd
            return n_big, n_small

        n_big, n_small = lax.fori_loop(0, N_EXPERTS // 2, per_pair, (jnp.int32(0), jnp.int32(0)))
        tot_ref[slot, 0] = n_big
        tot_ref[slot, 1] = n_small

    @staticmethod
    def wait_copies(copy, tot_ref, slot):
        def big(q, c):
            copy(0, 0, 2 * CH).wait()
            return c

        lax.fori_loop(0, tot_ref[slot, 0], big, 0)

        def small(q, c):
            copy(0, 0, CH).wait()
            return c

        lax.fori_loop(0, tot_ref[slot, 1], small, 0)


def _dispatch_kernel(reuse, cnt_ref, start_ref, rel_ref, seg_ref, h_ref, lpos_ref, *rest):
    xb_hbm, sort_s, zero_s, tot_s, sems = rest[1:] if reuse else rest
    i = pl.program_id(0)
    last = pl.num_programs(0) - 1
    slot = lax.rem(i, 2)
    runs = _Runs(cnt_ref, start_ref, rel_ref, seg_ref)
    n_sorted = runs.n_rows(i)
    blk = TM
    for r in range(SORT_ROWS // blk):
        @pl.when(r * blk < n_sorted)
        def _(r=r):
            lp = lpos_ref[...]
            srow = (lax.broadcasted_iota(I32, (blk, TM), 0) + r * blk).astype(F32)
            p = jnp.zeros((blk, TM), F32)
            for k in range(TOP_K):
                p = jnp.where(srow == lp[k:k + 1, :], 1.0, p)
            sort_s[slot, r * blk:(r + 1) * blk, :] = _dot(p.astype(BF16), h_ref[...]).astype(BF16)

    def copy_from(sl):
        return lambda s, d, n: _rows_copy(sort_s.at[sl], s, xb_hbm, d, n, sems.at[sl])

    runs.start_copies(i, copy_from(slot), tot_s, slot)

    @pl.when(i > 0)
    def _():
        runs.wait_copies(copy_from(1 - slot), tot_s, 1 - slot)

    @pl.when(i == last)
    def _():
        runs.wait_copies(copy_from(slot), tot_s, slot)
        _zero_fill_unused(seg_ref, xb_hbm, zero_s, sems.at[0], tail=not reuse)


def _zero_fill_unused(seg_ref, buf_hbm, zero_s, sem, tail):
    zero_s[...] = jnp.zeros_like(zero_s)

    def per_expert(e, c):
        first = seg_ref[SEG_PAD0, e]

        def z_issue(r, cc):
            _rows_copy(zero_s, 0, buf_hbm, first + r * CH, CH, sem).start()
            return cc

        lax.fori_loop(0, seg_ref[SEG_NPAD, e], z_issue, 0)

        def z_drain(r, cc):
            _rows_copy(zero_s, 0, buf_hbm, 0, CH, sem).wait()
            return cc

        lax.fori_loop(0, seg_ref[SEG_NPAD, e], z_drain, 0)
        return c

    lax.fori_loop(0, N_EXPERTS, per_expert, 0)
    if tail:
        _zero_fill_tail(seg_ref, buf_hbm, zero_s, sem)


def _zero_fill_tail(seg_ref, buf_hbm, zero_s, sem):
    n_blocks = buf_hbm.shape[0] // EB

    def blk_copy(b):
        return pltpu.make_async_copy(zero_s, buf_hbm.at[pl.ds(pl.multiple_of(b * EB, EB), EB)], sem)

    def t_issue(b, cc):
        blk_copy(b).start()
        return cc

    lax.fori_loop(seg_ref[SEG_USED, 0], n_blocks, t_issue, 0)

    def t_drain(b, cc):
        blk_copy(0).wait()
        return cc

    lax.fori_loop(seg_ref[SEG_USED, 0], n_blocks, t_drain, 0)


def _dispatch_call(cnt, start, rel, seg, h2, lpos, n_rows, prev=None):
    n_tok = h2.shape[0]
    nt = n_tok // TM
    smem = pl.BlockSpec(memory_space=pltpu.SMEM)
    reuse = prev is not None
    return pl.pallas_call(
        functools.partial(_dispatch_kernel, reuse),
        grid=(nt,),
        in_specs=[
            smem, smem, smem, smem,
            pl.BlockSpec((TM, D_MODEL), lambda i: (i, 0)),
            pl.BlockSpec((SLOT_ROWS, TM), lambda i: (0, i)),
        ] + ([pl.BlockSpec(memory_space=pl.ANY)] if reuse else []),
        input_output_aliases={6: 0} if reuse else {},
        out_specs=pl.BlockSpec(memory_space=pl.ANY),
        out_shape=jax.ShapeDtypeStruct((n_rows, D_MODEL), BF16),
        scratch_shapes=[
            pltpu.VMEM((2, SORT_ROWS, D_MODEL), BF16),
            pltpu.VMEM((EB, D_MODEL), BF16),
            pltpu.SMEM((2, 2), I32),
            pltpu.SemaphoreType.DMA((2,)),
        ],
        compiler_params=_cparams(("arbitrary",)),
        name="moe_dispatch",
    )(cnt, start, rel, seg, h2, lpos, *([prev] if reuse else []))


X_SLOTS = 4


def _experts_kernel(layer, seg_ref, wg_hbm, wu_hbm, wd_hbm, xb_hbm, yb_hbm,
                    wgf_s, wuf_s, wdf_s, wg_s, wu_s, wd_s, x_s, y_s, sem_w, sem_x, sem_y):
    n_used = seg_ref[SEG_USED, 0]
    first = seg_ref[SEG_USED, 1]

    def rows(g):
        return pl.ds(pl.multiple_of(g * EB, EB), EB)

    def x_copy(g, slot):
        return pltpu.make_async_copy(xb_hbm.at[rows(g)], x_s.at[slot], sem_x.at[slot])

    def y_copy(g, slot):
        return pltpu.make_async_copy(y_s.at[slot], yb_hbm.at[rows(g)], sem_y.at[slot])

    def w_copies(e, slot):
        return [pltpu.make_async_copy(hbm.at[layer, e], buf.at[slot], sem_w.at[slot, n])
                for n, (hbm, buf) in enumerate(((wg_hbm, wgf_s), (wu_hbm, wuf_s), (wd_hbm, wdf_s)))]

    def fetch_weights(e, slot):
        for c in w_copies(e, slot):
            c.start()

    def take_weights(slot):
        for c in w_copies(0, slot):
            c.wait()
        wg_s[...] = wgf_s[slot].astype(BF16)
        wu_s[...] = wuf_s[slot].astype(BF16)
        wd_s[...] = wdf_s[slot].astype(BF16)

    def next_expert(e):
        return seg_ref[SEG_NEXT, jnp.minimum(e, N_EXPERTS - 1)]

    @pl.when(n_used > 0)
    def _():
        for p in range(X_SLOTS - 1):
            @pl.when(p < n_used)
            def _(p=p):
                x_copy(p, p).start()

        fetch_weights(first, 0)

        @pl.when(next_expert(first) < N_EXPERTS)
        def _():
            fetch_weights(next_expert(first), 1)

        take_weights(0)

        def block(g, carry):
            e, left, wslot = carry
            ahead = g + (X_SLOTS - 1)

            @pl.when(ahead < n_used)
            def _():
                x_copy(ahead, lax.rem(ahead, X_SLOTS)).start()

            xslot = lax.rem(g, X_SLOTS)
            yslot = lax.rem(g, 2)
            x_copy(g, xslot).wait()

            @pl.when(g >= 2)
            def _():
                y_copy(g - 2, yslot).wait()

            xb = x_s[xslot]
            act = _silu(_dot(xb, wg_s[...])) * _dot(xb, wu_s[...])
            y_s[yslot] = _dot(act.astype(BF16), wd_s[...]).astype(BF16)
            y_copy(g, yslot).start()

            switch = jnp.logical_and(left == 1, g + 1 < n_used)
            nxt = next_expert(e)

            @pl.when(switch)
            def _():
                take_weights(1 - wslot)

                @pl.when(next_expert(nxt) < N_EXPERTS)
                def _():
                    fetch_weights(next_expert(nxt), wslot)

            nxt_c = jnp.minimum(nxt, N_EXPERTS - 1)
            return (jnp.where(switch, nxt_c, e), jnp.where(switch, seg_ref[SEG_NBLK, nxt_c], left - 1),
                    jnp.where(switch, 1 - wslot, wslot))

        lax.fori_loop(0, n_used, block,
                      (first, seg_ref[SEG_NBLK, jnp.minimum(first, N_EXPERTS - 1)], jnp.int32(0)))

        @pl.when(n_used >= 2)
        def _():
            y_copy(n_used - 2, lax.rem(n_used, 2)).wait()

        y_copy(n_used - 1, lax.rem(n_used - 1, 2)).wait()


def _experts_call(seg, xb, layer, wg, wu, wd):
    n_rows = xb.shape[0]
    hbm = pl.BlockSpec(memory_space=pl.ANY)
    return pl.pallas_call(
        functools.partial(_experts_kernel, layer),
        in_specs=[pl.BlockSpec(memory_space=pltpu.SMEM), hbm, hbm, hbm, hbm],
        out_specs=hbm,
        out_shape=jax.ShapeDtypeStruct((n_rows, D_MODEL), BF16),
        input_output_aliases={4: 0},
        scratch_shapes=[
            pltpu.VMEM((2, D_MODEL, EXPERT_FF), F32),
            pltpu.VMEM((2, D_MODEL, EXPERT_FF), F32),
            pltpu.VMEM((2, EXPERT_FF, D_MODEL), F32),
            pltpu.VMEM((D_MODEL, EXPERT_FF), BF16),
            pltpu.VMEM((D_MODEL, EXPERT_FF), BF16),
            pltpu.VMEM((EXPERT_FF, D_MODEL), BF16),
            pltpu.VMEM((X_SLOTS, EB, D_MODEL), BF16),
            pltpu.VMEM((2, EB, D_MODEL), BF16),
            pltpu.SemaphoreType.DMA((2, 3)),
            pltpu.SemaphoreType.DMA((X_SLOTS,)),
            pltpu.SemaphoreType.DMA((2,)),
        ],
        compiler_params=pltpu.CompilerParams(vmem_limit_bytes=VMEM_LIMIT),
        name="moe_experts",
    )(seg, wg, wu, wd, xb)


def _combine_kernel(final, tbl_ref, cnt_ref, start_ref, rel_ref, seg_ref, yb_hbm, base_ref, gtok_ref, ptok_ref,
                    mod_ref, gf_ref, *rest):
    *out_refs, sort_s, acc_s, tot_s, sems = rest
    i = pl.program_id(0)
    slot = lax.rem(i, 2)
    runs = _Runs(cnt_ref, start_ref, rel_ref, seg_ref)

    def copy_to(sl):
        return lambda s, d, n: _rows_copy(yb_hbm, d, sort_s.at[sl], s, n, sems.at[sl])

    @pl.when(i == 0)
    def _():
        sort_s[...] = jnp.zeros_like(sort_s)
        runs.start_copies(i, copy_to(slot), tot_s, slot)

    @pl.when(i + 1 < pl.num_programs(0))
    def _():
        runs.start_copies(i + 1, copy_to(1 - slot), tot_s, 1 - slot)

    runs.wait_copies(copy_to(slot), tot_s, slot)

    n_sorted = runs.n_rows(i)
    blk = TM
    acc_s[...] = jnp.zeros_like(acc_s)
    for r in range(SORT_ROWS // blk):
        @pl.when(r * blk < n_sorted)
        def _(r=r):
            gt = gtok_ref[...]
            pt = ptok_ref[...]
            col = (lax.broadcasted_iota(I32, (TM, blk), 1) + r * blk).astype(F32)
            w = jnp.zeros((TM, blk), F32)
            for k in range(TOP_K):
                w = jnp.where(col == pt[:, k:k + 1], gt[:, k:k + 1], w)
            acc_s[...] += _dot(w.astype(BF16), sort_s[slot, r * blk:(r + 1) * blk, :])

    gate2 = mod_ref[...][:, 5 * D_MODEL:6 * D_MODEL]
    y = base_ref[...] + gate2 * acc_s[...]
    if final:
        y = y * lax.rsqrt(jnp.mean(y * y, axis=-1, keepdims=True) + NORM_EPS) * gf_ref[...]
        yp_ref, ys_ref = out_refs
        is_context = i < tbl_ref[2, 0]

        @pl.when(is_context)
        def _():
            yp_ref[...] = y

        @pl.when(jnp.logical_not(is_context))
        def _():
            ys_ref[...] = y
    else:
        out_refs[0][...] = y


def _combine_call(final, layer, npt, tbl, cnt, start, rel, seg, yb, base, gtok, ptok, mod4, gfinal):
    n_tok = base.shape[0]
    nt = n_tok // TM
    tile = lambda i, t: (i, 0)
    smem = pl.BlockSpec(memory_space=pltpu.SMEM)
    if final:
        out_specs = [pl.BlockSpec((TM, D_MODEL), lambda i, t: (jnp.minimum(i, npt - 1), 0)),
                     pl.BlockSpec((TM, D_MODEL), lambda i, t: (jnp.maximum(i - npt, 0), 0))]
        out_shape = [jax.ShapeDtypeStruct((npt * TM, D_MODEL), F32),
                     jax.ShapeDtypeStruct((n_tok - npt * TM, D_MODEL), F32)]
    else:
        out_specs = pl.BlockSpec((TM, D_MODEL), tile)
        out_shape = jax.ShapeDtypeStruct((n_tok, D_MODEL), F32)
    gs = pltpu.PrefetchScalarGridSpec(
        num_scalar_prefetch=1,
        grid=(nt,),
        in_specs=[
            smem, smem, smem, smem,
            pl.BlockSpec(memory_space=pl.ANY),
            pl.BlockSpec((TM, D_MODEL), tile),
            pl.BlockSpec((TM, LANES), tile),
            pl.BlockSpec((TM, LANES), tile),
            _mod_spec(layer),
            pl.BlockSpec((1, D_MODEL), lambda i, t: (0, 0)),
        ],
        out_specs=out_specs,
        scratch_shapes=[
            pltpu.VMEM((2, SORT_ROWS, D_MODEL), BF16),
            pltpu.VMEM((TM, D_MODEL), F32),
            pltpu.SMEM((2, 2), I32),
            pltpu.SemaphoreType.DMA((2,)),
        ],
    )
    return pl.pallas_call(
        functools.partial(_combine_kernel, final),
        grid_spec=gs,
        out_shape=out_shape,
        compiler_params=_cparams(("arbitrary",)),
        name="moe_combine",
    )(tbl, cnt, start, rel, seg, yb, base, gtok, ptok, mod4, gfinal)


def _pad_cols(w, groups, width, slot):
    lead = w.shape[:-1]
    w = w.reshape(*lead, groups, width)
    pad = [(0, 0)] * (len(lead) + 1) + [(0, slot - width)]
    return jnp.pad(w, pad).reshape(*lead, groups * slot)


def _prep_w_in(w):
    c = np.cumsum([0, 256, 256, 256, 256, 256, 256, 256, 256, 128, 32])
    rq, rk, rv, rg, dq, dk, dv, cq, ckv, kpe = [w[..., c[n]:c[n + 1]] for n in range(10)]
    kpe_slot = jnp.concatenate([kpe, jnp.zeros_like(kpe), kpe, jnp.zeros_like(kpe)], axis=-1)
    cols = [rq, rk * (RET_DIM ** -0.5), rv, rg, _pad_cols(dq, DIFF_HEADS, 2 * DIFF_QK_DIM, SLOT),
            _pad_cols(dk, DIFF_HEADS, 2 * DIFF_QK_DIM, SLOT), dv, cq, ckv, kpe_slot]
    return jnp.concatenate(cols, axis=-1).astype(BF16)


def _rope_tables(n_pos, dim, lane_offsets):
    f32 = np.float32
    n_rows = n_pos // GRID_W
    row = np.repeat(np.arange(n_rows, dtype=f32), GRID_W)
    col = np.tile(np.arange(GRID_W, dtype=f32), n_rows)
    half = dim // 2
    freqs = f32(ROPE_THETA) ** (-np.arange(0, half, 2, dtype=f32) / f32(half))
    ar = row[:, None] * freqs[None, :]
    ac = col[:, None] * freqs[None, :]
    ang = np.concatenate([ar, ar, ac, ac], axis=-1).astype(f32)
    cos, sin = np.cos(ang), np.sin(ang)
    first = (np.arange(dim) % 16) < 8
    sa = np.where(first[None, :], -sin, f32(0))
    sb = np.where(first[None, :], f32(0), sin)
    c_t = np.ones((TM + n_pos, SLOT), f32)
    a_t = np.zeros((TM + n_pos, SLOT), f32)
    b_t = np.zeros((TM + n_pos, SLOT), f32)
    for off in lane_offsets:
        c_t[TM:, off:off + dim] = cos
        a_t[TM:, off:off + dim] = sa
        b_t[TM:, off:off + dim] = sb
    return tuple(jnp.asarray(t) for t in (c_t, a_t, b_t))


def _block_diag_states(s):
    b, l, h, dk, dv = s.shape
    eye = jnp.eye(h, dtype=s.dtype)
    return jnp.einsum('blhkv,hg->blhkgv', s, eye).reshape(b, l, h * dk, h * dv)


def kernel(x_prompt, x_sample, cache_diff_k, cache_diff_v, cache_mla_ckv, cache_mla_kpe, state_ret_fwd, state_ret_bwd, c, c_ctx, w_ada, b_ada, norm_mix, norm_ffn, norm_final, w_in, ret_decay_fwd, ret_decay_bwd, diff_lambda, diff_subln, mla_q_norm, mla_w_uq, mla_kv_norm, mla_w_ukv, w_out, router_w, router_bias, exp_w_gate, exp_w_up, exp_w_down, sh_w_gate, sh_w_up, sh_w_down):
    n_pb, p_len, _ = x_prompt.shape
    n_sb, s_len, _ = x_sample.shape
    past_len = cache_diff_k.shape[3]
    n_p = n_pb * p_len
    n_s = n_sb * s_len
    n_tok = n_p + n_s
    nt = n_tok // TM
    npt = n_p // TM
    assert p_len == TM and s_len % TM == 0 and n_p % s_len == 0 and past_len % 8 == 0

    tiles = np.arange(nt)
    mod_row = np.where(tiles < npt, n_sb, (tiles - npt) // (s_len // TM))
    rope_blk = np.where(tiles < npt, 0, 1 + (tiles - npt) % (s_len // TM))
    tbl = jnp.asarray(np.stack([mod_row, rope_blk, np.full(nt, npt)]).astype(np.int32))

    n_cond = 16
    cond = jnp.concatenate([c, c_ctx[None, :], jnp.zeros((n_cond - n_sb - 1, D_MODEL), F32)], axis=0)
    mod_all = _modulation(cond, w_ada, b_ada)

    rope_d = _rope_tables(s_len, DIFF_QK_DIM, (0, DIFF_QK_DIM))
    rope_m = _rope_tables(s_len, MLA_ROPE_DIM, (KR_LO,))
    place = np.zeros((MLA_ROPE_DIM, SLOT), np.float32)
    place[np.arange(MLA_ROPE_DIM), KR_LO + np.arange(MLA_ROPE_DIM)] = 1.0
    place = jnp.asarray(place, BF16)
    cache_dv_t = cache_diff_v.transpose(0, 1, 3, 2, 4).reshape(n_sb, DEPTH, past_len, DIFF_V_W)
    s0f_bd = _block_diag_states(state_ret_fwd)
    s0b_bd = _block_diag_states(state_ret_bwd)

    n_blocks = pl.cdiv(n_tok * TOP_K + nt * N_EXPERTS * (CH - 1) + N_EXPERTS * (EB - CH), EB)
    n_rows = n_blocks * EB

    xa, xb = x_prompt.reshape(n_p, D_MODEL), x_sample.reshape(n_s, D_MODEL)
    gfinal = norm_final.reshape(1, D_MODEL)

    mod4 = mod_all.reshape(DEPTH, n_cond, 1, 6 * D_MODEL)
    w_pre = _prep_w_in(w_in)
    wuq = _pad_cols(mla_w_uq, MLA_HEADS, MLA_NOPE_DIM + MLA_ROPE_DIM, SLOT).astype(BF16)
    ukv = mla_w_ukv.reshape(DEPTH, MLA_KV_LORA, MLA_HEADS, MLA_NOPE_DIM + MLA_V_DIM)
    wk = _pad_cols(ukv[..., :MLA_NOPE_DIM].reshape(DEPTH, MLA_KV_LORA, -1), MLA_HEADS, MLA_NOPE_DIM, SLOT).astype(BF16)
    wv = ukv[..., MLA_NOPE_DIM:].reshape(DEPTH, MLA_KV_LORA, MLA_V_W).astype(BF16)
    decf = jnp.repeat(ret_decay_fwd, RET_DIM, axis=-1).reshape(DEPTH, 1, RET_W)
    decb = jnp.repeat(ret_decay_bwd, RET_DIM, axis=-1).reshape(DEPTH, 1, RET_W)
    subln = jnp.tile(diff_subln, (1, DIFF_HEADS)).reshape(DEPTH, 1, DIFF_V_W)
    vec = lambda p: p.reshape(DEPTH, 1, -1)
    wout_b, shg_b, shu_b, shd_b = (w.astype(BF16) for w in (w_out, sh_w_gate, sh_w_up, sh_w_down))
    rwt = router_w.transpose(0, 2, 1)
    rb = router_bias.reshape(DEPTH, N_EXPERTS, 1)

    caches = []
    sorted_buf = None
    for l in range(DEPTH):
        lam_init = 0.8 - 0.6 * math.exp(-0.3 * l)
        ra, qa, ka = _pre_call(l, tbl, xa, xb, npt, n_tok, mod4, vec(norm_mix), w_pre, vec(mla_q_norm), wuq,
                               vec(mla_kv_norm), rope_d, rope_m)
        mix_p, sf, sb = _mix_prompt_call(lam_init, l, qa, ka, ra, n_pb, p_len, decf, decb,
                                         diff_lambda, subln, wk, wv)
        mix_s = _mix_sample_call(lam_init, l, qa, ka, ra, n_p, n_sb, s_len, past_len, cache_diff_k,
                                 cache_dv_t, cache_mla_ckv, cache_mla_kpe, s0f_bd, s0b_bd, decf, decb,
                                 diff_lambda, subln, wk, wv, place)
        base, h2, lpos, ptok, gtok, cnt, start, rel = _post_call(
            l, tbl, xa, xb, mix_p, mix_s, mod4, wout_b, vec(norm_ffn), shg_b, shu_b, shd_b, rwt, rb)
        seg = _plan_call(cnt, rel, nt)
        sorted_buf = _dispatch_call(cnt, start, rel, seg, h2, lpos, n_rows, prev=sorted_buf)
        sorted_buf = _experts_call(seg, sorted_buf, l, exp_w_gate, exp_w_up, exp_w_down)
        final = l == DEPTH - 1
        out = _combine_call(final, l, npt, tbl, cnt, start, rel, seg, sorted_buf, base, gtok, ptok, mod4, gfinal)
        xa, xb = out if final else (out, out)

        kp = ka[:n_p].reshape(n_pb, p_len, KA_W)
        dk = kp[:, :, KA_DK:KA_DV].reshape(n_pb, p_len, DIFF_HEADS, SLOT)[..., :2 * DIFF_QK_DIM]
        dv = kp[:, :, KA_DV:KA_CKV].reshape(n_pb, p_len, DIFF_HEADS, DIFF_V_DIM)
        diag = lambda s: jnp.einsum('bhkhv->bhkv', s.reshape(n_pb, RET_HEADS, RET_DIM, RET_HEADS, RET_DIM))
        caches.append((dk.transpose(0, 2, 1, 3), dv.transpose(0, 2, 1, 3), kp[:, :, KA_CKV:KA_KPE],
                       kp[:, :, KA_KPE:KA_KPE + MLA_ROPE_DIM], diag(sf), diag(sb)))

    y_prompt = xa.reshape(n_pb, p_len, D_MODEL)
    y_sample = xb.reshape(n_sb, s_len, D_MODEL)
    new = [jnp.stack([cs[n] for cs in caches], axis=1) for n in range(6)]
    return (y_prompt, y_sample, *new)
```

```python
import functools
import math

import numpy as np
import jax
import jax.numpy as jnp
from jax import lax
from jax.experimental import pallas as pl
from jax.experimental.pallas import tpu as pltpu

F32 = jnp.float32
BF16 = jnp.bfloat16
I32 = jnp.int32

D_MODEL = 1024
DEPTH = 2
GRID_W = 64
ROPE_THETA = 10000.0
NORM_EPS = 1e-6

RET_HEADS = 4
RET_DIM = 64
RET_CHUNK = 128
RET_W = RET_HEADS * RET_DIM
DIFF_HEADS = 4
DIFF_QK_DIM = 32
DIFF_V_DIM = 64
DIFF_V_W = DIFF_HEADS * DIFF_V_DIM
MLA_HEADS = 8
MLA_Q_LORA = 256
MLA_KV_LORA = 128
MLA_NOPE_DIM = 64
MLA_ROPE_DIM = 32
MLA_V_DIM = 64
MLA_V_W = MLA_HEADS * MLA_V_DIM
MIX_W = RET_W + DIFF_V_W + MLA_V_W

N_EXPERTS = 64
TOP_K = 6
N_GROUPS = 8
GROUP_SIZE = N_EXPERTS // N_GROUPS
TOPK_GROUPS = 4
EXPERT_FF = 256
ROUTED_SCALE = 2.5

LANES = 128
SLOT = LANES
TM = 256
SLOT_ROWS = 8
EB = 512
LOG_EB = 9
CH = 16
LOG_CH = 4
SORT_ROWS = 2560
VMEM_LIMIT = 48 * 1024 * 1024

C_RQ, C_RK, C_RV, C_RG = 0, 256, 512, 768
C_DQ = 1024
C_DK = 1536
C_DV = 2048
C_CQ = 2304
C_CKV = 2560
C_KPE = 2688
C_DVS = 2816
N_PRE = 3328
QA_W = 4 * SLOT + MLA_HEADS * SLOT
KA_DK, KA_DV, KA_CKV, KA_KPE = 0, 512, 768, 896
KA_W = 1024
KR_LO, KR_HI = 64, 96


def _dot(a, b):
    return jnp.dot(a, b, preferred_element_type=F32)


def _dot_nt(a, b):
    return lax.dot_general(a, b, (((1,), (1,)), ((), ())), preferred_element_type=F32)


def _dot_tn(a, b):
    return lax.dot_general(a, b, (((0,), (0,)), ((), ())), preferred_element_type=F32)


def _split_dot(x, w_bf16):
    hi = x.astype(BF16)
    lo = (x - hi.astype(F32)).astype(BF16)
    return _dot(hi, w_bf16) + _dot(lo, w_bf16)


def _split_dot_nt(w, x):
    wh = w.astype(BF16)
    wl = (w - wh.astype(F32)).astype(BF16)
    xh = x.astype(BF16)
    xl = (x - xh.astype(F32)).astype(BF16)
    return _dot_nt(wh, xh) + _dot_nt(wh, xl) + _dot_nt(wl, xh)


def _silu(x):
    return x * jax.nn.sigmoid(x)


def _cparams(sem):
    return pltpu.CompilerParams(dimension_semantics=sem, vmem_limit_bytes=VMEM_LIMIT)


MOD_TN = 512


def _mod_kernel(c_ref, w_ref, b_ref, o_ref):
    s = _silu(c_ref[...])
    o_ref[...] = _split_dot3(s, w_ref[...]) + b_ref[...]


def _split_dot3(x, w):
    xh = x.astype(BF16)
    xl = (x - xh.astype(F32)).astype(BF16)
    wh = w.astype(BF16)
    wl = (w - wh.astype(F32)).astype(BF16)
    return _dot(xh, wh) + _dot(xh, wl) + _dot(xl, wh)


def _modulation(cond, w_ada, b_ada):
    n_rows = cond.shape[0]
    n_out = w_ada.shape[-1]
    return pl.pallas_call(
        _mod_kernel,
        grid=(DEPTH, n_out // MOD_TN),
        in_specs=[
            pl.BlockSpec((n_rows, D_MODEL), lambda l, j: (0, 0)),
            pl.BlockSpec((None, D_MODEL, MOD_TN), lambda l, j: (l, 0, j)),
            pl.BlockSpec((None, 1, MOD_TN), lambda l, j: (l, 0, j)),
        ],
        out_specs=pl.BlockSpec((None, n_rows, MOD_TN), lambda l, j: (l, 0, j)),
        out_shape=jax.ShapeDtypeStruct((DEPTH, n_rows, n_out), F32),
        compiler_params=_cparams(("arbitrary", "arbitrary")),
        name="adaln_mod",
    )(cond, w_ada, b_ada.reshape(DEPTH, 1, n_out))


def _rope_slot(x, cos, sa, sb):
    up = pltpu.roll(x, LANES - 8, 1)
    dn = pltpu.roll(x, 8, 1)
    return x * cos + up * sa + dn * sb


def _tile_x(tbl_ref, xa_ref, xb_ref):
    return jnp.where(pl.program_id(0) < tbl_ref[2, 0], xa_ref[...], xb_ref[...])


def _x_specs(npt, combined):
    off = 0 if combined else npt
    return [pl.BlockSpec((TM, D_MODEL), lambda i, t: (jnp.minimum(i, npt - 1), 0)),
            pl.BlockSpec((TM, D_MODEL), lambda i, t: (jnp.maximum(i, npt) - off, 0))]


def _pre_kernel(tbl_ref, xa_ref, xb_ref, mod_ref, g_ref, w_ref, qg_ref, wuq_ref, kvg_ref,
                cd_ref, sad_ref, sbd_ref, cm_ref, sam_ref, sbm_ref,
                ra_ref, qa_ref, ka_ref, dkc_ref, dvc_ref, ckvc_ref, kpec_ref):
    x = _tile_x(tbl_ref, xa_ref, xb_ref)
    mod = mod_ref[...]
    shift1 = mod[:, 0:D_MODEL]
    scale1 = mod[:, D_MODEL:2 * D_MODEL]
    ms = jnp.mean(x * x, axis=-1, keepdims=True)
    h = x * lax.rsqrt(ms + NORM_EPS) * g_ref[...]
    h = h * (1.0 + scale1) + shift1
    hb = h.astype(BF16)

    def proj(lo, hi):
        return _dot(hb, w_ref[:, lo:hi])

    ra_ref[:, 0:C_RG] = proj(C_RQ, C_RG).astype(BF16)
    ra_ref[:, C_RG:C_DQ] = _silu(proj(C_RG, C_DQ)).astype(BF16)

    cd, sad, sbd = cd_ref[...], sad_ref[...], sbd_ref[...]
    cm, sam, sbm = cm_ref[...], sam_ref[...], sbm_ref[...]
    dq = proj(C_DQ, C_DK)
    dk = proj(C_DK, C_DV)
    dk_slots = []
    for hd in range(DIFF_HEADS):
        sl = slice(hd * SLOT, (hd + 1) * SLOT)
        qa_ref[:, sl] = _rope_slot(dq[:, sl], cd, sad, sbd).astype(BF16)
        dk_slots.append(_rope_slot(dk[:, sl], cd, sad, sbd))
        ka_ref[:, KA_DK + hd * SLOT:KA_DK + (hd + 1) * SLOT] = dk_slots[hd].astype(BF16)
    ka_ref[:, KA_DV:KA_CKV] = proj(C_DV, C_CQ).astype(BF16)

    cq = proj(C_CQ, C_CKV)
    cqn = cq * lax.rsqrt(jnp.mean(cq * cq, axis=-1, keepdims=True) + NORM_EPS) * qg_ref[...]
    qm = _dot(cqn.astype(BF16), wuq_ref[...])
    for hd in range(MLA_HEADS):
        sl = slice(hd * SLOT, (hd + 1) * SLOT)
        qa_ref[:, 4 * SLOT + hd * SLOT:4 * SLOT + (hd + 1) * SLOT] = _rope_slot(qm[:, sl], cm, sam, sbm).astype(BF16)

    ckv = proj(C_CKV, C_KPE)
    ckvn = ckv * lax.rsqrt(jnp.mean(ckv * ckv, axis=-1, keepdims=True) + NORM_EPS) * kvg_ref[...]
    kpe_slot = _rope_slot(proj(C_KPE, C_DVS), cm, sam, sbm)
    ka_ref[:, KA_CKV:KA_KPE] = ckvn.astype(BF16)
    ka_ref[:, KA_KPE:KA_W] = kpe_slot.astype(BF16)

    @pl.when(pl.program_id(0) < tbl_ref[2, 0])
    def _():
        dvs = proj(C_DVS, N_PRE)
        for hd in range(DIFF_HEADS):
            dkc_ref[hd] = dk_slots[hd][:, 0:2 * DIFF_QK_DIM]
            dvc_ref[hd] = dvs[:, hd * SLOT:hd * SLOT + DIFF_V_DIM]
        ckvc_ref[...] = ckvn
        kpec_ref[...] = kpe_slot[:, 0:MLA_ROPE_DIM]


def _layer_spec(layer, rows, cols):
    return pl.BlockSpec((None, rows, cols), lambda *_: (layer, 0, 0))


def _mod_spec(layer):
    return pl.BlockSpec((None, None, 1, 6 * D_MODEL), lambda i, t: (layer, t[0, i], 0, 0))


def _pre_call(layer, tbl, xa, xb, npt, n_tok, mod4, g, w_pre, qg, wuq, kvg, rope_d, rope_m):
    nt = n_tok // TM
    tile = lambda i, t: (i, 0)
    rope = lambda i, t: (t[1, i], 0)
    ctx5 = lambda i, t: (jnp.minimum(i, npt - 1), 0, 0, 0)
    ctx4 = lambda i, t: (jnp.minimum(i, npt - 1), 0, 0)
    cache_shapes = [
        jax.ShapeDtypeStruct((npt, DIFF_HEADS, TM, 2 * DIFF_QK_DIM), F32),
        jax.ShapeDtypeStruct((npt, DIFF_HEADS, TM, DIFF_V_DIM), F32),
        jax.ShapeDtypeStruct((npt, TM, MLA_KV_LORA), F32),
        jax.ShapeDtypeStruct((npt, TM, MLA_ROPE_DIM), F32),
    ]
    gs = pltpu.PrefetchScalarGridSpec(
        num_scalar_prefetch=1,
        grid=(nt,),
        in_specs=_x_specs(npt, xa is xb) + [
            _mod_spec(layer),
            _layer_spec(layer, 1, D_MODEL),
            _layer_spec(layer, D_MODEL, N_PRE),
            _layer_spec(layer, 1, MLA_Q_LORA),
            _layer_spec(layer, MLA_Q_LORA, MLA_HEADS * SLOT),
            _layer_spec(layer, 1, MLA_KV_LORA),
        ] + [pl.BlockSpec((TM, SLOT), rope)] * 6,
        out_specs=[
            pl.BlockSpec((TM, D_MODEL), tile),
            pl.BlockSpec((TM, QA_W), tile),
            pl.BlockSpec((TM, KA_W), tile),
            pl.BlockSpec((None, DIFF_HEADS, TM, 2 * DIFF_QK_DIM), ctx5),
            pl.BlockSpec((None, DIFF_HEADS, TM, DIFF_V_DIM), ctx5),
            pl.BlockSpec((None, TM, MLA_KV_LORA), ctx4),
            pl.BlockSpec((None, TM, MLA_ROPE_DIM), ctx4),
        ],
    )
    return pl.pallas_call(
        _pre_kernel,
        grid_spec=gs,
        out_shape=[
            jax.ShapeDtypeStruct((n_tok, D_MODEL), BF16),
            jax.ShapeDtypeStruct((n_tok, QA_W), BF16),
            jax.ShapeDtypeStruct((n_tok, KA_W), BF16),
        ] + cache_shapes,
        compiler_params=_cparams(("arbitrary",)),
        name="pre_proj",
    )(tbl, xa, xb, mod4, g, w_pre, qg, wuq, kvg, *rope_d, *rope_m)


def _lane_iota(shape):
    return lax.broadcasted_iota(I32, shape, len(shape) - 1)


def _head_mask(n_rows, width, head, head_w):
    lane = _lane_iota((n_rows, width))
    return (lane >= head * head_w) & (lane < (head + 1) * head_w)


def _seg_mean_sq(o, bd_ones):
    return _split_dot(o * o, bd_ones) * (1.0 / RET_DIM)


def _block_diag_ones(n, blk):
    r = lax.broadcasted_iota(I32, (n, n), 0) // blk
    c = lax.broadcasted_iota(I32, (n, n), 1) // blk
    return r == c


def _retention(ra_ref, seq_len, decf_ref, decb_ref, s0f, s0b):
    C = RET_CHUNK
    nc = seq_len // C
    lgf = -jnp.exp(decf_ref[...])
    lgb = -jnp.exp(decb_ref[...])
    pos = lax.broadcasted_iota(I32, (C, RET_W), 0).astype(F32)
    qdf = jnp.exp((pos + 1.0) * lgf)
    kdf = jnp.exp((C - 1.0 - pos) * lgf)
    cdf = jnp.exp(float(C) * lgf)
    qdb = jnp.exp((C - pos) * lgb)
    kdb = jnp.exp(pos * lgb)
    cdb = jnp.exp(float(C) * lgb)
    ii = lax.broadcasted_iota(I32, (C, C), 0).astype(F32)
    jj = lax.broadcasted_iota(I32, (C, C), 1).astype(F32)
    dist = ii - jj
    dmats = []
    for hd in range(RET_HEADS):
        lf = lgf[:, hd * RET_DIM:hd * RET_DIM + 1]
        lb = lgb[:, hd * RET_DIM:hd * RET_DIM + 1]
        dmats.append(jnp.where(dist >= 0, jnp.exp(dist * lf), jnp.exp(-dist * lb)))
    bd = _block_diag_ones(RET_W, RET_DIM)
    bd_ones = jnp.where(bd, 1.0, 0.0).astype(BF16)

    def chunk(n):
        rows = slice(n * C, (n + 1) * C)
        return (ra_ref[rows, C_RQ:C_RK], ra_ref[rows, C_RK:C_RV], ra_ref[rows, C_RV:C_RG])

    cross = [None] * nc
    sf = s0f
    for n in range(nc):
        q, k, v = chunk(n)
        cross[n] = _dot((q * qdf).astype(BF16), sf.astype(BF16))
        kv = _dot_tn((k * kdf).astype(BF16), v.astype(BF16))
        sf = sf * cdf + jnp.where(bd, kv, 0.0)
    sb = s0b
    for n in range(nc - 1, -1, -1):
        q, k, v = chunk(n)
        cross[n] = cross[n] + _dot((q * qdb).astype(BF16), sb.astype(BF16))
        kv = _dot_tn((k * kdb).astype(BF16), v.astype(BF16))
        sb = sb * cdb + jnp.where(bd, kv, 0.0)

    outs = []
    for n in range(nc):
        q, k, v = chunk(n)
        kb = k.astype(BF16)
        vb = v.astype(BF16)
        o = cross[n]
        for hd in range(RET_HEADS):
            hm = _head_mask(C, RET_W, hd, RET_DIM)
            sc = _dot_nt(jnp.where(hm, q, 0.0).astype(BF16), kb) * dmats[hd]
            o = o + jnp.where(hm, _dot(sc.astype(BF16), vb), 0.0)
        on = o * lax.rsqrt(_seg_mean_sq(o, bd_ones) + NORM_EPS)
        outs.append(on * ra_ref[n * C:(n + 1) * C, C_RG:C_DQ])
    return outs, sf, sb


def _softmax_pv(s_parts, v_parts, scale):
    m = None
    for s in s_parts:
        mm = jnp.max(s, axis=-1, keepdims=True)
        m = mm if m is None else jnp.maximum(m, mm)
    m = m * scale
    acc = None
    den = None
    for s, v in zip(s_parts, v_parts):
        e = jnp.exp(s * scale - m)
        ds = jnp.sum(e, axis=-1, keepdims=True)
        pv = _dot(e.astype(BF16), v)
        acc = pv if acc is None else acc + pv
        den = ds if den is None else den + ds
    return acc / den


def _diff_attention(dq, k_parts, v_parts, lam, subln, lam_init, bd_ones):
    lq = dq.shape[0]
    scale = DIFF_QK_DIM ** -0.5
    lane = _lane_iota((lq, SLOT))
    out = jnp.zeros((lq, DIFF_V_W), F32)
    for hd in range(DIFF_HEADS):
        qh = dq[:, hd * SLOT:(hd + 1) * SLOT]
        q1 = jnp.where(lane < DIFF_QK_DIM, qh, 0.0).astype(BF16)
        q2 = jnp.where(lane >= DIFF_QK_DIM, qh, 0.0).astype(BF16)
        s1 = [_dot_nt(q1[:, :kp[hd].shape[1]], kp[hd]) for kp in k_parts]
        s2 = [_dot_nt(q2[:, :kp[hd].shape[1]], kp[hd]) for kp in k_parts]
        o = _softmax_pv(s1, v_parts, scale) - lam * _softmax_pv(s2, v_parts, scale)
        out = jnp.where(_head_mask(lq, DIFF_V_W, hd, DIFF_V_DIM), o, out)
    on = out * lax.rsqrt(_seg_mean_sq(out, bd_ones) + NORM_EPS) * subln
    return on * (1.0 - lam_init)


def _mla_attention(qm, k_parts, v_parts):
    lq = qm.shape[0]
    scale = (MLA_NOPE_DIM + MLA_ROPE_DIM) ** -0.5
    halves = []
    for g in range(2):
        out = jnp.zeros((lq, 256), F32)
        for hh in range(4):
            hd = 4 * g + hh
            qh = qm[:, hd * SLOT:(hd + 1) * SLOT].astype(BF16)
            s = [_dot_nt(qh, kp[:, hd * SLOT:(hd + 1) * SLOT]) for kp in k_parts]
            o = _softmax_pv(s, [vp[:, 256 * g:256 * (g + 1)] for vp in v_parts], scale)
            out = jnp.where(_head_mask(lq, 256, hh, MLA_V_DIM), o, out)
        halves.append(out)
    return halves


def _mla_keys(ka_val_ckv, kr_slot, wk_ref, wv_ref):
    cb = ka_val_ckv.astype(BF16)
    kn = _dot(cb, wk_ref[...])
    ks = [(kn[:, hd * SLOT:(hd + 1) * SLOT] + kr_slot).astype(BF16) for hd in range(MLA_HEADS)]
    return jnp.concatenate(ks, axis=1), _dot(cb, wv_ref[...]).astype(BF16)


def _kr_only(kpe_slot):
    lane = _lane_iota(kpe_slot.shape)
    return jnp.where((lane >= KR_LO) & (lane < KR_HI), kpe_slot, 0.0)


def _diff_lambda(dl_ref, lam_init):
    dl = dl_ref[...]
    a = jnp.sum(dl[0:1] * dl[1:2], axis=-1, keepdims=True)
    b = jnp.sum(dl[2:3] * dl[3:4], axis=-1, keepdims=True)
    return jnp.exp(a) - jnp.exp(b) + lam_init


def _mix_prompt_kernel(lam_init, qa_ref, ka_ref, ra_ref, decf_ref, decb_ref, dl_ref, subln_ref,
                       wk_ref, wv_ref, mix_ref, sf_ref, sb_ref):
    seq = qa_ref.shape[0]
    zero_state = jnp.zeros((RET_W, RET_W), F32)
    outs, sf, sb = _retention(ra_ref, seq, decf_ref, decb_ref, zero_state, zero_state)
    for n, o in enumerate(outs):
        mix_ref[n * RET_CHUNK:(n + 1) * RET_CHUNK, 0:RET_W] = o.astype(BF16)
    for hd in range(RET_HEADS):
        blk = slice(hd * RET_DIM, (hd + 1) * RET_DIM)
        sf_ref[hd] = sf[blk, blk]
        sb_ref[hd] = sb[blk, blk]

    bd_ones = jnp.where(_block_diag_ones(DIFF_V_W, DIFF_V_DIM), 1.0, 0.0).astype(BF16)
    lam = _diff_lambda(dl_ref, lam_init)
    kd = [ka_ref[:, KA_DK + hd * SLOT:KA_DK + (hd + 1) * SLOT].astype(BF16) for hd in range(DIFF_HEADS)]
    vd = ka_ref[:, KA_DV:KA_CKV].astype(BF16)
    mix_ref[:, RET_W:RET_W + DIFF_V_W] = _diff_attention(
        qa_ref[:, 0:4 * SLOT], [kd], [vd], lam, subln_ref[...], lam_init, bd_ones).astype(BF16)

    km, vm = _mla_keys(ka_ref[:, KA_CKV:KA_KPE], _kr_only(ka_ref[:, KA_KPE:KA_W]), wk_ref, wv_ref)
    halves = _mla_attention(qa_ref[:, 4 * SLOT:QA_W], [km], [vm])
    mix_ref[:, 512:768] = halves[0].astype(BF16)
    mix_ref[:, 768:1024] = halves[1].astype(BF16)


def _mixer_param_specs(layer):
    return [
        _layer_spec(layer, 1, RET_W),
        _layer_spec(layer, 1, RET_W),
        _layer_spec(layer, 4, DIFF_QK_DIM),
        _layer_spec(layer, 1, DIFF_V_W),
        _layer_spec(layer, MLA_KV_LORA, MLA_HEADS * SLOT),
        _layer_spec(layer, MLA_KV_LORA, MLA_V_W),
    ]


def _mix_prompt_call(lam_init, layer, qa, ka, ra, n_seq, seq_len, decf, decb, dl, subln, wk, wv):
    seq = lambda b: (b, 0)
    state_spec = pl.BlockSpec((None, RET_HEADS, RET_DIM, RET_DIM), lambda b: (b, 0, 0, 0))
    state_shape = jax.ShapeDtypeStruct((n_seq, RET_HEADS, RET_DIM, RET_DIM), F32)
    return pl.pallas_call(
        functools.partial(_mix_prompt_kernel, lam_init),
        grid=(n_seq,),
        in_specs=[
            pl.BlockSpec((seq_len, QA_W), seq),
            pl.BlockSpec((seq_len, KA_W), seq),
            pl.BlockSpec((seq_len, D_MODEL), seq),
        ] + _mixer_param_specs(layer),
        out_specs=[pl.BlockSpec((seq_len, MIX_W), seq), state_spec, state_spec],
        out_shape=[jax.ShapeDtypeStruct((n_seq * seq_len, MIX_W), BF16), state_shape, state_shape],
        compiler_params=_cparams(("arbitrary",)),
        name="mix_prompt",
    )(qa, ka, ra, decf, decb, dl, subln, wk, wv)


def _mix_sample_kernel(lam_init, qa_ref, ka_ref, ra_ref, ckd_ref, cvd_ref, cckv_ref, ckpe_ref,
                       s0f_ref, s0b_ref, decf_ref, decb_ref, dl_ref, subln_ref, wk_ref, wv_ref,
                       place_ref, mix_ref,
                       ret_s, kdn_s, vdn_s, kdc_s, vdc_s, kmn_s, vmn_s, kmc_s, vmc_s):
    j = pl.program_id(1)
    seq = ka_ref.shape[0]

    @pl.when(j == 0)
    def _():
        outs, _, _ = _retention(ra_ref, seq, decf_ref, decb_ref, s0f_ref[...], s0b_ref[...])
        for n, o in enumerate(outs):
            ret_s[n * RET_CHUNK:(n + 1) * RET_CHUNK, :] = o
        kdn_s[...] = ka_ref[:, KA_DK:KA_DV].astype(BF16)
        vdn_s[...] = ka_ref[:, KA_DV:KA_CKV].astype(BF16)
        kdc_s[...] = ckd_ref[...].astype(BF16)
        vdc_s[...] = cvd_ref[...].astype(BF16)
        km, vm = _mla_keys(ka_ref[:, KA_CKV:KA_KPE], _kr_only(ka_ref[:, KA_KPE:KA_W]), wk_ref, wv_ref)
        kmn_s[...] = km
        vmn_s[...] = vm
        kr_ctx = _dot(ckpe_ref[...].astype(BF16), place_ref[...])
        km, vm = _mla_keys(cckv_ref[...], kr_ctx, wk_ref, wv_ref)
        kmc_s[...] = km
        vmc_s[...] = vm

    row0 = pl.multiple_of(j * TM, TM)
    mix_ref[:, 0:RET_W] = ret_s[pl.ds(row0, TM), :].astype(BF16)

    bd_ones = jnp.where(_block_diag_ones(DIFF_V_W, DIFF_V_DIM), 1.0, 0.0).astype(BF16)
    lam = _diff_lambda(dl_ref, lam_init)
    kd_ctx = [kdc_s[hd] for hd in range(DIFF_HEADS)]
    kd_new = [kdn_s[:, hd * SLOT:(hd + 1) * SLOT] for hd in range(DIFF_HEADS)]
    mix_ref[:, RET_W:RET_W + DIFF_V_W] = _diff_attention(
        qa_ref[:, 0:4 * SLOT], [kd_ctx, kd_new], [vdc_s[...], vdn_s[...]], lam, subln_ref[...],
        lam_init, bd_ones).astype(BF16)

    halves = _mla_attention(qa_ref[:, 4 * SLOT:QA_W], [kmc_s[...], kmn_s[...]], [vmc_s[...], vmn_s[...]])
    mix_ref[:, 512:768] = halves[0].astype(BF16)
    mix_ref[:, 768:1024] = halves[1].astype(BF16)


def _mix_sample_call(lam_init, layer, qa, ka, ra, tok0, n_seq, seq_len, past_len, cache_dk, cache_dv_t,
                     cache_ckv, cache_kpe, s0f_bd, s0b_bd, decf, decb, dl, subln, wk, wv, place):
    nq = seq_len // TM
    q0 = tok0 // TM
    s0 = tok0 // seq_len
    const = lambda b, j: (0, 0)
    return pl.pallas_call(
        functools.partial(_mix_sample_kernel, lam_init),
        grid=(n_seq, nq),
        in_specs=[
            pl.BlockSpec((TM, QA_W), lambda b, j: (q0 + b * nq + j, 0)),
            pl.BlockSpec((seq_len, KA_W), lambda b, j: (s0 + b, 0)),
            pl.BlockSpec((seq_len, D_MODEL), lambda b, j: (s0 + b, 0)),
            pl.BlockSpec((None, None, DIFF_HEADS, past_len, 2 * DIFF_QK_DIM), lambda b, j: (b, layer, 0, 0, 0)),
            pl.BlockSpec((None, None, past_len, DIFF_V_W), lambda b, j: (b, layer, 0, 0)),
            pl.BlockSpec((None, None, past_len, MLA_KV_LORA), lambda b, j: (b, layer, 0, 0)),
            pl.BlockSpec((None, None, past_len, MLA_ROPE_DIM), lambda b, j: (b, layer, 0, 0)),
            pl.BlockSpec((None, None, RET_W, RET_W), lambda b, j: (b, layer, 0, 0)),
            pl.BlockSpec((None, None, RET_W, RET_W), lambda b, j: (b, layer, 0, 0)),
        ] + _mixer_param_specs(layer) + [
            pl.BlockSpec((MLA_ROPE_DIM, SLOT), const),
        ],
        out_specs=pl.BlockSpec((TM, MIX_W), lambda b, j: (b * nq + j, 0)),
        out_shape=jax.ShapeDtypeStruct((n_seq * seq_len, MIX_W), BF16),
        scratch_shapes=[
            pltpu.VMEM((seq_len, RET_W), F32),
            pltpu.VMEM((seq_len, 4 * SLOT), BF16),
            pltpu.VMEM((seq_len, DIFF_V_W), BF16),
            pltpu.VMEM((DIFF_HEADS, past_len, 2 * DIFF_QK_DIM), BF16),
            pltpu.VMEM((past_len, DIFF_V_W), BF16),
            pltpu.VMEM((seq_len, MLA_HEADS * SLOT), BF16),
            pltpu.VMEM((seq_len, MLA_V_W), BF16),
            pltpu.VMEM((past_len, MLA_HEADS * SLOT), BF16),
            pltpu.VMEM((past_len, MLA_V_W), BF16),
        ],
        compiler_params=_cparams(("arbitrary", "arbitrary")),
        name="mix_sample",
    )(qa, ka, ra, cache_dk, cache_dv_t, cache_ckv, cache_kpe, s0f_bd, s0b_bd,
      decf, decb, dl, subln, wk, wv, place)


def _route(h2, rwt_ref, rb_ref):
    tm = h2.shape[0]
    neg = -jnp.inf
    logits = _split_dot_nt(rwt_ref[...], h2)
    sc = jax.nn.sigmoid(logits)
    sel = sc + rb_ref[...]
    member = lax.broadcasted_iota(I32, (GROUP_SIZE, tm), 0).astype(F32)
    gscore = []
    for g in range(N_GROUPS):
        sg = sel[g * GROUP_SIZE:(g + 1) * GROUP_SIZE, :]
        m1 = jnp.max(sg, axis=0, keepdims=True)
        f1 = jnp.min(jnp.where(sg == m1, member, float(GROUP_SIZE)), axis=0, keepdims=True)
        m2 = jnp.max(jnp.where(member == f1, neg, sg), axis=0, keepdims=True)
        gscore.append(m1 + m2)
    gsel = [jnp.zeros((1, tm), F32) for _ in range(N_GROUPS)]
    for _ in range(TOPK_GROUPS):
        mx = gscore[0]
        for g in range(1, N_GROUPS):
            mx = jnp.maximum(mx, gscore[g])
        fi = jnp.full((1, tm), float(N_GROUPS), F32)
        for g in range(N_GROUPS - 1, -1, -1):
            fi = jnp.where(gscore[g] == mx, float(g), fi)
        for g in range(N_GROUPS):
            hit = fi == float(g)
            gsel[g] = jnp.where(hit, 1.0, gsel[g])
            gscore[g] = jnp.where(hit, neg, gscore[g])
    cand = jnp.concatenate(
        [jnp.where(gsel[g] > 0.0, sel[g * GROUP_SIZE:(g + 1) * GROUP_SIZE, :], neg) for g in range(N_GROUPS)],
        axis=0)
    flat = lax.broadcasted_iota(I32, (N_EXPERTS, tm), 0).astype(F32)
    hits, gts = [], []
    chosen = jnp.zeros((N_EXPERTS, tm), F32)
    for _ in range(TOP_K):
        mx = jnp.max(cand, axis=0, keepdims=True)
        fk = jnp.min(jnp.where(cand == mx, flat, float(N_EXPERTS)), axis=0, keepdims=True)
        hit = flat == fk
        hits.append(hit)
        gts.append(jnp.sum(jnp.where(hit, sc, 0.0), axis=0, keepdims=True))
        chosen = jnp.where(hit, 1.0, chosen)
        cand = jnp.where(hit, neg, cand)
    gsum = gts[0]
    for g in gts[1:]:
        gsum = gsum + g
    gts = [g / gsum * ROUTED_SCALE for g in gts]

    before = (lax.broadcasted_iota(I32, (tm, tm), 0) < lax.broadcasted_iota(I32, (tm, tm), 1))
    rank_in = _dot(chosen.astype(BF16), jnp.where(before, 1.0, 0.0).astype(BF16))
    cnt = jnp.sum(chosen, axis=1, keepdims=True)
    cnt_pad = jnp.floor((cnt + (CH - 1.0)) * (1.0 / CH)) * CH
    below = (lax.broadcasted_iota(I32, (N_EXPERTS, N_EXPERTS), 1) < lax.broadcasted_iota(I32, (N_EXPERTS, N_EXPERTS), 0))
    start = _dot(jnp.where(below, 1.0, 0.0).astype(BF16),
                 jnp.broadcast_to(cnt_pad, (N_EXPERTS, LANES)).astype(BF16))[:, 0:1]
    pos = rank_in + start
    lpos = [jnp.sum(jnp.where(hit, pos, 0.0), axis=0, keepdims=True) for hit in hits]
    return lpos, gts, cnt_pad, start


def _slot_rows(vals, n_rows):
    tm = vals[0].shape[1]
    row = lax.broadcasted_iota(I32, (n_rows, tm), 0)
    out = jnp.zeros((n_rows, tm), F32)
    for k, v in enumerate(vals):
        out = jnp.where(row == k, v, out)
    return out


def _post_kernel(tbl_ref, xa_ref, xb_ref, mp_ref, ms_ref, mod_ref, wout_ref, g2_ref, shg_ref, shu_ref, shd_ref,
                 rwt_ref, rb_ref, base_ref, h2_ref, lpos_ref, ptok_ref, gtok_ref, cnt_ref, start_ref, rel_ref,
                 run_ref):
    i = pl.program_id(0)
    n_prompt_tiles = tbl_ref[2, 0]

    @pl.when(i == 0)
    def _():
        cnt_ref[...] = jnp.zeros_like(cnt_ref)
        start_ref[...] = jnp.zeros_like(start_ref)
        rel_ref[...] = jnp.zeros_like(rel_ref)
        run_ref[...] = jnp.zeros_like(run_ref)

    mod = mod_ref[...]
    gate1 = mod[:, 2 * D_MODEL:3 * D_MODEL]
    shift2 = mod[:, 3 * D_MODEL:4 * D_MODEL]
    scale2 = mod[:, 4 * D_MODEL:5 * D_MODEL]
    gate2 = mod[:, 5 * D_MODEL:6 * D_MODEL]
    mix = jnp.where(i < n_prompt_tiles, mp_ref[...], ms_ref[...])
    x1 = _tile_x(tbl_ref, xa_ref, xb_ref) + gate1 * _dot(mix, wout_ref[...])
    ms = jnp.mean(x1 * x1, axis=-1, keepdims=True)
    h2 = x1 * lax.rsqrt(ms + NORM_EPS) * g2_ref[...]
    h2 = h2 * (1.0 + scale2) + shift2
    hb = h2.astype(BF16)
    h2_ref[...] = hb
    act = _silu(_dot(hb, shg_ref[...])) * _dot(hb, shu_ref[...])
    base_ref[...] = x1 + gate2 * _dot(act.astype(BF16), shd_ref[...])

    lpos, gts, cnt_pad, start = _route(h2, rwt_ref, rb_ref)
    lpos_ref[...] = _slot_rows(lpos, SLOT_ROWS)
    ptok_ref[...] = _slot_rows(lpos, LANES).T
    gtok_ref[...] = _slot_rows(gts, LANES).T
    col = lax.broadcasted_iota(I32, cnt_ref.shape, 1)
    run = run_ref[...]
    cnt_ref[...] = jnp.where(col == i, cnt_pad.astype(I32), cnt_ref[...])
    start_ref[...] = jnp.where(col == i, start.astype(I32), start_ref[...])
    rel_ref[...] = jnp.where(col == i, run.astype(I32), rel_ref[...])
    run_ref[...] = run + cnt_pad


def _post_call(layer, tbl, xa, xb, mix_p, mix_s, mod4, wout, g2, shg, shu, shd, rwt, rb):
    npt = mix_p.shape[0] // TM
    n_tok = mix_p.shape[0] + mix_s.shape[0]
    nt = n_tok // TM
    const = lambda i, t: (0, 0)
    tile = lambda i, t: (i, 0)
    gs = pltpu.PrefetchScalarGridSpec(
        num_scalar_prefetch=1,
        grid=(nt,),
        in_specs=_x_specs(npt, xa is xb) + [
            pl.BlockSpec((TM, MIX_W), lambda i, t: (jnp.minimum(i, npt - 1), 0)),
            pl.BlockSpec((TM, MIX_W), lambda i, t: (jnp.maximum(i - npt, 0), 0)),
            _mod_spec(layer),
            _layer_spec(layer, MIX_W, D_MODEL),
            _layer_spec(layer, 1, D_MODEL),
            _layer_spec(layer, D_MODEL, EXPERT_FF),
            _layer_spec(layer, D_MODEL, EXPERT_FF),
            _layer_spec(layer, EXPERT_FF, D_MODEL),
            _layer_spec(layer, N_EXPERTS, D_MODEL),
            _layer_spec(layer, N_EXPERTS, 1),
        ],
        out_specs=[
            pl.BlockSpec((TM, D_MODEL), tile),
            pl.BlockSpec((TM, D_MODEL), tile),
            pl.BlockSpec((SLOT_ROWS, TM), lambda i, t: (0, i)),
            pl.BlockSpec((TM, LANES), tile),
            pl.BlockSpec((TM, LANES), tile),
            pl.BlockSpec((N_EXPERTS, LANES), const),
            pl.BlockSpec((N_EXPERTS, LANES), const),
            pl.BlockSpec((N_EXPERTS, LANES), const),
        ],
        scratch_shapes=[pltpu.VMEM((N_EXPERTS, 1), F32)],
    )
    assert nt <= LANES
    return pl.pallas_call(
        _post_kernel,
        grid_spec=gs,
        out_shape=[
            jax.ShapeDtypeStruct((n_tok, D_MODEL), F32),
            jax.ShapeDtypeStruct((n_tok, D_MODEL), BF16),
            jax.ShapeDtypeStruct((SLOT_ROWS, n_tok), F32),
            jax.ShapeDtypeStruct((n_tok, LANES), F32),
            jax.ShapeDtypeStruct((n_tok, LANES), F32),
            jax.ShapeDtypeStruct((N_EXPERTS, LANES), I32),
            jax.ShapeDtypeStruct((N_EXPERTS, LANES), I32),
            jax.ShapeDtypeStruct((N_EXPERTS, LANES), I32),
        ],
        compiler_params=_cparams(("arbitrary",)),
        name="post_route",
    )(tbl, xa, xb, mix_p, mix_s, mod4, wout, g2, shg, shu, shd, rwt, rb)


SEG_ROW0, SEG_NBLK, SEG_PAD0, SEG_NPAD, SEG_NEXT, SEG_USED = range(6)


def _plan_kernel(last_tile, cnt_ref, rel_ref, seg_ref):
    def per_expert(e, start):
        end = start + rel_ref[e, last_tile] + cnt_ref[e, last_tile]
        nb = lax.shift_right_logical(end - start + (EB - 1), LOG_EB)
        nxt = start + lax.shift_left(nb, LOG_EB)
        seg_ref[SEG_ROW0, e] = start
        seg_ref[SEG_NBLK, e] = nb
        seg_ref[SEG_PAD0, e] = end
        seg_ref[SEG_NPAD, e] = lax.shift_right_logical(nxt - end, LOG_CH)
        seg_ref[SEG_USED, e] = 0
        return nxt

    total = lax.fori_loop(0, N_EXPERTS, per_expert, jnp.int32(0))

    def link(k, nxt):
        e = N_EXPERTS - 1 - k
        seg_ref[SEG_NEXT, e] = nxt
        return jnp.where(seg_ref[SEG_NBLK, e] > 0, e, nxt)

    first = lax.fori_loop(0, N_EXPERTS, link, jnp.int32(N_EXPERTS))
    seg_ref[SEG_USED, 0] = lax.shift_right_logical(total, LOG_EB)
    seg_ref[SEG_USED, 1] = first


def _plan_call(cnt, rel, nt):
    smem = pl.BlockSpec(memory_space=pltpu.SMEM)
    return pl.pallas_call(
        functools.partial(_plan_kernel, nt - 1),
        in_specs=[smem, smem],
        out_specs=smem,
        out_shape=jax.ShapeDtypeStruct((6, N_EXPERTS), I32),
        name="moe_plan",
    )(cnt, rel)


def _rows_copy(src_ref, src_row, dst_ref, dst_row, n_rows, sem):
    return pltpu.make_async_copy(src_ref.at[pl.ds(pl.multiple_of(src_row, CH), n_rows)],
                                 dst_ref.at[pl.ds(pl.multiple_of(dst_row, CH), n_rows)], sem)


class _Runs:
    def __init__(self, cnt_ref, start_ref, rel_ref, seg_ref):
        self.cnt, self.start, self.rel, self.seg = cnt_ref, start_ref, rel_ref, seg_ref

    def start_copies(self, i, copy, tot_ref, slot):
        def per_pair(e2, carry):
            n_big, n_small = carry
            for par in range(2):
                e = 2 * e2 + par
                c = self.cnt[e, i]
                a0 = self.start[e, i]
                b0 = self.seg[SEG_ROW0, e] + self.rel[e, i]
                nb = lax.shift_right_logical(c, LOG_CH + 1)
                odd = jnp.bitwise_and(lax.shift_right_logical(c, LOG_CH), 1)

                def big(q, cc, a0=a0, b0=b0, par=par):
                    copy(a0 + q * (2 * CH), b0 + q * (2 * CH), 2 * CH).start(priority=par)
                    return cc

                lax.fori_loop(0, nb, big, 0)

                @pl.when(odd == 1)
                def _(a0=a0, b0=b0, nb=nb, par=par):
                    copy(a0 + nb * (2 * CH), b0 + nb * (2 * CH), CH).start(priority=par)

                n_big, n_small = n_big + nb, n_small + odd
            return n_big, n_small

        n_big, n_small = lax.fori_loop(0, N_EXPERTS // 2, per_pair, (jnp.int32(0), jnp.int32(0)))
        tot_ref[slot, 0] = n_big
        tot_ref[slot, 1] = n_small

    @staticmethod
    def wait_copies(copy, tot_ref, slot):
        def big(q, c):
            copy(0, 0, 2 * CH).wait()
            return c

        lax.fori_loop(0, tot_ref[slot, 0], big, 0)

        def small(q, c):
            copy(0, 0, CH).wait()
            return c

        lax.fori_loop(0, tot_ref[slot, 1], small, 0)


def _dispatch_kernel(reuse, cnt_ref, start_ref, rel_ref, seg_ref, h_ref, lpos_ref, *rest):
    xb_hbm, sort_s, zero_s, tot_s, sems = rest[1:] if reuse else rest
    i = pl.program_id(0)
    last = pl.num_programs(0) - 1
    slot = lax.rem(i, 2)
    runs = _Runs(cnt_ref, start_ref, rel_ref, seg_ref)
    lp = lpos_ref[...]
    hb = h_ref[...]
    blk = TM
    for r in range(SORT_ROWS // blk):
        srow = (lax.broadcasted_iota(I32, (blk, TM), 0) + r * blk).astype(F32)
        p = jnp.zeros((blk, TM), F32)
        for k in range(TOP_K):
            p = jnp.where(srow == lp[k:k + 1, :], 1.0, p)
        sort_s[slot, r * blk:(r + 1) * blk, :] = _dot(p.astype(BF16), hb).astype(BF16)

    def copy_from(sl):
        return lambda s, d, n: _rows_copy(sort_s.at[sl], s, xb_hbm, d, n, sems.at[sl])

    runs.start_copies(i, copy_from(slot), tot_s, slot)

    @pl.when(i > 0)
    def _():
        runs.wait_copies(copy_from(1 - slot), tot_s, 1 - slot)

    @pl.when(i == last)
    def _():
        runs.wait_copies(copy_from(slot), tot_s, slot)
        _zero_fill_unused(seg_ref, xb_hbm, zero_s, sems.at[0], tail=not reuse)


def _zero_fill_unused(seg_ref, buf_hbm, zero_s, sem, tail):
    zero_s[...] = jnp.zeros_like(zero_s)

    def per_expert(e, c):
        first = seg_ref[SEG_PAD0, e]

        def z_issue(r, cc):
            _rows_copy(zero_s, 0, buf_hbm, first + r * CH, CH, sem).start()
            return cc

        lax.fori_loop(0, seg_ref[SEG_NPAD, e], z_issue, 0)

        def z_drain(r, cc):
            _rows_copy(zero_s, 0, buf_hbm, 0, CH, sem).wait()
            return cc

        lax.fori_loop(0, seg_ref[SEG_NPAD, e], z_drain, 0)
        return c

    lax.fori_loop(0, N_EXPERTS, per_expert, 0)
    if tail:
        _zero_fill_tail(seg_ref, buf_hbm, zero_s, sem)


def _zero_fill_tail(seg_ref, buf_hbm, zero_s, sem):
    n_blocks = buf_hbm.shape[0] // EB

    def blk_copy(b):
        return pltpu.make_async_copy(zero_s, buf_hbm.at[pl.ds(pl.multiple_of(b * EB, EB), EB)], sem)

    def t_issue(b, cc):
        blk_copy(b).start()
        return cc

    lax.fori_loop(seg_ref[SEG_USED, 0], n_blocks, t_issue, 0)

    def t_drain(b, cc):
        blk_copy(0).wait()
        return cc

    lax.fori_loop(seg_ref[SEG_USED, 0], n_blocks, t_drain, 0)


def _dispatch_call(cnt, start, rel, seg, h2, lpos, n_rows, prev=None):
    n_tok = h2.shape[0]
    nt = n_tok // TM
    smem = pl.BlockSpec(memory_space=pltpu.SMEM)
    reuse = prev is not None
    return pl.pallas_call(
        functools.partial(_dispatch_kernel, reuse),
        grid=(nt,),
        in_specs=[
            smem, smem, smem, smem,
            pl.BlockSpec((TM, D_MODEL), lambda i: (i, 0)),
            pl.BlockSpec((SLOT_ROWS, TM), lambda i: (0, i)),
        ] + ([pl.BlockSpec(memory_space=pl.ANY)] if reuse else []),
        input_output_aliases={6: 0} if reuse else {},
        out_specs=pl.BlockSpec(memory_space=pl.ANY),
        out_shape=jax.ShapeDtypeStruct((n_rows, D_MODEL), BF16),
        scratch_shapes=[
            pltpu.VMEM((2, SORT_ROWS, D_MODEL), BF16),
            pltpu.VMEM((EB, D_MODEL), BF16),
            pltpu.SMEM((2, 2), I32),
            pltpu.SemaphoreType.DMA((2,)),
        ],
        compiler_params=_cparams(("arbitrary",)),
        name="moe_dispatch",
    )(cnt, start, rel, seg, h2, lpos, *([prev] if reuse else []))


X_SLOTS = 4


def _experts_kernel(layer, seg_ref, wg_hbm, wu_hbm, wd_hbm, xb_hbm, yb_hbm,
                    wgf_s, wuf_s, wdf_s, wg_s, wu_s, wd_s, x_s, y_s, sem_w, sem_x, sem_y):
    n_used = seg_ref[SEG_USED, 0]
    first = seg_ref[SEG_USED, 1]

    def rows(g):
        return pl.ds(pl.multiple_of(g * EB, EB), EB)

    def x_copy(g, slot):
        return pltpu.make_async_copy(xb_hbm.at[rows(g)], x_s.at[slot], sem_x.at[slot])

    def y_copy(g, slot):
        return pltpu.make_async_copy(y_s.at[slot], yb_hbm.at[rows(g)], sem_y.at[slot])

    def w_copies(e, slot):
        return [pltpu.make_async_copy(hbm.at[layer, e], buf.at[slot], sem_w.at[slot, n])
                for n, (hbm, buf) in enumerate(((wg_hbm, wgf_s), (wu_hbm, wuf_s), (wd_hbm, wdf_s)))]

    def fetch_weights(e, slot):
        for c in w_copies(e, slot):
            c.start()

    def take_weights(slot):
        for c in w_copies(0, slot):
            c.wait()
        wg_s[...] = wgf_s[slot].astype(BF16)
        wu_s[...] = wuf_s[slot].astype(BF16)
        wd_s[...] = wdf_s[slot].astype(BF16)

    def next_expert(e):
        return seg_ref[SEG_NEXT, jnp.minimum(e, N_EXPERTS - 1)]

    @pl.when(n_used > 0)
    def _():
        for p in range(X_SLOTS - 1):
            @pl.when(p < n_used)
            def _(p=p):
                x_copy(p, p).start()

        fetch_weights(first, 0)

        @pl.when(next_expert(first) < N_EXPERTS)
        def _():
            fetch_weights(next_expert(first), 1)

        take_weights(0)

        def block(g, carry):
            e, left, wslot = carry
            ahead = g + (X_SLOTS - 1)

            @pl.when(ahead < n_used)
            def _():
                x_copy(ahead, lax.rem(ahead, X_SLOTS)).start()

            xslot = lax.rem(g, X_SLOTS)
            yslot = lax.rem(g, 2)
            x_copy(g, xslot).wait()

            @pl.when(g >= 2)
            def _():
                y_copy(g - 2, yslot).wait()

            xb = x_s[xslot]
            act = _silu(_dot(xb, wg_s[...])) * _dot(xb, wu_s[...])
            y_s[yslot] = _dot(act.astype(BF16), wd_s[...]).astype(BF16)
            y_copy(g, yslot).start()

            switch = jnp.logical_and(left == 1, g + 1 < n_used)
            nxt = next_expert(e)

            @pl.when(switch)
            def _():
                take_weights(1 - wslot)

                @pl.when(next_expert(nxt) < N_EXPERTS)
                def _():
                    fetch_weights(next_expert(nxt), wslot)

            nxt_c = jnp.minimum(nxt, N_EXPERTS - 1)
            return (jnp.where(switch, nxt_c, e), jnp.where(switch, seg_ref[SEG_NBLK, nxt_c], left - 1),
                    jnp.where(switch, 1 - wslot, wslot))

        lax.fori_loop(0, n_used, block,
                      (first, seg_ref[SEG_NBLK, jnp.minimum(first, N_EXPERTS - 1)], jnp.int32(0)))

        @pl.when(n_used >= 2)
        def _():
            y_copy(n_used - 2, lax.rem(n_used, 2)).wait()

        y_copy(n_used - 1, lax.rem(n_used - 1, 2)).wait()


def _experts_call(seg, xb, layer, wg, wu, wd):
    n_rows = xb.shape[0]
    hbm = pl.BlockSpec(memory_space=pl.ANY)
    return pl.pallas_call(
        functools.partial(_experts_kernel, layer),
        in_specs=[pl.BlockSpec(memory_space=pltpu.SMEM), hbm, hbm, hbm, hbm],
        out_specs=hbm,
        out_shape=jax.ShapeDtypeStruct((n_rows, D_MODEL), BF16),
        input_output_aliases={4: 0},
        scratch_shapes=[
            pltpu.VMEM((2, D_MODEL, EXPERT_FF), F32),
            pltpu.VMEM((2, D_MODEL, EXPERT_FF), F32),
            pltpu.VMEM((2, EXPERT_FF, D_MODEL), F32),
            pltpu.VMEM((D_MODEL, EXPERT_FF), BF16),
            pltpu.VMEM((D_MODEL, EXPERT_FF), BF16),
            pltpu.VMEM((EXPERT_FF, D_MODEL), BF16),
            pltpu.VMEM((X_SLOTS, EB, D_MODEL), BF16),
            pltpu.VMEM((2, EB, D_MODEL), BF16),
            pltpu.SemaphoreType.DMA((2, 3)),
            pltpu.SemaphoreType.DMA((X_SLOTS,)),
            pltpu.SemaphoreType.DMA((2,)),
        ],
        compiler_params=pltpu.CompilerParams(vmem_limit_bytes=VMEM_LIMIT),
        name="moe_experts",
    )(seg, wg, wu, wd, xb)


def _combine_kernel(final, tbl_ref, cnt_ref, start_ref, rel_ref, seg_ref, yb_hbm, base_ref, gtok_ref, ptok_ref,
                    mod_ref, gf_ref, *rest):
    *out_refs, sort_s, tot_s, sems = rest
    i = pl.program_id(0)
    slot = lax.rem(i, 2)
    runs = _Runs(cnt_ref, start_ref, rel_ref, seg_ref)

    def copy_to(sl):
        return lambda s, d, n: _rows_copy(yb_hbm, d, sort_s.at[sl], s, n, sems.at[sl])

    @pl.when(i == 0)
    def _():
        sort_s[...] = jnp.zeros_like(sort_s)
        runs.start_copies(i, copy_to(slot), tot_s, slot)

    @pl.when(i + 1 < pl.num_programs(0))
    def _():
        runs.start_copies(i + 1, copy_to(1 - slot), tot_s, 1 - slot)

    runs.wait_copies(copy_to(slot), tot_s, slot)

    gt = gtok_ref[...]
    pt = ptok_ref[...]
    col = lax.broadcasted_iota(I32, (TM, SORT_ROWS), 1).astype(F32)
    w = jnp.zeros((TM, SORT_ROWS), F32)
    for k in range(TOP_K):
        w = jnp.where(col == pt[:, k:k + 1], gt[:, k:k + 1], w)
    routed = _dot(w.astype(BF16), sort_s[slot])
    gate2 = mod_ref[...][:, 5 * D_MODEL:6 * D_MODEL]
    y = base_ref[...] + gate2 * routed
    if final:
        y = y * lax.rsqrt(jnp.mean(y * y, axis=-1, keepdims=True) + NORM_EPS) * gf_ref[...]
        yp_ref, ys_ref = out_refs
        is_context = i < tbl_ref[2, 0]

        @pl.when(is_context)
        def _():
            yp_ref[...] = y

        @pl.when(jnp.logical_not(is_context))
        def _():
            ys_ref[...] = y
    else:
        out_refs[0][...] = y


def _combine_call(final, layer, npt, tbl, cnt, start, rel, seg, yb, base, gtok, ptok, mod4, gfinal):
    n_tok = base.shape[0]
    nt = n_tok // TM
    tile = lambda i, t: (i, 0)
    smem = pl.BlockSpec(memory_space=pltpu.SMEM)
    if final:
        out_specs = [pl.BlockSpec((TM, D_MODEL), lambda i, t: (jnp.minimum(i, npt - 1), 0)),
                     pl.BlockSpec((TM, D_MODEL), lambda i, t: (jnp.maximum(i - npt, 0), 0))]
        out_shape = [jax.ShapeDtypeStruct((npt * TM, D_MODEL), F32),
                     jax.ShapeDtypeStruct((n_tok - npt * TM, D_MODEL), F32)]
    else:
        out_specs = pl.BlockSpec((TM, D_MODEL), tile)
        out_shape = jax.ShapeDtypeStruct((n_tok, D_MODEL), F32)
    gs = pltpu.PrefetchScalarGridSpec(
        num_scalar_prefetch=1,
        grid=(nt,),
        in_specs=[
            smem, smem, smem, smem,
            pl.BlockSpec(memory_space=pl.ANY),
            pl.BlockSpec((TM, D_MODEL), tile),
            pl.BlockSpec((TM, LANES), tile),
            pl.BlockSpec((TM, LANES), tile),
            _mod_spec(layer),
            pl.BlockSpec((1, D_MODEL), lambda i, t: (0, 0)),
        ],
        out_specs=out_specs,
        scratch_shapes=[
            pltpu.VMEM((2, SORT_ROWS, D_MODEL), BF16),
            pltpu.SMEM((2, 2), I32),
            pltpu.SemaphoreType.DMA((2,)),
        ],
    )
    return pl.pallas_call(
        functools.partial(_combine_kernel, final),
        grid_spec=gs,
        out_shape=out_shape,
        compiler_params=_cparams(("arbitrary",)),
        name="moe_combine",
    )(tbl, cnt, start, rel, seg, yb, base, gtok, ptok, mod4, gfinal)


def _pad_cols(w, groups, width, slot):
    lead = w.shape[:-1]
    w = w.reshape(*lead, groups, width)
    pad = [(0, 0)] * (len(lead) + 1) + [(0, slot - width)]
    return jnp.pad(w, pad).reshape(*lead, groups * slot)


def _prep_w_in(w):
    c = np.cumsum([0, 256, 256, 256, 256, 256, 256, 256, 256, 128, 32])
    rq, rk, rv, rg, dq, dk, dv, cq, ckv, kpe = [w[..., c[n]:c[n + 1]] for n in range(10)]
    kpe_slot = jnp.concatenate([kpe, jnp.zeros_like(kpe), kpe, jnp.zeros_like(kpe)], axis=-1)
    cols = [rq, rk * (RET_DIM ** -0.5), rv, rg, _pad_cols(dq, DIFF_HEADS, 2 * DIFF_QK_DIM, SLOT),
            _pad_cols(dk, DIFF_HEADS, 2 * DIFF_QK_DIM, SLOT), dv, cq, ckv, kpe_slot,
            _pad_cols(dv, DIFF_HEADS, DIFF_V_DIM, SLOT)]
    return jnp.concatenate(cols, axis=-1).astype(BF16)


def _rope_tables(n_pos, dim, lane_offsets):
    f32 = np.float32
    n_rows = n_pos // GRID_W
    row = np.repeat(np.arange(n_rows, dtype=f32), GRID_W)
    col = np.tile(np.arange(GRID_W, dtype=f32), n_rows)
    half = dim // 2
    freqs = f32(ROPE_THETA) ** (-np.arange(0, half, 2, dtype=f32) / f32(half))
    ar = row[:, None] * freqs[None, :]
    ac = col[:, None] * freqs[None, :]
    ang = np.concatenate([ar, ar, ac, ac], axis=-1).astype(f32)
    cos, sin = np.cos(ang), np.sin(ang)
    first = (np.arange(dim) % 16) < 8
    sa = np.where(first[None, :], -sin, f32(0))
    sb = np.where(first[None, :], f32(0), sin)
    c_t = np.ones((TM + n_pos, SLOT), f32)
    a_t = np.zeros((TM + n_pos, SLOT), f32)
    b_t = np.zeros((TM + n_pos, SLOT), f32)
    for off in lane_offsets:
        c_t[TM:, off:off + dim] = cos
        a_t[TM:, off:off + dim] = sa
        b_t[TM:, off:off + dim] = sb
    return tuple(jnp.asarray(t) for t in (c_t, a_t, b_t))


def _block_diag_states(s):
    b, l, h, dk, dv = s.shape
    eye = jnp.eye(h, dtype=s.dtype)
    return jnp.einsum('blhkv,hg->blhkgv', s, eye).reshape(b, l, h * dk, h * dv)


def kernel(x_prompt, x_sample, cache_diff_k, cache_diff_v, cache_mla_ckv, cache_mla_kpe, state_ret_fwd, state_ret_bwd, c, c_ctx, w_ada, b_ada, norm_mix, norm_ffn, norm_final, w_in, ret_decay_fwd, ret_decay_bwd, diff_lambda, diff_subln, mla_q_norm, mla_w_uq, mla_kv_norm, mla_w_ukv, w_out, router_w, router_bias, exp_w_gate, exp_w_up, exp_w_down, sh_w_gate, sh_w_up, sh_w_down):
    n_pb, p_len, _ = x_prompt.shape
    n_sb, s_len, _ = x_sample.shape
    past_len = cache_diff_k.shape[3]
    n_p = n_pb * p_len
    n_s = n_sb * s_len
    n_tok = n_p + n_s
    nt = n_tok // TM
    npt = n_p // TM
    assert p_len == TM and s_len % TM == 0 and n_p % s_len == 0 and past_len % 8 == 0

    tiles = np.arange(nt)
    mod_row = np.where(tiles < npt, n_sb, (tiles - npt) // (s_len // TM))
    rope_blk = np.where(tiles < npt, 0, 1 + (tiles - npt) % (s_len // TM))
    tbl = jnp.asarray(np.stack([mod_row, rope_blk, np.full(nt, npt)]).astype(np.int32))

    n_cond = 16
    cond = jnp.concatenate([c, c_ctx[None, :], jnp.zeros((n_cond - n_sb - 1, D_MODEL), F32)], axis=0)
    mod_all = _modulation(cond, w_ada, b_ada)

    rope_d = _rope_tables(s_len, DIFF_QK_DIM, (0, DIFF_QK_DIM))
    rope_m = _rope_tables(s_len, MLA_ROPE_DIM, (KR_LO,))
    place = np.zeros((MLA_ROPE_DIM, SLOT), np.float32)
    place[np.arange(MLA_ROPE_DIM), KR_LO + np.arange(MLA_ROPE_DIM)] = 1.0
    place = jnp.asarray(place, BF16)
    cache_dv_t = cache_diff_v.transpose(0, 1, 3, 2, 4).reshape(n_sb, DEPTH, past_len, DIFF_V_W)
    s0f_bd = _block_diag_states(state_ret_fwd)
    s0b_bd = _block_diag_states(state_ret_bwd)

    n_blocks = pl.cdiv(n_tok * TOP_K + nt * N_EXPERTS * (CH - 1) + N_EXPERTS * (EB - CH), EB)
    n_rows = n_blocks * EB

    xa, xb = x_prompt.reshape(n_p, D_MODEL), x_sample.reshape(n_s, D_MODEL)
    gfinal = norm_final.reshape(1, D_MODEL)

    mod4 = mod_all.reshape(DEPTH, n_cond, 1, 6 * D_MODEL)
    w_pre = _prep_w_in(w_in)
    wuq = _pad_cols(mla_w_uq, MLA_HEADS, MLA_NOPE_DIM + MLA_ROPE_DIM, SLOT).astype(BF16)
    ukv = mla_w_ukv.reshape(DEPTH, MLA_KV_LORA, MLA_HEADS, MLA_NOPE_DIM + MLA_V_DIM)
    wk = _pad_cols(ukv[..., :MLA_NOPE_DIM].reshape(DEPTH, MLA_KV_LORA, -1), MLA_HEADS, MLA_NOPE_DIM, SLOT).astype(BF16)
    wv = ukv[..., MLA_NOPE_DIM:].reshape(DEPTH, MLA_KV_LORA, MLA_V_W).astype(BF16)
    decf = jnp.repeat(ret_decay_fwd, RET_DIM, axis=-1).reshape(DEPTH, 1, RET_W)
    decb = jnp.repeat(ret_decay_bwd, RET_DIM, axis=-1).reshape(DEPTH, 1, RET_W)
    subln = jnp.tile(diff_subln, (1, DIFF_HEADS)).reshape(DEPTH, 1, DIFF_V_W)
    vec = lambda p: p.reshape(DEPTH, 1, -1)
    wout_b, shg_b, shu_b, shd_b = (w.astype(BF16) for w in (w_out, sh_w_gate, sh_w_up, sh_w_down))
    rwt = router_w.transpose(0, 2, 1)
    rb = router_bias.reshape(DEPTH, N_EXPERTS, 1)

    new_ctx = []
    sorted_buf = None
    for l in range(DEPTH):
        lam_init = 0.8 - 0.6 * math.exp(-0.3 * l)
        ra, qa, ka, *caches = _pre_call(l, tbl, xa, xb, npt, n_tok, mod4, vec(norm_mix), w_pre, vec(mla_q_norm),
                                        wuq, vec(mla_kv_norm), rope_d, rope_m)
        mix_p, *states = _mix_prompt_call(lam_init, l, qa, ka, ra, n_pb, p_len, decf, decb,
                                          diff_lambda, subln, wk, wv)
        new_ctx.append(caches + states)
        mix_s = _mix_sample_call(lam_init, l, qa, ka, ra, n_p, n_sb, s_len, past_len, cache_diff_k,
                                 cache_dv_t, cache_mla_ckv, cache_mla_kpe, s0f_bd, s0b_bd, decf, decb,
                                 diff_lambda, subln, wk, wv, place)
        base, h2, lpos, ptok, gtok, cnt, start, rel = _post_call(
            l, tbl, xa, xb, mix_p, mix_s, mod4, wout_b, vec(norm_ffn), shg_b, shu_b, shd_b, rwt, rb)
        seg = _plan_call(cnt, rel, nt)
        sorted_buf = _dispatch_call(cnt, start, rel, seg, h2, lpos, n_rows, prev=sorted_buf)
        sorted_buf = _experts_call(seg, sorted_buf, l, exp_w_gate, exp_w_up, exp_w_down)
        final = l == DEPTH - 1
        out = _combine_call(final, l, npt, tbl, cnt, start, rel, seg, sorted_buf, base, gtok, ptok, mod4, gfinal)
        xa, xb = out if final else (out, out)

    y_prompt = xa.reshape(n_pb, p_len, D_MODEL)
    y_sample = xb.reshape(n_sb, s_len, D_MODEL)
    return (y_prompt, y_sample, *(jnp.stack(per_layer, axis=1) for per_layer in zip(*new_ctx)))
```

```python
import functools
import math

import numpy as np
import jax
import jax.numpy as jnp
from jax import lax
from jax.experimental import pallas as pl
from jax.experimental.pallas import tpu as pltpu

F32 = jnp.float32
BF16 = jnp.bfloat16
I32 = jnp.int32

D_MODEL = 1024
DEPTH = 2
GRID_W = 64
ROPE_THETA = 10000.0
NORM_EPS = 1e-6

RET_HEADS = 4
RET_DIM = 64
RET_CHUNK = 128
RET_W = RET_HEADS * RET_DIM
DIFF_HEADS = 4
DIFF_QK_DIM = 32
DIFF_V_DIM = 64
DIFF_V_W = DIFF_HEADS * DIFF_V_DIM
MLA_HEADS = 8
MLA_Q_LORA = 256
MLA_KV_LORA = 128
MLA_NOPE_DIM = 64
MLA_ROPE_DIM = 32
MLA_V_DIM = 64
MLA_V_W = MLA_HEADS * MLA_V_DIM
MIX_W = RET_W + DIFF_V_W + MLA_V_W

N_EXPERTS = 64
TOP_K = 6
N_GROUPS = 8
GROUP_SIZE = N_EXPERTS // N_GROUPS
TOPK_GROUPS = 4
EXPERT_FF = 256
ROUTED_SCALE = 2.5

LANES = 128
SLOT = LANES
TM = 256
SLOT_ROWS = 8
EB = 512
LOG_EB = 9
CH = 16
LOG_CH = 4
SORT_ROWS = 2560
VMEM_LIMIT = 48 * 1024 * 1024

C_RQ, C_RK, C_RV, C_RG = 0, 256, 512, 768
C_DQ, C_DK, C_DV, C_CQ, C_CKV, C_KPE = 1024, 1280, 1536, 1792, 2048, 2176
N_PRE = 2304
QA_W = 4 * SLOT + MLA_HEADS * SLOT
KA_DK, KA_DV, KA_CKV, KA_KPE = 0, 512, 768, 896
KA_W = 1024
KR_LO, KR_HI = 64, 96


def _dot(a, b):
    return jnp.dot(a, b, preferred_element_type=F32)


def _dot_nt(a, b):
    return lax.dot_general(a, b, (((1,), (1,)), ((), ())), preferred_element_type=F32)


def _dot_tn(a, b):
    return lax.dot_general(a, b, (((0,), (0,)), ((), ())), preferred_element_type=F32)


def _split_dot(x, w_bf16):
    hi = x.astype(BF16)
    lo = (x - hi.astype(F32)).astype(BF16)
    return _dot(hi, w_bf16) + _dot(lo, w_bf16)


def _split_dot_nt(w, x):
    wh = w.astype(BF16)
    wl = (w - wh.astype(F32)).astype(BF16)
    xh = x.astype(BF16)
    xl = (x - xh.astype(F32)).astype(BF16)
    return _dot_nt(wh, xh) + _dot_nt(wh, xl) + _dot_nt(wl, xh)


def _silu(x):
    return x * jax.nn.sigmoid(x)


def _cparams(sem):
    return pltpu.CompilerParams(dimension_semantics=sem, vmem_limit_bytes=VMEM_LIMIT)


MOD_TN = 512


def _mod_kernel(c_ref, w_ref, b_ref, o_ref):
    s = _silu(c_ref[...])
    o_ref[...] = _split_dot3(s, w_ref[...]) + b_ref[...]


def _split_dot3(x, w):
    xh = x.astype(BF16)
    xl = (x - xh.astype(F32)).astype(BF16)
    wh = w.astype(BF16)
    wl = (w - wh.astype(F32)).astype(BF16)
    return _dot(xh, wh) + _dot(xh, wl) + _dot(xl, wh)


def _modulation(cond, w_ada, b_ada):
    n_rows = cond.shape[0]
    n_out = w_ada.shape[-1]
    return pl.pallas_call(
        _mod_kernel,
        grid=(DEPTH, n_out // MOD_TN),
        in_specs=[
            pl.BlockSpec((n_rows, D_MODEL), lambda l, j: (0, 0)),
            pl.BlockSpec((None, D_MODEL, MOD_TN), lambda l, j: (l, 0, j)),
            pl.BlockSpec((None, 1, MOD_TN), lambda l, j: (l, 0, j)),
        ],
        out_specs=pl.BlockSpec((None, n_rows, MOD_TN), lambda l, j: (l, 0, j)),
        out_shape=jax.ShapeDtypeStruct((DEPTH, n_rows, n_out), F32),
        compiler_params=_cparams(("arbitrary", "arbitrary")),
        name="adaln_mod",
    )(cond, w_ada, b_ada.reshape(DEPTH, 1, n_out))


def _rope_slot(x, cos, sa, sb):
    up = pltpu.roll(x, LANES - 8, 1)
    dn = pltpu.roll(x, 8, 1)
    return x * cos + up * sa + dn * sb


def _tile_x(tbl_ref, xa_ref, xb_ref):
    return jnp.where(pl.program_id(0) < tbl_ref[2, 0], xa_ref[...], xb_ref[...])


def _x_specs(npt, combined):
    off = 0 if combined else npt
    return [pl.BlockSpec((TM, D_MODEL), lambda i, t: (jnp.minimum(i, npt - 1), 0)),
            pl.BlockSpec((TM, D_MODEL), lambda i, t: (jnp.maximum(i, npt) - off, 0))]


def _head_slots(x):
    half = SLOT // 2
    low = _lane_iota((x.shape[0], SLOT)) < half
    slots = []
    for c in range(x.shape[1] // SLOT):
        pair = x[:, c * SLOT:(c + 1) * SLOT]
        slots.append(jnp.where(low, pair, 0.0))
        slots.append(jnp.where(low, pltpu.roll(pair, half, 1), 0.0))
    return slots


def _pre_kernel(tbl_ref, xa_ref, xb_ref, mod_ref, g_ref, w_ref, qg_ref, wuq_ref, kvg_ref,
                cd_ref, sad_ref, sbd_ref, cm_ref, sam_ref, sbm_ref,
                ra_ref, qa_ref, ka_ref, dkc_ref, dvc_ref, ckvc_ref, kpec_ref):
    x = _tile_x(tbl_ref, xa_ref, xb_ref)
    mod = mod_ref[...]
    shift1 = mod[:, 0:D_MODEL]
    scale1 = mod[:, D_MODEL:2 * D_MODEL]
    ms = jnp.mean(x * x, axis=-1, keepdims=True)
    h = x * lax.rsqrt(ms + NORM_EPS) * g_ref[...]
    h = h * (1.0 + scale1) + shift1
    hb = h.astype(BF16)

    def proj(lo, hi):
        return _dot(hb, w_ref[:, lo:hi])

    ra_ref[:, C_RQ:C_RK] = proj(C_RQ, C_RK).astype(BF16)
    ra_ref[:, C_RK:C_RV] = (proj(C_RK, C_RV) * (RET_DIM ** -0.5)).astype(BF16)
    ra_ref[:, C_RV:C_RG] = proj(C_RV, C_RG).astype(BF16)
    ra_ref[:, C_RG:C_DQ] = _silu(proj(C_RG, C_DQ)).astype(BF16)

    cd, sad, sbd = cd_ref[...], sad_ref[...], sbd_ref[...]
    cm, sam, sbm = cm_ref[...], sam_ref[...], sbm_ref[...]
    dq_slots = _head_slots(proj(C_DQ, C_DK))
    dk_slots = [_rope_slot(s, cd, sad, sbd) for s in _head_slots(proj(C_DK, C_DV))]
    for hd in range(DIFF_HEADS):
        qa_ref[:, hd * SLOT:(hd + 1) * SLOT] = _rope_slot(dq_slots[hd], cd, sad, sbd).astype(BF16)
        ka_ref[:, KA_DK + hd * SLOT:KA_DK + (hd + 1) * SLOT] = dk_slots[hd].astype(BF16)
    dv = proj(C_DV, C_CQ)
    ka_ref[:, KA_DV:KA_CKV] = dv.astype(BF16)

    cq = proj(C_CQ, C_CKV)
    cqn = cq * lax.rsqrt(jnp.mean(cq * cq, axis=-1, keepdims=True) + NORM_EPS) * qg_ref[...]
    qm = _dot(cqn.astype(BF16), wuq_ref[...])
    for hd in range(MLA_HEADS):
        sl = slice(hd * SLOT, (hd + 1) * SLOT)
        qa_ref[:, 4 * SLOT + hd * SLOT:4 * SLOT + (hd + 1) * SLOT] = _rope_slot(qm[:, sl], cm, sam, sbm).astype(BF16)

    ckv = proj(C_CKV, C_KPE)
    ckvn = ckv * lax.rsqrt(jnp.mean(ckv * ckv, axis=-1, keepdims=True) + NORM_EPS) * kvg_ref[...]
    kpe = proj(C_KPE, N_PRE)
    kpe_slot = _rope_slot(kpe + pltpu.roll(kpe, SLOT // 2, 1), cm, sam, sbm)
    ka_ref[:, KA_CKV:KA_KPE] = ckvn.astype(BF16)
    ka_ref[:, KA_KPE:KA_W] = kpe_slot.astype(BF16)

    @pl.when(pl.program_id(0) < tbl_ref[2, 0])
    def _():
        dv_slots = _head_slots(dv)
        for hd in range(DIFF_HEADS):
            dkc_ref[hd] = dk_slots[hd][:, 0:2 * DIFF_QK_DIM]
            dvc_ref[hd] = dv_slots[hd][:, 0:DIFF_V_DIM]
        ckvc_ref[...] = ckvn
        kpec_ref[...] = kpe_slot[:, 0:MLA_ROPE_DIM]


def _layer_spec(layer, rows, cols):
    return pl.BlockSpec((None, rows, cols), lambda *_: (layer, 0, 0))


def _mod_spec(layer):
    return pl.BlockSpec((None, None, 1, 6 * D_MODEL), lambda i, t: (layer, t[0, i], 0, 0))


def _pre_call(layer, tbl, xa, xb, npt, n_tok, mod4, g, w_pre, qg, wuq, kvg, rope_d, rope_m):
    nt = n_tok // TM
    tile = lambda i, t: (i, 0)
    rope = lambda i, t: (t[1, i], 0)
    ctx5 = lambda i, t: (jnp.minimum(i, npt - 1), 0, 0, 0)
    ctx4 = lambda i, t: (jnp.minimum(i, npt - 1), 0, 0)
    cache_shapes = [
        jax.ShapeDtypeStruct((npt, DIFF_HEADS, TM, 2 * DIFF_QK_DIM), F32),
        jax.ShapeDtypeStruct((npt, DIFF_HEADS, TM, DIFF_V_DIM), F32),
        jax.ShapeDtypeStruct((npt, TM, MLA_KV_LORA), F32),
        jax.ShapeDtypeStruct((npt, TM, MLA_ROPE_DIM), F32),
    ]
    gs = pltpu.PrefetchScalarGridSpec(
        num_scalar_prefetch=1,
        grid=(nt,),
        in_specs=_x_specs(npt, xa is xb) + [
            _mod_spec(layer),
            _layer_spec(layer, 1, D_MODEL),
            _layer_spec(layer, D_MODEL, N_PRE),
            _layer_spec(layer, 1, MLA_Q_LORA),
            _layer_spec(layer, MLA_Q_LORA, MLA_HEADS * SLOT),
            _layer_spec(layer, 1, MLA_KV_LORA),
        ] + [pl.BlockSpec((TM, SLOT), rope)] * 6,
        out_specs=[
            pl.BlockSpec((TM, D_MODEL), tile),
            pl.BlockSpec((TM, QA_W), tile),
            pl.BlockSpec((TM, KA_W), tile),
            pl.BlockSpec((None, DIFF_HEADS, TM, 2 * DIFF_QK_DIM), ctx5),
            pl.BlockSpec((None, DIFF_HEADS, TM, DIFF_V_DIM), ctx5),
            pl.BlockSpec((None, TM, MLA_KV_LORA), ctx4),
            pl.BlockSpec((None, TM, MLA_ROPE_DIM), ctx4),
        ],
    )
    return pl.pallas_call(
        _pre_kernel,
        grid_spec=gs,
        out_shape=[
            jax.ShapeDtypeStruct((n_tok, D_MODEL), BF16),
            jax.ShapeDtypeStruct((n_tok, QA_W), BF16),
            jax.ShapeDtypeStruct((n_tok, KA_W), BF16),
        ] + cache_shapes,
        compiler_params=_cparams(("arbitrary",)),
        name="pre_proj",
    )(tbl, xa, xb, mod4, g, w_pre, qg, wuq, kvg, *rope_d, *rope_m)


def _lane_iota(shape):
    return lax.broadcasted_iota(I32, shape, len(shape) - 1)


def _head_mask(n_rows, width, head, head_w):
    lane = _lane_iota((n_rows, width))
    return (lane >= head * head_w) & (lane < (head + 1) * head_w)


def _seg_mean_sq(o, bd_ones):
    return _split_dot(o * o, bd_ones) * (1.0 / RET_DIM)


def _block_diag_ones(n, blk):
    r = lax.broadcasted_iota(I32, (n, n), 0) // blk
    c = lax.broadcasted_iota(I32, (n, n), 1) // blk
    return r == c


def _retention(ra_ref, seq_len, decf_ref, decb_ref, s0f, s0b):
    C = RET_CHUNK
    nc = seq_len // C
    lgf = -jnp.exp(decf_ref[...])
    lgb = -jnp.exp(decb_ref[...])
    pos = lax.broadcasted_iota(I32, (C, RET_W), 0).astype(F32)
    qdf = jnp.exp((pos + 1.0) * lgf)
    kdf = jnp.exp((C - 1.0 - pos) * lgf)
    cdf = jnp.exp(float(C) * lgf)
    qdb = jnp.exp((C - pos) * lgb)
    kdb = jnp.exp(pos * lgb)
    cdb = jnp.exp(float(C) * lgb)
    ii = lax.broadcasted_iota(I32, (C, C), 0).astype(F32)
    jj = lax.broadcasted_iota(I32, (C, C), 1).astype(F32)
    dist = ii - jj
    dmats = []
    for hd in range(RET_HEADS):
        lf = lgf[:, hd * RET_DIM:hd * RET_DIM + 1]
        lb = lgb[:, hd * RET_DIM:hd * RET_DIM + 1]
        dmats.append(jnp.where(dist >= 0, jnp.exp(dist * lf), jnp.exp(-dist * lb)))
    bd = _block_diag_ones(RET_W, RET_DIM)
    bd_ones = jnp.where(bd, 1.0, 0.0).astype(BF16)

    def chunk(n):
        rows = slice(n * C, (n + 1) * C)
        return (ra_ref[rows, C_RQ:C_RK], ra_ref[rows, C_RK:C_RV], ra_ref[rows, C_RV:C_RG])

    cross = [None] * nc
    sf = s0f
    for n in range(nc):
        q, k, v = chunk(n)
        cross[n] = _dot((q * qdf).astype(BF16), sf.astype(BF16))
        kv = _dot_tn((k * kdf).astype(BF16), v.astype(BF16))
        sf = sf * cdf + jnp.where(bd, kv, 0.0)
    sb = s0b
    for n in range(nc - 1, -1, -1):
        q, k, v = chunk(n)
        cross[n] = cross[n] + _dot((q * qdb).astype(BF16), sb.astype(BF16))
        kv = _dot_tn((k * kdb).astype(BF16), v.astype(BF16))
        sb = sb * cdb + jnp.where(bd, kv, 0.0)

    outs = []
    for n in range(nc):
        q, k, v = chunk(n)
        kb = k.astype(BF16)
        vb = v.astype(BF16)
        o = cross[n]
        for hd in range(RET_HEADS):
            hm = _head_mask(C, RET_W, hd, RET_DIM)
            sc = _dot_nt(jnp.where(hm, q, 0.0).astype(BF16), kb) * dmats[hd]
            o = o + jnp.where(hm, _dot(sc.astype(BF16), vb), 0.0)
        on = o * lax.rsqrt(_seg_mean_sq(o, bd_ones) + NORM_EPS)
        outs.append(on * ra_ref[n * C:(n + 1) * C, C_RG:C_DQ])
    return outs, sf, sb


def _softmax_pv(s_parts, v_parts, scale):
    m = None
    for s in s_parts:
        mm = jnp.max(s, axis=-1, keepdims=True)
        m = mm if m is None else jnp.maximum(m, mm)
    m = m * scale
    acc = None
    den = None
    for s, v in zip(s_parts, v_parts):
        e = jnp.exp(s * scale - m)
        ds = jnp.sum(e, axis=-1, keepdims=True)
        pv = _dot(e.astype(BF16), v)
        acc = pv if acc is None else acc + pv
        den = ds if den is None else den + ds
    return acc / den


def _diff_attention(dq, k_parts, v_parts, lam, subln, lam_init, bd_ones):
    lq = dq.shape[0]
    scale = DIFF_QK_DIM ** -0.5
    lane = _lane_iota((lq, SLOT))
    out = jnp.zeros((lq, DIFF_V_W), F32)
    for hd in range(DIFF_HEADS):
        qh = dq[:, hd * SLOT:(hd + 1) * SLOT]
        q1 = jnp.where(lane < DIFF_QK_DIM, qh, 0.0).astype(BF16)
        q2 = jnp.where(lane >= DIFF_QK_DIM, qh, 0.0).astype(BF16)
        s1 = [_dot_nt(q1[:, :kp[hd].shape[1]], kp[hd]) for kp in k_parts]
        s2 = [_dot_nt(q2[:, :kp[hd].shape[1]], kp[hd]) for kp in k_parts]
        o = _softmax_pv(s1, v_parts, scale) - lam * _softmax_pv(s2, v_parts, scale)
        out = jnp.where(_head_mask(lq, DIFF_V_W, hd, DIFF_V_DIM), o, out)
    on = out * lax.rsqrt(_seg_mean_sq(out, bd_ones) + NORM_EPS) * subln
    return on * (1.0 - lam_init)


def _mla_attention(qm, k_parts, v_parts):
    lq = qm.shape[0]
    scale = (MLA_NOPE_DIM + MLA_ROPE_DIM) ** -0.5
    halves = []
    for g in range(2):
        out = jnp.zeros((lq, 256), F32)
        for hh in range(4):
            hd = 4 * g + hh
            qh = qm[:, hd * SLOT:(hd + 1) * SLOT].astype(BF16)
            s = [_dot_nt(qh, kp[:, hd * SLOT:(hd + 1) * SLOT]) for kp in k_parts]
            o = _softmax_pv(s, [vp[:, 256 * g:256 * (g + 1)] for vp in v_parts], scale)
            out = jnp.where(_head_mask(lq, 256, hh, MLA_V_DIM), o, out)
        halves.append(out)
    return halves


def _mla_keys(ka_val_ckv, kr_slot, wk_ref, wv_ref):
    cb = ka_val_ckv.astype(BF16)
    kn = _dot(cb, wk_ref[...])
    ks = [(kn[:, hd * SLOT:(hd + 1) * SLOT] + kr_slot).astype(BF16) for hd in range(MLA_HEADS)]
    return jnp.concatenate(ks, axis=1), _dot(cb, wv_ref[...]).astype(BF16)


def _kr_only(kpe_slot):
    lane = _lane_iota(kpe_slot.shape)
    return jnp.where((lane >= KR_LO) & (lane < KR_HI), kpe_slot, 0.0)


def _diff_lambda(dl_ref, lam_init):
    dl = dl_ref[...]
    a = jnp.sum(dl[0:1] * dl[1:2], axis=-1, keepdims=True)
    b = jnp.sum(dl[2:3] * dl[3:4], axis=-1, keepdims=True)
    return jnp.exp(a) - jnp.exp(b) + lam_init


def _mix_prompt_kernel(lam_init, qa_ref, ka_ref, ra_ref, decf_ref, decb_ref, dl_ref, subln_ref,
                       wk_ref, wv_ref, mix_ref, sf_ref, sb_ref):
    seq = qa_ref.shape[0]
    zero_state = jnp.zeros((RET_W, RET_W), F32)
    outs, sf, sb = _retention(ra_ref, seq, decf_ref, decb_ref, zero_state, zero_state)
    for n, o in enumerate(outs):
        mix_ref[n * RET_CHUNK:(n + 1) * RET_CHUNK, 0:RET_W] = o.astype(BF16)
    for hd in range(RET_HEADS):
        blk = slice(hd * RET_DIM, (hd + 1) * RET_DIM)
        sf_ref[hd] = sf[blk, blk]
        sb_ref[hd] = sb[blk, blk]

    bd_ones = jnp.where(_block_diag_ones(DIFF_V_W, DIFF_V_DIM), 1.0, 0.0).astype(BF16)
    lam = _diff_lambda(dl_ref, lam_init)
    kd = [ka_ref[:, KA_DK + hd * SLOT:KA_DK + (hd + 1) * SLOT].astype(BF16) for hd in range(DIFF_HEADS)]
    vd = ka_ref[:, KA_DV:KA_CKV].astype(BF16)
    mix_ref[:, RET_W:RET_W + DIFF_V_W] = _diff_attention(
        qa_ref[:, 0:4 * SLOT], [kd], [vd], lam, subln_ref[...], lam_init, bd_ones).astype(BF16)

    km, vm = _mla_keys(ka_ref[:, KA_CKV:KA_KPE], _kr_only(ka_ref[:, KA_KPE:KA_W]), wk_ref, wv_ref)
    halves = _mla_attention(qa_ref[:, 4 * SLOT:QA_W], [km], [vm])
    mix_ref[:, 512:768] = halves[0].astype(BF16)
    mix_ref[:, 768:1024] = halves[1].astype(BF16)


def _mixer_param_specs(layer):
    return [
        _layer_spec(layer, 1, RET_W),
        _layer_spec(layer, 1, RET_W),
        _layer_spec(layer, 4, DIFF_QK_DIM),
        _layer_spec(layer, 1, DIFF_V_W),
        _layer_spec(layer, MLA_KV_LORA, MLA_HEADS * SLOT),
        _layer_spec(layer, MLA_KV_LORA, MLA_V_W),
    ]


def _mix_prompt_call(lam_init, layer, qa, ka, ra, n_seq, seq_len, decf, decb, dl, subln, wk, wv):
    seq = lambda b: (b, 0)
    state_spec = pl.BlockSpec((None, RET_HEADS, RET_DIM, RET_DIM), lambda b: (b, 0, 0, 0))
    state_shape = jax.ShapeDtypeStruct((n_seq, RET_HEADS, RET_DIM, RET_DIM), F32)
    return pl.pallas_call(
        functools.partial(_mix_prompt_kernel, lam_init),
        grid=(n_seq,),
        in_specs=[
            pl.BlockSpec((seq_len, QA_W), seq),
            pl.BlockSpec((seq_len, KA_W), seq),
            pl.BlockSpec((seq_len, D_MODEL), seq),
        ] + _mixer_param_specs(layer),
        out_specs=[pl.BlockSpec((seq_len, MIX_W), seq), state_spec, state_spec],
        out_shape=[jax.ShapeDtypeStruct((n_seq * seq_len, MIX_W), BF16), state_shape, state_shape],
        compiler_params=_cparams(("arbitrary",)),
        name="mix_prompt",
    )(qa, ka, ra, decf, decb, dl, subln, wk, wv)


def _mix_sample_kernel(lam_init, qa_ref, ka_ref, ra_ref, ckd_ref, cvd_ref, cckv_ref, ckpe_ref,
                       s0f_ref, s0b_ref, decf_ref, decb_ref, dl_ref, subln_ref, wk_ref, wv_ref,
                       place_ref, mix_ref,
                       ret_s, kdn_s, vdn_s, kdc_s, vdc_s, kmn_s, vmn_s, kmc_s, vmc_s):
    j = pl.program_id(1)
    seq = ka_ref.shape[0]

    @pl.when(j == 0)
    def _():
        outs, _, _ = _retention(ra_ref, seq, decf_ref, decb_ref, s0f_ref[...], s0b_ref[...])
        for n, o in enumerate(outs):
            ret_s[n * RET_CHUNK:(n + 1) * RET_CHUNK, :] = o
        kdn_s[...] = ka_ref[:, KA_DK:KA_DV].astype(BF16)
        vdn_s[...] = ka_ref[:, KA_DV:KA_CKV].astype(BF16)
        kdc_s[...] = ckd_ref[...].astype(BF16)
        vdc_s[...] = cvd_ref[...].astype(BF16)
        km, vm = _mla_keys(ka_ref[:, KA_CKV:KA_KPE], _kr_only(ka_ref[:, KA_KPE:KA_W]), wk_ref, wv_ref)
        kmn_s[...] = km
        vmn_s[...] = vm
        kr_ctx = _dot(ckpe_ref[...].astype(BF16), place_ref[...])
        km, vm = _mla_keys(cckv_ref[...], kr_ctx, wk_ref, wv_ref)
        kmc_s[...] = km
        vmc_s[...] = vm

    row0 = pl.multiple_of(j * TM, TM)
    mix_ref[:, 0:RET_W] = ret_s[pl.ds(row0, TM), :].astype(BF16)

    bd_ones = jnp.where(_block_diag_ones(DIFF_V_W, DIFF_V_DIM), 1.0, 0.0).astype(BF16)
    lam = _diff_lambda(dl_ref, lam_init)
    kd_ctx = [kdc_s[hd] for hd in range(DIFF_HEADS)]
    kd_new = [kdn_s[:, hd * SLOT:(hd + 1) * SLOT] for hd in range(DIFF_HEADS)]
    mix_ref[:, RET_W:RET_W + DIFF_V_W] = _diff_attention(
        qa_ref[:, 0:4 * SLOT], [kd_ctx, kd_new], [vdc_s[...], vdn_s[...]], lam, subln_ref[...],
        lam_init, bd_ones).astype(BF16)

    halves = _mla_attention(qa_ref[:, 4 * SLOT:QA_W], [kmc_s[...], kmn_s[...]], [vmc_s[...], vmn_s[...]])
    mix_ref[:, 512:768] = halves[0].astype(BF16)
    mix_ref[:, 768:1024] = halves[1].astype(BF16)


def _mix_sample_call(lam_init, layer, qa, ka, ra, tok0, n_seq, seq_len, past_len, cache_dk, cache_dv_t,
                     cache_ckv, cache_kpe, s0f_bd, s0b_bd, decf, decb, dl, subln, wk, wv, place):
    nq = seq_len // TM
    q0 = tok0 // TM
    s0 = tok0 // seq_len
    const = lambda b, j: (0, 0)
    return pl.pallas_call(
        functools.partial(_mix_sample_kernel, lam_init),
        grid=(n_seq, nq),
        in_specs=[
            pl.BlockSpec((TM, QA_W), lambda b, j: (q0 + b * nq + j, 0)),
            pl.BlockSpec((seq_len, KA_W), lambda b, j: (s0 + b, 0)),
            pl.BlockSpec((seq_len, D_MODEL), lambda b, j: (s0 + b, 0)),
            pl.BlockSpec((None, None, DIFF_HEADS, past_len, 2 * DIFF_QK_DIM), lambda b, j: (b, layer, 0, 0, 0)),
            pl.BlockSpec((None, None, past_len, DIFF_V_W), lambda b, j: (b, layer, 0, 0)),
            pl.BlockSpec((None, None, past_len, MLA_KV_LORA), lambda b, j: (b, layer, 0, 0)),
            pl.BlockSpec((None, None, past_len, MLA_ROPE_DIM), lambda b, j: (b, layer, 0, 0)),
            pl.BlockSpec((None, None, RET_W, RET_W), lambda b, j: (b, layer, 0, 0)),
            pl.BlockSpec((None, None, RET_W, RET_W), lambda b, j: (b, layer, 0, 0)),
        ] + _mixer_param_specs(layer) + [
            pl.BlockSpec((MLA_ROPE_DIM, SLOT), const),
        ],
        out_specs=pl.BlockSpec((TM, MIX_W), lambda b, j: (b * nq + j, 0)),
        out_shape=jax.ShapeDtypeStruct((n_seq * seq_len, MIX_W), BF16),
        scratch_shapes=[
            pltpu.VMEM((seq_len, RET_W), F32),
            pltpu.VMEM((seq_len, 4 * SLOT), BF16),
            pltpu.VMEM((seq_len, DIFF_V_W), BF16),
            pltpu.VMEM((DIFF_HEADS, past_len, 2 * DIFF_QK_DIM), BF16),
            pltpu.VMEM((past_len, DIFF_V_W), BF16),
            pltpu.VMEM((seq_len, MLA_HEADS * SLOT), BF16),
            pltpu.VMEM((seq_len, MLA_V_W), BF16),
            pltpu.VMEM((past_len, MLA_HEADS * SLOT), BF16),
            pltpu.VMEM((past_len, MLA_V_W), BF16),
        ],
        compiler_params=_cparams(("arbitrary", "arbitrary")),
        name="mix_sample",
    )(qa, ka, ra, cache_dk, cache_dv_t, cache_ckv, cache_kpe, s0f_bd, s0b_bd,
      decf, decb, dl, subln, wk, wv, place)


def _route(h2, rwt_ref, rb_ref):
    tm = h2.shape[0]
    neg = -jnp.inf
    logits = _split_dot_nt(rwt_ref[...], h2)
    sc = jax.nn.sigmoid(logits)
    sel = sc + rb_ref[...]
    member = lax.broadcasted_iota(I32, (GROUP_SIZE, tm), 0).astype(F32)
    gscore = []
    for g in range(N_GROUPS):
        sg = sel[g * GROUP_SIZE:(g + 1) * GROUP_SIZE, :]
        m1 = jnp.max(sg, axis=0, keepdims=True)
        f1 = jnp.min(jnp.where(sg == m1, member, float(GROUP_SIZE)), axis=0, keepdims=True)
        m2 = jnp.max(jnp.where(member == f1, neg, sg), axis=0, keepdims=True)
        gscore.append(m1 + m2)
    gsel = [jnp.zeros((1, tm), F32) for _ in range(N_GROUPS)]
    for _ in range(TOPK_GROUPS):
        mx = gscore[0]
        for g in range(1, N_GROUPS):
            mx = jnp.maximum(mx, gscore[g])
        fi = jnp.full((1, tm), float(N_GROUPS), F32)
        for g in range(N_GROUPS - 1, -1, -1):
            fi = jnp.where(gscore[g] == mx, float(g), fi)
        for g in range(N_GROUPS):
            hit = fi == float(g)
            gsel[g] = jnp.where(hit, 1.0, gsel[g])
            gscore[g] = jnp.where(hit, neg, gscore[g])
    cand = jnp.concatenate(
        [jnp.where(gsel[g] > 0.0, sel[g * GROUP_SIZE:(g + 1) * GROUP_SIZE, :], neg) for g in range(N_GROUPS)],
        axis=0)
    flat = lax.broadcasted_iota(I32, (N_EXPERTS, tm), 0).astype(F32)
    hits, gts = [], []
    chosen = jnp.zeros((N_EXPERTS, tm), F32)
    for _ in range(TOP_K):
        mx = jnp.max(cand, axis=0, keepdims=True)
        fk = jnp.min(jnp.where(cand == mx, flat, float(N_EXPERTS)), axis=0, keepdims=True)
        hit = flat == fk
        hits.append(hit)
        gts.append(jnp.sum(jnp.where(hit, sc, 0.0), axis=0, keepdims=True))
        chosen = jnp.where(hit, 1.0, chosen)
        cand = jnp.where(hit, neg, cand)
    gsum = gts[0]
    for g in gts[1:]:
        gsum = gsum + g
    gts = [g / gsum * ROUTED_SCALE for g in gts]

    before = (lax.broadcasted_iota(I32, (tm, tm), 0) < lax.broadcasted_iota(I32, (tm, tm), 1))
    rank_in = _dot(chosen.astype(BF16), jnp.where(before, 1.0, 0.0).astype(BF16))
    cnt = jnp.sum(chosen, axis=1, keepdims=True)
    cnt_pad = jnp.floor((cnt + (CH - 1.0)) * (1.0 / CH)) * CH
    below = (lax.broadcasted_iota(I32, (N_EXPERTS, N_EXPERTS), 1) < lax.broadcasted_iota(I32, (N_EXPERTS, N_EXPERTS), 0))
    start = _dot(jnp.where(below, 1.0, 0.0).astype(BF16),
                 jnp.broadcast_to(cnt_pad, (N_EXPERTS, LANES)).astype(BF16))[:, 0:1]
    pos = rank_in + start
    lpos = [jnp.sum(jnp.where(hit, pos, 0.0), axis=0, keepdims=True) for hit in hits]
    return lpos, gts, cnt_pad, start


def _slot_rows(vals, n_rows):
    tm = vals[0].shape[1]
    row = lax.broadcasted_iota(I32, (n_rows, tm), 0)
    out = jnp.zeros((n_rows, tm), F32)
    for k, v in enumerate(vals):
        out = jnp.where(row == k, v, out)
    return out


def _post_kernel(tbl_ref, xa_ref, xb_ref, mp_ref, ms_ref, mod_ref, wout_ref, g2_ref, shg_ref, shu_ref, shd_ref,
                 rwt_ref, rb_ref, base_ref, h2_ref, lpos_ref, ptok_ref, gtok_ref, cnt_ref, start_ref, rel_ref,
                 run_ref):
    i = pl.program_id(0)
    n_prompt_tiles = tbl_ref[2, 0]

    @pl.when(i == 0)
    def _():
        cnt_ref[...] = jnp.zeros_like(cnt_ref)
        start_ref[...] = jnp.zeros_like(start_ref)
        rel_ref[...] = jnp.zeros_like(rel_ref)
        run_ref[...] = jnp.zeros_like(run_ref)

    mod = mod_ref[...]
    gate1 = mod[:, 2 * D_MODEL:3 * D_MODEL]
    shift2 = mod[:, 3 * D_MODEL:4 * D_MODEL]
    scale2 = mod[:, 4 * D_MODEL:5 * D_MODEL]
    gate2 = mod[:, 5 * D_MODEL:6 * D_MODEL]
    mix = jnp.where(i < n_prompt_tiles, mp_ref[...], ms_ref[...])
    x1 = _tile_x(tbl_ref, xa_ref, xb_ref) + gate1 * _dot(mix, wout_ref[...])
    ms = jnp.mean(x1 * x1, axis=-1, keepdims=True)
    h2 = x1 * lax.rsqrt(ms + NORM_EPS) * g2_ref[...]
    h2 = h2 * (1.0 + scale2) + shift2
    hb = h2.astype(BF16)
    h2_ref[...] = hb
    act = _silu(_dot(hb, shg_ref[...])) * _dot(hb, shu_ref[...])
    base_ref[...] = x1 + gate2 * _dot(act.astype(BF16), shd_ref[...])

    lpos, gts, cnt_pad, start = _route(h2, rwt_ref, rb_ref)
    lpos_ref[...] = _slot_rows(lpos, SLOT_ROWS)
    ptok_ref[...] = _slot_rows(lpos, LANES).T
    gtok_ref[...] = _slot_rows(gts, LANES).T
    col = lax.broadcasted_iota(I32, cnt_ref.shape, 1)
    run = run_ref[...]
    cnt_ref[...] = jnp.where(col == i, cnt_pad.astype(I32), cnt_ref[...])
    start_ref[...] = jnp.where(col == i, start.astype(I32), start_ref[...])
    rel_ref[...] = jnp.where(col == i, run.astype(I32), rel_ref[...])
    run_ref[...] = run + cnt_pad


def _post_call(layer, tbl, xa, xb, mix_p, mix_s, mod4, wout, g2, shg, shu, shd, rwt, rb):
    npt = mix_p.shape[0] // TM
    n_tok = mix_p.shape[0] + mix_s.shape[0]
    nt = n_tok // TM
    const = lambda i, t: (0, 0)
    tile = lambda i, t: (i, 0)
    gs = pltpu.PrefetchScalarGridSpec(
        num_scalar_prefetch=1,
        grid=(nt,),
        in_specs=_x_specs(npt, xa is xb) + [
            pl.BlockSpec((TM, MIX_W), lambda i, t: (jnp.minimum(i, npt - 1), 0)),
            pl.BlockSpec((TM, MIX_W), lambda i, t: (jnp.maximum(i - npt, 0), 0)),
            _mod_spec(layer),
            _layer_spec(layer, MIX_W, D_MODEL),
            _layer_spec(layer, 1, D_MODEL),
            _layer_spec(layer, D_MODEL, EXPERT_FF),
            _layer_spec(layer, D_MODEL, EXPERT_FF),
            _layer_spec(layer, EXPERT_FF, D_MODEL),
            _layer_spec(layer, N_EXPERTS, D_MODEL),
            _layer_spec(layer, N_EXPERTS, 1),
        ],
        out_specs=[
            pl.BlockSpec((TM, D_MODEL), tile),
            pl.BlockSpec((TM, D_MODEL), tile),
            pl.BlockSpec((SLOT_ROWS, TM), lambda i, t: (0, i)),
            pl.BlockSpec((TM, LANES), tile),
            pl.BlockSpec((TM, LANES), tile),
            pl.BlockSpec((N_EXPERTS, LANES), const),
            pl.BlockSpec((N_EXPERTS, LANES), const),
            pl.BlockSpec((N_EXPERTS, LANES), const),
        ],
        scratch_shapes=[pltpu.VMEM((N_EXPERTS, 1), F32)],
    )
    assert nt <= LANES
    return pl.pallas_call(
        _post_kernel,
        grid_spec=gs,
        out_shape=[
            jax.ShapeDtypeStruct((n_tok, D_MODEL), F32),
            jax.ShapeDtypeStruct((n_tok, D_MODEL), BF16),
            jax.ShapeDtypeStruct((SLOT_ROWS, n_tok), F32),
            jax.ShapeDtypeStruct((n_tok, LANES), F32),
            jax.ShapeDtypeStruct((n_tok, LANES), F32),
            jax.ShapeDtypeStruct((N_EXPERTS, LANES), I32),
            jax.ShapeDtypeStruct((N_EXPERTS, LANES), I32),
            jax.ShapeDtypeStruct((N_EXPERTS, LANES), I32),
        ],
        compiler_params=_cparams(("arbitrary",)),
        name="post_route",
    )(tbl, xa, xb, mix_p, mix_s, mod4, wout, g2, shg, shu, shd, rwt, rb)


SEG_ROW0, SEG_NBLK, SEG_PAD0, SEG_NPAD, SEG_NEXT, SEG_USED = range(6)


def _plan_kernel(last_tile, cnt_ref, rel_ref, seg_ref):
    def per_expert(e, start):
        end = start + rel_ref[e, last_tile] + cnt_ref[e, last_tile]
        nb = lax.shift_right_logical(end - start + (EB - 1), LOG_EB)
        nxt = start + lax.shift_left(nb, LOG_EB)
        seg_ref[SEG_ROW0, e] = start
        seg_ref[SEG_NBLK, e] = nb
        seg_ref[SEG_PAD0, e] = end
        seg_ref[SEG_NPAD, e] = lax.shift_right_logical(nxt - end, LOG_CH)
        seg_ref[SEG_USED, e] = 0
        return nxt

    total = lax.fori_loop(0, N_EXPERTS, per_expert, jnp.int32(0))

    def link(k, nxt):
        e = N_EXPERTS - 1 - k
        seg_ref[SEG_NEXT, e] = nxt
        return jnp.where(seg_ref[SEG_NBLK, e] > 0, e, nxt)

    first = lax.fori_loop(0, N_EXPERTS, link, jnp.int32(N_EXPERTS))
    seg_ref[SEG_USED, 0] = lax.shift_right_logical(total, LOG_EB)
    seg_ref[SEG_USED, 1] = first


def _plan_call(cnt, rel, nt):
    smem = pl.BlockSpec(memory_space=pltpu.SMEM)
    return pl.pallas_call(
        functools.partial(_plan_kernel, nt - 1),
        in_specs=[smem, smem],
        out_specs=smem,
        out_shape=jax.ShapeDtypeStruct((6, N_EXPERTS), I32),
        name="moe_plan",
    )(cnt, rel)


def _rows_copy(src_ref, src_row, dst_ref, dst_row, n_rows, sem):
    return pltpu.make_async_copy(src_ref.at[pl.ds(pl.multiple_of(src_row, CH), n_rows)],
                                 dst_ref.at[pl.ds(pl.multiple_of(dst_row, CH), n_rows)], sem)


class _Runs:
    def __init__(self, cnt_ref, start_ref, rel_ref, seg_ref):
        self.cnt, self.start, self.rel, self.seg = cnt_ref, start_ref, rel_ref, seg_ref

    def start_copies(self, i, copy, tot_ref, slot):
        def per_pair(e2, carry):
            n_big, n_small = carry
            for par in range(2):
                e = 2 * e2 + par
                c = self.cnt[e, i]
                a0 = self.start[e, i]
                b0 = self.seg[SEG_ROW0, e] + self.rel[e, i]
                nb = lax.shift_right_logical(c, LOG_CH + 1)
                odd = jnp.bitwise_and(lax.shift_right_logical(c, LOG_CH), 1)

                def big(q, cc, a0=a0, b0=b0, par=par):
                    copy(a0 + q * (2 * CH), b0 + q * (2 * CH), 2 * CH).start(priority=par)
                    return cc

                lax.fori_loop(0, nb, big, 0)

                @pl.when(odd == 1)
                def _(a0=a0, b0=b0, nb=nb, par=par):
                    copy(a0 + nb * (2 * CH), b0 + nb * (2 * CH), CH).start(priority=par)

                n_big, n_small = n_big + nb, n_small + odd
            return n_big, n_small

        n_big, n_small = lax.fori_loop(0, N_EXPERTS // 2, per_pair, (jnp.int32(0), jnp.int32(0)))
        tot_ref[slot, 0] = n_big
        tot_ref[slot, 1] = n_small

    @staticmethod
    def wait_copies(copy, tot_ref, slot):
        def big(q, c):
            copy(0, 0, 2 * CH).wait()
            return c

        lax.fori_loop(0, tot_ref[slot, 0], big, 0)

        def small(q, c):
            copy(0, 0, CH).wait()
            return c

        lax.fori_loop(0, tot_ref[slot, 1], small, 0)


def _dispatch_kernel(reuse, cnt_ref, start_ref, rel_ref, seg_ref, h_ref, lpos_ref, *rest):
    xb_hbm, sort_s, zero_s, tot_s, sems = rest[1:] if reuse else rest
    i = pl.program_id(0)
    last = pl.num_programs(0) - 1
    slot = lax.rem(i, 2)
    runs = _Runs(cnt_ref, start_ref, rel_ref, seg_ref)
    lp = lpos_ref[...]
    hb = h_ref[...]
    blk = TM
    for r in range(SORT_ROWS // blk):
        srow = (lax.broadcasted_iota(I32, (blk, TM), 0) + r * blk).astype(F32)
        p = jnp.zeros((blk, TM), F32)
        for k in range(TOP_K):
            p = jnp.where(srow == lp[k:k + 1, :], 1.0, p)
        sort_s[slot, r * blk:(r + 1) * blk, :] = _dot(p.astype(BF16), hb).astype(BF16)

    def copy_from(sl):
        return lambda s, d, n: _rows_copy(sort_s.at[sl], s, xb_hbm, d, n, sems.at[sl])

    runs.start_copies(i, copy_from(slot), tot_s, slot)

    @pl.when(i > 0)
    def _():
        runs.wait_copies(copy_from(1 - slot), tot_s, 1 - slot)

    @pl.when(i == last)
    def _():
        runs.wait_copies(copy_from(slot), tot_s, slot)
        _zero_fill_unused(seg_ref, xb_hbm, zero_s, sems.at[0], tail=not reuse)


def _zero_fill_unused(seg_ref, buf_hbm, zero_s, sem, tail):
    zero_s[...] = jnp.zeros_like(zero_s)

    def per_expert(e, c):
        first = seg_ref[SEG_PAD0, e]

        def z_issue(r, cc):
            _rows_copy(zero_s, 0, buf_hbm, first + r * CH, CH, sem).start()
            return cc

        lax.fori_loop(0, seg_ref[SEG_NPAD, e], z_issue, 0)

        def z_drain(r, cc):
            _rows_copy(zero_s, 0, buf_hbm, 0, CH, sem).wait()
            return cc

        lax.fori_loop(0, seg_ref[SEG_NPAD, e], z_drain, 0)
        return c

    lax.fori_loop(0, N_EXPERTS, per_expert, 0)
    if tail:
        _zero_fill_tail(seg_ref, buf_hbm, zero_s, sem)


def _zero_fill_tail(seg_ref, buf_hbm, zero_s, sem):
    n_blocks = buf_hbm.shape[0] // EB

    def blk_copy(b):
        return pltpu.make_async_copy(zero_s, buf_hbm.at[pl.ds(pl.multiple_of(b * EB, EB), EB)], sem)

    def t_issue(b, cc):
        blk_copy(b).start()
        return cc

    lax.fori_loop(seg_ref[SEG_USED, 0], n_blocks, t_issue, 0)

    def t_drain(b, cc):
        blk_copy(0).wait()
        return cc

    lax.fori_loop(seg_ref[SEG_USED, 0], n_blocks, t_drain, 0)


def _dispatch_call(cnt, start, rel, seg, h2, lpos, n_rows, prev=None):
    n_tok = h2.shape[0]
    nt = n_tok // TM
    smem = pl.BlockSpec(memory_space=pltpu.SMEM)
    reuse = prev is not None
    return pl.pallas_call(
        functools.partial(_dispatch_kernel, reuse),
        grid=(nt,),
        in_specs=[
            smem, smem, smem, smem,
            pl.BlockSpec((TM, D_MODEL), lambda i: (i, 0)),
            pl.BlockSpec((SLOT_ROWS, TM), lambda i: (0, i)),
        ] + ([pl.BlockSpec(memory_space=pl.ANY)] if reuse else []),
        input_output_aliases={6: 0} if reuse else {},
        out_specs=pl.BlockSpec(memory_space=pl.ANY),
        out_shape=jax.ShapeDtypeStruct((n_rows, D_MODEL), BF16),
        scratch_shapes=[
            pltpu.VMEM((2, SORT_ROWS, D_MODEL), BF16),
            pltpu.VMEM((EB, D_MODEL), BF16),
            pltpu.SMEM((2, 2), I32),
            pltpu.SemaphoreType.DMA((2,)),
        ],
        compiler_params=_cparams(("arbitrary",)),
        name="moe_dispatch",
    )(cnt, start, rel, seg, h2, lpos, *([prev] if reuse else []))


X_SLOTS = 4


def _experts_kernel(layer, seg_ref, wg_hbm, wu_hbm, wd_hbm, xb_hbm, yb_hbm,
                    wgf_s, wuf_s, wdf_s, wg_s, wu_s, wd_s, x_s, y_s, sem_w, sem_x, sem_y):
    n_used = seg_ref[SEG_USED, 0]
    first = seg_ref[SEG_USED, 1]

    def rows(g):
        return pl.ds(pl.multiple_of(g * EB, EB), EB)

    def x_copy(g, slot):
        return pltpu.make_async_copy(xb_hbm.at[rows(g)], x_s.at[slot], sem_x.at[slot])

    def y_copy(g, slot):
        return pltpu.make_async_copy(y_s.at[slot], yb_hbm.at[rows(g)], sem_y.at[slot])

    def w_copies(e, slot):
        return [pltpu.make_async_copy(hbm.at[layer, e], buf.at[slot], sem_w.at[slot, n])
                for n, (hbm, buf) in enumerate(((wg_hbm, wgf_s), (wu_hbm, wuf_s), (wd_hbm, wdf_s)))]

    def fetch_weights(e, slot):
        for c in w_copies(e, slot):
            c.start()

    def take_weights(slot):
        for c in w_copies(0, slot):
            c.wait()
        wg_s[...] = wgf_s[slot].astype(BF16)
        wu_s[...] = wuf_s[slot].astype(BF16)
        wd_s[...] = wdf_s[slot].astype(BF16)

    def next_expert(e):
        return seg_ref[SEG_NEXT, jnp.minimum(e, N_EXPERTS - 1)]

    @pl.when(n_used > 0)
    def _():
        for p in range(X_SLOTS - 1):
            @pl.when(p < n_used)
            def _(p=p):
                x_copy(p, p).start()

        fetch_weights(first, 0)

        @pl.when(next_expert(first) < N_EXPERTS)
        def _():
            fetch_weights(next_expert(first), 1)

        take_weights(0)

        def block(g, carry):
            e, left, wslot = carry
            ahead = g + (X_SLOTS - 1)

            @pl.when(ahead < n_used)
            def _():
                x_copy(ahead, lax.rem(ahead, X_SLOTS)).start()

            xslot = lax.rem(g, X_SLOTS)
            yslot = lax.rem(g, 2)
            x_copy(g, xslot).wait()

            @pl.when(g >= 2)
            def _():
                y_copy(g - 2, yslot).wait()

            xb = x_s[xslot]
            act = _silu(_dot(xb, wg_s[...])) * _dot(xb, wu_s[...])
            y_s[yslot] = _dot(act.astype(BF16), wd_s[...]).astype(BF16)
            y_copy(g, yslot).start()

            switch = jnp.logical_and(left == 1, g + 1 < n_used)
            nxt = next_expert(e)

            @pl.when(switch)
            def _():
                take_weights(1 - wslot)

                @pl.when(next_expert(nxt) < N_EXPERTS)
                def _():
                    fetch_weights(next_expert(nxt), wslot)

            nxt_c = jnp.minimum(nxt, N_EXPERTS - 1)
            return (jnp.where(switch, nxt_c, e), jnp.where(switch, seg_ref[SEG_NBLK, nxt_c], left - 1),
                    jnp.where(switch, 1 - wslot, wslot))

        lax.fori_loop(0, n_used, block,
                      (first, seg_ref[SEG_NBLK, jnp.minimum(first, N_EXPERTS - 1)], jnp.int32(0)))

        @pl.when(n_used >= 2)
        def _():
            y_copy(n_used - 2, lax.rem(n_used, 2)).wait()

        y_copy(n_used - 1, lax.rem(n_used - 1, 2)).wait()


def _experts_call(seg, xb, layer, wg, wu, wd):
    n_rows = xb.shape[0]
    hbm = pl.BlockSpec(memory_space=pl.ANY)
    return pl.pallas_call(
        functools.partial(_experts_kernel, layer),
        in_specs=[pl.BlockSpec(memory_space=pltpu.SMEM), hbm, hbm, hbm, hbm],
        out_specs=hbm,
        out_shape=jax.ShapeDtypeStruct((n_rows, D_MODEL), BF16),
        input_output_aliases={4: 0},
        scratch_shapes=[
            pltpu.VMEM((2, D_MODEL, EXPERT_FF), F32),
            pltpu.VMEM((2, D_MODEL, EXPERT_FF), F32),
            pltpu.VMEM((2, EXPERT_FF, D_MODEL), F32),
            pltpu.VMEM((D_MODEL, EXPERT_FF), BF16),
            pltpu.VMEM((D_MODEL, EXPERT_FF), BF16),
            pltpu.VMEM((EXPERT_FF, D_MODEL), BF16),
            pltpu.VMEM((X_SLOTS, EB, D_MODEL), BF16),
            pltpu.VMEM((2, EB, D_MODEL), BF16),
            pltpu.SemaphoreType.DMA((2, 3)),
            pltpu.SemaphoreType.DMA((X_SLOTS,)),
            pltpu.SemaphoreType.DMA((2,)),
        ],
        compiler_params=pltpu.CompilerParams(vmem_limit_bytes=VMEM_LIMIT),
        name="moe_experts",
    )(seg, wg, wu, wd, xb)


def _combine_kernel(final, tbl_ref, cnt_ref, start_ref, rel_ref, seg_ref, yb_hbm, base_ref, gtok_ref, ptok_ref,
                    mod_ref, gf_ref, *rest):
    *out_refs, sort_s, tot_s, sems = rest
    i = pl.program_id(0)
    slot = lax.rem(i, 2)
    runs = _Runs(cnt_ref, start_ref, rel_ref, seg_ref)

    def copy_to(sl):
        return lambda s, d, n: _rows_copy(yb_hbm, d, sort_s.at[sl], s, n, sems.at[sl])

    @pl.when(i == 0)
    def _():
        sort_s[...] = jnp.zeros_like(sort_s)
        runs.start_copies(i, copy_to(slot), tot_s, slot)

    @pl.when(i + 1 < pl.num_programs(0))
    def _():
        runs.start_copies(i + 1, copy_to(1 - slot), tot_s, 1 - slot)

    runs.wait_copies(copy_to(slot), tot_s, slot)

    gt = gtok_ref[...]
    pt = ptok_ref[...]
    col = lax.broadcasted_iota(I32, (TM, SORT_ROWS), 1).astype(F32)
    w = jnp.zeros((TM, SORT_ROWS), F32)
    for k in range(TOP_K):
        w = jnp.where(col == pt[:, k:k + 1], gt[:, k:k + 1], w)
    routed = _dot(w.astype(BF16), sort_s[slot])
    gate2 = mod_ref[...][:, 5 * D_MODEL:6 * D_MODEL]
    y = base_ref[...] + gate2 * routed
    if final:
        y = y * lax.rsqrt(jnp.mean(y * y, axis=-1, keepdims=True) + NORM_EPS) * gf_ref[...]
        yp_ref, ys_ref = out_refs
        is_context = i < tbl_ref[2, 0]

        @pl.when(is_context)
        def _():
            yp_ref[...] = y

        @pl.when(jnp.logical_not(is_context))
        def _():
            ys_ref[...] = y
    else:
        out_refs[0][...] = y


def _combine_call(final, layer, npt, tbl, cnt, start, rel, seg, yb, base, gtok, ptok, mod4, gfinal):
    n_tok = base.shape[0]
    nt = n_tok // TM
    tile = lambda i, t: (i, 0)
    smem = pl.BlockSpec(memory_space=pltpu.SMEM)
    if final:
        out_specs = [pl.BlockSpec((TM, D_MODEL), lambda i, t: (jnp.minimum(i, npt - 1), 0)),
                     pl.BlockSpec((TM, D_MODEL), lambda i, t: (jnp.maximum(i - npt, 0), 0))]
        out_shape = [jax.ShapeDtypeStruct((npt * TM, D_MODEL), F32),
                     jax.ShapeDtypeStruct((n_tok - npt * TM, D_MODEL), F32)]
    else:
        out_specs = pl.BlockSpec((TM, D_MODEL), tile)
        out_shape = jax.ShapeDtypeStruct((n_tok, D_MODEL), F32)
    gs = pltpu.PrefetchScalarGridSpec(
        num_scalar_prefetch=1,
        grid=(nt,),
        in_specs=[
            smem, smem, smem, smem,
            pl.BlockSpec(memory_space=pl.ANY),
            pl.BlockSpec((TM, D_MODEL), tile),
            pl.BlockSpec((TM, LANES), tile),
            pl.BlockSpec((TM, LANES), tile),
            _mod_spec(layer),
            pl.BlockSpec((1, D_MODEL), lambda i, t: (0, 0)),
        ],
        out_specs=out_specs,
        scratch_shapes=[
            pltpu.VMEM((2, SORT_ROWS, D_MODEL), BF16),
            pltpu.SMEM((2, 2), I32),
            pltpu.SemaphoreType.DMA((2,)),
        ],
    )
    return pl.pallas_call(
        functools.partial(_combine_kernel, final),
        grid_spec=gs,
        out_shape=out_shape,
        compiler_params=_cparams(("arbitrary",)),
        name="moe_combine",
    )(tbl, cnt, start, rel, seg, yb, base, gtok, ptok, mod4, gfinal)


def _pad_cols(w, groups, width, slot):
    lead = w.shape[:-1]
    w = w.reshape(*lead, groups, width)
    pad = [(0, 0)] * (len(lead) + 1) + [(0, slot - width)]
    return jnp.pad(w, pad).reshape(*lead, groups * slot)


def _prep_w_in(w):
    assert w.shape[-1] == C_KPE + MLA_ROPE_DIM
    return jnp.pad(w.astype(BF16), ((0, 0), (0, 0), (0, N_PRE - w.shape[-1])))


def _rope_tables(n_pos, dim, lane_offsets):
    f32 = np.float32
    n_rows = n_pos // GRID_W
    row = np.repeat(np.arange(n_rows, dtype=f32), GRID_W)
    col = np.tile(np.arange(GRID_W, dtype=f32), n_rows)
    half = dim // 2
    freqs = f32(ROPE_THETA) ** (-np.arange(0, half, 2, dtype=f32) / f32(half))
    ar = row[:, None] * freqs[None, :]
    ac = col[:, None] * freqs[None, :]
    ang = np.concatenate([ar, ar, ac, ac], axis=-1).astype(f32)
    cos, sin = np.cos(ang), np.sin(ang)
    first = (np.arange(dim) % 16) < 8
    sa = np.where(first[None, :], -sin, f32(0))
    sb = np.where(first[None, :], f32(0), sin)
    c_t = np.ones((TM + n_pos, SLOT), f32)
    a_t = np.zeros((TM + n_pos, SLOT), f32)
    b_t = np.zeros((TM + n_pos, SLOT), f32)
    for off in lane_offsets:
        c_t[TM:, off:off + dim] = cos
        a_t[TM:, off:off + dim] = sa
        b_t[TM:, off:off + dim] = sb
    return tuple(jnp.asarray(t) for t in (c_t, a_t, b_t))


def _block_diag_states(s):
    b, l, h, dk, dv = s.shape
    eye = jnp.eye(h, dtype=s.dtype)
    return jnp.einsum('blhkv,hg->blhkgv', s, eye).reshape(b, l, h * dk, h * dv)


def kernel(x_prompt, x_sample, cache_diff_k, cache_diff_v, cache_mla_ckv, cache_mla_kpe, state_ret_fwd, state_ret_bwd, c, c_ctx, w_ada, b_ada, norm_mix, norm_ffn, norm_final, w_in, ret_decay_fwd, ret_decay_bwd, diff_lambda, diff_subln, mla_q_norm, mla_w_uq, mla_kv_norm, mla_w_ukv, w_out, router_w, router_bias, exp_w_gate, exp_w_up, exp_w_down, sh_w_gate, sh_w_up, sh_w_down):
    n_pb, p_len, _ = x_prompt.shape
    n_sb, s_len, _ = x_sample.shape
    past_len = cache_diff_k.shape[3]
    n_p = n_pb * p_len
    n_s = n_sb * s_len
    n_tok = n_p + n_s
    nt = n_tok // TM
    npt = n_p // TM
    assert p_len == TM and s_len % TM == 0 and n_p % s_len == 0 and past_len % 8 == 0

    tiles = np.arange(nt)
    mod_row = np.where(tiles < npt, n_sb, (tiles - npt) // (s_len // TM))
    rope_blk = np.where(tiles < npt, 0, 1 + (tiles - npt) % (s_len // TM))
    tbl = jnp.asarray(np.stack([mod_row, rope_blk, np.full(nt, npt)]).astype(np.int32))

    n_cond = 16
    cond = jnp.concatenate([c, c_ctx[None, :], jnp.zeros((n_cond - n_sb - 1, D_MODEL), F32)], axis=0)
    mod_all = _modulation(cond, w_ada, b_ada)

    rope_d = _rope_tables(s_len, DIFF_QK_DIM, (0, DIFF_QK_DIM))
    rope_m = _rope_tables(s_len, MLA_ROPE_DIM, (KR_LO,))
    place = np.zeros((MLA_ROPE_DIM, SLOT), np.float32)
    place[np.arange(MLA_ROPE_DIM), KR_LO + np.arange(MLA_ROPE_DIM)] = 1.0
    place = jnp.asarray(place, BF16)
    cache_dv_t = cache_diff_v.transpose(0, 1, 3, 2, 4).reshape(n_sb, DEPTH, past_len, DIFF_V_W)
    s0f_bd = _block_diag_states(state_ret_fwd)
    s0b_bd = _block_diag_states(state_ret_bwd)

    n_blocks = pl.cdiv(n_tok * TOP_K + nt * N_EXPERTS * (CH - 1) + N_EXPERTS * (EB - CH), EB)
    n_rows = n_blocks * EB

    xa, xb = x_prompt.reshape(n_p, D_MODEL), x_sample.reshape(n_s, D_MODEL)
    gfinal = norm_final.reshape(1, D_MODEL)

    mod4 = mod_all.reshape(DEPTH, n_cond, 1, 6 * D_MODEL)
    w_pre = _prep_w_in(w_in)
    wuq = _pad_cols(mla_w_uq, MLA_HEADS, MLA_NOPE_DIM + MLA_ROPE_DIM, SLOT).astype(BF16)
    ukv = mla_w_ukv.reshape(DEPTH, MLA_KV_LORA, MLA_HEADS, MLA_NOPE_DIM + MLA_V_DIM)
    wk = _pad_cols(ukv[..., :MLA_NOPE_DIM].reshape(DEPTH, MLA_KV_LORA, -1), MLA_HEADS, MLA_NOPE_DIM, SLOT).astype(BF16)
    wv = ukv[..., MLA_NOPE_DIM:].reshape(DEPTH, MLA_KV_LORA, MLA_V_W).astype(BF16)
    decf = jnp.repeat(ret_decay_fwd, RET_DIM, axis=-1).reshape(DEPTH, 1, RET_W)
    decb = jnp.repeat(ret_decay_bwd, RET_DIM, axis=-1).reshape(DEPTH, 1, RET_W)
    subln = jnp.tile(diff_subln, (1, DIFF_HEADS)).reshape(DEPTH, 1, DIFF_V_W)
    vec = lambda p: p.reshape(DEPTH, 1, -1)
    wout_b, shg_b, shu_b, shd_b = (w.astype(BF16) for w in (w_out, sh_w_gate, sh_w_up, sh_w_down))
    rwt = router_w.transpose(0, 2, 1)
    rb = router_bias.reshape(DEPTH, N_EXPERTS, 1)

    new_ctx = []
    sorted_buf = None
    for l in range(DEPTH):
        lam_init = 0.8 - 0.6 * math.exp(-0.3 * l)
        ra, qa, ka, *caches = _pre_call(l, tbl, xa, xb, npt, n_tok, mod4, vec(norm_mix), w_pre, vec(mla_q_norm),
                                        wuq, vec(mla_kv_norm), rope_d, rope_m)
        mix_p, *states = _mix_prompt_call(lam_init, l, qa, ka, ra, n_pb, p_len, decf, decb,
                                          diff_lambda, subln, wk, wv)
        new_ctx.append(caches + states)
        mix_s = _mix_sample_call(lam_init, l, qa, ka, ra, n_p, n_sb, s_len, past_len, cache_diff_k,
                                 cache_dv_t, cache_mla_ckv, cache_mla_kpe, s0f_bd, s0b_bd, decf, decb,
                                 diff_lambda, subln, wk, wv, place)
        base, h2, lpos, ptok, gtok, cnt, start, rel = _post_call(
            l, tbl, xa, xb, mix_p, mix_s, mod4, wout_b, vec(norm_ffn), shg_b, shu_b, shd_b, rwt, rb)
        seg = _plan_call(cnt, rel, nt)
        sorted_buf = _dispatch_call(cnt, start, rel, seg, h2, lpos, n_rows, prev=sorted_buf)
        sorted_buf = _experts_call(seg, sorted_buf, l, exp_w_gate, exp_w_up, exp_w_down)
        final = l == DEPTH - 1
        out = _combine_call(final, l, npt, tbl, cnt, start, rel, seg, sorted_buf, base, gtok, ptok, mod4, gfinal)
        xa, xb = out if final else (out, out)

    y_prompt = xa.reshape(n_pb, p_len, D_MODEL)
    y_sample = xb.reshape(n_sb, s_len, D_MODEL)
    return (y_prompt, y_sample, *(jnp.stack(per_layer, axis=1) for per_layer in zip(*new_ctx)))
```

```python
import functools
import math

import numpy as np
import jax
import jax.numpy as jnp
from jax import lax
from jax.experimental import pallas as pl
from jax.experimental.pallas import tpu as pltpu

F32 = jnp.float32
BF16 = jnp.bfloat16
I32 = jnp.int32

D_MODEL = 1024
DEPTH = 2
GRID_W = 64
ROPE_THETA = 10000.0
NORM_EPS = 1e-6

RET_HEADS = 4
RET_DIM = 64
RET_CHUNK = 128
RET_W = RET_HEADS * RET_DIM
DIFF_HEADS = 4
DIFF_QK_DIM = 32
DIFF_V_DIM = 64
DIFF_V_W = DIFF_HEADS * DIFF_V_DIM
MLA_HEADS = 8
MLA_Q_LORA = 256
MLA_KV_LORA = 128
MLA_NOPE_DIM = 64
MLA_ROPE_DIM = 32
MLA_V_DIM = 64
MLA_V_W = MLA_HEADS * MLA_V_DIM
MIX_W = RET_W + DIFF_V_W + MLA_V_W

N_EXPERTS = 64
TOP_K = 6
N_GROUPS = 8
GROUP_SIZE = N_EXPERTS // N_GROUPS
TOPK_GROUPS = 4
EXPERT_FF = 256
ROUTED_SCALE = 2.5

LANES = 128
SLOT = LANES
TM = 256
TS = 2
STEP = TS * TM
SLOT_ROWS = 8
EB = 512
LOG_EB = 9
CH = 16
LOG_CH = 4
SORT_ROWS = 2560
VMEM_LIMIT = 48 * 1024 * 1024

C_RQ, C_RK, C_RV, C_RG = 0, 256, 512, 768
C_DQ, C_DK, C_DV, C_CQ, C_CKV, C_KPE = 1024, 1280, 1536, 1792, 2048, 2176
N_PRE = 2304
QA_W = 4 * SLOT + MLA_HEADS * SLOT
KA_DK, KA_DV, KA_CKV, KA_KPE = 0, 512, 768, 896
KA_W = 1024
KR_LO, KR_HI = 64, 96


def _dot(a, b):
    return jnp.dot(a, b, preferred_element_type=F32)


def _dot_nt(a, b):
    return lax.dot_general(a, b, (((1,), (1,)), ((), ())), preferred_element_type=F32)


def _dot_tn(a, b):
    return lax.dot_general(a, b, (((0,), (0,)), ((), ())), preferred_element_type=F32)


def _split_dot(x, w_bf16):
    hi = x.astype(BF16)
    lo = (x - hi.astype(F32)).astype(BF16)
    return _dot(hi, w_bf16) + _dot(lo, w_bf16)


def _split_dot_nt(w, x):
    wh = w.astype(BF16)
    wl = (w - wh.astype(F32)).astype(BF16)
    xh = x.astype(BF16)
    xl = (x - xh.astype(F32)).astype(BF16)
    return _dot_nt(wh, xh) + _dot_nt(wh, xl) + _dot_nt(wl, xh)


def _silu(x):
    return x * jax.nn.sigmoid(x)


def _cparams(sem):
    return pltpu.CompilerParams(dimension_semantics=sem, vmem_limit_bytes=VMEM_LIMIT)


MOD_TN = 512


def _mod_kernel(c_ref, w_ref, b_ref, o_ref):
    s = _silu(c_ref[...])
    o_ref[...] = _split_dot3(s, w_ref[...]) + b_ref[...]


def _split_dot3(x, w):
    xh = x.astype(BF16)
    xl = (x - xh.astype(F32)).astype(BF16)
    wh = w.astype(BF16)
    wl = (w - wh.astype(F32)).astype(BF16)
    return _dot(xh, wh) + _dot(xh, wl) + _dot(xl, wh)


def _modulation(cond, w_ada, b_ada):
    n_rows = cond.shape[0]
    n_out = w_ada.shape[-1]
    return pl.pallas_call(
        _mod_kernel,
        grid=(DEPTH, n_out // MOD_TN),
        in_specs=[
            pl.BlockSpec((n_rows, D_MODEL), lambda l, j: (0, 0)),
            pl.BlockSpec((None, D_MODEL, MOD_TN), lambda l, j: (l, 0, j)),
            pl.BlockSpec((None, 1, MOD_TN), lambda l, j: (l, 0, j)),
        ],
        out_specs=pl.BlockSpec((None, n_rows, MOD_TN), lambda l, j: (l, 0, j)),
        out_shape=jax.ShapeDtypeStruct((DEPTH, n_rows, n_out), F32),
        compiler_params=_cparams(("arbitrary", "arbitrary")),
        name="adaln_mod",
    )(cond, w_ada, b_ada.reshape(DEPTH, 1, n_out))


def _rope_slot(x, cos, sa, sb):
    up = pltpu.roll(x, LANES - 8, 1)
    dn = pltpu.roll(x, 8, 1)
    return x * cos + up * sa + dn * sb


def _x_specs(nps, combined):
    off = 0 if combined else nps
    return [pl.BlockSpec((STEP, D_MODEL), lambda i, t: (jnp.minimum(i, nps - 1), 0)),
            pl.BlockSpec((STEP, D_MODEL), lambda i, t: (jnp.maximum(i, nps) - off, 0))]


def _head_slots(x):
    half = SLOT // 2
    low = _lane_iota((x.shape[0], SLOT)) < half
    slots = []
    for c in range(x.shape[1] // SLOT):
        pair = x[:, c * SLOT:(c + 1) * SLOT]
        slots.append(jnp.where(low, pair, 0.0))
        slots.append(jnp.where(low, pltpu.roll(pair, half, 1), 0.0))
    return slots


def _pre_kernel(tbl_ref, xa_ref, xb_ref, mod_ref, g_ref, w_ref, qg_ref, wuq_ref, kvg_ref,
                cd_ref, sad_ref, sbd_ref, cm_ref, sam_ref, sbm_ref,
                ra_ref, qa_ref, ka_ref, dkc_ref, dvc_ref, ckvc_ref, kpec_ref):
    is_context = pl.program_id(0) < tbl_ref[2, 0]
    mod = mod_ref[...]
    shift1 = mod[:, 0:D_MODEL]
    scale1 = mod[:, D_MODEL:2 * D_MODEL]
    for s in range(TS):
        rows = slice(s * TM, (s + 1) * TM)
        x = jnp.where(is_context, xa_ref[rows, :], xb_ref[rows, :])
        ms = jnp.mean(x * x, axis=-1, keepdims=True)
        h = x * lax.rsqrt(ms + NORM_EPS) * g_ref[...]
        h = h * (1.0 + scale1) + shift1
        hb = h.astype(BF16)

        def proj(lo, hi, hb=hb):
            return _dot(hb, w_ref[:, lo:hi])

        ra_ref[rows, C_RQ:C_RK] = proj(C_RQ, C_RK).astype(BF16)
        ra_ref[rows, C_RK:C_RV] = (proj(C_RK, C_RV) * (RET_DIM ** -0.5)).astype(BF16)
        ra_ref[rows, C_RV:C_RG] = proj(C_RV, C_RG).astype(BF16)
        ra_ref[rows, C_RG:C_DQ] = _silu(proj(C_RG, C_DQ)).astype(BF16)

        cd, sad, sbd = cd_ref[rows, :], sad_ref[rows, :], sbd_ref[rows, :]
        cm, sam, sbm = cm_ref[rows, :], sam_ref[rows, :], sbm_ref[rows, :]
        dq_slots = _head_slots(proj(C_DQ, C_DK))
        dk_slots = [_rope_slot(v, cd, sad, sbd) for v in _head_slots(proj(C_DK, C_DV))]
        for hd in range(DIFF_HEADS):
            qa_ref[rows, hd * SLOT:(hd + 1) * SLOT] = _rope_slot(dq_slots[hd], cd, sad, sbd).astype(BF16)
            ka_ref[rows, KA_DK + hd * SLOT:KA_DK + (hd + 1) * SLOT] = dk_slots[hd].astype(BF16)
        dv = proj(C_DV, C_CQ)
        ka_ref[rows, KA_DV:KA_CKV] = dv.astype(BF16)

        cq = proj(C_CQ, C_CKV)
        cqn = cq * lax.rsqrt(jnp.mean(cq * cq, axis=-1, keepdims=True) + NORM_EPS) * qg_ref[...]
        qm = _dot(cqn.astype(BF16), wuq_ref[...])
        for hd in range(MLA_HEADS):
            sl = slice(hd * SLOT, (hd + 1) * SLOT)
            qa_ref[rows, 4 * SLOT + hd * SLOT:4 * SLOT + (hd + 1) * SLOT] = _rope_slot(qm[:, sl], cm, sam, sbm).astype(BF16)

        ckv = proj(C_CKV, C_KPE)
        ckvn = ckv * lax.rsqrt(jnp.mean(ckv * ckv, axis=-1, keepdims=True) + NORM_EPS) * kvg_ref[...]
        kpe = proj(C_KPE, N_PRE)
        kpe_slot = _rope_slot(kpe + pltpu.roll(kpe, SLOT // 2, 1), cm, sam, sbm)
        ka_ref[rows, KA_CKV:KA_KPE] = ckvn.astype(BF16)
        ka_ref[rows, KA_KPE:KA_W] = kpe_slot.astype(BF16)

        @pl.when(is_context)
        def _(s=s, dk_slots=dk_slots, dv=dv, ckvn=ckvn, kpe_slot=kpe_slot):
            dv_slots = _head_slots(dv)
            for hd in range(DIFF_HEADS):
                dkc_ref[s, hd] = dk_slots[hd][:, 0:2 * DIFF_QK_DIM]
                dvc_ref[s, hd] = dv_slots[hd][:, 0:DIFF_V_DIM]
            ckvc_ref[s] = ckvn
            kpec_ref[s] = kpe_slot[:, 0:MLA_ROPE_DIM]


def _layer_spec(layer, rows, cols):
    return pl.BlockSpec((None, rows, cols), lambda *_: (layer, 0, 0))


def _mod_spec(layer):
    return pl.BlockSpec((None, None, 1, 6 * D_MODEL), lambda i, t: (layer, t[0, i], 0, 0))


def _pre_call(layer, tbl, xa, xb, npt, n_tok, mod4, g, w_pre, qg, wuq, kvg, rope_d, rope_m):
    ns = n_tok // STEP
    nps = npt // TS
    tile = lambda i, t: (i, 0)
    rope = lambda i, t: (t[1, i], 0)
    ctx5 = lambda i, t: (jnp.minimum(i, nps - 1), 0, 0, 0)
    ctx4 = lambda i, t: (jnp.minimum(i, nps - 1), 0, 0)
    cache_shapes = [
        jax.ShapeDtypeStruct((npt, DIFF_HEADS, TM, 2 * DIFF_QK_DIM), F32),
        jax.ShapeDtypeStruct((npt, DIFF_HEADS, TM, DIFF_V_DIM), F32),
        jax.ShapeDtypeStruct((npt, TM, MLA_KV_LORA), F32),
        jax.ShapeDtypeStruct((npt, TM, MLA_ROPE_DIM), F32),
    ]
    gs = pltpu.PrefetchScalarGridSpec(
        num_scalar_prefetch=1,
        grid=(ns,),
        in_specs=_x_specs(nps, xa is xb) + [
            _mod_spec(layer),
            _layer_spec(layer, 1, D_MODEL),
            _layer_spec(layer, D_MODEL, N_PRE),
            _layer_spec(layer, 1, MLA_Q_LORA),
            _layer_spec(layer, MLA_Q_LORA, MLA_HEADS * SLOT),
            _layer_spec(layer, 1, MLA_KV_LORA),
        ] + [pl.BlockSpec((STEP, SLOT), rope)] * 6,
        out_specs=[
            pl.BlockSpec((STEP, D_MODEL), tile),
            pl.BlockSpec((STEP, QA_W), tile),
            pl.BlockSpec((STEP, KA_W), tile),
            pl.BlockSpec((TS, DIFF_HEADS, TM, 2 * DIFF_QK_DIM), ctx5),
            pl.BlockSpec((TS, DIFF_HEADS, TM, DIFF_V_DIM), ctx5),
            pl.BlockSpec((TS, TM, MLA_KV_LORA), ctx4),
            pl.BlockSpec((TS, TM, MLA_ROPE_DIM), ctx4),
        ],
    )
    return pl.pallas_call(
        _pre_kernel,
        grid_spec=gs,
        out_shape=[
            jax.ShapeDtypeStruct((n_tok, D_MODEL), BF16),
            jax.ShapeDtypeStruct((n_tok, QA_W), BF16),
            jax.ShapeDtypeStruct((n_tok, KA_W), BF16),
        ] + cache_shapes,
        compiler_params=_cparams(("arbitrary",)),
        name="pre_proj",
    )(tbl, xa, xb, mod4, g, w_pre, qg, wuq, kvg, *rope_d, *rope_m)


def _lane_iota(shape):
    return lax.broadcasted_iota(I32, shape, len(shape) - 1)


def _head_mask(n_rows, width, head, head_w):
    lane = _lane_iota((n_rows, width))
    return (lane >= head * head_w) & (lane < (head + 1) * head_w)


def _seg_mean_sq(o, bd_ones):
    return _split_dot(o * o, bd_ones) * (1.0 / RET_DIM)


def _block_diag_ones(n, blk):
    r = lax.broadcasted_iota(I32, (n, n), 0) // blk
    c = lax.broadcasted_iota(I32, (n, n), 1) // blk
    return r == c


def _retention(ra_ref, seq_len, decf_ref, decb_ref, s0f, s0b):
    C = RET_CHUNK
    nc = seq_len // C
    lgf = -jnp.exp(decf_ref[...])
    lgb = -jnp.exp(decb_ref[...])
    pos = lax.broadcasted_iota(I32, (C, RET_W), 0).astype(F32)
    qdf = jnp.exp((pos + 1.0) * lgf)
    kdf = jnp.exp((C - 1.0 - pos) * lgf)
    cdf = jnp.exp(float(C) * lgf)
    qdb = jnp.exp((C - pos) * lgb)
    kdb = jnp.exp(pos * lgb)
    cdb = jnp.exp(float(C) * lgb)
    ii = lax.broadcasted_iota(I32, (C, C), 0).astype(F32)
    jj = lax.broadcasted_iota(I32, (C, C), 1).astype(F32)
    dist = ii - jj
    dmats = []
    for hd in range(RET_HEADS):
        lf = lgf[:, hd * RET_DIM:hd * RET_DIM + 1]
        lb = lgb[:, hd * RET_DIM:hd * RET_DIM + 1]
        dmats.append(jnp.where(dist >= 0, jnp.exp(dist * lf), jnp.exp(-dist * lb)))
    bd = _block_diag_ones(RET_W, RET_DIM)
    bd_ones = jnp.where(bd, 1.0, 0.0).astype(BF16)

    def chunk(n):
        rows = slice(n * C, (n + 1) * C)
        return (ra_ref[rows, C_RQ:C_RK], ra_ref[rows, C_RK:C_RV], ra_ref[rows, C_RV:C_RG])

    cross = [None] * nc
    sf = s0f
    for n in range(nc):
        q, k, v = chunk(n)
        cross[n] = _dot((q * qdf).astype(BF16), sf.astype(BF16))
        kv = _dot_tn((k * kdf).astype(BF16), v.astype(BF16))
        sf = sf * cdf + jnp.where(bd, kv, 0.0)
    sb = s0b
    for n in range(nc - 1, -1, -1):
        q, k, v = chunk(n)
        cross[n] = cross[n] + _dot((q * qdb).astype(BF16), sb.astype(BF16))
        kv = _dot_tn((k * kdb).astype(BF16), v.astype(BF16))
        sb = sb * cdb + jnp.where(bd, kv, 0.0)

    outs = []
    for n in range(nc):
        q, k, v = chunk(n)
        kb = k.astype(BF16)
        vb = v.astype(BF16)
        o = cross[n]
        for hd in range(RET_HEADS):
            hm = _head_mask(C, RET_W, hd, RET_DIM)
            sc = _dot_nt(jnp.where(hm, q, 0.0).astype(BF16), kb) * dmats[hd]
            o = o + jnp.where(hm, _dot(sc.astype(BF16), vb), 0.0)
        on = o * lax.rsqrt(_seg_mean_sq(o, bd_ones) + NORM_EPS)
        outs.append(on * ra_ref[n * C:(n + 1) * C, C_RG:C_DQ])
    return outs, sf, sb


def _softmax_pv(s_parts, v_parts, scale):
    m = None
    for s in s_parts:
        mm = jnp.max(s, axis=-1, keepdims=True)
        m = mm if m is None else jnp.maximum(m, mm)
    m = m * scale
    acc = None
    den = None
    for s, v in zip(s_parts, v_parts):
        e = jnp.exp(s * scale - m)
        ds = jnp.sum(e, axis=-1, keepdims=True)
        pv = _dot(e.astype(BF16), v)
        acc = pv if acc is None else acc + pv
        den = ds if den is None else den + ds
    return acc / den


def _diff_attention(dq, k_parts, v_parts, lam, subln, lam_init, bd_ones):
    lq = dq.shape[0]
    scale = DIFF_QK_DIM ** -0.5
    lane = _lane_iota((lq, SLOT))
    out = jnp.zeros((lq, DIFF_V_W), F32)
    for hd in range(DIFF_HEADS):
        qh = dq[:, hd * SLOT:(hd + 1) * SLOT]
        q1 = jnp.where(lane < DIFF_QK_DIM, qh, 0.0).astype(BF16)
        q2 = jnp.where(lane >= DIFF_QK_DIM, qh, 0.0).astype(BF16)
        s1 = [_dot_nt(q1[:, :kp[hd].shape[1]], kp[hd]) for kp in k_parts]
        s2 = [_dot_nt(q2[:, :kp[hd].shape[1]], kp[hd]) for kp in k_parts]
        o = _softmax_pv(s1, v_parts, scale) - lam * _softmax_pv(s2, v_parts, scale)
        out = jnp.where(_head_mask(lq, DIFF_V_W, hd, DIFF_V_DIM), o, out)
    on = out * lax.rsqrt(_seg_mean_sq(out, bd_ones) + NORM_EPS) * subln
    return on * (1.0 - lam_init)


def _mla_attention(qm, k_parts, v_parts):
    lq = qm.shape[0]
    scale = (MLA_NOPE_DIM + MLA_ROPE_DIM) ** -0.5
    halves = []
    for g in range(2):
        out = jnp.zeros((lq, 256), F32)
        for hh in range(4):
            hd = 4 * g + hh
            qh = qm[:, hd * SLOT:(hd + 1) * SLOT].astype(BF16)
            s = [_dot_nt(qh, kp[:, hd * SLOT:(hd + 1) * SLOT]) for kp in k_parts]
            o = _softmax_pv(s, [vp[:, 256 * g:256 * (g + 1)] for vp in v_parts], scale)
            out = jnp.where(_head_mask(lq, 256, hh, MLA_V_DIM), o, out)
        halves.append(out)
    return halves


def _mla_keys(ka_val_ckv, kr_slot, wk_ref, wv_ref):
    cb = ka_val_ckv.astype(BF16)
    kn = _dot(cb, wk_ref[...])
    ks = [(kn[:, hd * SLOT:(hd + 1) * SLOT] + kr_slot).astype(BF16) for hd in range(MLA_HEADS)]
    return jnp.concatenate(ks, axis=1), _dot(cb, wv_ref[...]).astype(BF16)


def _kr_only(kpe_slot):
    lane = _lane_iota(kpe_slot.shape)
    return jnp.where((lane >= KR_LO) & (lane < KR_HI), kpe_slot, 0.0)


def _diff_lambda(dl_ref, lam_init):
    dl = dl_ref[...]
    a = jnp.sum(dl[0:1] * dl[1:2], axis=-1, keepdims=True)
    b = jnp.sum(dl[2:3] * dl[3:4], axis=-1, keepdims=True)
    return jnp.exp(a) - jnp.exp(b) + lam_init


def _mix_prompt_kernel(lam_init, qa_ref, ka_ref, ra_ref, decf_ref, decb_ref, dl_ref, subln_ref,
                       wk_ref, wv_ref, mix_ref, sf_ref, sb_ref):
    seq = qa_ref.shape[0]
    zero_state = jnp.zeros((RET_W, RET_W), F32)
    outs, sf, sb = _retention(ra_ref, seq, decf_ref, decb_ref, zero_state, zero_state)
    for n, o in enumerate(outs):
        mix_ref[n * RET_CHUNK:(n + 1) * RET_CHUNK, 0:RET_W] = o.astype(BF16)
    for hd in range(RET_HEADS):
        blk = slice(hd * RET_DIM, (hd + 1) * RET_DIM)
        sf_ref[hd] = sf[blk, blk]
        sb_ref[hd] = sb[blk, blk]

    bd_ones = jnp.where(_block_diag_ones(DIFF_V_W, DIFF_V_DIM), 1.0, 0.0).astype(BF16)
    lam = _diff_lambda(dl_ref, lam_init)
    kd = [ka_ref[:, KA_DK + hd * SLOT:KA_DK + (hd + 1) * SLOT].astype(BF16) for hd in range(DIFF_HEADS)]
    vd = ka_ref[:, KA_DV:KA_CKV].astype(BF16)
    mix_ref[:, RET_W:RET_W + DIFF_V_W] = _diff_attention(
        qa_ref[:, 0:4 * SLOT], [kd], [vd], lam, subln_ref[...], lam_init, bd_ones).astype(BF16)

    km, vm = _mla_keys(ka_ref[:, KA_CKV:KA_KPE], _kr_only(ka_ref[:, KA_KPE:KA_W]), wk_ref, wv_ref)
    halves = _mla_attention(qa_ref[:, 4 * SLOT:QA_W], [km], [vm])
    mix_ref[:, 512:768] = halves[0].astype(BF16)
    mix_ref[:, 768:1024] = halves[1].astype(BF16)


def _mixer_param_specs(layer):
    return [
        _layer_spec(layer, 1, RET_W),
        _layer_spec(layer, 1, RET_W),
        _layer_spec(layer, 4, DIFF_QK_DIM),
        _layer_spec(layer, 1, DIFF_V_W),
        _layer_spec(layer, MLA_KV_LORA, MLA_HEADS * SLOT),
        _layer_spec(layer, MLA_KV_LORA, MLA_V_W),
    ]


def _mix_prompt_call(lam_init, layer, qa, ka, ra, n_seq, seq_len, decf, decb, dl, subln, wk, wv):
    seq = lambda b: (b, 0)
    state_spec = pl.BlockSpec((None, RET_HEADS, RET_DIM, RET_DIM), lambda b: (b, 0, 0, 0))
    state_shape = jax.ShapeDtypeStruct((n_seq, RET_HEADS, RET_DIM, RET_DIM), F32)
    return pl.pallas_call(
        functools.partial(_mix_prompt_kernel, lam_init),
        grid=(n_seq,),
        in_specs=[
            pl.BlockSpec((seq_len, QA_W), seq),
            pl.BlockSpec((seq_len, KA_W), seq),
            pl.BlockSpec((seq_len, D_MODEL), seq),
        ] + _mixer_param_specs(layer),
        out_specs=[pl.BlockSpec((seq_len, MIX_W), seq), state_spec, state_spec],
        out_shape=[jax.ShapeDtypeStruct((n_seq * seq_len, MIX_W), BF16), state_shape, state_shape],
        compiler_params=_cparams(("arbitrary",)),
        name="mix_prompt",
    )(qa, ka, ra, decf, decb, dl, subln, wk, wv)


def _mix_sample_kernel(lam_init, qa_ref, ka_ref, ra_ref, ckd_ref, cvd_ref, cckv_ref, ckpe_ref,
                       s0f_ref, s0b_ref, decf_ref, decb_ref, dl_ref, subln_ref, wk_ref, wv_ref,
                       place_ref, mix_ref,
                       ret_s, kdn_s, vdn_s, kdc_s, vdc_s, kmn_s, vmn_s, kmc_s, vmc_s):
    j = pl.program_id(1)
    seq = ka_ref.shape[0]

    @pl.when(j == 0)
    def _():
        outs, _, _ = _retention(ra_ref, seq, decf_ref, decb_ref, s0f_ref[...], s0b_ref[...])
        for n, o in enumerate(outs):
            ret_s[n * RET_CHUNK:(n + 1) * RET_CHUNK, :] = o
        kdn_s[...] = ka_ref[:, KA_DK:KA_DV].astype(BF16)
        vdn_s[...] = ka_ref[:, KA_DV:KA_CKV].astype(BF16)
        kdc_s[...] = ckd_ref[...].astype(BF16)
        vdc_s[...] = cvd_ref[...].astype(BF16)
        km, vm = _mla_keys(ka_ref[:, KA_CKV:KA_KPE], _kr_only(ka_ref[:, KA_KPE:KA_W]), wk_ref, wv_ref)
        kmn_s[...] = km
        vmn_s[...] = vm
        kr_ctx = _dot(ckpe_ref[...].astype(BF16), place_ref[...])
        km, vm = _mla_keys(cckv_ref[...], kr_ctx, wk_ref, wv_ref)
        kmc_s[...] = km
        vmc_s[...] = vm

    row0 = pl.multiple_of(j * TM, TM)
    mix_ref[:, 0:RET_W] = ret_s[pl.ds(row0, TM), :].astype(BF16)

    bd_ones = jnp.where(_block_diag_ones(DIFF_V_W, DIFF_V_DIM), 1.0, 0.0).astype(BF16)
    lam = _diff_lambda(dl_ref, lam_init)
    kd_ctx = [kdc_s[hd] for hd in range(DIFF_HEADS)]
    kd_new = [kdn_s[:, hd * SLOT:(hd + 1) * SLOT] for hd in range(DIFF_HEADS)]
    mix_ref[:, RET_W:RET_W + DIFF_V_W] = _diff_attention(
        qa_ref[:, 0:4 * SLOT], [kd_ctx, kd_new], [vdc_s[...], vdn_s[...]], lam, subln_ref[...],
        lam_init, bd_ones).astype(BF16)

    halves = _mla_attention(qa_ref[:, 4 * SLOT:QA_W], [kmc_s[...], kmn_s[...]], [vmc_s[...], vmn_s[...]])
    mix_ref[:, 512:768] = halves[0].astype(BF16)
    mix_ref[:, 768:1024] = halves[1].astype(BF16)


def _mix_sample_call(lam_init, layer, qa, ka, ra, tok0, n_seq, seq_len, past_len, cache_dk, cache_dv_t,
                     cache_ckv, cache_kpe, s0f_bd, s0b_bd, decf, decb, dl, subln, wk, wv, place):
    nq = seq_len // TM
    q0 = tok0 // TM
    s0 = tok0 // seq_len
    const = lambda b, j: (0, 0)
    return pl.pallas_call(
        functools.partial(_mix_sample_kernel, lam_init),
        grid=(n_seq, nq),
        in_specs=[
            pl.BlockSpec((TM, QA_W), lambda b, j: (q0 + b * nq + j, 0)),
            pl.BlockSpec((seq_len, KA_W), lambda b, j: (s0 + b, 0)),
            pl.BlockSpec((seq_len, D_MODEL), lambda b, j: (s0 + b, 0)),
            pl.BlockSpec((None, None, DIFF_HEADS, past_len, 2 * DIFF_QK_DIM), lambda b, j: (b, layer, 0, 0, 0)),
            pl.BlockSpec((None, None, past_len, DIFF_V_W), lambda b, j: (b, layer, 0, 0)),
            pl.BlockSpec((None, None, past_len, MLA_KV_LORA), lambda b, j: (b, layer, 0, 0)),
            pl.BlockSpec((None, None, past_len, MLA_ROPE_DIM), lambda b, j: (b, layer, 0, 0)),
            pl.BlockSpec((None, None, RET_W, RET_W), lambda b, j: (b, layer, 0, 0)),
            pl.BlockSpec((None, None, RET_W, RET_W), lambda b, j: (b, layer, 0, 0)),
        ] + _mixer_param_specs(layer) + [
            pl.BlockSpec((MLA_ROPE_DIM, SLOT), const),
        ],
        out_specs=pl.BlockSpec((TM, MIX_W), lambda b, j: (b * nq + j, 0)),
        out_shape=jax.ShapeDtypeStruct((n_seq * seq_len, MIX_W), BF16),
        scratch_shapes=[
            pltpu.VMEM((seq_len, RET_W), F32),
            pltpu.VMEM((seq_len, 4 * SLOT), BF16),
            pltpu.VMEM((seq_len, DIFF_V_W), BF16),
            pltpu.VMEM((DIFF_HEADS, past_len, 2 * DIFF_QK_DIM), BF16),
            pltpu.VMEM((past_len, DIFF_V_W), BF16),
            pltpu.VMEM((seq_len, MLA_HEADS * SLOT), BF16),
            pltpu.VMEM((seq_len, MLA_V_W), BF16),
            pltpu.VMEM((past_len, MLA_HEADS * SLOT), BF16),
            pltpu.VMEM((past_len, MLA_V_W), BF16),
        ],
        compiler_params=_cparams(("arbitrary", "arbitrary")),
        name="mix_sample",
    )(qa, ka, ra, cache_dk, cache_dv_t, cache_ckv, cache_kpe, s0f_bd, s0b_bd,
      decf, decb, dl, subln, wk, wv, place)


def _route(h2, rwt_ref, rb_ref):
    tm = h2.shape[0]
    neg = -jnp.inf
    logits = _split_dot_nt(rwt_ref[...], h2)
    sc = jax.nn.sigmoid(logits)
    sel = sc + rb_ref[...]
    member = lax.broadcasted_iota(I32, (GROUP_SIZE, tm), 0).astype(F32)
    gscore = []
    for g in range(N_GROUPS):
        sg = sel[g * GROUP_SIZE:(g + 1) * GROUP_SIZE, :]
        m1 = jnp.max(sg, axis=0, keepdims=True)
        f1 = jnp.min(jnp.where(sg == m1, member, float(GROUP_SIZE)), axis=0, keepdims=True)
        m2 = jnp.max(jnp.where(member == f1, neg, sg), axis=0, keepdims=True)
        gscore.append(m1 + m2)
    gsel = [jnp.zeros((1, tm), F32) for _ in range(N_GROUPS)]
    for _ in range(TOPK_GROUPS):
        mx = gscore[0]
        for g in range(1, N_GROUPS):
            mx = jnp.maximum(mx, gscore[g])
        fi = jnp.full((1, tm), float(N_GROUPS), F32)
        for g in range(N_GROUPS - 1, -1, -1):
            fi = jnp.where(gscore[g] == mx, float(g), fi)
        for g in range(N_GROUPS):
            hit = fi == float(g)
            gsel[g] = jnp.where(hit, 1.0, gsel[g])
            gscore[g] = jnp.where(hit, neg, gscore[g])
    cand = jnp.concatenate(
        [jnp.where(gsel[g] > 0.0, sel[g * GROUP_SIZE:(g + 1) * GROUP_SIZE, :], neg) for g in range(N_GROUPS)],
        axis=0)
    flat = lax.broadcasted_iota(I32, (N_EXPERTS, tm), 0).astype(F32)
    hits, gts = [], []
    chosen = jnp.zeros((N_EXPERTS, tm), F32)
    for _ in range(TOP_K):
        mx = jnp.max(cand, axis=0, keepdims=True)
        fk = jnp.min(jnp.where(cand == mx, flat, float(N_EXPERTS)), axis=0, keepdims=True)
        hit = flat == fk
        hits.append(hit)
        gts.append(jnp.sum(jnp.where(hit, sc, 0.0), axis=0, keepdims=True))
        chosen = jnp.where(hit, 1.0, chosen)
        cand = jnp.where(hit, neg, cand)
    gsum = gts[0]
    for g in gts[1:]:
        gsum = gsum + g
    gts = [g / gsum * ROUTED_SCALE for g in gts]

    before = (lax.broadcasted_iota(I32, (tm, tm), 0) < lax.broadcasted_iota(I32, (tm, tm), 1))
    rank_in = _dot(chosen.astype(BF16), jnp.where(before, 1.0, 0.0).astype(BF16))
    cnt = jnp.sum(chosen, axis=1, keepdims=True)
    cnt_pad = jnp.floor((cnt + (CH - 1.0)) * (1.0 / CH)) * CH
    below = (lax.broadcasted_iota(I32, (N_EXPERTS, N_EXPERTS), 1) < lax.broadcasted_iota(I32, (N_EXPERTS, N_EXPERTS), 0))
    start = _dot(jnp.where(below, 1.0, 0.0).astype(BF16),
                 jnp.broadcast_to(cnt_pad, (N_EXPERTS, LANES)).astype(BF16))[:, 0:1]
    pos = rank_in + start
    lpos = [jnp.sum(jnp.where(hit, pos, 0.0), axis=0, keepdims=True) for hit in hits]
    return lpos, gts, cnt_pad, start


def _slot_rows(vals, n_rows):
    tm = vals[0].shape[1]
    row = lax.broadcasted_iota(I32, (n_rows, tm), 0)
    out = jnp.zeros((n_rows, tm), F32)
    for k, v in enumerate(vals):
        out = jnp.where(row == k, v, out)
    return out


def _post_kernel(tbl_ref, xa_ref, xb_ref, mp_ref, ms_ref, mod_ref, wout_ref, g2_ref, shg_ref, shu_ref, shd_ref,
                 rwt_ref, rb_ref, base_ref, h2_ref, lpos_ref, ptok_ref, gtok_ref, cnt_ref, start_ref, rel_ref,
                 run_ref):
    i = pl.program_id(0)
    is_context = i < tbl_ref[2, 0]

    @pl.when(i == 0)
    def _():
        cnt_ref[...] = jnp.zeros_like(cnt_ref)
        start_ref[...] = jnp.zeros_like(start_ref)
        rel_ref[...] = jnp.zeros_like(rel_ref)
        run_ref[...] = jnp.zeros_like(run_ref)

    mod = mod_ref[...]
    gate1 = mod[:, 2 * D_MODEL:3 * D_MODEL]
    shift2 = mod[:, 3 * D_MODEL:4 * D_MODEL]
    scale2 = mod[:, 4 * D_MODEL:5 * D_MODEL]
    gate2 = mod[:, 5 * D_MODEL:6 * D_MODEL]
    for s in range(TS):
        rows = slice(s * TM, (s + 1) * TM)
        mix = jnp.where(is_context, mp_ref[rows, :], ms_ref[rows, :])
        x = jnp.where(is_context, xa_ref[rows, :], xb_ref[rows, :])
        x1 = x + gate1 * _dot(mix, wout_ref[...])
        ms = jnp.mean(x1 * x1, axis=-1, keepdims=True)
        h2 = x1 * lax.rsqrt(ms + NORM_EPS) * g2_ref[...]
        h2 = h2 * (1.0 + scale2) + shift2
        hb = h2.astype(BF16)
        h2_ref[rows, :] = hb
        act = _silu(_dot(hb, shg_ref[...])) * _dot(hb, shu_ref[...])
        base_ref[rows, :] = x1 + gate2 * _dot(act.astype(BF16), shd_ref[...])

        lpos, gts, cnt_pad, start = _route(h2, rwt_ref, rb_ref)
        lpos_ref[:, rows] = _slot_rows(lpos, SLOT_ROWS)
        ptok_ref[rows, :] = _slot_rows(lpos, LANES).T
        gtok_ref[rows, :] = _slot_rows(gts, LANES).T
        tile_col = lax.broadcasted_iota(I32, cnt_ref.shape, 1) == i * TS + s
        run = run_ref[...]
        cnt_ref[...] = jnp.where(tile_col, cnt_pad.astype(I32), cnt_ref[...])
        start_ref[...] = jnp.where(tile_col, start.astype(I32), start_ref[...])
        rel_ref[...] = jnp.where(tile_col, run.astype(I32), rel_ref[...])
        run_ref[...] = run + cnt_pad


def _post_call(layer, tbl, xa, xb, mix_p, mix_s, mod4, wout, g2, shg, shu, shd, rwt, rb):
    nps = mix_p.shape[0] // STEP
    n_tok = mix_p.shape[0] + mix_s.shape[0]
    nt = n_tok // TM
    ns = n_tok // STEP
    const = lambda i, t: (0, 0)
    tile = lambda i, t: (i, 0)
    gs = pltpu.PrefetchScalarGridSpec(
        num_scalar_prefetch=1,
        grid=(ns,),
        in_specs=_x_specs(nps, xa is xb) + [
            pl.BlockSpec((STEP, MIX_W), lambda i, t: (jnp.minimum(i, nps - 1), 0)),
            pl.BlockSpec((STEP, MIX_W), lambda i, t: (jnp.maximum(i - nps, 0), 0)),
            _mod_spec(layer),
            _layer_spec(layer, MIX_W, D_MODEL),
            _layer_spec(layer, 1, D_MODEL),
            _layer_spec(layer, D_MODEL, EXPERT_FF),
            _layer_spec(layer, D_MODEL, EXPERT_FF),
            _layer_spec(layer, EXPERT_FF, D_MODEL),
            _layer_spec(layer, N_EXPERTS, D_MODEL),
            _layer_spec(layer, N_EXPERTS, 1),
        ],
        out_specs=[
            pl.BlockSpec((STEP, D_MODEL), tile),
            pl.BlockSpec((STEP, D_MODEL), tile),
            pl.BlockSpec((SLOT_ROWS, STEP), lambda i, t: (0, i)),
            pl.BlockSpec((STEP, LANES), tile),
            pl.BlockSpec((STEP, LANES), tile),
            pl.BlockSpec((N_EXPERTS, LANES), const),
            pl.BlockSpec((N_EXPERTS, LANES), const),
            pl.BlockSpec((N_EXPERTS, LANES), const),
        ],
        scratch_shapes=[pltpu.VMEM((N_EXPERTS, 1), F32)],
    )
    assert nt <= LANES
    return pl.pallas_call(
        _post_kernel,
        grid_spec=gs,
        out_shape=[
            jax.ShapeDtypeStruct((n_tok, D_MODEL), F32),
            jax.ShapeDtypeStruct((n_tok, D_MODEL), BF16),
            jax.ShapeDtypeStruct((SLOT_ROWS, n_tok), F32),
            jax.ShapeDtypeStruct((n_tok, LANES), F32),
            jax.ShapeDtypeStruct((n_tok, LANES), F32),
            jax.ShapeDtypeStruct((N_EXPERTS, LANES), I32),
            jax.ShapeDtypeStruct((N_EXPERTS, LANES), I32),
            jax.ShapeDtypeStruct((N_EXPERTS, LANES), I32),
        ],
        compiler_params=_cparams(("arbitrary",)),
        name="post_route",
    )(tbl, xa, xb, mix_p, mix_s, mod4, wout, g2, shg, shu, shd, rwt, rb)


SEG_ROW0, SEG_NBLK, SEG_PAD0, SEG_NPAD, SEG_NEXT, SEG_USED = range(6)


def _plan_kernel(last_tile, cnt_ref, rel_ref, seg_ref):
    def per_expert(e, start):
        end = start + rel_ref[e, last_tile] + cnt_ref[e, last_tile]
        nb = lax.shift_right_logical(end - start + (EB - 1), LOG_EB)
        nxt = start + lax.shift_left(nb, LOG_EB)
        seg_ref[SEG_ROW0, e] = start
        seg_ref[SEG_NBLK, e] = nb
        seg_ref[SEG_PAD0, e] = end
        seg_ref[SEG_NPAD, e] = lax.shift_right_logical(nxt - end, LOG_CH)
        seg_ref[SEG_USED, e] = 0
        return nxt

    total = lax.fori_loop(0, N_EXPERTS, per_expert, jnp.int32(0))

    def link(k, nxt):
        e = N_EXPERTS - 1 - k
        seg_ref[SEG_NEXT, e] = nxt
        return jnp.where(seg_ref[SEG_NBLK, e] > 0, e, nxt)

    first = lax.fori_loop(0, N_EXPERTS, link, jnp.int32(N_EXPERTS))
    seg_ref[SEG_USED, 0] = lax.shift_right_logical(total, LOG_EB)
    seg_ref[SEG_USED, 1] = first


def _plan_call(cnt, rel, nt):
    smem = pl.BlockSpec(memory_space=pltpu.SMEM)
    return pl.pallas_call(
        functools.partial(_plan_kernel, nt - 1),
        in_specs=[smem, smem],
        out_specs=smem,
        out_shape=jax.ShapeDtypeStruct((6, N_EXPERTS), I32),
        name="moe_plan",
    )(cnt, rel)


def _rows_copy(src_ref, src_row, dst_ref, dst_row, n_rows, sem):
    return pltpu.make_async_copy(src_ref.at[pl.ds(pl.multiple_of(src_row, CH), n_rows)],
                                 dst_ref.at[pl.ds(pl.multiple_of(dst_row, CH), n_rows)], sem)


class _Runs:
    def __init__(self, cnt_ref, start_ref, rel_ref, seg_ref):
        self.cnt, self.start, self.rel, self.seg = cnt_ref, start_ref, rel_ref, seg_ref

    def start_copies(self, i, copy, tot_ref, slot):
        def per_pair(e2, carry):
            n_big, n_small = carry
            for par in range(2):
                e = 2 * e2 + par
                c = self.cnt[e, i]
                a0 = self.start[e, i]
                b0 = self.seg[SEG_ROW0, e] + self.rel[e, i]
                nb = lax.shift_right_logical(c, LOG_CH + 1)
                odd = jnp.bitwise_and(lax.shift_right_logical(c, LOG_CH), 1)

                def big(q, cc, a0=a0, b0=b0, par=par):
                    copy(a0 + q * (2 * CH), b0 + q * (2 * CH), 2 * CH).start(priority=par)
                    return cc

                lax.fori_loop(0, nb, big, 0)

                @pl.when(odd == 1)
                def _(a0=a0, b0=b0, nb=nb, par=par):
                    copy(a0 + nb * (2 * CH), b0 + nb * (2 * CH), CH).start(priority=par)

                n_big, n_small = n_big + nb, n_small + odd
            return n_big, n_small

        n_big, n_small = lax.fori_loop(0, N_EXPERTS // 2, per_pair, (jnp.int32(0), jnp.int32(0)))
        tot_ref[slot, 0] = n_big
        tot_ref[slot, 1] = n_small

    @staticmethod
    def wait_copies(copy, tot_ref, slot):
        def big(q, c):
            copy(0, 0, 2 * CH).wait()
            return c

        lax.fori_loop(0, tot_ref[slot, 0], big, 0)

        def small(q, c):
            copy(0, 0, CH).wait()
            return c

        lax.fori_loop(0, tot_ref[slot, 1], small, 0)


def _dispatch_kernel(reuse, cnt_ref, start_ref, rel_ref, seg_ref, h_ref, lpos_ref, *rest):
    xb_hbm, sort_s, zero_s, tot_s, sems = rest[1:] if reuse else rest
    i = pl.program_id(0)
    last = pl.num_programs(0) - 1
    slot = lax.rem(i, 2)
    runs = _Runs(cnt_ref, start_ref, rel_ref, seg_ref)
    lp = lpos_ref[...]
    hb = h_ref[...]
    blk = TM
    for r in range(SORT_ROWS // blk):
        srow = (lax.broadcasted_iota(I32, (blk, TM), 0) + r * blk).astype(F32)
        p = jnp.zeros((blk, TM), F32)
        for k in range(TOP_K):
            p = jnp.where(srow == lp[k:k + 1, :], 1.0, p)
        sort_s[slot, r * blk:(r + 1) * blk, :] = _dot(p.astype(BF16), hb).astype(BF16)

    def copy_from(sl):
        return lambda s, d, n: _rows_copy(sort_s.at[sl], s, xb_hbm, d, n, sems.at[sl])

    runs.start_copies(i, copy_from(slot), tot_s, slot)

    @pl.when(i > 0)
    def _():
        runs.wait_copies(copy_from(1 - slot), tot_s, 1 - slot)

    @pl.when(i == last)
    def _():
        runs.wait_copies(copy_from(slot), tot_s, slot)
        _zero_fill_unused(seg_ref, xb_hbm, zero_s, sems.at[0], tail=not reuse)


def _zero_fill_unused(seg_ref, buf_hbm, zero_s, sem, tail):
    zero_s[...] = jnp.zeros_like(zero_s)

    def per_expert(e, c):
        first = seg_ref[SEG_PAD0, e]

        def z_issue(r, cc):
            _rows_copy(zero_s, 0, buf_hbm, first + r * CH, CH, sem).start()
            return cc

        lax.fori_loop(0, seg_ref[SEG_NPAD, e], z_issue, 0)

        def z_drain(r, cc):
            _rows_copy(zero_s, 0, buf_hbm, 0, CH, sem).wait()
            return cc

        lax.fori_loop(0, seg_ref[SEG_NPAD, e], z_drain, 0)
        return c

    lax.fori_loop(0, N_EXPERTS, per_expert, 0)
    if tail:
        _zero_fill_tail(seg_ref, buf_hbm, zero_s, sem)


def _zero_fill_tail(seg_ref, buf_hbm, zero_s, sem):
    n_blocks = buf_hbm.shape[0] // EB

    def blk_copy(b):
        return pltpu.make_async_copy(zero_s, buf_hbm.at[pl.ds(pl.multiple_of(b * EB, EB), EB)], sem)

    def t_issue(b, cc):
        blk_copy(b).start()
        return cc

    lax.fori_loop(seg_ref[SEG_USED, 0], n_blocks, t_issue, 0)

    def t_drain(b, cc):
        blk_copy(0).wait()
        return cc

    lax.fori_loop(seg_ref[SEG_USED, 0], n_blocks, t_drain, 0)


def _dispatch_call(cnt, start, rel, seg, h2, lpos, n_rows, prev=None):
    n_tok = h2.shape[0]
    nt = n_tok // TM
    smem = pl.BlockSpec(memory_space=pltpu.SMEM)
    reuse = prev is not None
    return pl.pallas_call(
        functools.partial(_dispatch_kernel, reuse),
        grid=(nt,),
        in_specs=[
            smem, smem, smem, smem,
            pl.BlockSpec((TM, D_MODEL), lambda i: (i, 0)),
            pl.BlockSpec((SLOT_ROWS, TM), lambda i: (0, i)),
        ] + ([pl.BlockSpec(memory_space=pl.ANY)] if reuse else []),
        input_output_aliases={6: 0} if reuse else {},
        out_specs=pl.BlockSpec(memory_space=pl.ANY),
        out_shape=jax.ShapeDtypeStruct((n_rows, D_MODEL), BF16),
        scratch_shapes=[
            pltpu.VMEM((2, SORT_ROWS, D_MODEL), BF16),
            pltpu.VMEM((EB, D_MODEL), BF16),
            pltpu.SMEM((2, 2), I32),
            pltpu.SemaphoreType.DMA((2,)),
        ],
        compiler_params=_cparams(("arbitrary",)),
        name="moe_dispatch",
    )(cnt, start, rel, seg, h2, lpos, *([prev] if reuse else []))


X_SLOTS = 4


def _experts_kernel(layer, seg_ref, wg_hbm, wu_hbm, wd_hbm, xb_hbm, yb_hbm,
                    wgf_s, wuf_s, wdf_s, wg_s, wu_s, wd_s, x_s, y_s, sem_w, sem_x, sem_y):
    n_used = seg_ref[SEG_USED, 0]
    first = seg_ref[SEG_USED, 1]

    def rows(g):
        return pl.ds(pl.multiple_of(g * EB, EB), EB)

    def x_copy(g, slot):
        return pltpu.make_async_copy(xb_hbm.at[rows(g)], x_s.at[slot], sem_x.at[slot])

    def y_copy(g, slot):
        return pltpu.make_async_copy(y_s.at[slot], yb_hbm.at[rows(g)], sem_y.at[slot])

    def w_copies(e, slot):
        return [pltpu.make_async_copy(hbm.at[layer, e], buf.at[slot], sem_w.at[slot, n])
                for n, (hbm, buf) in enumerate(((wg_hbm, wgf_s), (wu_hbm, wuf_s), (wd_hbm, wdf_s)))]

    def fetch_weights(e, slot):
        for c in w_copies(e, slot):
            c.start()

    def take_weights(slot):
        for c in w_copies(0, slot):
            c.wait()
        wg_s[...] = wgf_s[slot].astype(BF16)
        wu_s[...] = wuf_s[slot].astype(BF16)
        wd_s[...] = wdf_s[slot].astype(BF16)

    def next_expert(e):
        return seg_ref[SEG_NEXT, jnp.minimum(e, N_EXPERTS - 1)]

    @pl.when(n_used > 0)
    def _():
        for p in range(X_SLOTS - 1):
            @pl.when(p < n_used)
            def _(p=p):
                x_copy(p, p).start()

        fetch_weights(first, 0)

        @pl.when(next_expert(first) < N_EXPERTS)
        def _():
            fetch_weights(next_expert(first), 1)

        take_weights(0)

        def block(g, carry):
            e, left, wslot = carry
            ahead = g + (X_SLOTS - 1)

            @pl.when(ahead < n_used)
            def _():
                x_copy(ahead, lax.rem(ahead, X_SLOTS)).start()

            xslot = lax.rem(g, X_SLOTS)
            yslot = lax.rem(g, 2)
            x_copy(g, xslot).wait()

            @pl.when(g >= 2)
            def _():
                y_copy(g - 2, yslot).wait()

            xb = x_s[xslot]
            act = _silu(_dot(xb, wg_s[...])) * _dot(xb, wu_s[...])
            y_s[yslot] = _dot(act.astype(BF16), wd_s[...]).astype(BF16)
            y_copy(g, yslot).start()

            switch = jnp.logical_and(left == 1, g + 1 < n_used)
            nxt = next_expert(e)

            @pl.when(switch)
            def _():
                take_weights(1 - wslot)

                @pl.when(next_expert(nxt) < N_EXPERTS)
                def _():
                    fetch_weights(next_expert(nxt), wslot)

            nxt_c = jnp.minimum(nxt, N_EXPERTS - 1)
            return (jnp.where(switch, nxt_c, e), jnp.where(switch, seg_ref[SEG_NBLK, nxt_c], left - 1),
                    jnp.where(switch, 1 - wslot, wslot))

        lax.fori_loop(0, n_used, block,
                      (first, seg_ref[SEG_NBLK, jnp.minimum(first, N_EXPERTS - 1)], jnp.int32(0)))

        @pl.when(n_used >= 2)
        def _():
            y_copy(n_used - 2, lax.rem(n_used, 2)).wait()

        y_copy(n_used - 1, lax.rem(n_used - 1, 2)).wait()


def _experts_call(seg, xb, layer, wg, wu, wd):
    n_rows = xb.shape[0]
    hbm = pl.BlockSpec(memory_space=pl.ANY)
    return pl.pallas_call(
        functools.partial(_experts_kernel, layer),
        in_specs=[pl.BlockSpec(memory_space=pltpu.SMEM), hbm, hbm, hbm, hbm],
        out_specs=hbm,
        out_shape=jax.ShapeDtypeStruct((n_rows, D_MODEL), BF16),
        input_output_aliases={4: 0},
        scratch_shapes=[
            pltpu.VMEM((2, D_MODEL, EXPERT_FF), F32),
            pltpu.VMEM((2, D_MODEL, EXPERT_FF), F32),
            pltpu.VMEM((2, EXPERT_FF, D_MODEL), F32),
            pltpu.VMEM((D_MODEL, EXPERT_FF), BF16),
            pltpu.VMEM((D_MODEL, EXPERT_FF), BF16),
            pltpu.VMEM((EXPERT_FF, D_MODEL), BF16),
            pltpu.VMEM((X_SLOTS, EB, D_MODEL), BF16),
            pltpu.VMEM((2, EB, D_MODEL), BF16),
            pltpu.SemaphoreType.DMA((2, 3)),
            pltpu.SemaphoreType.DMA((X_SLOTS,)),
            pltpu.SemaphoreType.DMA((2,)),
        ],
        compiler_params=pltpu.CompilerParams(vmem_limit_bytes=VMEM_LIMIT),
        name="moe_experts",
    )(seg, wg, wu, wd, xb)


def _combine_kernel(final, tbl_ref, cnt_ref, start_ref, rel_ref, seg_ref, yb_hbm, base_ref, gtok_ref, ptok_ref,
                    mod_ref, gf_ref, *rest):
    *out_refs, sort_s, tot_s, sems = rest
    i = pl.program_id(0)
    slot = lax.rem(i, 2)
    runs = _Runs(cnt_ref, start_ref, rel_ref, seg_ref)

    def copy_to(sl):
        return lambda s, d, n: _rows_copy(yb_hbm, d, sort_s.at[sl], s, n, sems.at[sl])

    @pl.when(i == 0)
    def _():
        sort_s[...] = jnp.zeros_like(sort_s)
        runs.start_copies(i, copy_to(slot), tot_s, slot)

    @pl.when(i + 1 < pl.num_programs(0))
    def _():
        runs.start_copies(i + 1, copy_to(1 - slot), tot_s, 1 - slot)

    runs.wait_copies(copy_to(slot), tot_s, slot)

    gt = gtok_ref[...]
    pt = ptok_ref[...]
    col = lax.broadcasted_iota(I32, (TM, SORT_ROWS), 1).astype(F32)
    w = jnp.zeros((TM, SORT_ROWS), F32)
    for k in range(TOP_K):
        w = jnp.where(col == pt[:, k:k + 1], gt[:, k:k + 1], w)
    routed = _dot(w.astype(BF16), sort_s[slot])
    gate2 = mod_ref[...][:, 5 * D_MODEL:6 * D_MODEL]
    y = base_ref[...] + gate2 * routed
    if final:
        y = y * lax.rsqrt(jnp.mean(y * y, axis=-1, keepdims=True) + NORM_EPS) * gf_ref[...]
        yp_ref, ys_ref = out_refs
        is_context = i < tbl_ref[2, 0]

        @pl.when(is_context)
        def _():
            yp_ref[...] = y

        @pl.when(jnp.logical_not(is_context))
        def _():
            ys_ref[...] = y
    else:
        out_refs[0][...] = y


def _combine_call(final, layer, npt, tbl, cnt, start, rel, seg, yb, base, gtok, ptok, mod4, gfinal):
    n_tok = base.shape[0]
    nt = n_tok // TM
    tile = lambda i, t: (i, 0)
    smem = pl.BlockSpec(memory_space=pltpu.SMEM)
    if final:
        out_specs = [pl.BlockSpec((TM, D_MODEL), lambda i, t: (jnp.minimum(i, npt - 1), 0)),
                     pl.BlockSpec((TM, D_MODEL), lambda i, t: (jnp.maximum(i - npt, 0), 0))]
        out_shape = [jax.ShapeDtypeStruct((npt * TM, D_MODEL), F32),
                     jax.ShapeDtypeStruct((n_tok - npt * TM, D_MODEL), F32)]
    else:
        out_specs = pl.BlockSpec((TM, D_MODEL), tile)
        out_shape = jax.ShapeDtypeStruct((n_tok, D_MODEL), F32)
    gs = pltpu.PrefetchScalarGridSpec(
        num_scalar_prefetch=1,
        grid=(nt,),
        in_specs=[
            smem, smem, smem, smem,
            pl.BlockSpec(memory_space=pl.ANY),
            pl.BlockSpec((TM, D_MODEL), tile),
            pl.BlockSpec((TM, LANES), tile),
            pl.BlockSpec((TM, LANES), tile),
            _mod_spec(layer),
            pl.BlockSpec((1, D_MODEL), lambda i, t: (0, 0)),
        ],
        out_specs=out_specs,
        scratch_shapes=[
            pltpu.VMEM((2, SORT_ROWS, D_MODEL), BF16),
            pltpu.SMEM((2, 2), I32),
            pltpu.SemaphoreType.DMA((2,)),
        ],
    )
    return pl.pallas_call(
        functools.partial(_combine_kernel, final),
        grid_spec=gs,
        out_shape=out_shape,
        compiler_params=_cparams(("arbitrary",)),
        name="moe_combine",
    )(tbl, cnt, start, rel, seg, yb, base, gtok, ptok, mod4, gfinal)


def _pad_cols(w, groups, width, slot):
    lead = w.shape[:-1]
    w = w.reshape(*lead, groups, width)
    pad = [(0, 0)] * (len(lead) + 1) + [(0, slot - width)]
    return jnp.pad(w, pad).reshape(*lead, groups * slot)


def _prep_w_in(w):
    assert w.shape[-1] == C_KPE + MLA_ROPE_DIM
    return jnp.pad(w.astype(BF16), ((0, 0), (0, 0), (0, N_PRE - w.shape[-1])))


def _rope_tables(n_pos, dim, lane_offsets):
    f32 = np.float32
    n_rows = n_pos // GRID_W
    row = np.repeat(np.arange(n_rows, dtype=f32), GRID_W)
    col = np.tile(np.arange(GRID_W, dtype=f32), n_rows)
    half = dim // 2
    freqs = f32(ROPE_THETA) ** (-np.arange(0, half, 2, dtype=f32) / f32(half))
    ar = row[:, None] * freqs[None, :]
    ac = col[:, None] * freqs[None, :]
    ang = np.concatenate([ar, ar, ac, ac], axis=-1).astype(f32)
    cos, sin = np.cos(ang), np.sin(ang)
    first = (np.arange(dim) % 16) < 8
    sa = np.where(first[None, :], -sin, f32(0))
    sb = np.where(first[None, :], f32(0), sin)
    c_t = np.ones((STEP + n_pos, SLOT), f32)
    a_t = np.zeros((STEP + n_pos, SLOT), f32)
    b_t = np.zeros((STEP + n_pos, SLOT), f32)
    for off in lane_offsets:
        c_t[STEP:, off:off + dim] = cos
        a_t[STEP:, off:off + dim] = sa
        b_t[STEP:, off:off + dim] = sb
    return tuple(jnp.asarray(t) for t in (c_t, a_t, b_t))


def _block_diag_states(s):
    b, l, h, dk, dv = s.shape
    eye = jnp.eye(h, dtype=s.dtype)
    return jnp.einsum('blhkv,hg->blhkgv', s, eye).reshape(b, l, h * dk, h * dv)


def kernel(x_prompt, x_sample, cache_diff_k, cache_diff_v, cache_mla_ckv, cache_mla_kpe, state_ret_fwd, state_ret_bwd, c, c_ctx, w_ada, b_ada, norm_mix, norm_ffn, norm_final, w_in, ret_decay_fwd, ret_decay_bwd, diff_lambda, diff_subln, mla_q_norm, mla_w_uq, mla_kv_norm, mla_w_ukv, w_out, router_w, router_bias, exp_w_gate, exp_w_up, exp_w_down, sh_w_gate, sh_w_up, sh_w_down):
    n_pb, p_len, _ = x_prompt.shape
    n_sb, s_len, _ = x_sample.shape
    past_len = cache_diff_k.shape[3]
    n_p = n_pb * p_len
    n_s = n_sb * s_len
    n_tok = n_p + n_s
    nt = n_tok // TM
    npt = n_p // TM
    assert p_len == TM and s_len % TM == 0 and n_p % s_len == 0 and past_len % 8 == 0

    def step_tables(rows):
        steps = np.arange(n_tok // rows)
        n_ctx = n_p // rows
        per_seq = s_len // rows
        mod_row = np.where(steps < n_ctx, n_sb, (steps - n_ctx) // per_seq)
        rope_blk = np.where(steps < n_ctx, 0, 1 + (steps - n_ctx) % per_seq)
        return jnp.asarray(np.stack([mod_row, rope_blk, np.full(len(steps), n_ctx)]).astype(np.int32))

    assert n_p % STEP == 0 and s_len % STEP == 0
    tbl = step_tables(TM)
    tbl_step = step_tables(STEP)

    n_cond = 16
    cond = jnp.concatenate([c, c_ctx[None, :], jnp.zeros((n_cond - n_sb - 1, D_MODEL), F32)], axis=0)
    mod_all = _modulation(cond, w_ada, b_ada)

    rope_d = _rope_tables(s_len, DIFF_QK_DIM, (0, DIFF_QK_DIM))
    rope_m = _rope_tables(s_len, MLA_ROPE_DIM, (KR_LO,))
    place = np.zeros((MLA_ROPE_DIM, SLOT), np.float32)
    place[np.arange(MLA_ROPE_DIM), KR_LO + np.arange(MLA_ROPE_DIM)] = 1.0
    place = jnp.asarray(place, BF16)
    cache_dv_t = cache_diff_v.transpose(0, 1, 3, 2, 4).reshape(n_sb, DEPTH, past_len, DIFF_V_W)
    s0f_bd = _block_diag_states(state_ret_fwd)
    s0b_bd = _block_diag_states(state_ret_bwd)

    n_blocks = pl.cdiv(n_tok * TOP_K + nt * N_EXPERTS * (CH - 1) + N_EXPERTS * (EB - CH), EB)
    n_rows = n_blocks * EB

    xa, xb = x_prompt.reshape(n_p, D_MODEL), x_sample.reshape(n_s, D_MODEL)
    gfinal = norm_final.reshape(1, D_MODEL)

    mod4 = mod_all.reshape(DEPTH, n_cond, 1, 6 * D_MODEL)
    w_pre = _prep_w_in(w_in)
    wuq = _pad_cols(mla_w_uq, MLA_HEADS, MLA_NOPE_DIM + MLA_ROPE_DIM, SLOT).astype(BF16)
    ukv = mla_w_ukv.reshape(DEPTH, MLA_KV_LORA, MLA_HEADS, MLA_NOPE_DIM + MLA_V_DIM)
    wk = _pad_cols(ukv[..., :MLA_NOPE_DIM].reshape(DEPTH, MLA_KV_LORA, -1), MLA_HEADS, MLA_NOPE_DIM, SLOT).astype(BF16)
    wv = ukv[..., MLA_NOPE_DIM:].reshape(DEPTH, MLA_KV_LORA, MLA_V_W).astype(BF16)
    decf = jnp.repeat(ret_decay_fwd, RET_DIM, axis=-1).reshape(DEPTH, 1, RET_W)
    decb = jnp.repeat(ret_decay_bwd, RET_DIM, axis=-1).reshape(DEPTH, 1, RET_W)
    subln = jnp.tile(diff_subln, (1, DIFF_HEADS)).reshape(DEPTH, 1, DIFF_V_W)
    vec = lambda p: p.reshape(DEPTH, 1, -1)
    wout_b, shg_b, shu_b, shd_b = (w.astype(BF16) for w in (w_out, sh_w_gate, sh_w_up, sh_w_down))
    rwt = router_w.transpose(0, 2, 1)
    rb = router_bias.reshape(DEPTH, N_EXPERTS, 1)

    new_ctx = []
    sorted_buf = None
    for l in range(DEPTH):
        lam_init = 0.8 - 0.6 * math.exp(-0.3 * l)
        ra, qa, ka, *caches = _pre_call(l, tbl_step, xa, xb, npt, n_tok, mod4, vec(norm_mix), w_pre, vec(mla_q_norm),
                                        wuq, vec(mla_kv_norm), rope_d, rope_m)
        mix_p, *states = _mix_prompt_call(lam_init, l, qa, ka, ra, n_pb, p_len, decf, decb,
                                          diff_lambda, subln, wk, wv)
        new_ctx.append(caches + states)
        mix_s = _mix_sample_call(lam_init, l, qa, ka, ra, n_p, n_sb, s_len, past_len, cache_diff_k,
                                 cache_dv_t, cache_mla_ckv, cache_mla_kpe, s0f_bd, s0b_bd, decf, decb,
                                 diff_lambda, subln, wk, wv, place)
        base, h2, lpos, ptok, gtok, cnt, start, rel = _post_call(
            l, tbl_step, xa, xb, mix_p, mix_s, mod4, wout_b, vec(norm_ffn), shg_b, shu_b, shd_b, rwt, rb)
        seg = _plan_call(cnt, rel, nt)
        sorted_buf = _dispatch_call(cnt, start, rel, seg, h2, lpos, n_rows, prev=sorted_buf)
        sorted_buf = _experts_call(seg, sorted_buf, l, exp_w_gate, exp_w_up, exp_w_down)
        final = l == DEPTH - 1
        out = _combine_call(final, l, npt, tbl, cnt, start, rel, seg, sorted_buf, base, gtok, ptok, mod4, gfinal)
        xa, xb = out if final else (out, out)

    y_prompt = xa.reshape(n_pb, p_len, D_MODEL)
    y_sample = xb.reshape(n_sb, s_len, D_MODEL)
    return (y_prompt, y_sample, *(jnp.stack(per_layer, axis=1) for per_layer in zip(*new_ctx)))
```

```python
import functools
import math

import numpy as np
import jax
import jax.numpy as jnp
from jax import lax
from jax.experimental import pallas as pl
from jax.experimental.pallas import tpu as pltpu

F32 = jnp.float32
BF16 = jnp.bfloat16
I32 = jnp.int32

D_MODEL = 1024
DEPTH = 2
GRID_W = 64
ROPE_THETA = 10000.0
NORM_EPS = 1e-6

RET_HEADS = 4
RET_DIM = 64
RET_CHUNK = 128
RET_W = RET_HEADS * RET_DIM
DIFF_HEADS = 4
DIFF_QK_DIM = 32
DIFF_V_DIM = 64
DIFF_V_W = DIFF_HEADS * DIFF_V_DIM
MLA_HEADS = 8
MLA_Q_LORA = 256
MLA_KV_LORA = 128
MLA_NOPE_DIM = 64
MLA_ROPE_DIM = 32
MLA_V_DIM = 64
MLA_V_W = MLA_HEADS * MLA_V_DIM
MIX_W = RET_W + DIFF_V_W + MLA_V_W

N_EXPERTS = 64
TOP_K = 6
N_GROUPS = 8
GROUP_SIZE = N_EXPERTS // N_GROUPS
TOPK_GROUPS = 4
EXPERT_FF = 256
ROUTED_SCALE = 2.5

LANES = 128
SLOT = LANES
TM = 256
TS = 2
STEP = TS * TM
SLOT_ROWS = 8
EB = 512
LOG_EB = 9
CH = 16
LOG_CH = 4
SORT_ROWS = 2560
VMEM_LIMIT = 48 * 1024 * 1024

C_RQ, C_RK, C_RV, C_RG = 0, 256, 512, 768
C_DQ, C_DK, C_DV, C_CQ, C_CKV, C_KPE = 1024, 1280, 1536, 1792, 2048, 2176
N_PRE = 2304
QA_W = 4 * SLOT + MLA_HEADS * SLOT
KA_DK, KA_DV, KA_CKV, KA_KPE = 0, 512, 768, 896
KA_W = 1024
KR_LO, KR_HI = 64, 96


def _dot(a, b):
    return jnp.dot(a, b, preferred_element_type=F32)


def _dot_nt(a, b):
    return lax.dot_general(a, b, (((1,), (1,)), ((), ())), preferred_element_type=F32)


def _dot_tn(a, b):
    return lax.dot_general(a, b, (((0,), (0,)), ((), ())), preferred_element_type=F32)


def _split_dot(x, w_bf16):
    hi = x.astype(BF16)
    lo = (x - hi.astype(F32)).astype(BF16)
    return _dot(hi, w_bf16) + _dot(lo, w_bf16)


def _split_dot_nt(w, x):
    wh = w.astype(BF16)
    wl = (w - wh.astype(F32)).astype(BF16)
    xh = x.astype(BF16)
    xl = (x - xh.astype(F32)).astype(BF16)
    return _dot_nt(wh, xh) + _dot_nt(wh, xl) + _dot_nt(wl, xh)


def _silu(x):
    return x * jax.nn.sigmoid(x)


def _cparams(sem):
    return pltpu.CompilerParams(dimension_semantics=sem, vmem_limit_bytes=VMEM_LIMIT)


MOD_TN = 512


def _mod_kernel(c_ref, w_ref, b_ref, o_ref):
    s = _silu(c_ref[...])
    o_ref[...] = _split_dot3(s, w_ref[...]) + b_ref[...]


def _split_dot3(x, w):
    xh = x.astype(BF16)
    xl = (x - xh.astype(F32)).astype(BF16)
    wh = w.astype(BF16)
    wl = (w - wh.astype(F32)).astype(BF16)
    return _dot(xh, wh) + _dot(xh, wl) + _dot(xl, wh)


def _modulation(cond, w_ada, b_ada):
    n_rows = cond.shape[0]
    n_out = w_ada.shape[-1]
    return pl.pallas_call(
        _mod_kernel,
        grid=(DEPTH, n_out // MOD_TN),
        in_specs=[
            pl.BlockSpec((n_rows, D_MODEL), lambda l, j: (0, 0)),
            pl.BlockSpec((None, D_MODEL, MOD_TN), lambda l, j: (l, 0, j)),
            pl.BlockSpec((None, 1, MOD_TN), lambda l, j: (l, 0, j)),
        ],
        out_specs=pl.BlockSpec((None, n_rows, MOD_TN), lambda l, j: (l, 0, j)),
        out_shape=jax.ShapeDtypeStruct((DEPTH, n_rows, n_out), F32),
        compiler_params=_cparams(("arbitrary", "arbitrary")),
        name="adaln_mod",
    )(cond, w_ada, b_ada.reshape(DEPTH, 1, n_out))


def _rope_slot(x, cos, sa, sb):
    up = pltpu.roll(x, LANES - 8, 1)
    dn = pltpu.roll(x, 8, 1)
    return x * cos + up * sa + dn * sb


def _x_specs(nps, combined):
    off = 0 if combined else nps
    return [pl.BlockSpec((STEP, D_MODEL), lambda i, t: (jnp.minimum(i, nps - 1), 0)),
            pl.BlockSpec((STEP, D_MODEL), lambda i, t: (jnp.maximum(i, nps) - off, 0))]


def _head_slots(x):
    half = SLOT // 2
    low = _lane_iota((x.shape[0], SLOT)) < half
    slots = []
    for c in range(x.shape[1] // SLOT):
        pair = x[:, c * SLOT:(c + 1) * SLOT]
        slots.append(jnp.where(low, pair, 0.0))
        slots.append(jnp.where(low, pltpu.roll(pair, half, 1), 0.0))
    return slots


def _pre_kernel(tbl_ref, xa_ref, xb_ref, mod_ref, g_ref, w_ref, qg_ref, wuq_ref, kvg_ref,
                cd_ref, sad_ref, sbd_ref, cm_ref, sam_ref, sbm_ref,
                ra_ref, qa_ref, ka_ref, dkc_ref, dvc_ref, ckvc_ref, kpec_ref):
    is_context = pl.program_id(0) < tbl_ref[2, 0]
    mod = mod_ref[...]
    shift1 = mod[:, 0:D_MODEL]
    scale1 = mod[:, D_MODEL:2 * D_MODEL]
    for s in range(TS):
        rows = slice(s * TM, (s + 1) * TM)
        x = jnp.where(is_context, xa_ref[rows, :], xb_ref[rows, :])
        ms = jnp.mean(x * x, axis=-1, keepdims=True)
        h = x * lax.rsqrt(ms + NORM_EPS) * g_ref[...]
        h = h * (1.0 + scale1) + shift1
        hb = h.astype(BF16)

        def proj(lo, hi, hb=hb):
            return _dot(hb, w_ref[:, lo:hi])

        ra_ref[rows, C_RQ:C_RK] = proj(C_RQ, C_RK).astype(BF16)
        ra_ref[rows, C_RK:C_RV] = (proj(C_RK, C_RV) * (RET_DIM ** -0.5)).astype(BF16)
        ra_ref[rows, C_RV:C_RG] = proj(C_RV, C_RG).astype(BF16)
        ra_ref[rows, C_RG:C_DQ] = _silu(proj(C_RG, C_DQ)).astype(BF16)

        cd, sad, sbd = cd_ref[rows, :], sad_ref[rows, :], sbd_ref[rows, :]
        cm, sam, sbm = cm_ref[rows, :], sam_ref[rows, :], sbm_ref[rows, :]
        dq_slots = _head_slots(proj(C_DQ, C_DK))
        dk_slots = [_rope_slot(v, cd, sad, sbd) for v in _head_slots(proj(C_DK, C_DV))]
        for hd in range(DIFF_HEADS):
            qa_ref[rows, hd * SLOT:(hd + 1) * SLOT] = _rope_slot(dq_slots[hd], cd, sad, sbd).astype(BF16)
            ka_ref[rows, KA_DK + hd * SLOT:KA_DK + (hd + 1) * SLOT] = dk_slots[hd].astype(BF16)
        dv = proj(C_DV, C_CQ)
        ka_ref[rows, KA_DV:KA_CKV] = dv.astype(BF16)

        cq = proj(C_CQ, C_CKV)
        cqn = cq * lax.rsqrt(jnp.mean(cq * cq, axis=-1, keepdims=True) + NORM_EPS) * qg_ref[...]
        qm = _dot(cqn.astype(BF16), wuq_ref[...])
        for hd in range(MLA_HEADS):
            sl = slice(hd * SLOT, (hd + 1) * SLOT)
            qa_ref[rows, 4 * SLOT + hd * SLOT:4 * SLOT + (hd + 1) * SLOT] = _rope_slot(qm[:, sl], cm, sam, sbm).astype(BF16)

        ckv = proj(C_CKV, C_KPE)
        ckvn = ckv * lax.rsqrt(jnp.mean(ckv * ckv, axis=-1, keepdims=True) + NORM_EPS) * kvg_ref[...]
        kpe = proj(C_KPE, N_PRE)
        kpe_slot = _rope_slot(kpe + pltpu.roll(kpe, SLOT // 2, 1), cm, sam, sbm)
        ka_ref[rows, KA_CKV:KA_KPE] = ckvn.astype(BF16)
        ka_ref[rows, KA_KPE:KA_W] = kpe_slot.astype(BF16)

        @pl.when(is_context)
        def _(s=s, dk_slots=dk_slots, dv=dv, ckvn=ckvn, kpe_slot=kpe_slot):
            dv_slots = _head_slots(dv)
            for hd in range(DIFF_HEADS):
                dkc_ref[s, hd] = dk_slots[hd][:, 0:2 * DIFF_QK_DIM]
                dvc_ref[s, hd] = dv_slots[hd][:, 0:DIFF_V_DIM]
            ckvc_ref[s] = ckvn
            kpec_ref[s] = kpe_slot[:, 0:MLA_ROPE_DIM]


def _layer_spec(layer, rows, cols):
    return pl.BlockSpec((None, rows, cols), lambda *_: (layer, 0, 0))


def _mod_spec(layer):
    return pl.BlockSpec((None, None, 1, 6 * D_MODEL), lambda i, t: (layer, t[0, i], 0, 0))


def _pre_call(layer, tbl, xa, xb, npt, n_tok, mod4, g, w_pre, qg, wuq, kvg, rope_d, rope_m):
    ns = n_tok // STEP
    nps = npt // TS
    tile = lambda i, t: (i, 0)
    rope = lambda i, t: (t[1, i], 0)
    ctx5 = lambda i, t: (jnp.minimum(i, nps - 1), 0, 0, 0)
    ctx4 = lambda i, t: (jnp.minimum(i, nps - 1), 0, 0)
    cache_shapes = [
        jax.ShapeDtypeStruct((npt, DIFF_HEADS, TM, 2 * DIFF_QK_DIM), F32),
        jax.ShapeDtypeStruct((npt, DIFF_HEADS, TM, DIFF_V_DIM), F32),
        jax.ShapeDtypeStruct((npt, TM, MLA_KV_LORA), F32),
        jax.ShapeDtypeStruct((npt, TM, MLA_ROPE_DIM), F32),
    ]
    gs = pltpu.PrefetchScalarGridSpec(
        num_scalar_prefetch=1,
        grid=(ns,),
        in_specs=_x_specs(nps, xa is xb) + [
            _mod_spec(layer),
            _layer_spec(layer, 1, D_MODEL),
            _layer_spec(layer, D_MODEL, N_PRE),
            _layer_spec(layer, 1, MLA_Q_LORA),
            _layer_spec(layer, MLA_Q_LORA, MLA_HEADS * SLOT),
            _layer_spec(layer, 1, MLA_KV_LORA),
        ] + [pl.BlockSpec((STEP, SLOT), rope)] * 6,
        out_specs=[
            pl.BlockSpec((STEP, D_MODEL), tile),
            pl.BlockSpec((STEP, QA_W), tile),
            pl.BlockSpec((STEP, KA_W), tile),
            pl.BlockSpec((TS, DIFF_HEADS, TM, 2 * DIFF_QK_DIM), ctx5),
            pl.BlockSpec((TS, DIFF_HEADS, TM, DIFF_V_DIM), ctx5),
            pl.BlockSpec((TS, TM, MLA_KV_LORA), ctx4),
            pl.BlockSpec((TS, TM, MLA_ROPE_DIM), ctx4),
        ],
    )
    return pl.pallas_call(
        _pre_kernel,
        grid_spec=gs,
        out_shape=[
            jax.ShapeDtypeStruct((n_tok, D_MODEL), BF16),
            jax.ShapeDtypeStruct((n_tok, QA_W), BF16),
            jax.ShapeDtypeStruct((n_tok, KA_W), BF16),
        ] + cache_shapes,
        compiler_params=_cparams(("arbitrary",)),
        name="pre_proj",
    )(tbl, xa, xb, mod4, g, w_pre, qg, wuq, kvg, *rope_d, *rope_m)


def _lane_iota(shape):
    return lax.broadcasted_iota(I32, shape, len(shape) - 1)


def _head_mask(n_rows, width, head, head_w):
    lane = _lane_iota((n_rows, width))
    return (lane >= head * head_w) & (lane < (head + 1) * head_w)


def _seg_mean_sq(o, bd_ones):
    return _split_dot(o * o, bd_ones) * (1.0 / RET_DIM)


def _block_diag_ones(n, blk):
    r = lax.broadcasted_iota(I32, (n, n), 0) // blk
    c = lax.broadcasted_iota(I32, (n, n), 1) // blk
    return r == c


def _retention(ra_ref, seq_len, decf_ref, decb_ref, s0f, s0b):
    C = RET_CHUNK
    nc = seq_len // C
    lgf = -jnp.exp(decf_ref[...])
    lgb = -jnp.exp(decb_ref[...])
    pos = lax.broadcasted_iota(I32, (C, RET_W), 0).astype(F32)
    qdf = jnp.exp((pos + 1.0) * lgf)
    kdf = jnp.exp((C - 1.0 - pos) * lgf)
    cdf = jnp.exp(float(C) * lgf)
    qdb = jnp.exp((C - pos) * lgb)
    kdb = jnp.exp(pos * lgb)
    cdb = jnp.exp(float(C) * lgb)
    ii = lax.broadcasted_iota(I32, (C, C), 0).astype(F32)
    jj = lax.broadcasted_iota(I32, (C, C), 1).astype(F32)
    dist = ii - jj
    dmats = []
    for hd in range(RET_HEADS):
        lf = lgf[:, hd * RET_DIM:hd * RET_DIM + 1]
        lb = lgb[:, hd * RET_DIM:hd * RET_DIM + 1]
        dmats.append(jnp.where(dist >= 0, jnp.exp(dist * lf), jnp.exp(-dist * lb)))
    bd = _block_diag_ones(RET_W, RET_DIM)
    bd_ones = jnp.where(bd, 1.0, 0.0).astype(BF16)

    def chunk(n):
        rows = slice(n * C, (n + 1) * C)
        return (ra_ref[rows, C_RQ:C_RK], ra_ref[rows, C_RK:C_RV], ra_ref[rows, C_RV:C_RG])

    cross = [None] * nc
    sf = s0f
    for n in range(nc):
        q, k, v = chunk(n)
        cross[n] = _dot((q * qdf).astype(BF16), sf.astype(BF16))
        kv = _dot_tn((k * kdf).astype(BF16), v.astype(BF16))
        sf = sf * cdf + jnp.where(bd, kv, 0.0)
    sb = s0b
    for n in range(nc - 1, -1, -1):
        q, k, v = chunk(n)
        cross[n] = cross[n] + _dot((q * qdb).astype(BF16), sb.astype(BF16))
        kv = _dot_tn((k * kdb).astype(BF16), v.astype(BF16))
        sb = sb * cdb + jnp.where(bd, kv, 0.0)

    outs = []
    for n in range(nc):
        q, k, v = chunk(n)
        kb = k.astype(BF16)
        vb = v.astype(BF16)
        o = cross[n]
        for hd in range(RET_HEADS):
            hm = _head_mask(C, RET_W, hd, RET_DIM)
            sc = _dot_nt(jnp.where(hm, q, 0.0).astype(BF16), kb) * dmats[hd]
            o = o + jnp.where(hm, _dot(sc.astype(BF16), vb), 0.0)
        on = o * lax.rsqrt(_seg_mean_sq(o, bd_ones) + NORM_EPS)
        outs.append(on * ra_ref[n * C:(n + 1) * C, C_RG:C_DQ])
    return outs, sf, sb


def _softmax_pv(s_parts, v_parts, scale):
    scale = scale * math.log2(math.e)
    m = None
    for s in s_parts:
        mm = jnp.max(s, axis=-1, keepdims=True)
        m = mm if m is None else jnp.maximum(m, mm)
    m = m * scale
    acc = None
    den = None
    for s, v in zip(s_parts, v_parts):
        e = jnp.exp2(s * scale - m)
        ds = jnp.sum(e, axis=-1, keepdims=True)
        pv = _dot(e.astype(BF16), v)
        acc = pv if acc is None else acc + pv
        den = ds if den is None else den + ds
    return acc / den


def _diff_attention(dq, k_parts, v_parts, lam, subln, lam_init, bd_ones):
    lq = dq.shape[0]
    scale = DIFF_QK_DIM ** -0.5
    lane = _lane_iota((lq, SLOT))
    out = jnp.zeros((lq, DIFF_V_W), F32)
    for hd in range(DIFF_HEADS):
        qh = dq[:, hd * SLOT:(hd + 1) * SLOT]
        q1 = jnp.where(lane < DIFF_QK_DIM, qh, 0.0).astype(BF16)
        q2 = jnp.where(lane >= DIFF_QK_DIM, qh, 0.0).astype(BF16)
        s1 = [_dot_nt(q1[:, :kp[hd].shape[1]], kp[hd]) for kp in k_parts]
        s2 = [_dot_nt(q2[:, :kp[hd].shape[1]], kp[hd]) for kp in k_parts]
        o = _softmax_pv(s1, v_parts, scale) - lam * _softmax_pv(s2, v_parts, scale)
        out = jnp.where(_head_mask(lq, DIFF_V_W, hd, DIFF_V_DIM), o, out)
    on = out * lax.rsqrt(_seg_mean_sq(out, bd_ones) + NORM_EPS) * subln
    return on * (1.0 - lam_init)


def _mla_attention(qm, k_parts, v_parts):
    lq = qm.shape[0]
    scale = (MLA_NOPE_DIM + MLA_ROPE_DIM) ** -0.5
    halves = []
    for g in range(2):
        out = jnp.zeros((lq, 256), F32)
        for hh in range(4):
            hd = 4 * g + hh
            qh = qm[:, hd * SLOT:(hd + 1) * SLOT].astype(BF16)
            s = [_dot_nt(qh, kp[:, hd * SLOT:(hd + 1) * SLOT]) for kp in k_parts]
            o = _softmax_pv(s, [vp[:, 256 * g:256 * (g + 1)] for vp in v_parts], scale)
            out = jnp.where(_head_mask(lq, 256, hh, MLA_V_DIM), o, out)
        halves.append(out)
    return halves


def _mla_keys(ka_val_ckv, kr_slot, wk_ref, wv_ref):
    cb = ka_val_ckv.astype(BF16)
    kn = _dot(cb, wk_ref[...])
    ks = [(kn[:, hd * SLOT:(hd + 1) * SLOT] + kr_slot).astype(BF16) for hd in range(MLA_HEADS)]
    return jnp.concatenate(ks, axis=1), _dot(cb, wv_ref[...]).astype(BF16)


def _kr_only(kpe_slot):
    lane = _lane_iota(kpe_slot.shape)
    return jnp.where((lane >= KR_LO) & (lane < KR_HI), kpe_slot, 0.0)


def _diff_lambda(dl_ref, lam_init):
    dl = dl_ref[...]
    a = jnp.sum(dl[0:1] * dl[1:2], axis=-1, keepdims=True)
    b = jnp.sum(dl[2:3] * dl[3:4], axis=-1, keepdims=True)
    return jnp.exp(a) - jnp.exp(b) + lam_init


def _mix_prompt_kernel(lam_init, qa_ref, ka_ref, ra_ref, decf_ref, decb_ref, dl_ref, subln_ref,
                       wk_ref, wv_ref, mix_ref, sf_ref, sb_ref):
    seq = qa_ref.shape[0]
    zero_state = jnp.zeros((RET_W, RET_W), F32)
    outs, sf, sb = _retention(ra_ref, seq, decf_ref, decb_ref, zero_state, zero_state)
    for n, o in enumerate(outs):
        mix_ref[n * RET_CHUNK:(n + 1) * RET_CHUNK, 0:RET_W] = o.astype(BF16)
    for hd in range(RET_HEADS):
        blk = slice(hd * RET_DIM, (hd + 1) * RET_DIM)
        sf_ref[hd] = sf[blk, blk]
        sb_ref[hd] = sb[blk, blk]

    bd_ones = jnp.where(_block_diag_ones(DIFF_V_W, DIFF_V_DIM), 1.0, 0.0).astype(BF16)
    lam = _diff_lambda(dl_ref, lam_init)
    kd = [ka_ref[:, KA_DK + hd * SLOT:KA_DK + (hd + 1) * SLOT].astype(BF16) for hd in range(DIFF_HEADS)]
    vd = ka_ref[:, KA_DV:KA_CKV].astype(BF16)
    mix_ref[:, RET_W:RET_W + DIFF_V_W] = _diff_attention(
        qa_ref[:, 0:4 * SLOT], [kd], [vd], lam, subln_ref[...], lam_init, bd_ones).astype(BF16)

    km, vm = _mla_keys(ka_ref[:, KA_CKV:KA_KPE], _kr_only(ka_ref[:, KA_KPE:KA_W]), wk_ref, wv_ref)
    halves = _mla_attention(qa_ref[:, 4 * SLOT:QA_W], [km], [vm])
    mix_ref[:, 512:768] = halves[0].astype(BF16)
    mix_ref[:, 768:1024] = halves[1].astype(BF16)


def _mixer_param_specs(layer):
    return [
        _layer_spec(layer, 1, RET_W),
        _layer_spec(layer, 1, RET_W),
        _layer_spec(layer, 4, DIFF_QK_DIM),
        _layer_spec(layer, 1, DIFF_V_W),
        _layer_spec(layer, MLA_KV_LORA, MLA_HEADS * SLOT),
        _layer_spec(layer, MLA_KV_LORA, MLA_V_W),
    ]


def _mix_prompt_call(lam_init, layer, qa, ka, ra, n_seq, seq_len, decf, decb, dl, subln, wk, wv):
    seq = lambda b: (b, 0)
    state_spec = pl.BlockSpec((None, RET_HEADS, RET_DIM, RET_DIM), lambda b: (b, 0, 0, 0))
    state_shape = jax.ShapeDtypeStruct((n_seq, RET_HEADS, RET_DIM, RET_DIM), F32)
    return pl.pallas_call(
        functools.partial(_mix_prompt_kernel, lam_init),
        grid=(n_seq,),
        in_specs=[
            pl.BlockSpec((seq_len, QA_W), seq),
            pl.BlockSpec((seq_len, KA_W), seq),
            pl.BlockSpec((seq_len, D_MODEL), seq),
        ] + _mixer_param_specs(layer),
        out_specs=[pl.BlockSpec((seq_len, MIX_W), seq), state_spec, state_spec],
        out_shape=[jax.ShapeDtypeStruct((n_seq * seq_len, MIX_W), BF16), state_shape, state_shape],
        compiler_params=_cparams(("arbitrary",)),
        name="mix_prompt",
    )(qa, ka, ra, decf, decb, dl, subln, wk, wv)


def _mix_sample_kernel(lam_init, qa_ref, ka_ref, ra_ref, ckd_ref, cvd_ref, cckv_ref, ckpe_ref,
                       s0f_ref, s0b_ref, decf_ref, decb_ref, dl_ref, subln_ref, wk_ref, wv_ref,
                       place_ref, mix_ref,
                       ret_s, kdn_s, vdn_s, kdc_s, vdc_s, kmn_s, vmn_s, kmc_s, vmc_s):
    j = pl.program_id(1)
    seq = ka_ref.shape[0]

    @pl.when(j == 0)
    def _():
        outs, _, _ = _retention(ra_ref, seq, decf_ref, decb_ref, s0f_ref[...], s0b_ref[...])
        for n, o in enumerate(outs):
            ret_s[n * RET_CHUNK:(n + 1) * RET_CHUNK, :] = o
        kdn_s[...] = ka_ref[:, KA_DK:KA_DV].astype(BF16)
        vdn_s[...] = ka_ref[:, KA_DV:KA_CKV].astype(BF16)
        kdc_s[...] = ckd_ref[...].astype(BF16)
        vdc_s[...] = cvd_ref[...].astype(BF16)
        km, vm = _mla_keys(ka_ref[:, KA_CKV:KA_KPE], _kr_only(ka_ref[:, KA_KPE:KA_W]), wk_ref, wv_ref)
        kmn_s[...] = km
        vmn_s[...] = vm
        kr_ctx = _dot(ckpe_ref[...].astype(BF16), place_ref[...])
        km, vm = _mla_keys(cckv_ref[...], kr_ctx, wk_ref, wv_ref)
        kmc_s[...] = km
        vmc_s[...] = vm

    row0 = pl.multiple_of(j * TM, TM)
    mix_ref[:, 0:RET_W] = ret_s[pl.ds(row0, TM), :].astype(BF16)

    bd_ones = jnp.where(_block_diag_ones(DIFF_V_W, DIFF_V_DIM), 1.0, 0.0).astype(BF16)
    lam = _diff_lambda(dl_ref, lam_init)
    kd_ctx = [kdc_s[hd] for hd in range(DIFF_HEADS)]
    kd_new = [kdn_s[:, hd * SLOT:(hd + 1) * SLOT] for hd in range(DIFF_HEADS)]
    mix_ref[:, RET_W:RET_W + DIFF_V_W] = _diff_attention(
        qa_ref[:, 0:4 * SLOT], [kd_ctx, kd_new], [vdc_s[...], vdn_s[...]], lam, subln_ref[...],
        lam_init, bd_ones).astype(BF16)

    halves = _mla_attention(qa_ref[:, 4 * SLOT:QA_W], [kmc_s[...], kmn_s[...]], [vmc_s[...], vmn_s[...]])
    mix_ref[:, 512:768] = halves[0].astype(BF16)
    mix_ref[:, 768:1024] = halves[1].astype(BF16)


def _mix_sample_call(lam_init, layer, qa, ka, ra, tok0, n_seq, seq_len, past_len, cache_dk, cache_dv_t,
                     cache_ckv, cache_kpe, s0f_bd, s0b_bd, decf, decb, dl, subln, wk, wv, place):
    nq = seq_len // TM
    q0 = tok0 // TM
    s0 = tok0 // seq_len
    const = lambda b, j: (0, 0)
    return pl.pallas_call(
        functools.partial(_mix_sample_kernel, lam_init),
        grid=(n_seq, nq),
        in_specs=[
            pl.BlockSpec((TM, QA_W), lambda b, j: (q0 + b * nq + j, 0)),
            pl.BlockSpec((seq_len, KA_W), lambda b, j: (s0 + b, 0)),
            pl.BlockSpec((seq_len, D_MODEL), lambda b, j: (s0 + b, 0)),
            pl.BlockSpec((None, None, DIFF_HEADS, past_len, 2 * DIFF_QK_DIM), lambda b, j: (b, layer, 0, 0, 0)),
            pl.BlockSpec((None, None, past_len, DIFF_V_W), lambda b, j: (b, layer, 0, 0)),
            pl.BlockSpec((None, None, past_len, MLA_KV_LORA), lambda b, j: (b, layer, 0, 0)),
            pl.BlockSpec((None, None, past_len, MLA_ROPE_DIM), lambda b, j: (b, layer, 0, 0)),
            pl.BlockSpec((None, None, RET_W, RET_W), lambda b, j: (b, layer, 0, 0)),
            pl.BlockSpec((None, None, RET_W, RET_W), lambda b, j: (b, layer, 0, 0)),
        ] + _mixer_param_specs(layer) + [
            pl.BlockSpec((MLA_ROPE_DIM, SLOT), const),
        ],
        out_specs=pl.BlockSpec((TM, MIX_W), lambda b, j: (b * nq + j, 0)),
        out_shape=jax.ShapeDtypeStruct((n_seq * seq_len, MIX_W), BF16),
        scratch_shapes=[
            pltpu.VMEM((seq_len, RET_W), F32),
            pltpu.VMEM((seq_len, 4 * SLOT), BF16),
            pltpu.VMEM((seq_len, DIFF_V_W), BF16),
            pltpu.VMEM((DIFF_HEADS, past_len, 2 * DIFF_QK_DIM), BF16),
            pltpu.VMEM((past_len, DIFF_V_W), BF16),
            pltpu.VMEM((seq_len, MLA_HEADS * SLOT), BF16),
            pltpu.VMEM((seq_len, MLA_V_W), BF16),
            pltpu.VMEM((past_len, MLA_HEADS * SLOT), BF16),
            pltpu.VMEM((past_len, MLA_V_W), BF16),
        ],
        compiler_params=_cparams(("arbitrary", "arbitrary")),
        name="mix_sample",
    )(qa, ka, ra, cache_dk, cache_dv_t, cache_ckv, cache_kpe, s0f_bd, s0b_bd,
      decf, decb, dl, subln, wk, wv, place)


def _route(h2, rwt_ref, rb_ref):
    tm = h2.shape[0]
    neg = -jnp.inf
    logits = _split_dot_nt(rwt_ref[...], h2)
    sc = jax.nn.sigmoid(logits)
    sel = sc + rb_ref[...]
    member = lax.broadcasted_iota(I32, (GROUP_SIZE, tm), 0).astype(F32)
    gscore = []
    for g in range(N_GROUPS):
        sg = sel[g * GROUP_SIZE:(g + 1) * GROUP_SIZE, :]
        m1 = jnp.max(sg, axis=0, keepdims=True)
        f1 = jnp.min(jnp.where(sg == m1, member, float(GROUP_SIZE)), axis=0, keepdims=True)
        m2 = jnp.max(jnp.where(member == f1, neg, sg), axis=0, keepdims=True)
        gscore.append(m1 + m2)
    gsel = [jnp.zeros((1, tm), F32) for _ in range(N_GROUPS)]
    for _ in range(TOPK_GROUPS):
        mx = gscore[0]
        for g in range(1, N_GROUPS):
            mx = jnp.maximum(mx, gscore[g])
        fi = jnp.full((1, tm), float(N_GROUPS), F32)
        for g in range(N_GROUPS - 1, -1, -1):
            fi = jnp.where(gscore[g] == mx, float(g), fi)
        for g in range(N_GROUPS):
            hit = fi == float(g)
            gsel[g] = jnp.where(hit, 1.0, gsel[g])
            gscore[g] = jnp.where(hit, neg, gscore[g])
    cand = jnp.concatenate(
        [jnp.where(gsel[g] > 0.0, sel[g * GROUP_SIZE:(g + 1) * GROUP_SIZE, :], neg) for g in range(N_GROUPS)],
        axis=0)
    flat = lax.broadcasted_iota(I32, (N_EXPERTS, tm), 0).astype(F32)
    hits, gts = [], []
    chosen = jnp.zeros((N_EXPERTS, tm), F32)
    for _ in range(TOP_K):
        mx = jnp.max(cand, axis=0, keepdims=True)
        fk = jnp.min(jnp.where(cand == mx, flat, float(N_EXPERTS)), axis=0, keepdims=True)
        hit = flat == fk
        hits.append(hit)
        gts.append(jnp.sum(jnp.where(hit, sc, 0.0), axis=0, keepdims=True))
        chosen = jnp.where(hit, 1.0, chosen)
        cand = jnp.where(hit, neg, cand)
    gsum = gts[0]
    for g in gts[1:]:
        gsum = gsum + g
    gts = [g / gsum * ROUTED_SCALE for g in gts]

    before = (lax.broadcasted_iota(I32, (tm, tm), 0) < lax.broadcasted_iota(I32, (tm, tm), 1))
    rank_in = _dot(chosen.astype(BF16), jnp.where(before, 1.0, 0.0).astype(BF16))
    cnt = jnp.sum(chosen, axis=1, keepdims=True)
    cnt_pad = jnp.floor((cnt + (CH - 1.0)) * (1.0 / CH)) * CH
    below = (lax.broadcasted_iota(I32, (N_EXPERTS, N_EXPERTS), 1) < lax.broadcasted_iota(I32, (N_EXPERTS, N_EXPERTS), 0))
    start = _dot(jnp.where(below, 1.0, 0.0).astype(BF16),
                 jnp.broadcast_to(cnt_pad, (N_EXPERTS, LANES)).astype(BF16))[:, 0:1]
    pos = rank_in + start
    lpos = [jnp.sum(jnp.where(hit, pos, 0.0), axis=0, keepdims=True) for hit in hits]
    return lpos, gts, cnt_pad, start


def _slot_rows(vals, n_rows):
    tm = vals[0].shape[1]
    row = lax.broadcasted_iota(I32, (n_rows, tm), 0)
    out = jnp.zeros((n_rows, tm), F32)
    for k, v in enumerate(vals):
        out = jnp.where(row == k, v, out)
    return out


def _post_kernel(tbl_ref, xa_ref, xb_ref, mp_ref, ms_ref, mod_ref, wout_ref, g2_ref, shg_ref, shu_ref, shd_ref,
                 rwt_ref, rb_ref, base_ref, h2_ref, lpos_ref, ptok_ref, gtok_ref, cnt_ref, start_ref, rel_ref,
                 run_ref):
    i = pl.program_id(0)
    is_context = i < tbl_ref[2, 0]

    @pl.when(i == 0)
    def _():
        cnt_ref[...] = jnp.zeros_like(cnt_ref)
        start_ref[...] = jnp.zeros_like(start_ref)
        rel_ref[...] = jnp.zeros_like(rel_ref)
        run_ref[...] = jnp.zeros_like(run_ref)

    mod = mod_ref[...]
    gate1 = mod[:, 2 * D_MODEL:3 * D_MODEL]
    shift2 = mod[:, 3 * D_MODEL:4 * D_MODEL]
    scale2 = mod[:, 4 * D_MODEL:5 * D_MODEL]
    gate2 = mod[:, 5 * D_MODEL:6 * D_MODEL]
    for s in range(TS):
        rows = slice(s * TM, (s + 1) * TM)
        mix = jnp.where(is_context, mp_ref[rows, :], ms_ref[rows, :])
        x = jnp.where(is_context, xa_ref[rows, :], xb_ref[rows, :])
        x1 = x + gate1 * _dot(mix, wout_ref[...])
        ms = jnp.mean(x1 * x1, axis=-1, keepdims=True)
        h2 = x1 * lax.rsqrt(ms + NORM_EPS) * g2_ref[...]
        h2 = h2 * (1.0 + scale2) + shift2
        hb = h2.astype(BF16)
        h2_ref[rows, :] = hb
        act = _silu(_dot(hb, shg_ref[...])) * _dot(hb, shu_ref[...])
        base_ref[rows, :] = x1 + gate2 * _dot(act.astype(BF16), shd_ref[...])

        lpos, gts, cnt_pad, start = _route(h2, rwt_ref, rb_ref)
        lpos_ref[:, rows] = _slot_rows(lpos, SLOT_ROWS)
        ptok_ref[rows, :] = _slot_rows(lpos, LANES).T
        gtok_ref[rows, :] = _slot_rows(gts, LANES).T
        tile_col = lax.broadcasted_iota(I32, cnt_ref.shape, 1) == i * TS + s
        run = run_ref[...]
        cnt_ref[...] = jnp.where(tile_col, cnt_pad.astype(I32), cnt_ref[...])
        start_ref[...] = jnp.where(tile_col, start.astype(I32), start_ref[...])
        rel_ref[...] = jnp.where(tile_col, run.astype(I32), rel_ref[...])
        run_ref[...] = run + cnt_pad


def _post_call(layer, tbl, xa, xb, mix_p, mix_s, mod4, wout, g2, shg, shu, shd, rwt, rb):
    nps = mix_p.shape[0] // STEP
    n_tok = mix_p.shape[0] + mix_s.shape[0]
    nt = n_tok // TM
    ns = n_tok // STEP
    const = lambda i, t: (0, 0)
    tile = lambda i, t: (i, 0)
    gs = pltpu.PrefetchScalarGridSpec(
        num_scalar_prefetch=1,
        grid=(ns,),
        in_specs=_x_specs(nps, xa is xb) + [
            pl.BlockSpec((STEP, MIX_W), lambda i, t: (jnp.minimum(i, nps - 1), 0)),
            pl.BlockSpec((STEP, MIX_W), lambda i, t: (jnp.maximum(i - nps, 0), 0)),
            _mod_spec(layer),
            _layer_spec(layer, MIX_W, D_MODEL),
            _layer_spec(layer, 1, D_MODEL),
            _layer_spec(layer, D_MODEL, EXPERT_FF),
            _layer_spec(layer, D_MODEL, EXPERT_FF),
            _layer_spec(layer, EXPERT_FF, D_MODEL),
            _layer_spec(layer, N_EXPERTS, D_MODEL),
            _layer_spec(layer, N_EXPERTS, 1),
        ],
        out_specs=[
            pl.BlockSpec((STEP, D_MODEL), tile),
            pl.BlockSpec((STEP, D_MODEL), tile),
            pl.BlockSpec((SLOT_ROWS, STEP), lambda i, t: (0, i)),
            pl.BlockSpec((STEP, LANES), tile),
            pl.BlockSpec((STEP, LANES), tile),
            pl.BlockSpec((N_EXPERTS, LANES), const),
            pl.BlockSpec((N_EXPERTS, LANES), const),
            pl.BlockSpec((N_EXPERTS, LANES), const),
        ],
        scratch_shapes=[pltpu.VMEM((N_EXPERTS, 1), F32)],
    )
    assert nt <= LANES
    return pl.pallas_call(
        _post_kernel,
        grid_spec=gs,
        out_shape=[
            jax.ShapeDtypeStruct((n_tok, D_MODEL), F32),
            jax.ShapeDtypeStruct((n_tok, D_MODEL), BF16),
            jax.ShapeDtypeStruct((SLOT_ROWS, n_tok), F32),
            jax.ShapeDtypeStruct((n_tok, LANES), F32),
            jax.ShapeDtypeStruct((n_tok, LANES), F32),
            jax.ShapeDtypeStruct((N_EXPERTS, LANES), I32),
            jax.ShapeDtypeStruct((N_EXPERTS, LANES), I32),
            jax.ShapeDtypeStruct((N_EXPERTS, LANES), I32),
        ],
        compiler_params=_cparams(("arbitrary",)),
        name="post_route",
    )(tbl, xa, xb, mix_p, mix_s, mod4, wout, g2, shg, shu, shd, rwt, rb)


SEG_ROW0, SEG_NBLK, SEG_PAD0, SEG_NPAD, SEG_NEXT, SEG_USED = range(6)


def _plan_kernel(last_tile, cnt_ref, rel_ref, seg_ref):
    def per_expert(e, start):
        end = start + rel_ref[e, last_tile] + cnt_ref[e, last_tile]
        nb = lax.shift_right_logical(end - start + (EB - 1), LOG_EB)
        nxt = start + lax.shift_left(nb, LOG_EB)
        seg_ref[SEG_ROW0, e] = start
        seg_ref[SEG_NBLK, e] = nb
        seg_ref[SEG_PAD0, e] = end
        seg_ref[SEG_NPAD, e] = lax.shift_right_logical(nxt - end, LOG_CH)
        seg_ref[SEG_USED, e] = 0
        return nxt

    total = lax.fori_loop(0, N_EXPERTS, per_expert, jnp.int32(0))

    def link(k, nxt):
        e = N_EXPERTS - 1 - k
        seg_ref[SEG_NEXT, e] = nxt
        return jnp.where(seg_ref[SEG_NBLK, e] > 0, e, nxt)

    first = lax.fori_loop(0, N_EXPERTS, link, jnp.int32(N_EXPERTS))
    seg_ref[SEG_USED, 0] = lax.shift_right_logical(total, LOG_EB)
    seg_ref[SEG_USED, 1] = first


def _plan_call(cnt, rel, nt):
    smem = pl.BlockSpec(memory_space=pltpu.SMEM)
    return pl.pallas_call(
        functools.partial(_plan_kernel, nt - 1),
        in_specs=[smem, smem],
        out_specs=smem,
        out_shape=jax.ShapeDtypeStruct((6, N_EXPERTS), I32),
        name="moe_plan",
    )(cnt, rel)


def _rows_copy(src_ref, src_row, dst_ref, dst_row, n_rows, sem):
    return pltpu.make_async_copy(src_ref.at[pl.ds(pl.multiple_of(src_row, CH), n_rows)],
                                 dst_ref.at[pl.ds(pl.multiple_of(dst_row, CH), n_rows)], sem)


class _Runs:
    def __init__(self, cnt_ref, start_ref, rel_ref, seg_ref):
        self.cnt, self.start, self.rel, self.seg = cnt_ref, start_ref, rel_ref, seg_ref

    def n_rows(self, i):
        return self.start[N_EXPERTS - 1, i] + self.cnt[N_EXPERTS - 1, i]

    def start_copies(self, i, copy, tot_ref, slot):
        def per_pair(e2, carry):
            n_big, n_small = carry
            for par in range(2):
                e = 2 * e2 + par
                c = self.cnt[e, i]
                a0 = self.start[e, i]
                b0 = self.seg[SEG_ROW0, e] + self.rel[e, i]
                nb = lax.shift_right_logical(c, LOG_CH + 1)
                odd = jnp.bitwise_and(lax.shift_right_logical(c, LOG_CH), 1)

                def big(q, cc, a0=a0, b0=b0, par=par):
                    copy(a0 + q * (2 * CH), b0 + q * (2 * CH), 2 * CH).start(priority=par)
                    return cc

                lax.fori_loop(0, nb, big, 0)

                @pl.when(odd == 1)
                def _(a0=a0, b0=b0, nb=nb, par=par):
                    copy(a0 + nb * (2 * CH), b0 + nb * (2 * CH), CH).start(priority=par)

                n_big, n_small = n_big + nb, n_small + odd
            return n_big, n_small

        n_big, n_small = lax.fori_loop(0, N_EXPERTS // 2, per_pair, (jnp.int32(0), jnp.int32(0)))
        tot_ref[slot, 0] = n_big
        tot_ref[slot, 1] = n_small

    @staticmethod
    def wait_copies(copy, tot_ref, slot):
        def big(q, c):
            copy(0, 0, 2 * CH).wait()
            return c

        lax.fori_loop(0, tot_ref[slot, 0], big, 0)

        def small(q, c):
            copy(0, 0, CH).wait()
            return c

        lax.fori_loop(0, tot_ref[slot, 1], small, 0)


SORT_TIERS = (2048, 2304, SORT_ROWS)


def _for_sorted_rows(n_sorted, body):
    lo = -1
    for hi in SORT_TIERS:
        @pl.when(jnp.logical_and(n_sorted > lo, n_sorted <= hi))
        def _(hi=hi):
            body(hi)
        lo = hi


def _dispatch_kernel(reuse, cnt_ref, start_ref, rel_ref, seg_ref, h_ref, lpos_ref, *rest):
    xb_hbm, sort_s, zero_s, tot_s, sems = rest[1:] if reuse else rest
    i = pl.program_id(0)
    last = pl.num_programs(0) - 1
    slot = lax.rem(i, 2)
    runs = _Runs(cnt_ref, start_ref, rel_ref, seg_ref)
    def sort_rows(n_rows):
        lp = lpos_ref[...]
        hb = h_ref[...]
        blk = TM
        for r in range(n_rows // blk):
            srow = (lax.broadcasted_iota(I32, (blk, TM), 0) + r * blk).astype(F32)
            p = jnp.zeros((blk, TM), F32)
            for k in range(TOP_K):
                p = jnp.where(srow == lp[k:k + 1, :], 1.0, p)
            sort_s[slot, r * blk:(r + 1) * blk, :] = _dot(p.astype(BF16), hb).astype(BF16)

    _for_sorted_rows(runs.n_rows(i), sort_rows)

    def copy_from(sl):
        return lambda s, d, n: _rows_copy(sort_s.at[sl], s, xb_hbm, d, n, sems.at[sl])

    runs.start_copies(i, copy_from(slot), tot_s, slot)

    @pl.when(i > 0)
    def _():
        runs.wait_copies(copy_from(1 - slot), tot_s, 1 - slot)

    @pl.when(i == last)
    def _():
        runs.wait_copies(copy_from(slot), tot_s, slot)
        _zero_fill_unused(seg_ref, xb_hbm, zero_s, sems.at[0], tail=not reuse)


def _zero_fill_unused(seg_ref, buf_hbm, zero_s, sem, tail):
    zero_s[...] = jnp.zeros_like(zero_s)

    def per_expert(e, c):
        first = seg_ref[SEG_PAD0, e]

        def z_issue(r, cc):
            _rows_copy(zero_s, 0, buf_hbm, first + r * CH, CH, sem).start()
            return cc

        lax.fori_loop(0, seg_ref[SEG_NPAD, e], z_issue, 0)

        def z_drain(r, cc):
            _rows_copy(zero_s, 0, buf_hbm, 0, CH, sem).wait()
            return cc

        lax.fori_loop(0, seg_ref[SEG_NPAD, e], z_drain, 0)
        return c

    lax.fori_loop(0, N_EXPERTS, per_expert, 0)
    if tail:
        _zero_fill_tail(seg_ref, buf_hbm, zero_s, sem)


def _zero_fill_tail(seg_ref, buf_hbm, zero_s, sem):
    n_blocks = buf_hbm.shape[0] // EB

    def blk_copy(b):
        return pltpu.make_async_copy(zero_s, buf_hbm.at[pl.ds(pl.multiple_of(b * EB, EB), EB)], sem)

    def t_issue(b, cc):
        blk_copy(b).start()
        return cc

    lax.fori_loop(seg_ref[SEG_USED, 0], n_blocks, t_issue, 0)

    def t_drain(b, cc):
        blk_copy(0).wait()
        return cc

    lax.fori_loop(seg_ref[SEG_USED, 0], n_blocks, t_drain, 0)


def _dispatch_call(cnt, start, rel, seg, h2, lpos, n_rows, prev=None):
    n_tok = h2.shape[0]
    nt = n_tok // TM
    smem = pl.BlockSpec(memory_space=pltpu.SMEM)
    reuse = prev is not None
    return pl.pallas_call(
        functools.partial(_dispatch_kernel, reuse),
        grid=(nt,),
        in_specs=[
            smem, smem, smem, smem,
            pl.BlockSpec((TM, D_MODEL), lambda i: (i, 0)),
            pl.BlockSpec((SLOT_ROWS, TM), lambda i: (0, i)),
        ] + ([pl.BlockSpec(memory_space=pl.ANY)] if reuse else []),
        input_output_aliases={6: 0} if reuse else {},
        out_specs=pl.BlockSpec(memory_space=pl.ANY),
        out_shape=jax.ShapeDtypeStruct((n_rows, D_MODEL), BF16),
        scratch_shapes=[
            pltpu.VMEM((2, SORT_ROWS, D_MODEL), BF16),
            pltpu.VMEM((EB, D_MODEL), BF16),
            pltpu.SMEM((2, 2), I32),
            pltpu.SemaphoreType.DMA((2,)),
        ],
        compiler_params=_cparams(("arbitrary",)),
        name="moe_dispatch",
    )(cnt, start, rel, seg, h2, lpos, *([prev] if reuse else []))


X_SLOTS = 4


def _experts_kernel(layer, seg_ref, wg_hbm, wu_hbm, wd_hbm, xb_hbm, yb_hbm,
                    wgf_s, wuf_s, wdf_s, wg_s, wu_s, wd_s, x_s, y_s, sem_w, sem_x, sem_y):
    n_used = seg_ref[SEG_USED, 0]
    first = seg_ref[SEG_USED, 1]

    def rows(g):
        return pl.ds(pl.multiple_of(g * EB, EB), EB)

    def x_copy(g, slot):
        return pltpu.make_async_copy(xb_hbm.at[rows(g)], x_s.at[slot], sem_x.at[slot])

    def y_copy(g, slot):
        return pltpu.make_async_copy(y_s.at[slot], yb_hbm.at[rows(g)], sem_y.at[slot])

    def w_copies(e, slot):
        return [pltpu.make_async_copy(hbm.at[layer, e], buf.at[slot], sem_w.at[slot, n])
                for n, (hbm, buf) in enumerate(((wg_hbm, wgf_s), (wu_hbm, wuf_s), (wd_hbm, wdf_s)))]

    def fetch_weights(e, slot):
        for c in w_copies(e, slot):
            c.start()

    def take_weights(slot):
        for c in w_copies(0, slot):
            c.wait()
        wg_s[...] = wgf_s[slot].astype(BF16)
        wu_s[...] = wuf_s[slot].astype(BF16)
        wd_s[...] = wdf_s[slot].astype(BF16)

    def next_expert(e):
        return seg_ref[SEG_NEXT, jnp.minimum(e, N_EXPERTS - 1)]

    @pl.when(n_used > 0)
    def _():
        for p in range(X_SLOTS - 1):
            @pl.when(p < n_used)
            def _(p=p):
                x_copy(p, p).start()

        fetch_weights(first, 0)

        @pl.when(next_expert(first) < N_EXPERTS)
        def _():
            fetch_weights(next_expert(first), 1)

        take_weights(0)

        def block(g, carry):
            e, left, wslot = carry
            ahead = g + (X_SLOTS - 1)

            @pl.when(ahead < n_used)
            def _():
                x_copy(ahead, lax.rem(ahead, X_SLOTS)).start()

            xslot = lax.rem(g, X_SLOTS)
            yslot = lax.rem(g, 2)
            x_copy(g, xslot).wait()

            @pl.when(g >= 2)
            def _():
                y_copy(g - 2, yslot).wait()

            xb = x_s[xslot]
            act = _silu(_dot(xb, wg_s[...])) * _dot(xb, wu_s[...])
            y_s[yslot] = _dot(act.astype(BF16), wd_s[...]).astype(BF16)
            y_copy(g, yslot).start()

            switch = jnp.logical_and(left == 1, g + 1 < n_used)
            nxt = next_expert(e)

            @pl.when(switch)
            def _():
                take_weights(1 - wslot)

                @pl.when(next_expert(nxt) < N_EXPERTS)
                def _():
                    fetch_weights(next_expert(nxt), wslot)

            nxt_c = jnp.minimum(nxt, N_EXPERTS - 1)
            return (jnp.where(switch, nxt_c, e), jnp.where(switch, seg_ref[SEG_NBLK, nxt_c], left - 1),
                    jnp.where(switch, 1 - wslot, wslot))

        lax.fori_loop(0, n_used, block,
                      (first, seg_ref[SEG_NBLK, jnp.minimum(first, N_EXPERTS - 1)], jnp.int32(0)))

        @pl.when(n_used >= 2)
        def _():
            y_copy(n_used - 2, lax.rem(n_used, 2)).wait()

        y_copy(n_used - 1, lax.rem(n_used - 1, 2)).wait()


def _experts_call(seg, xb, layer, wg, wu, wd):
    n_rows = xb.shape[0]
    hbm = pl.BlockSpec(memory_space=pl.ANY)
    return pl.pallas_call(
        functools.partial(_experts_kernel, layer),
        in_specs=[pl.BlockSpec(memory_space=pltpu.SMEM), hbm, hbm, hbm, hbm],
        out_specs=hbm,
        out_shape=jax.ShapeDtypeStruct((n_rows, D_MODEL), BF16),
        input_output_aliases={4: 0},
        scratch_shapes=[
            pltpu.VMEM((2, D_MODEL, EXPERT_FF), F32),
            pltpu.VMEM((2, D_MODEL, EXPERT_FF), F32),
            pltpu.VMEM((2, EXPERT_FF, D_MODEL), F32),
            pltpu.VMEM((D_MODEL, EXPERT_FF), BF16),
            pltpu.VMEM((D_MODEL, EXPERT_FF), BF16),
            pltpu.VMEM((EXPERT_FF, D_MODEL), BF16),
            pltpu.VMEM((X_SLOTS, EB, D_MODEL), BF16),
            pltpu.VMEM((2, EB, D_MODEL), BF16),
            pltpu.SemaphoreType.DMA((2, 3)),
            pltpu.SemaphoreType.DMA((X_SLOTS,)),
            pltpu.SemaphoreType.DMA((2,)),
        ],
        compiler_params=pltpu.CompilerParams(vmem_limit_bytes=VMEM_LIMIT),
        name="moe_experts",
    )(seg, wg, wu, wd, xb)


def _combine_kernel(final, tbl_ref, cnt_ref, start_ref, rel_ref, seg_ref, yb_hbm, base_ref, gtok_ref, ptok_ref,
                    mod_ref, gf_ref, *rest):
    *out_refs, sort_s, routed_s, tot_s, sems = rest
    i = pl.program_id(0)
    slot = lax.rem(i, 2)
    runs = _Runs(cnt_ref, start_ref, rel_ref, seg_ref)

    def copy_to(sl):
        return lambda s, d, n: _rows_copy(yb_hbm, d, sort_s.at[sl], s, n, sems.at[sl])

    @pl.when(i == 0)
    def _():
        sort_s[...] = jnp.zeros_like(sort_s)
        runs.start_copies(i, copy_to(slot), tot_s, slot)

    @pl.when(i + 1 < pl.num_programs(0))
    def _():
        runs.start_copies(i + 1, copy_to(1 - slot), tot_s, 1 - slot)

    runs.wait_copies(copy_to(slot), tot_s, slot)

    def unsort(n_rows):
        gt = gtok_ref[...]
        pt = ptok_ref[...]
        col = lax.broadcasted_iota(I32, (TM, n_rows), 1).astype(F32)
        w = jnp.zeros((TM, n_rows), F32)
        for k in range(TOP_K):
            w = jnp.where(col == pt[:, k:k + 1], gt[:, k:k + 1], w)
        routed_s[...] = _dot(w.astype(BF16), sort_s[slot, 0:n_rows, :])

    _for_sorted_rows(runs.n_rows(i), unsort)
    gate2 = mod_ref[...][:, 5 * D_MODEL:6 * D_MODEL]
    y = base_ref[...] + gate2 * routed_s[...]
    if final:
        y = y * lax.rsqrt(jnp.mean(y * y, axis=-1, keepdims=True) + NORM_EPS) * gf_ref[...]
        yp_ref, ys_ref = out_refs
        is_context = i < tbl_ref[2, 0]

        @pl.when(is_context)
        def _():
            yp_ref[...] = y

        @pl.when(jnp.logical_not(is_context))
        def _():
            ys_ref[...] = y
    else:
        out_refs[0][...] = y


def _combine_call(final, layer, npt, tbl, cnt, start, rel, seg, yb, base, gtok, ptok, mod4, gfinal):
    n_tok = base.shape[0]
    nt = n_tok // TM
    tile = lambda i, t: (i, 0)
    smem = pl.BlockSpec(memory_space=pltpu.SMEM)
    if final:
        out_specs = [pl.BlockSpec((TM, D_MODEL), lambda i, t: (jnp.minimum(i, npt - 1), 0)),
                     pl.BlockSpec((TM, D_MODEL), lambda i, t: (jnp.maximum(i - npt, 0), 0))]
        out_shape = [jax.ShapeDtypeStruct((npt * TM, D_MODEL), F32),
                     jax.ShapeDtypeStruct((n_tok - npt * TM, D_MODEL), F32)]
    else:
        out_specs = pl.BlockSpec((TM, D_MODEL), tile)
        out_shape = jax.ShapeDtypeStruct((n_tok, D_MODEL), F32)
    gs = pltpu.PrefetchScalarGridSpec(
        num_scalar_prefetch=1,
        grid=(nt,),
        in_specs=[
            smem, smem, smem, smem,
            pl.BlockSpec(memory_space=pl.ANY),
            pl.BlockSpec((TM, D_MODEL), tile),
            pl.BlockSpec((TM, LANES), tile),
            pl.BlockSpec((TM, LANES), tile),
            _mod_spec(layer),
            pl.BlockSpec((1, D_MODEL), lambda i, t: (0, 0)),
        ],
        out_specs=out_specs,
        scratch_shapes=[
            pltpu.VMEM((2, SORT_ROWS, D_MODEL), BF16),
            pltpu.VMEM((TM, D_MODEL), F32),
            pltpu.SMEM((2, 2), I32),
            pltpu.SemaphoreType.DMA((2,)),
        ],
    )
    return pl.pallas_call(
        functools.partial(_combine_kernel, final),
        grid_spec=gs,
        out_shape=out_shape,
        compiler_params=_cparams(("arbitrary",)),
        name="moe_combine",
    )(tbl, cnt, start, rel, seg, yb, base, gtok, ptok, mod4, gfinal)


def _pad_cols(w, groups, width, slot):
    lead = w.shape[:-1]
    w = w.reshape(*lead, groups, width)
    pad = [(0, 0)] * (len(lead) + 1) + [(0, slot - width)]
    return jnp.pad(w, pad).reshape(*lead, groups * slot)


def _prep_w_in(w):
    assert w.shape[-1] == C_KPE + MLA_ROPE_DIM
    return jnp.pad(w.astype(BF16), ((0, 0), (0, 0), (0, N_PRE - w.shape[-1])))


def _rope_tables(n_pos, dim, lane_offsets):
    f32 = np.float32
    n_rows = n_pos // GRID_W
    row = np.repeat(np.arange(n_rows, dtype=f32), GRID_W)
    col = np.tile(np.arange(GRID_W, dtype=f32), n_rows)
    half = dim // 2
    freqs = f32(ROPE_THETA) ** (-np.arange(0, half, 2, dtype=f32) / f32(half))
    ar = row[:, None] * freqs[None, :]
    ac = col[:, None] * freqs[None, :]
    ang = np.concatenate([ar, ar, ac, ac], axis=-1).astype(f32)
    cos, sin = np.cos(ang), np.sin(ang)
    first = (np.arange(dim) % 16) < 8
    sa = np.where(first[None, :], -sin, f32(0))
    sb = np.where(first[None, :], f32(0), sin)
    c_t = np.ones((STEP + n_pos, SLOT), f32)
    a_t = np.zeros((STEP + n_pos, SLOT), f32)
    b_t = np.zeros((STEP + n_pos, SLOT), f32)
    for off in lane_offsets:
        c_t[STEP:, off:off + dim] = cos
        a_t[STEP:, off:off + dim] = sa
        b_t[STEP:, off:off + dim] = sb
    return tuple(jnp.asarray(t) for t in (c_t, a_t, b_t))


def _block_diag_states(s):
    b, l, h, dk, dv = s.shape
    eye = jnp.eye(h, dtype=s.dtype)
    return jnp.einsum('blhkv,hg->blhkgv', s, eye).reshape(b, l, h * dk, h * dv)


def kernel(x_prompt, x_sample, cache_diff_k, cache_diff_v, cache_mla_ckv, cache_mla_kpe, state_ret_fwd, state_ret_bwd, c, c_ctx, w_ada, b_ada, norm_mix, norm_ffn, norm_final, w_in, ret_decay_fwd, ret_decay_bwd, diff_lambda, diff_subln, mla_q_norm, mla_w_uq, mla_kv_norm, mla_w_ukv, w_out, router_w, router_bias, exp_w_gate, exp_w_up, exp_w_down, sh_w_gate, sh_w_up, sh_w_down):
    n_pb, p_len, _ = x_prompt.shape
    n_sb, s_len, _ = x_sample.shape
    past_len = cache_diff_k.shape[3]
    n_p = n_pb * p_len
    n_s = n_sb * s_len
    n_tok = n_p + n_s
    nt = n_tok // TM
    npt = n_p // TM
    assert p_len == TM and s_len % TM == 0 and n_p % s_len == 0 and past_len % 8 == 0

    def step_tables(rows):
        steps = np.arange(n_tok // rows)
        n_ctx = n_p // rows
        per_seq = s_len // rows
        mod_row = np.where(steps < n_ctx, n_sb, (steps - n_ctx) // per_seq)
        rope_blk = np.where(steps < n_ctx, 0, 1 + (steps - n_ctx) % per_seq)
        return jnp.asarray(np.stack([mod_row, rope_blk, np.full(len(steps), n_ctx)]).astype(np.int32))

    assert n_p % STEP == 0 and s_len % STEP == 0
    tbl = step_tables(TM)
    tbl_step = step_tables(STEP)

    n_cond = 16
    cond = jnp.concatenate([c, c_ctx[None, :], jnp.zeros((n_cond - n_sb - 1, D_MODEL), F32)], axis=0)
    mod_all = _modulation(cond, w_ada, b_ada)

    rope_d = _rope_tables(s_len, DIFF_QK_DIM, (0, DIFF_QK_DIM))
    rope_m = _rope_tables(s_len, MLA_ROPE_DIM, (KR_LO,))
    place = np.zeros((MLA_ROPE_DIM, SLOT), np.float32)
    place[np.arange(MLA_ROPE_DIM), KR_LO + np.arange(MLA_ROPE_DIM)] = 1.0
    place = jnp.asarray(place, BF16)
    cache_dv_t = cache_diff_v.transpose(0, 1, 3, 2, 4).reshape(n_sb, DEPTH, past_len, DIFF_V_W)
    s0f_bd = _block_diag_states(state_ret_fwd)
    s0b_bd = _block_diag_states(state_ret_bwd)

    n_blocks = pl.cdiv(n_tok * TOP_K + nt * N_EXPERTS * (CH - 1) + N_EXPERTS * (EB - CH), EB)
    n_rows = n_blocks * EB

    xa, xb = x_prompt.reshape(n_p, D_MODEL), x_sample.reshape(n_s, D_MODEL)
    gfinal = norm_final.reshape(1, D_MODEL)

    mod4 = mod_all.reshape(DEPTH, n_cond, 1, 6 * D_MODEL)
    w_pre = _prep_w_in(w_in)
    wuq = _pad_cols(mla_w_uq, MLA_HEADS, MLA_NOPE_DIM + MLA_ROPE_DIM, SLOT).astype(BF16)
    ukv = mla_w_ukv.reshape(DEPTH, MLA_KV_LORA, MLA_HEADS, MLA_NOPE_DIM + MLA_V_DIM)
    wk = _pad_cols(ukv[..., :MLA_NOPE_DIM].reshape(DEPTH, MLA_KV_LORA, -1), MLA_HEADS, MLA_NOPE_DIM, SLOT).astype(BF16)
    wv = ukv[..., MLA_NOPE_DIM:].reshape(DEPTH, MLA_KV_LORA, MLA_V_W).astype(BF16)
    decf = jnp.repeat(ret_decay_fwd, RET_DIM, axis=-1).reshape(DEPTH, 1, RET_W)
    decb = jnp.repeat(ret_decay_bwd, RET_DIM, axis=-1).reshape(DEPTH, 1, RET_W)
    subln = jnp.tile(diff_subln, (1, DIFF_HEADS)).reshape(DEPTH, 1, DIFF_V_W)
    vec = lambda p: p.reshape(DEPTH, 1, -1)
    wout_b, shg_b, shu_b, shd_b = (w.astype(BF16) for w in (w_out, sh_w_gate, sh_w_up, sh_w_down))
    rwt = router_w.transpose(0, 2, 1)
    rb = router_bias.reshape(DEPTH, N_EXPERTS, 1)

    new_ctx = []
    sorted_buf = None
    for l in range(DEPTH):
        lam_init = 0.8 - 0.6 * math.exp(-0.3 * l)
        ra, qa, ka, *caches = _pre_call(l, tbl_step, xa, xb, npt, n_tok, mod4, vec(norm_mix), w_pre, vec(mla_q_norm),
                                        wuq, vec(mla_kv_norm), rope_d, rope_m)
        mix_p, *states = _mix_prompt_call(lam_init, l, qa, ka, ra, n_pb, p_len, decf, decb,
                                          diff_lambda, subln, wk, wv)
        new_ctx.append(caches + states)
        mix_s = _mix_sample_call(lam_init, l, qa, ka, ra, n_p, n_sb, s_len, past_len, cache_diff_k,
                                 cache_dv_t, cache_mla_ckv, cache_mla_kpe, s0f_bd, s0b_bd, decf, decb,
                                 diff_lambda, subln, wk, wv, place)
        base, h2, lpos, ptok, gtok, cnt, start, rel = _post_call(
            l, tbl_step, xa, xb, mix_p, mix_s, mod4, wout_b, vec(norm_ffn), shg_b, shu_b, shd_b, rwt, rb)
        seg = _plan_call(cnt, rel, nt)
        sorted_buf = _dispatch_call(cnt, start, rel, seg, h2, lpos, n_rows, prev=sorted_buf)
        sorted_buf = _experts_call(seg, sorted_buf, l, exp_w_gate, exp_w_up, exp_w_down)
        final = l == DEPTH - 1
        out = _combine_call(final, l, npt, tbl, cnt, start, rel, seg, sorted_buf, base, gtok, ptok, mod4, gfinal)
        xa, xb = out if final else (out, out)

    y_prompt = xa.reshape(n_pb, p_len, D_MODEL)
    y_sample = xb.reshape(n_sb, s_len, D_MODEL)
    return (y_prompt, y_sample, *(jnp.stack(per_layer, axis=1) for per_layer in zip(*new_ctx)))
```

```python
import functools
import math

import numpy as np
import jax
import jax.numpy as jnp
from jax import lax
from jax.experimental import pallas as pl
from jax.experimental.pallas import tpu as pltpu

F32 = jnp.float32
BF16 = jnp.bfloat16
I32 = jnp.int32

D_MODEL = 1024
DEPTH = 2
GRID_W = 64
ROPE_THETA = 10000.0
NORM_EPS = 1e-6

RET_HEADS = 4
RET_DIM = 64
RET_CHUNK = 128
RET_W = RET_HEADS * RET_DIM
DIFF_HEADS = 4
DIFF_QK_DIM = 32
DIFF_V_DIM = 64
DIFF_V_W = DIFF_HEADS * DIFF_V_DIM
MLA_HEADS = 8
MLA_Q_LORA = 256
MLA_KV_LORA = 128
MLA_NOPE_DIM = 64
MLA_ROPE_DIM = 32
MLA_V_DIM = 64
MLA_V_W = MLA_HEADS * MLA_V_DIM
MIX_W = RET_W + DIFF_V_W + MLA_V_W

N_EXPERTS = 64
TOP_K = 6
N_GROUPS = 8
GROUP_SIZE = N_EXPERTS // N_GROUPS
TOPK_GROUPS = 4
EXPERT_FF = 256
ROUTED_SCALE = 2.5

LANES = 128
SLOT = LANES
TM = 256
TS = 2
STEP = TS * TM
SLOT_ROWS = 8
EB = 512
LOG_EB = 9
CH = 16
LOG_CH = 4
SORT_ROWS = 2560
VMEM_LIMIT = 48 * 1024 * 1024

C_RQ, C_RK, C_RV, C_RG = 0, 256, 512, 768
C_DQ, C_DK, C_DV, C_CQ, C_CKV, C_KPE = 1024, 1280, 1536, 1792, 2048, 2176
N_PRE = 2304
QA_W = 4 * SLOT + MLA_HEADS * SLOT
KA_DK, KA_DV, KA_CKV, KA_KPE = 0, 512, 768, 896
KA_W = 1024
KR_LO, KR_HI = 64, 96


def _dot(a, b):
    return jnp.dot(a, b, preferred_element_type=F32)


def _dot_nt(a, b):
    return lax.dot_general(a, b, (((1,), (1,)), ((), ())), preferred_element_type=F32)


def _dot_tn(a, b):
    return lax.dot_general(a, b, (((0,), (0,)), ((), ())), preferred_element_type=F32)


def _split_dot(x, w_bf16):
    hi = x.astype(BF16)
    lo = (x - hi.astype(F32)).astype(BF16)
    return _dot(hi, w_bf16) + _dot(lo, w_bf16)


def _split_dot_nt(w, x):
    wh = w.astype(BF16)
    wl = (w - wh.astype(F32)).astype(BF16)
    xh = x.astype(BF16)
    xl = (x - xh.astype(F32)).astype(BF16)
    return _dot_nt(wh, xh) + _dot_nt(wh, xl) + _dot_nt(wl, xh)


def _silu(x):
    return x * jax.nn.sigmoid(x)


def _cparams(sem):
    return pltpu.CompilerParams(dimension_semantics=sem, vmem_limit_bytes=VMEM_LIMIT)


MOD_TN = 512


def _mod_kernel(c_ref, w_ref, b_ref, o_ref):
    s = _silu(c_ref[...])
    o_ref[...] = _split_dot3(s, w_ref[...]) + b_ref[...]


def _split_dot3(x, w):
    xh = x.astype(BF16)
    xl = (x - xh.astype(F32)).astype(BF16)
    wh = w.astype(BF16)
    wl = (w - wh.astype(F32)).astype(BF16)
    return _dot(xh, wh) + _dot(xh, wl) + _dot(xl, wh)


def _modulation(cond, w_ada, b_ada):
    n_rows = cond.shape[0]
    n_out = w_ada.shape[-1]
    return pl.pallas_call(
        _mod_kernel,
        grid=(DEPTH, n_out // MOD_TN),
        in_specs=[
            pl.BlockSpec((n_rows, D_MODEL), lambda l, j: (0, 0)),
            pl.BlockSpec((None, D_MODEL, MOD_TN), lambda l, j: (l, 0, j)),
            pl.BlockSpec((None, 1, MOD_TN), lambda l, j: (l, 0, j)),
        ],
        out_specs=pl.BlockSpec((None, n_rows, MOD_TN), lambda l, j: (l, 0, j)),
        out_shape=jax.ShapeDtypeStruct((DEPTH, n_rows, n_out), F32),
        compiler_params=_cparams(("arbitrary", "arbitrary")),
        name="adaln_mod",
    )(cond, w_ada, b_ada.reshape(DEPTH, 1, n_out))


def _rope_slot(x, cos, sa, sb):
    up = pltpu.roll(x, LANES - 8, 1)
    dn = pltpu.roll(x, 8, 1)
    return x * cos + up * sa + dn * sb


def _x_specs(nps, combined):
    off = 0 if combined else nps
    return [pl.BlockSpec((STEP, D_MODEL), lambda i, t: (jnp.minimum(i, nps - 1), 0)),
            pl.BlockSpec((STEP, D_MODEL), lambda i, t: (jnp.maximum(i, nps) - off, 0))]


def _head_slots(x):
    half = SLOT // 2
    low = _lane_iota((x.shape[0], SLOT)) < half
    slots = []
    for c in range(x.shape[1] // SLOT):
        pair = x[:, c * SLOT:(c + 1) * SLOT]
        slots.append(jnp.where(low, pair, 0.0))
        slots.append(jnp.where(low, pltpu.roll(pair, half, 1), 0.0))
    return slots


def _pre_kernel(tbl_ref, xa_ref, xb_ref, mod_ref, g_ref, w_ref, qg_ref, wuq_ref, kvg_ref,
                cd_ref, sad_ref, sbd_ref, cm_ref, sam_ref, sbm_ref,
                ra_ref, qa_ref, ka_ref, dkc_ref, dvc_ref, ckvc_ref, kpec_ref):
    is_context = pl.program_id(0) < tbl_ref[2, 0]
    mod = mod_ref[...]
    shift1 = mod[:, 0:D_MODEL]
    scale1 = mod[:, D_MODEL:2 * D_MODEL]
    for s in range(TS):
        rows = slice(s * TM, (s + 1) * TM)
        x = jnp.where(is_context, xa_ref[rows, :], xb_ref[rows, :])
        ms = jnp.mean(x * x, axis=-1, keepdims=True)
        h = x * lax.rsqrt(ms + NORM_EPS) * g_ref[...]
        h = h * (1.0 + scale1) + shift1
        hb = h.astype(BF16)

        def proj(lo, hi, hb=hb):
            return _dot(hb, w_ref[:, lo:hi])

        ra_ref[rows, C_RQ:C_RK] = proj(C_RQ, C_RK).astype(BF16)
        ra_ref[rows, C_RK:C_RV] = (proj(C_RK, C_RV) * (RET_DIM ** -0.5)).astype(BF16)
        ra_ref[rows, C_RV:C_RG] = proj(C_RV, C_RG).astype(BF16)
        ra_ref[rows, C_RG:C_DQ] = _silu(proj(C_RG, C_DQ)).astype(BF16)

        cd, sad, sbd = cd_ref[rows, :], sad_ref[rows, :], sbd_ref[rows, :]
        cm, sam, sbm = cm_ref[rows, :], sam_ref[rows, :], sbm_ref[rows, :]
        dq_slots = _head_slots(proj(C_DQ, C_DK))
        dk_slots = [_rope_slot(v, cd, sad, sbd) for v in _head_slots(proj(C_DK, C_DV))]
        for hd in range(DIFF_HEADS):
            qa_ref[rows, hd * SLOT:(hd + 1) * SLOT] = _rope_slot(dq_slots[hd], cd, sad, sbd).astype(BF16)
            ka_ref[rows, KA_DK + hd * SLOT:KA_DK + (hd + 1) * SLOT] = dk_slots[hd].astype(BF16)
        dv = proj(C_DV, C_CQ)
        ka_ref[rows, KA_DV:KA_CKV] = dv.astype(BF16)

        cq = proj(C_CQ, C_CKV)
        cqn = cq * lax.rsqrt(jnp.mean(cq * cq, axis=-1, keepdims=True) + NORM_EPS) * qg_ref[...]
        qm = _dot(cqn.astype(BF16), wuq_ref[...])
        for hd in range(MLA_HEADS):
            sl = slice(hd * SLOT, (hd + 1) * SLOT)
            qa_ref[rows, 4 * SLOT + hd * SLOT:4 * SLOT + (hd + 1) * SLOT] = _rope_slot(qm[:, sl], cm, sam, sbm).astype(BF16)

        ckv = proj(C_CKV, C_KPE)
        ckvn = ckv * lax.rsqrt(jnp.mean(ckv * ckv, axis=-1, keepdims=True) + NORM_EPS) * kvg_ref[...]
        kpe = proj(C_KPE, N_PRE)
        kpe_slot = _rope_slot(kpe + pltpu.roll(kpe, SLOT // 2, 1), cm, sam, sbm)
        ka_ref[rows, KA_CKV:KA_KPE] = ckvn.astype(BF16)
        ka_ref[rows, KA_KPE:KA_W] = kpe_slot.astype(BF16)

        @pl.when(is_context)
        def _(s=s, dk_slots=dk_slots, dv=dv, ckvn=ckvn, kpe_slot=kpe_slot):
            dv_slots = _head_slots(dv)
            for hd in range(DIFF_HEADS):
                dkc_ref[s, hd] = dk_slots[hd][:, 0:2 * DIFF_QK_DIM]
                dvc_ref[s, hd] = dv_slots[hd][:, 0:DIFF_V_DIM]
            ckvc_ref[s] = ckvn
            kpec_ref[s] = kpe_slot[:, 0:MLA_ROPE_DIM]


def _layer_spec(layer, rows, cols):
    return pl.BlockSpec((None, rows, cols), lambda *_: (layer, 0, 0))


def _mod_spec(layer):
    return pl.BlockSpec((None, None, 1, 6 * D_MODEL), lambda i, t: (layer, t[0, i], 0, 0))


def _pre_call(layer, tbl, xa, xb, npt, n_tok, mod4, g, w_pre, qg, wuq, kvg, rope_d, rope_m):
    ns = n_tok // STEP
    nps = npt // TS
    tile = lambda i, t: (i, 0)
    rope = lambda i, t: (t[1, i], 0)
    ctx5 = lambda i, t: (jnp.minimum(i, nps - 1), 0, 0, 0)
    ctx4 = lambda i, t: (jnp.minimum(i, nps - 1), 0, 0)
    cache_shapes = [
        jax.ShapeDtypeStruct((npt, DIFF_HEADS, TM, 2 * DIFF_QK_DIM), F32),
        jax.ShapeDtypeStruct((npt, DIFF_HEADS, TM, DIFF_V_DIM), F32),
        jax.ShapeDtypeStruct((npt, TM, MLA_KV_LORA), F32),
        jax.ShapeDtypeStruct((npt, TM, MLA_ROPE_DIM), F32),
    ]
    gs = pltpu.PrefetchScalarGridSpec(
        num_scalar_prefetch=1,
        grid=(ns,),
        in_specs=_x_specs(nps, xa is xb) + [
            _mod_spec(layer),
            _layer_spec(layer, 1, D_MODEL),
            _layer_spec(layer, D_MODEL, N_PRE),
            _layer_spec(layer, 1, MLA_Q_LORA),
            _layer_spec(layer, MLA_Q_LORA, MLA_HEADS * SLOT),
            _layer_spec(layer, 1, MLA_KV_LORA),
        ] + [pl.BlockSpec((STEP, SLOT), rope)] * 6,
        out_specs=[
            pl.BlockSpec((STEP, D_MODEL), tile),
            pl.BlockSpec((STEP, QA_W), tile),
            pl.BlockSpec((STEP, KA_W), tile),
            pl.BlockSpec((TS, DIFF_HEADS, TM, 2 * DIFF_QK_DIM), ctx5),
            pl.BlockSpec((TS, DIFF_HEADS, TM, DIFF_V_DIM), ctx5),
            pl.BlockSpec((TS, TM, MLA_KV_LORA), ctx4),
            pl.BlockSpec((TS, TM, MLA_ROPE_DIM), ctx4),
        ],
    )
    return pl.pallas_call(
        _pre_kernel,
        grid_spec=gs,
        out_shape=[
            jax.ShapeDtypeStruct((n_tok, D_MODEL), BF16),
            jax.ShapeDtypeStruct((n_tok, QA_W), BF16),
            jax.ShapeDtypeStruct((n_tok, KA_W), BF16),
        ] + cache_shapes,
        compiler_params=_cparams(("arbitrary",)),
        name="pre_proj",
    )(tbl, xa, xb, mod4, g, w_pre, qg, wuq, kvg, *rope_d, *rope_m)


def _lane_iota(shape):
    return lax.broadcasted_iota(I32, shape, len(shape) - 1)


def _head_mask(n_rows, width, head, head_w):
    lane = _lane_iota((n_rows, width))
    return (lane >= head * head_w) & (lane < (head + 1) * head_w)


def _seg_mean_sq(o, bd_ones):
    return _split_dot(o * o, bd_ones) * (1.0 / RET_DIM)


def _block_diag_ones(n, blk):
    r = lax.broadcasted_iota(I32, (n, n), 0) // blk
    c = lax.broadcasted_iota(I32, (n, n), 1) // blk
    return r == c


def _retention(ra_ref, seq_len, decf_ref, decb_ref, s0f, s0b):
    C = RET_CHUNK
    nc = seq_len // C
    lgf = -jnp.exp(decf_ref[...])
    lgb = -jnp.exp(decb_ref[...])
    pos = lax.broadcasted_iota(I32, (C, RET_W), 0).astype(F32)
    qdf = jnp.exp((pos + 1.0) * lgf)
    kdf = jnp.exp((C - 1.0 - pos) * lgf)
    cdf = jnp.exp(float(C) * lgf)
    qdb = jnp.exp((C - pos) * lgb)
    kdb = jnp.exp(pos * lgb)
    cdb = jnp.exp(float(C) * lgb)
    ii = lax.broadcasted_iota(I32, (C, C), 0).astype(F32)
    jj = lax.broadcasted_iota(I32, (C, C), 1).astype(F32)
    dist = ii - jj
    dmats = []
    for hd in range(RET_HEADS):
        lf = lgf[:, hd * RET_DIM:hd * RET_DIM + 1]
        lb = lgb[:, hd * RET_DIM:hd * RET_DIM + 1]
        dmats.append(jnp.where(dist >= 0, jnp.exp(dist * lf), jnp.exp(-dist * lb)))
    bd = _block_diag_ones(RET_W, RET_DIM)
    bd_ones = jnp.where(bd, 1.0, 0.0).astype(BF16)

    def chunk(n):
        rows = slice(n * C, (n + 1) * C)
        return (ra_ref[rows, C_RQ:C_RK], ra_ref[rows, C_RK:C_RV], ra_ref[rows, C_RV:C_RG])

    cross = [None] * nc
    sf = s0f
    for n in range(nc):
        q, k, v = chunk(n)
        cross[n] = _dot((q * qdf).astype(BF16), sf.astype(BF16))
        kv = _dot_tn((k * kdf).astype(BF16), v.astype(BF16))
        sf = sf * cdf + jnp.where(bd, kv, 0.0)
    sb = s0b
    for n in range(nc - 1, -1, -1):
        q, k, v = chunk(n)
        cross[n] = cross[n] + _dot((q * qdb).astype(BF16), sb.astype(BF16))
        kv = _dot_tn((k * kdb).astype(BF16), v.astype(BF16))
        sb = sb * cdb + jnp.where(bd, kv, 0.0)

    outs = []
    for n in range(nc):
        q, k, v = chunk(n)
        kb = k.astype(BF16)
        vb = v.astype(BF16)
        o = cross[n]
        for hd in range(RET_HEADS):
            hm = _head_mask(C, RET_W, hd, RET_DIM)
            sc = _dot_nt(jnp.where(hm, q, 0.0).astype(BF16), kb) * dmats[hd]
            o = o + jnp.where(hm, _dot(sc.astype(BF16), vb), 0.0)
        on = o * lax.rsqrt(_seg_mean_sq(o, bd_ones) + NORM_EPS)
        outs.append(on * ra_ref[n * C:(n + 1) * C, C_RG:C_DQ])
    return outs, sf, sb


def _softmax_pv(s_parts, v_parts, scale):
    scale = scale * math.log2(math.e)
    m = None
    for s in s_parts:
        mm = jnp.max(s, axis=-1, keepdims=True)
        m = mm if m is None else jnp.maximum(m, mm)
    m = m * scale
    acc = None
    den = None
    for s, v in zip(s_parts, v_parts):
        e = jnp.exp2(s * scale - m)
        ds = jnp.sum(e, axis=-1, keepdims=True)
        pv = _dot(e.astype(BF16), v)
        acc = pv if acc is None else acc + pv
        den = ds if den is None else den + ds
    return acc / den


def _diff_attention(dq, k_parts, v_parts, lam, subln, lam_init, bd_ones):
    lq = dq.shape[0]
    scale = DIFF_QK_DIM ** -0.5
    lane = _lane_iota((lq, SLOT))
    out = jnp.zeros((lq, DIFF_V_W), F32)
    for hd in range(DIFF_HEADS):
        qh = dq[:, hd * SLOT:(hd + 1) * SLOT]
        q1 = jnp.where(lane < DIFF_QK_DIM, qh, 0.0).astype(BF16)
        q2 = jnp.where(lane >= DIFF_QK_DIM, qh, 0.0).astype(BF16)
        s1 = [_dot_nt(q1[:, :kp[hd].shape[1]], kp[hd]) for kp in k_parts]
        s2 = [_dot_nt(q2[:, :kp[hd].shape[1]], kp[hd]) for kp in k_parts]
        o = _softmax_pv(s1, v_parts, scale) - lam * _softmax_pv(s2, v_parts, scale)
        out = jnp.where(_head_mask(lq, DIFF_V_W, hd, DIFF_V_DIM), o, out)
    on = out * lax.rsqrt(_seg_mean_sq(out, bd_ones) + NORM_EPS) * subln
    return on * (1.0 - lam_init)


def _mla_attention(qm, k_parts, v_parts):
    lq = qm.shape[0]
    scale = (MLA_NOPE_DIM + MLA_ROPE_DIM) ** -0.5
    halves = []
    for g in range(2):
        out = jnp.zeros((lq, 256), F32)
        for hh in range(4):
            hd = 4 * g + hh
            qh = qm[:, hd * SLOT:(hd + 1) * SLOT].astype(BF16)
            s = [_dot_nt(qh, kp[:, hd * SLOT:(hd + 1) * SLOT]) for kp in k_parts]
            o = _softmax_pv(s, [vp[:, 256 * g:256 * (g + 1)] for vp in v_parts], scale)
            out = jnp.where(_head_mask(lq, 256, hh, MLA_V_DIM), o, out)
        halves.append(out)
    return halves


def _mla_keys(ka_val_ckv, kr_slot, wk_ref, wv_ref):
    cb = ka_val_ckv.astype(BF16)
    kn = _dot(cb, wk_ref[...])
    ks = [(kn[:, hd * SLOT:(hd + 1) * SLOT] + kr_slot).astype(BF16) for hd in range(MLA_HEADS)]
    return jnp.concatenate(ks, axis=1), _dot(cb, wv_ref[...]).astype(BF16)


def _kr_only(kpe_slot):
    lane = _lane_iota(kpe_slot.shape)
    return jnp.where((lane >= KR_LO) & (lane < KR_HI), kpe_slot, 0.0)


def _diff_lambda(dl_ref, lam_init):
    dl = dl_ref[...]
    a = jnp.sum(dl[0:1] * dl[1:2], axis=-1, keepdims=True)
    b = jnp.sum(dl[2:3] * dl[3:4], axis=-1, keepdims=True)
    return jnp.exp(a) - jnp.exp(b) + lam_init


def _mix_prompt_kernel(lam_init, qa_ref, ka_ref, ra_ref, decf_ref, decb_ref, dl_ref, subln_ref,
                       wk_ref, wv_ref, mix_ref, sf_ref, sb_ref):
    seq = qa_ref.shape[0]
    zero_state = jnp.zeros((RET_W, RET_W), F32)
    outs, sf, sb = _retention(ra_ref, seq, decf_ref, decb_ref, zero_state, zero_state)
    for n, o in enumerate(outs):
        mix_ref[n * RET_CHUNK:(n + 1) * RET_CHUNK, 0:RET_W] = o.astype(BF16)
    for hd in range(RET_HEADS):
        blk = slice(hd * RET_DIM, (hd + 1) * RET_DIM)
        sf_ref[hd] = sf[blk, blk]
        sb_ref[hd] = sb[blk, blk]

    bd_ones = jnp.where(_block_diag_ones(DIFF_V_W, DIFF_V_DIM), 1.0, 0.0).astype(BF16)
    lam = _diff_lambda(dl_ref, lam_init)
    kd = [ka_ref[:, KA_DK + hd * SLOT:KA_DK + (hd + 1) * SLOT].astype(BF16) for hd in range(DIFF_HEADS)]
    vd = ka_ref[:, KA_DV:KA_CKV].astype(BF16)
    mix_ref[:, RET_W:RET_W + DIFF_V_W] = _diff_attention(
        qa_ref[:, 0:4 * SLOT], [kd], [vd], lam, subln_ref[...], lam_init, bd_ones).astype(BF16)

    km, vm = _mla_keys(ka_ref[:, KA_CKV:KA_KPE], _kr_only(ka_ref[:, KA_KPE:KA_W]), wk_ref, wv_ref)
    halves = _mla_attention(qa_ref[:, 4 * SLOT:QA_W], [km], [vm])
    mix_ref[:, 512:768] = halves[0].astype(BF16)
    mix_ref[:, 768:1024] = halves[1].astype(BF16)


def _mixer_param_specs(layer):
    return [
        _layer_spec(layer, 1, RET_W),
        _layer_spec(layer, 1, RET_W),
        _layer_spec(layer, 4, DIFF_QK_DIM),
        _layer_spec(layer, 1, DIFF_V_W),
        _layer_spec(layer, MLA_KV_LORA, MLA_HEADS * SLOT),
        _layer_spec(layer, MLA_KV_LORA, MLA_V_W),
    ]


def _mix_prompt_call(lam_init, layer, qa, ka, ra, n_seq, seq_len, decf, decb, dl, subln, wk, wv):
    seq = lambda b: (b, 0)
    state_spec = pl.BlockSpec((None, RET_HEADS, RET_DIM, RET_DIM), lambda b: (b, 0, 0, 0))
    state_shape = jax.ShapeDtypeStruct((n_seq, RET_HEADS, RET_DIM, RET_DIM), F32)
    return pl.pallas_call(
        functools.partial(_mix_prompt_kernel, lam_init),
        grid=(n_seq,),
        in_specs=[
            pl.BlockSpec((seq_len, QA_W), seq),
            pl.BlockSpec((seq_len, KA_W), seq),
            pl.BlockSpec((seq_len, D_MODEL), seq),
        ] + _mixer_param_specs(layer),
        out_specs=[pl.BlockSpec((seq_len, MIX_W), seq), state_spec, state_spec],
        out_shape=[jax.ShapeDtypeStruct((n_seq * seq_len, MIX_W), BF16), state_shape, state_shape],
        compiler_params=_cparams(("arbitrary",)),
        name="mix_prompt",
    )(qa, ka, ra, decf, decb, dl, subln, wk, wv)


def _mix_sample_kernel(lam_init, qa_ref, ka_ref, ra_ref, ckd_ref, cvd_ref, cckv_ref, ckpe_ref,
                       s0f_ref, s0b_ref, decf_ref, decb_ref, dl_ref, subln_ref, wk_ref, wv_ref,
                       place_ref, mix_ref,
                       ret_s, kdn_s, vdn_s, kdc_s, vdc_s, kmn_s, vmn_s, kmc_s, vmc_s):
    j = pl.program_id(1)
    seq = ka_ref.shape[0]

    @pl.when(j == 0)
    def _():
        outs, _, _ = _retention(ra_ref, seq, decf_ref, decb_ref, s0f_ref[...], s0b_ref[...])
        for n, o in enumerate(outs):
            ret_s[n * RET_CHUNK:(n + 1) * RET_CHUNK, :] = o
        kdn_s[...] = ka_ref[:, KA_DK:KA_DV].astype(BF16)
        vdn_s[...] = ka_ref[:, KA_DV:KA_CKV].astype(BF16)
        kdc_s[...] = ckd_ref[...].astype(BF16)
        vdc_s[...] = cvd_ref[...].astype(BF16)
        km, vm = _mla_keys(ka_ref[:, KA_CKV:KA_KPE], _kr_only(ka_ref[:, KA_KPE:KA_W]), wk_ref, wv_ref)
        kmn_s[...] = km
        vmn_s[...] = vm
        kr_ctx = _dot(ckpe_ref[...].astype(BF16), place_ref[...])
        km, vm = _mla_keys(cckv_ref[...], kr_ctx, wk_ref, wv_ref)
        kmc_s[...] = km
        vmc_s[...] = vm

    row0 = pl.multiple_of(j * TM, TM)
    mix_ref[:, 0:RET_W] = ret_s[pl.ds(row0, TM), :].astype(BF16)

    bd_ones = jnp.where(_block_diag_ones(DIFF_V_W, DIFF_V_DIM), 1.0, 0.0).astype(BF16)
    lam = _diff_lambda(dl_ref, lam_init)
    kd_ctx = [kdc_s[hd] for hd in range(DIFF_HEADS)]
    kd_new = [kdn_s[:, hd * SLOT:(hd + 1) * SLOT] for hd in range(DIFF_HEADS)]
    mix_ref[:, RET_W:RET_W + DIFF_V_W] = _diff_attention(
        qa_ref[:, 0:4 * SLOT], [kd_ctx, kd_new], [vdc_s[...], vdn_s[...]], lam, subln_ref[...],
        lam_init, bd_ones).astype(BF16)

    halves = _mla_attention(qa_ref[:, 4 * SLOT:QA_W], [kmc_s[...], kmn_s[...]], [vmc_s[...], vmn_s[...]])
    mix_ref[:, 512:768] = halves[0].astype(BF16)
    mix_ref[:, 768:1024] = halves[1].astype(BF16)


def _mix_sample_call(lam_init, layer, qa, ka, ra, tok0, n_seq, seq_len, past_len, cache_dk, cache_dv_t,
                     cache_ckv, cache_kpe, s0f_bd, s0b_bd, decf, decb, dl, subln, wk, wv, place):
    nq = seq_len // TM
    q0 = tok0 // TM
    s0 = tok0 // seq_len
    const = lambda b, j: (0, 0)
    return pl.pallas_call(
        functools.partial(_mix_sample_kernel, lam_init),
        grid=(n_seq, nq),
        in_specs=[
            pl.BlockSpec((TM, QA_W), lambda b, j: (q0 + b * nq + j, 0)),
            pl.BlockSpec((seq_len, KA_W), lambda b, j: (s0 + b, 0)),
            pl.BlockSpec((seq_len, D_MODEL), lambda b, j: (s0 + b, 0)),
            pl.BlockSpec((None, None, DIFF_HEADS, past_len, 2 * DIFF_QK_DIM), lambda b, j: (b, layer, 0, 0, 0)),
            pl.BlockSpec((None, None, past_len, DIFF_V_W), lambda b, j: (b, layer, 0, 0)),
            pl.BlockSpec((None, None, past_len, MLA_KV_LORA), lambda b, j: (b, layer, 0, 0)),
            pl.BlockSpec((None, None, past_len, MLA_ROPE_DIM), lambda b, j: (b, layer, 0, 0)),
            pl.BlockSpec((None, None, RET_W, RET_W), lambda b, j: (b, layer, 0, 0)),
            pl.BlockSpec((None, None, RET_W, RET_W), lambda b, j: (b, layer, 0, 0)),
        ] + _mixer_param_specs(layer) + [
            pl.BlockSpec((MLA_ROPE_DIM, SLOT), const),
        ],
        out_specs=pl.BlockSpec((TM, MIX_W), lambda b, j: (b * nq + j, 0)),
        out_shape=jax.ShapeDtypeStruct((n_seq * seq_len, MIX_W), BF16),
        scratch_shapes=[
            pltpu.VMEM((seq_len, RET_W), F32),
            pltpu.VMEM((seq_len, 4 * SLOT), BF16),
            pltpu.VMEM((seq_len, DIFF_V_W), BF16),
            pltpu.VMEM((DIFF_HEADS, past_len, 2 * DIFF_QK_DIM), BF16),
            pltpu.VMEM((past_len, DIFF_V_W), BF16),
            pltpu.VMEM((seq_len, MLA_HEADS * SLOT), BF16),
            pltpu.VMEM((seq_len, MLA_V_W), BF16),
            pltpu.VMEM((past_len, MLA_HEADS * SLOT), BF16),
            pltpu.VMEM((past_len, MLA_V_W), BF16),
        ],
        compiler_params=_cparams(("arbitrary", "arbitrary")),
        name="mix_sample",
    )(qa, ka, ra, cache_dk, cache_dv_t, cache_ckv, cache_kpe, s0f_bd, s0b_bd,
      decf, decb, dl, subln, wk, wv, place)


def _route(h2, rwt_ref, rb_ref):
    tm = h2.shape[0]
    neg = -jnp.inf
    logits = _split_dot_nt(rwt_ref[...], h2)
    sc = jax.nn.sigmoid(logits)
    sel = sc + rb_ref[...]
    member = lax.broadcasted_iota(I32, (GROUP_SIZE, tm), 0).astype(F32)
    gscore = []
    for g in range(N_GROUPS):
        sg = sel[g * GROUP_SIZE:(g + 1) * GROUP_SIZE, :]
        m1 = jnp.max(sg, axis=0, keepdims=True)
        f1 = jnp.min(jnp.where(sg == m1, member, float(GROUP_SIZE)), axis=0, keepdims=True)
        m2 = jnp.max(jnp.where(member == f1, neg, sg), axis=0, keepdims=True)
        gscore.append(m1 + m2)
    gsel = [jnp.zeros((1, tm), F32) for _ in range(N_GROUPS)]
    for _ in range(TOPK_GROUPS):
        mx = gscore[0]
        for g in range(1, N_GROUPS):
            mx = jnp.maximum(mx, gscore[g])
        fi = jnp.full((1, tm), float(N_GROUPS), F32)
        for g in range(N_GROUPS - 1, -1, -1):
            fi = jnp.where(gscore[g] == mx, float(g), fi)
        for g in range(N_GROUPS):
            hit = fi == float(g)
            gsel[g] = jnp.where(hit, 1.0, gsel[g])
            gscore[g] = jnp.where(hit, neg, gscore[g])
    cand = jnp.concatenate(
        [jnp.where(gsel[g] > 0.0, sel[g * GROUP_SIZE:(g + 1) * GROUP_SIZE, :], neg) for g in range(N_GROUPS)],
        axis=0)
    flat = lax.broadcasted_iota(I32, (N_EXPERTS, tm), 0).astype(F32)
    hits, gts = [], []
    chosen = jnp.zeros((N_EXPERTS, tm), F32)
    for _ in range(TOP_K):
        mx = jnp.max(cand, axis=0, keepdims=True)
        fk = jnp.min(jnp.where(cand == mx, flat, float(N_EXPERTS)), axis=0, keepdims=True)
        hit = flat == fk
        hits.append(hit)
        gts.append(jnp.sum(jnp.where(hit, sc, 0.0), axis=0, keepdims=True))
        chosen = jnp.where(hit, 1.0, chosen)
        cand = jnp.where(hit, neg, cand)
    gsum = gts[0]
    for g in gts[1:]:
        gsum = gsum + g
    gts = [g / gsum * ROUTED_SCALE for g in gts]

    before = (lax.broadcasted_iota(I32, (tm, tm), 0) < lax.broadcasted_iota(I32, (tm, tm), 1))
    rank_in = _dot(chosen.astype(BF16), jnp.where(before, 1.0, 0.0).astype(BF16))
    cnt = jnp.sum(chosen, axis=1, keepdims=True)
    cnt_pad = jnp.floor((cnt + (CH - 1.0)) * (1.0 / CH)) * CH
    below = (lax.broadcasted_iota(I32, (N_EXPERTS, N_EXPERTS), 1) < lax.broadcasted_iota(I32, (N_EXPERTS, N_EXPERTS), 0))
    start = _dot(jnp.where(below, 1.0, 0.0).astype(BF16),
                 jnp.broadcast_to(cnt_pad, (N_EXPERTS, LANES)).astype(BF16))[:, 0:1]
    pos = rank_in + start
    lpos = [jnp.sum(jnp.where(hit, pos, 0.0), axis=0, keepdims=True) for hit in hits]
    return lpos, gts, cnt_pad, start


def _slot_rows(vals, n_rows):
    tm = vals[0].shape[1]
    row = lax.broadcasted_iota(I32, (n_rows, tm), 0)
    out = jnp.zeros((n_rows, tm), F32)
    for k, v in enumerate(vals):
        out = jnp.where(row == k, v, out)
    return out


def _post_kernel(tbl_ref, xa_ref, xb_ref, mp_ref, ms_ref, mod_ref, wout_ref, g2_ref, shg_ref, shu_ref, shd_ref,
                 rwt_ref, rb_ref, base_ref, h2_ref, lpos_ref, ptok_ref, gtok_ref, cnt_ref, start_ref, rel_ref,
                 run_ref):
    i = pl.program_id(0)
    is_context = i < tbl_ref[2, 0]

    @pl.when(i == 0)
    def _():
        cnt_ref[...] = jnp.zeros_like(cnt_ref)
        start_ref[...] = jnp.zeros_like(start_ref)
        rel_ref[...] = jnp.zeros_like(rel_ref)
        run_ref[...] = jnp.zeros_like(run_ref)

    mod = mod_ref[...]
    gate1 = mod[:, 2 * D_MODEL:3 * D_MODEL]
    shift2 = mod[:, 3 * D_MODEL:4 * D_MODEL]
    scale2 = mod[:, 4 * D_MODEL:5 * D_MODEL]
    gate2 = mod[:, 5 * D_MODEL:6 * D_MODEL]
    for s in range(TS):
        rows = slice(s * TM, (s + 1) * TM)
        mix = jnp.where(is_context, mp_ref[rows, :], ms_ref[rows, :])
        x = jnp.where(is_context, xa_ref[rows, :], xb_ref[rows, :])
        x1 = x + gate1 * _dot(mix, wout_ref[...])
        ms = jnp.mean(x1 * x1, axis=-1, keepdims=True)
        h2 = x1 * lax.rsqrt(ms + NORM_EPS) * g2_ref[...]
        h2 = h2 * (1.0 + scale2) + shift2
        hb = h2.astype(BF16)
        h2_ref[rows, :] = hb
        act = _silu(_dot(hb, shg_ref[...])) * _dot(hb, shu_ref[...])
        base_ref[rows, :] = x1 + gate2 * _dot(act.astype(BF16), shd_ref[...])

        lpos, gts, cnt_pad, start = _route(h2, rwt_ref, rb_ref)
        lpos_ref[:, rows] = _slot_rows(lpos, SLOT_ROWS)
        ptok_ref[rows, :] = _slot_rows(lpos, LANES).T
        gtok_ref[rows, :] = _slot_rows(gts, LANES).T
        tile_col = lax.broadcasted_iota(I32, cnt_ref.shape, 1) == i * TS + s
        run = run_ref[...]
        cnt_ref[...] = jnp.where(tile_col, cnt_pad.astype(I32), cnt_ref[...])
        start_ref[...] = jnp.where(tile_col, start.astype(I32), start_ref[...])
        rel_ref[...] = jnp.where(tile_col, run.astype(I32), rel_ref[...])
        run_ref[...] = run + cnt_pad


def _post_call(layer, tbl, xa, xb, mix_p, mix_s, mod4, wout, g2, shg, shu, shd, rwt, rb):
    nps = mix_p.shape[0] // STEP
    n_tok = mix_p.shape[0] + mix_s.shape[0]
    nt = n_tok // TM
    ns = n_tok // STEP
    const = lambda i, t: (0, 0)
    tile = lambda i, t: (i, 0)
    gs = pltpu.PrefetchScalarGridSpec(
        num_scalar_prefetch=1,
        grid=(ns,),
        in_specs=_x_specs(nps, xa is xb) + [
            pl.BlockSpec((STEP, MIX_W), lambda i, t: (jnp.minimum(i, nps - 1), 0)),
            pl.BlockSpec((STEP, MIX_W), lambda i, t: (jnp.maximum(i - nps, 0), 0)),
            _mod_spec(layer),
            _layer_spec(layer, MIX_W, D_MODEL),
            _layer_spec(layer, 1, D_MODEL),
            _layer_spec(layer, D_MODEL, EXPERT_FF),
            _layer_spec(layer, D_MODEL, EXPERT_FF),
            _layer_spec(layer, EXPERT_FF, D_MODEL),
            _layer_spec(layer, N_EXPERTS, D_MODEL),
            _layer_spec(layer, N_EXPERTS, 1),
        ],
        out_specs=[
            pl.BlockSpec((STEP, D_MODEL), tile),
            pl.BlockSpec((STEP, D_MODEL), tile),
            pl.BlockSpec((SLOT_ROWS, STEP), lambda i, t: (0, i)),
            pl.BlockSpec((STEP, LANES), tile),
            pl.BlockSpec((STEP, LANES), tile),
            pl.BlockSpec((N_EXPERTS, LANES), const),
            pl.BlockSpec((N_EXPERTS, LANES), const),
            pl.BlockSpec((N_EXPERTS, LANES), const),
        ],
        scratch_shapes=[pltpu.VMEM((N_EXPERTS, 1), F32)],
    )
    assert nt <= LANES
    return pl.pallas_call(
        _post_kernel,
        grid_spec=gs,
        out_shape=[
            jax.ShapeDtypeStruct((n_tok, D_MODEL), F32),
            jax.ShapeDtypeStruct((n_tok, D_MODEL), BF16),
            jax.ShapeDtypeStruct((SLOT_ROWS, n_tok), F32),
            jax.ShapeDtypeStruct((n_tok, LANES), F32),
            jax.ShapeDtypeStruct((n_tok, LANES), F32),
            jax.ShapeDtypeStruct((N_EXPERTS, LANES), I32),
            jax.ShapeDtypeStruct((N_EXPERTS, LANES), I32),
            jax.ShapeDtypeStruct((N_EXPERTS, LANES), I32),
        ],
        compiler_params=_cparams(("arbitrary",)),
        name="post_route",
    )(tbl, xa, xb, mix_p, mix_s, mod4, wout, g2, shg, shu, shd, rwt, rb)


SEG_ROW0, SEG_NBLK, SEG_PAD0, SEG_NPAD, SEG_NEXT, SEG_USED = range(6)


def _plan_kernel(last_tile, cnt_ref, rel_ref, seg_ref):
    def per_expert(e, start):
        end = start + rel_ref[e, last_tile] + cnt_ref[e, last_tile]
        nb = lax.shift_right_logical(end - start + (EB - 1), LOG_EB)
        nxt = start + lax.shift_left(nb, LOG_EB)
        seg_ref[SEG_ROW0, e] = start
        seg_ref[SEG_NBLK, e] = nb
        seg_ref[SEG_PAD0, e] = end
        seg_ref[SEG_NPAD, e] = lax.shift_right_logical(nxt - end, LOG_CH)
        seg_ref[SEG_USED, e] = 0
        return nxt

    total = lax.fori_loop(0, N_EXPERTS, per_expert, jnp.int32(0))

    def link(k, nxt):
        e = N_EXPERTS - 1 - k
        seg_ref[SEG_NEXT, e] = nxt
        return jnp.where(seg_ref[SEG_NBLK, e] > 0, e, nxt)

    first = lax.fori_loop(0, N_EXPERTS, link, jnp.int32(N_EXPERTS))
    seg_ref[SEG_USED, 0] = lax.shift_right_logical(total, LOG_EB)
    seg_ref[SEG_USED, 1] = first


def _plan_call(cnt, rel, nt):
    smem = pl.BlockSpec(memory_space=pltpu.SMEM)
    return pl.pallas_call(
        functools.partial(_plan_kernel, nt - 1),
        in_specs=[smem, smem],
        out_specs=smem,
        out_shape=jax.ShapeDtypeStruct((6, N_EXPERTS), I32),
        name="moe_plan",
    )(cnt, rel)


def _rows_copy(src_ref, src_row, dst_ref, dst_row, n_rows, sem):
    return pltpu.make_async_copy(src_ref.at[pl.ds(pl.multiple_of(src_row, CH), n_rows)],
                                 dst_ref.at[pl.ds(pl.multiple_of(dst_row, CH), n_rows)], sem)


class _Runs:
    def __init__(self, cnt_ref, start_ref, rel_ref, seg_ref):
        self.cnt, self.start, self.rel, self.seg = cnt_ref, start_ref, rel_ref, seg_ref

    def n_rows(self, i):
        return self.start[N_EXPERTS - 1, i] + self.cnt[N_EXPERTS - 1, i]

    def start_copies(self, i, copy, tot_ref, slot):
        def per_pair(e2, carry):
            n_big, n_small = carry
            for par in range(2):
                e = 2 * e2 + par
                c = self.cnt[e, i]
                a0 = self.start[e, i]
                b0 = self.seg[SEG_ROW0, e] + self.rel[e, i]
                nb = lax.shift_right_logical(c, LOG_CH + 1)
                odd = jnp.bitwise_and(lax.shift_right_logical(c, LOG_CH), 1)

                def big(q, cc, a0=a0, b0=b0, par=par):
                    copy(a0 + q * (2 * CH), b0 + q * (2 * CH), 2 * CH).start(priority=par)
                    return cc

                lax.fori_loop(0, nb, big, 0)

                @pl.when(odd == 1)
                def _(a0=a0, b0=b0, nb=nb, par=par):
                    copy(a0 + nb * (2 * CH), b0 + nb * (2 * CH), CH).start(priority=par)

                n_big, n_small = n_big + nb, n_small + odd
            return n_big, n_small

        n_big, n_small = lax.fori_loop(0, N_EXPERTS // 2, per_pair, (jnp.int32(0), jnp.int32(0)))
        tot_ref[slot, 0] = n_big
        tot_ref[slot, 1] = n_small

    @staticmethod
    def wait_copies(copy, tot_ref, slot):
        def big(q, c):
            copy(0, 0, 2 * CH).wait()
            return c

        lax.fori_loop(0, tot_ref[slot, 0], big, 0)

        def small(q, c):
            copy(0, 0, CH).wait()
            return c

        lax.fori_loop(0, tot_ref[slot, 1], small, 0)


SORT_TIERS = (2048, 2304, SORT_ROWS)


def _for_sorted_rows(n_sorted, body):
    lo = -1
    for hi in SORT_TIERS:
        @pl.when(jnp.logical_and(n_sorted > lo, n_sorted <= hi))
        def _(hi=hi):
            body(hi)
        lo = hi


def _dispatch_kernel(reuse, cnt_ref, start_ref, rel_ref, seg_ref, h_ref, lpos_ref, *rest):
    xb_hbm, sort_s, zero_s, tot_s, sems = rest[1:] if reuse else rest
    i = pl.program_id(0)
    last = pl.num_programs(0) - 1
    slot = lax.rem(i, 2)
    runs = _Runs(cnt_ref, start_ref, rel_ref, seg_ref)
    def sort_rows(n_rows):
        lp = lpos_ref[...]
        hb = h_ref[...]
        blk = TM
        for r in range(n_rows // blk):
            srow = (lax.broadcasted_iota(I32, (blk, TM), 0) + r * blk).astype(F32)
            p = jnp.zeros((blk, TM), F32)
            for k in range(TOP_K):
                p = jnp.where(srow == lp[k:k + 1, :], 1.0, p)
            sort_s[slot, r * blk:(r + 1) * blk, :] = _dot(p.astype(BF16), hb).astype(BF16)

    _for_sorted_rows(runs.n_rows(i), sort_rows)

    def copy_from(sl):
        return lambda s, d, n: _rows_copy(sort_s.at[sl], s, xb_hbm, d, n, sems.at[sl])

    runs.start_copies(i, copy_from(slot), tot_s, slot)

    @pl.when(i > 0)
    def _():
        runs.wait_copies(copy_from(1 - slot), tot_s, 1 - slot)

    @pl.when(i == last)
    def _():
        runs.wait_copies(copy_from(slot), tot_s, slot)
        _zero_fill_unused(seg_ref, xb_hbm, zero_s, sems.at[0], tail=not reuse)


def _zero_fill_unused(seg_ref, buf_hbm, zero_s, sem, tail):
    zero_s[...] = jnp.zeros_like(zero_s)

    def per_expert(e, c):
        first = seg_ref[SEG_PAD0, e]

        def z_issue(r, cc):
            _rows_copy(zero_s, 0, buf_hbm, first + r * CH, CH, sem).start()
            return cc

        lax.fori_loop(0, seg_ref[SEG_NPAD, e], z_issue, 0)

        def z_drain(r, cc):
            _rows_copy(zero_s, 0, buf_hbm, 0, CH, sem).wait()
            return cc

        lax.fori_loop(0, seg_ref[SEG_NPAD, e], z_drain, 0)
        return c

    lax.fori_loop(0, N_EXPERTS, per_expert, 0)
    if tail:
        _zero_fill_tail(seg_ref, buf_hbm, zero_s, sem)


def _zero_fill_tail(seg_ref, buf_hbm, zero_s, sem):
    n_blocks = buf_hbm.shape[0] // EB

    def blk_copy(b):
        return pltpu.make_async_copy(zero_s, buf_hbm.at[pl.ds(pl.multiple_of(b * EB, EB), EB)], sem)

    def t_issue(b, cc):
        blk_copy(b).start()
        return cc

    lax.fori_loop(seg_ref[SEG_USED, 0], n_blocks, t_issue, 0)

    def t_drain(b, cc):
        blk_copy(0).wait()
        return cc

    lax.fori_loop(seg_ref[SEG_USED, 0], n_blocks, t_drain, 0)


def _dispatch_call(cnt, start, rel, seg, h2, lpos, n_rows, prev=None):
    n_tok = h2.shape[0]
    nt = n_tok // TM
    smem = pl.BlockSpec(memory_space=pltpu.SMEM)
    reuse = prev is not None
    return pl.pallas_call(
        functools.partial(_dispatch_kernel, reuse),
        grid=(nt,),
        in_specs=[
            smem, smem, smem, smem,
            pl.BlockSpec((TM, D_MODEL), lambda i: (i, 0)),
            pl.BlockSpec((SLOT_ROWS, TM), lambda i: (0, i)),
        ] + ([pl.BlockSpec(memory_space=pl.ANY)] if reuse else []),
        input_output_aliases={6: 0} if reuse else {},
        out_specs=pl.BlockSpec(memory_space=pl.ANY),
        out_shape=jax.ShapeDtypeStruct((n_rows, D_MODEL), BF16),
        scratch_shapes=[
            pltpu.VMEM((2, SORT_ROWS, D_MODEL), BF16),
            pltpu.VMEM((EB, D_MODEL), BF16),
            pltpu.SMEM((2, 2), I32),
            pltpu.SemaphoreType.DMA((2,)),
        ],
        compiler_params=_cparams(("arbitrary",)),
        name="moe_dispatch",
    )(cnt, start, rel, seg, h2, lpos, *([prev] if reuse else []))


X_SLOTS = 4
Y_SLOTS = 4


def _experts_kernel(layer, seg_ref, wg_hbm, wu_hbm, wd_hbm, xb_hbm, yb_hbm,
                    wgf_s, wuf_s, wdf_s, wg_s, wu_s, wd_s, x_s, y_s, sem_w, sem_x, sem_y):
    n_used = seg_ref[SEG_USED, 0]
    first = seg_ref[SEG_USED, 1]

    def rows(g):
        return pl.ds(pl.multiple_of(g * EB, EB), EB)

    def x_copy(g, slot):
        return pltpu.make_async_copy(xb_hbm.at[rows(g)], x_s.at[slot], sem_x.at[slot])

    def y_copy(g, slot):
        return pltpu.make_async_copy(y_s.at[slot], yb_hbm.at[rows(g)], sem_y.at[slot])

    def w_copies(e, slot):
        return [pltpu.make_async_copy(hbm.at[layer, e], buf.at[slot], sem_w.at[slot, n])
                for n, (hbm, buf) in enumerate(((wg_hbm, wgf_s), (wu_hbm, wuf_s), (wd_hbm, wdf_s)))]

    def fetch_weights(e, slot):
        for c in w_copies(e, slot):
            c.start()

    def take_weights(slot):
        for c in w_copies(0, slot):
            c.wait()
        wg_s[...] = wgf_s[slot].astype(BF16)
        wu_s[...] = wuf_s[slot].astype(BF16)
        wd_s[...] = wdf_s[slot].astype(BF16)

    def next_expert(e):
        return seg_ref[SEG_NEXT, jnp.minimum(e, N_EXPERTS - 1)]

    @pl.when(n_used > 0)
    def _():
        for p in range(X_SLOTS - 1):
            @pl.when(p < n_used)
            def _(p=p):
                x_copy(p, p).start()

        fetch_weights(first, 0)

        @pl.when(next_expert(first) < N_EXPERTS)
        def _():
            fetch_weights(next_expert(first), 1)

        take_weights(0)

        def block(g, carry):
            e, left, wslot = carry
            ahead = g + (X_SLOTS - 1)

            @pl.when(ahead < n_used)
            def _():
                x_copy(ahead, lax.rem(ahead, X_SLOTS)).start()

            xslot = lax.rem(g, X_SLOTS)
            yslot = lax.rem(g, Y_SLOTS)
            x_copy(g, xslot).wait()

            @pl.when(g >= Y_SLOTS)
            def _():
                y_copy(g - Y_SLOTS, yslot).wait()

            xb = x_s[xslot]
            act = _silu(_dot(xb, wg_s[...])) * _dot(xb, wu_s[...])
            y_s[yslot] = _dot(act.astype(BF16), wd_s[...]).astype(BF16)
            y_copy(g, yslot).start()

            switch = jnp.logical_and(left == 1, g + 1 < n_used)
            nxt = next_expert(e)

            @pl.when(switch)
            def _():
                take_weights(1 - wslot)

                @pl.when(next_expert(nxt) < N_EXPERTS)
                def _():
                    fetch_weights(next_expert(nxt), wslot)

            nxt_c = jnp.minimum(nxt, N_EXPERTS - 1)
            return (jnp.where(switch, nxt_c, e), jnp.where(switch, seg_ref[SEG_NBLK, nxt_c], left - 1),
                    jnp.where(switch, 1 - wslot, wslot))

        lax.fori_loop(0, n_used, block,
                      (first, seg_ref[SEG_NBLK, jnp.minimum(first, N_EXPERTS - 1)], jnp.int32(0)))

        for back in range(Y_SLOTS, 0, -1):
            @pl.when(n_used >= back)
            def _(back=back):
                y_copy(n_used - back, lax.rem(n_used - back, Y_SLOTS)).wait()


def _experts_call(seg, xb, layer, wg, wu, wd):
    n_rows = xb.shape[0]
    hbm = pl.BlockSpec(memory_space=pl.ANY)
    return pl.pallas_call(
        functools.partial(_experts_kernel, layer),
        in_specs=[pl.BlockSpec(memory_space=pltpu.SMEM), hbm, hbm, hbm, hbm],
        out_specs=hbm,
        out_shape=jax.ShapeDtypeStruct((n_rows, D_MODEL), BF16),
        input_output_aliases={4: 0},
        scratch_shapes=[
            pltpu.VMEM((2, D_MODEL, EXPERT_FF), F32),
            pltpu.VMEM((2, D_MODEL, EXPERT_FF), F32),
            pltpu.VMEM((2, EXPERT_FF, D_MODEL), F32),
            pltpu.VMEM((D_MODEL, EXPERT_FF), BF16),
            pltpu.VMEM((D_MODEL, EXPERT_FF), BF16),
            pltpu.VMEM((EXPERT_FF, D_MODEL), BF16),
            pltpu.VMEM((X_SLOTS, EB, D_MODEL), BF16),
            pltpu.VMEM((Y_SLOTS, EB, D_MODEL), BF16),
            pltpu.SemaphoreType.DMA((2, 3)),
            pltpu.SemaphoreType.DMA((X_SLOTS,)),
            pltpu.SemaphoreType.DMA((Y_SLOTS,)),
        ],
        compiler_params=pltpu.CompilerParams(vmem_limit_bytes=VMEM_LIMIT),
        name="moe_experts",
    )(seg, wg, wu, wd, xb)


def _combine_kernel(final, tbl_ref, cnt_ref, start_ref, rel_ref, seg_ref, yb_hbm, base_ref, gtok_ref, ptok_ref,
                    mod_ref, gf_ref, *rest):
    *out_refs, sort_s, routed_s, tot_s, sems = rest
    i = pl.program_id(0)
    slot = lax.rem(i, 2)
    runs = _Runs(cnt_ref, start_ref, rel_ref, seg_ref)

    def copy_to(sl):
        return lambda s, d, n: _rows_copy(yb_hbm, d, sort_s.at[sl], s, n, sems.at[sl])

    @pl.when(i == 0)
    def _():
        sort_s[...] = jnp.zeros_like(sort_s)
        runs.start_copies(i, copy_to(slot), tot_s, slot)

    @pl.when(i + 1 < pl.num_programs(0))
    def _():
        runs.start_copies(i + 1, copy_to(1 - slot), tot_s, 1 - slot)

    runs.wait_copies(copy_to(slot), tot_s, slot)

    def unsort(n_rows):
        gt = gtok_ref[...]
        pt = ptok_ref[...]
        col = lax.broadcasted_iota(I32, (TM, n_rows), 1).astype(F32)
        w = jnp.zeros((TM, n_rows), F32)
        for k in range(TOP_K):
            w = jnp.where(col == pt[:, k:k + 1], gt[:, k:k + 1], w)
        routed_s[...] = _dot(w.astype(BF16), sort_s[slot, 0:n_rows, :])

    _for_sorted_rows(runs.n_rows(i), unsort)
    gate2 = mod_ref[...][:, 5 * D_MODEL:6 * D_MODEL]
    y = base_ref[...] + gate2 * routed_s[...]
    if final:
        y = y * lax.rsqrt(jnp.mean(y * y, axis=-1, keepdims=True) + NORM_EPS) * gf_ref[...]
        yp_ref, ys_ref = out_refs
        is_context = i < tbl_ref[2, 0]

        @pl.when(is_context)
        def _():
            yp_ref[...] = y

        @pl.when(jnp.logical_not(is_context))
        def _():
            ys_ref[...] = y
    else:
        out_refs[0][...] = y


def _combine_call(final, layer, npt, tbl, cnt, start, rel, seg, yb, base, gtok, ptok, mod4, gfinal):
    n_tok = base.shape[0]
    nt = n_tok // TM
    tile = lambda i, t: (i, 0)
    smem = pl.BlockSpec(memory_space=pltpu.SMEM)
    if final:
        out_specs = [pl.BlockSpec((TM, D_MODEL), lambda i, t: (jnp.minimum(i, npt - 1), 0)),
                     pl.BlockSpec((TM, D_MODEL), lambda i, t: (jnp.maximum(i - npt, 0), 0))]
        out_shape = [jax.ShapeDtypeStruct((npt * TM, D_MODEL), F32),
                     jax.ShapeDtypeStruct((n_tok - npt * TM, D_MODEL), F32)]
    else:
        out_specs = pl.BlockSpec((TM, D_MODEL), tile)
        out_shape = jax.ShapeDtypeStruct((n_tok, D_MODEL), F32)
    gs = pltpu.PrefetchScalarGridSpec(
        num_scalar_prefetch=1,
        grid=(nt,),
        in_specs=[
            smem, smem, smem, smem,
            pl.BlockSpec(memory_space=pl.ANY),
            pl.BlockSpec((TM, D_MODEL), tile),
            pl.BlockSpec((TM, LANES), tile),
            pl.BlockSpec((TM, LANES), tile),
            _mod_spec(layer),
            pl.BlockSpec((1, D_MODEL), lambda i, t: (0, 0)),
        ],
        out_specs=out_specs,
        scratch_shapes=[
            pltpu.VMEM((2, SORT_ROWS, D_MODEL), BF16),
            pltpu.VMEM((TM, D_MODEL), F32),
            pltpu.SMEM((2, 2), I32),
            pltpu.SemaphoreType.DMA((2,)),
        ],
    )
    return pl.pallas_call(
        functools.partial(_combine_kernel, final),
        grid_spec=gs,
        out_shape=out_shape,
        compiler_params=_cparams(("arbitrary",)),
        name="moe_combine",
    )(tbl, cnt, start, rel, seg, yb, base, gtok, ptok, mod4, gfinal)


def _pad_cols(w, groups, width, slot):
    lead = w.shape[:-1]
    w = w.reshape(*lead, groups, width)
    pad = [(0, 0)] * (len(lead) + 1) + [(0, slot - width)]
    return jnp.pad(w, pad).reshape(*lead, groups * slot)


def _prep_w_in(w):
    assert w.shape[-1] == C_KPE + MLA_ROPE_DIM
    return jnp.pad(w.astype(BF16), ((0, 0), (0, 0), (0, N_PRE - w.shape[-1])))


def _rope_tables(n_pos, dim, lane_offsets):
    f32 = np.float32
    n_rows = n_pos // GRID_W
    row = np.repeat(np.arange(n_rows, dtype=f32), GRID_W)
    col = np.tile(np.arange(GRID_W, dtype=f32), n_rows)
    half = dim // 2
    freqs = f32(ROPE_THETA) ** (-np.arange(0, half, 2, dtype=f32) / f32(half))
    ar = row[:, None] * freqs[None, :]
    ac = col[:, None] * freqs[None, :]
    ang = np.concatenate([ar, ar, ac, ac], axis=-1).astype(f32)
    cos, sin = np.cos(ang), np.sin(ang)
    first = (np.arange(dim) % 16) < 8
    sa = np.where(first[None, :], -sin, f32(0))
    sb = np.where(first[None, :], f32(0), sin)
    c_t = np.ones((STEP + n_pos, SLOT), f32)
    a_t = np.zeros((STEP + n_pos, SLOT), f32)
    b_t = np.zeros((STEP + n_pos, SLOT), f32)
    for off in lane_offsets:
        c_t[STEP:, off:off + dim] = cos
        a_t[STEP:, off:off + dim] = sa
        b_t[STEP:, off:off + dim] = sb
    return tuple(jnp.asarray(t) for t in (c_t, a_t, b_t))


def _block_diag_states(s):
    b, l, h, dk, dv = s.shape
    eye = jnp.eye(h, dtype=s.dtype)
    return jnp.einsum('blhkv,hg->blhkgv', s, eye).reshape(b, l, h * dk, h * dv)


def kernel(x_prompt, x_sample, cache_diff_k, cache_diff_v, cache_mla_ckv, cache_mla_kpe, state_ret_fwd, state_ret_bwd, c, c_ctx, w_ada, b_ada, norm_mix, norm_ffn, norm_final, w_in, ret_decay_fwd, ret_decay_bwd, diff_lambda, diff_subln, mla_q_norm, mla_w_uq, mla_kv_norm, mla_w_ukv, w_out, router_w, router_bias, exp_w_gate, exp_w_up, exp_w_down, sh_w_gate, sh_w_up, sh_w_down):
    n_pb, p_len, _ = x_prompt.shape
    n_sb, s_len, _ = x_sample.shape
    past_len = cache_diff_k.shape[3]
    n_p = n_pb * p_len
    n_s = n_sb * s_len
    n_tok = n_p + n_s
    nt = n_tok // TM
    npt = n_p // TM
    assert p_len == TM and s_len % TM == 0 and n_p % s_len == 0 and past_len % 8 == 0

    def step_tables(rows):
        steps = np.arange(n_tok // rows)
        n_ctx = n_p // rows
        per_seq = s_len // rows
        mod_row = np.where(steps < n_ctx, n_sb, (steps - n_ctx) // per_seq)
        rope_blk = np.where(steps < n_ctx, 0, 1 + (steps - n_ctx) % per_seq)
        return jnp.asarray(np.stack([mod_row, rope_blk, np.full(len(steps), n_ctx)]).astype(np.int32))

    assert n_p % STEP == 0 and s_len % STEP == 0
    tbl = step_tables(TM)
    tbl_step = step_tables(STEP)

    n_cond = 16
    cond = jnp.concatenate([c, c_ctx[None, :], jnp.zeros((n_cond - n_sb - 1, D_MODEL), F32)], axis=0)
    mod_all = _modulation(cond, w_ada, b_ada)

    rope_d = _rope_tables(s_len, DIFF_QK_DIM, (0, DIFF_QK_DIM))
    rope_m = _rope_tables(s_len, MLA_ROPE_DIM, (KR_LO,))
    place = np.zeros((MLA_ROPE_DIM, SLOT), np.float32)
    place[np.arange(MLA_ROPE_DIM), KR_LO + np.arange(MLA_ROPE_DIM)] = 1.0
    place = jnp.asarray(place, BF16)
    cache_dv_t = cache_diff_v.transpose(0, 1, 3, 2, 4).reshape(n_sb, DEPTH, past_len, DIFF_V_W)
    s0f_bd = _block_diag_states(state_ret_fwd)
    s0b_bd = _block_diag_states(state_ret_bwd)

    n_blocks = pl.cdiv(n_tok * TOP_K + nt * N_EXPERTS * (CH - 1) + N_EXPERTS * (EB - CH), EB)
    n_rows = n_blocks * EB

    xa, xb = x_prompt.reshape(n_p, D_MODEL), x_sample.reshape(n_s, D_MODEL)
    gfinal = norm_final.reshape(1, D_MODEL)

    mod4 = mod_all.reshape(DEPTH, n_cond, 1, 6 * D_MODEL)
    w_pre = _prep_w_in(w_in)
    wuq = _pad_cols(mla_w_uq, MLA_HEADS, MLA_NOPE_DIM + MLA_ROPE_DIM, SLOT).astype(BF16)
    ukv = mla_w_ukv.reshape(DEPTH, MLA_KV_LORA, MLA_HEADS, MLA_NOPE_DIM + MLA_V_DIM)
    wk = _pad_cols(ukv[..., :MLA_NOPE_DIM].reshape(DEPTH, MLA_KV_LORA, -1), MLA_HEADS, MLA_NOPE_DIM, SLOT).astype(BF16)
    wv = ukv[..., MLA_NOPE_DIM:].reshape(DEPTH, MLA_KV_LORA, MLA_V_W).astype(BF16)
    decf = jnp.repeat(ret_decay_fwd, RET_DIM, axis=-1).reshape(DEPTH, 1, RET_W)
    decb = jnp.repeat(ret_decay_bwd, RET_DIM, axis=-1).reshape(DEPTH, 1, RET_W)
    subln = jnp.tile(diff_subln, (1, DIFF_HEADS)).reshape(DEPTH, 1, DIFF_V_W)
    vec = lambda p: p.reshape(DEPTH, 1, -1)
    wout_b, shg_b, shu_b, shd_b = (w.astype(BF16) for w in (w_out, sh_w_gate, sh_w_up, sh_w_down))
    rwt = router_w.transpose(0, 2, 1)
    rb = router_bias.reshape(DEPTH, N_EXPERTS, 1)

    new_ctx = []
    sorted_buf = None
    for l in range(DEPTH):
        lam_init = 0.8 - 0.6 * math.exp(-0.3 * l)
        ra, qa, ka, *caches = _pre_call(l, tbl_step, xa, xb, npt, n_tok, mod4, vec(norm_mix), w_pre, vec(mla_q_norm),
                                        wuq, vec(mla_kv_norm), rope_d, rope_m)
        mix_p, *states = _mix_prompt_call(lam_init, l, qa, ka, ra, n_pb, p_len, decf, decb,
                                          diff_lambda, subln, wk, wv)
        new_ctx.append(caches + states)
        mix_s = _mix_sample_call(lam_init, l, qa, ka, ra, n_p, n_sb, s_len, past_len, cache_diff_k,
                                 cache_dv_t, cache_mla_ckv, cache_mla_kpe, s0f_bd, s0b_bd, decf, decb,
                                 diff_lambda, subln, wk, wv, place)
        base, h2, lpos, ptok, gtok, cnt, start, rel = _post_call(
            l, tbl_step, xa, xb, mix_p, mix_s, mod4, wout_b, vec(norm_ffn), shg_b, shu_b, shd_b, rwt, rb)
        seg = _plan_call(cnt, rel, nt)
        sorted_buf = _dispatch_call(cnt, start, rel, seg, h2, lpos, n_rows, prev=sorted_buf)
        sorted_buf = _experts_call(seg, sorted_buf, l, exp_w_gate, exp_w_up, exp_w_down)
        final = l == DEPTH - 1
        out = _combine_call(final, l, npt, tbl, cnt, start, rel, seg, sorted_buf, base, gtok, ptok, mod4, gfinal)
        xa, xb = out if final else (out, out)

    y_prompt = xa.reshape(n_pb, p_len, D_MODEL)
    y_sample = xb.reshape(n_sb, s_len, D_MODEL)
    return (y_prompt, y_sample, *(jnp.stack(per_layer, axis=1) for per_layer in zip(*new_ctx)))
```

```python
import functools
import math

import numpy as np
import jax
import jax.numpy as jnp
from jax import lax
from jax.experimental import pallas as pl
from jax.experimental.pallas import tpu as pltpu

F32 = jnp.float32
BF16 = jnp.bfloat16
I32 = jnp.int32

D_MODEL = 1024
DEPTH = 2
GRID_W = 64
ROPE_THETA = 10000.0
NORM_EPS = 1e-6

RET_HEADS = 4
RET_DIM = 64
RET_CHUNK = 128
RET_W = RET_HEADS * RET_DIM
DIFF_HEADS = 4
DIFF_QK_DIM = 32
DIFF_V_DIM = 64
DIFF_V_W = DIFF_HEADS * DIFF_V_DIM
MLA_HEADS = 8
MLA_Q_LORA = 256
MLA_KV_LORA = 128
MLA_NOPE_DIM = 64
MLA_ROPE_DIM = 32
MLA_V_DIM = 64
MLA_V_W = MLA_HEADS * MLA_V_DIM
MIX_W = RET_W + DIFF_V_W + MLA_V_W

N_EXPERTS = 64
TOP_K = 6
N_GROUPS = 8
GROUP_SIZE = N_EXPERTS // N_GROUPS
TOPK_GROUPS = 4
EXPERT_FF = 256
ROUTED_SCALE = 2.5

LANES = 128
SLOT = LANES
TM = 256
TS = 2
STEP = TS * TM
SLOT_ROWS = 8
EB = 512
LOG_EB = 9
CH = 16
LOG_CH = 4
SORT_ROWS = 2560
VMEM_LIMIT = 48 * 1024 * 1024

C_RQ, C_RK, C_RV, C_RG = 0, 256, 512, 768
C_DQ, C_DK, C_DV, C_CQ, C_CKV, C_KPE = 1024, 1280, 1536, 1792, 2048, 2176
N_PRE = 2304
QA_W = 4 * SLOT + MLA_HEADS * SLOT
KA_DK, KA_DV, KA_CKV, KA_KPE = 0, 512, 768, 896
KA_W = 1024
KR_LO, KR_HI = 64, 96


def _dot(a, b):
    return jnp.dot(a, b, preferred_element_type=F32)


def _dot_nt(a, b):
    return lax.dot_general(a, b, (((1,), (1,)), ((), ())), preferred_element_type=F32)


def _dot_tn(a, b):
    return lax.dot_general(a, b, (((0,), (0,)), ((), ())), preferred_element_type=F32)


def _split_dot(x, w_bf16):
    hi = x.astype(BF16)
    lo = (x - hi.astype(F32)).astype(BF16)
    return _dot(hi, w_bf16) + _dot(lo, w_bf16)


def _split_dot_nt(w, x):
    wh = w.astype(BF16)
    wl = (w - wh.astype(F32)).astype(BF16)
    xh = x.astype(BF16)
    xl = (x - xh.astype(F32)).astype(BF16)
    return _dot_nt(wh, xh) + _dot_nt(wh, xl) + _dot_nt(wl, xh)


def _silu(x):
    return x * jax.nn.sigmoid(x)


def _cparams(sem):
    return pltpu.CompilerParams(dimension_semantics=sem, vmem_limit_bytes=VMEM_LIMIT)


MOD_TN = 512


def _mod_kernel(c_ref, w_ref, b_ref, o_ref):
    s = _silu(c_ref[...])
    o_ref[...] = _split_dot3(s, w_ref[...]) + b_ref[...]


def _split_dot3(x, w):
    xh = x.astype(BF16)
    xl = (x - xh.astype(F32)).astype(BF16)
    wh = w.astype(BF16)
    wl = (w - wh.astype(F32)).astype(BF16)
    return _dot(xh, wh) + _dot(xh, wl) + _dot(xl, wh)


def _modulation(cond, w_ada, b_ada):
    n_rows = cond.shape[0]
    n_out = w_ada.shape[-1]
    return pl.pallas_call(
        _mod_kernel,
        grid=(DEPTH, n_out // MOD_TN),
        in_specs=[
            pl.BlockSpec((n_rows, D_MODEL), lambda l, j: (0, 0)),
            pl.BlockSpec((None, D_MODEL, MOD_TN), lambda l, j: (l, 0, j)),
            pl.BlockSpec((None, 1, MOD_TN), lambda l, j: (l, 0, j)),
        ],
        out_specs=pl.BlockSpec((None, n_rows, MOD_TN), lambda l, j: (l, 0, j)),
        out_shape=jax.ShapeDtypeStruct((DEPTH, n_rows, n_out), F32),
        compiler_params=_cparams(("arbitrary", "arbitrary")),
        name="adaln_mod",
    )(cond, w_ada, b_ada.reshape(DEPTH, 1, n_out))


def _rope_slot(x, cos, sa, sb):
    up = pltpu.roll(x, LANES - 8, 1)
    dn = pltpu.roll(x, 8, 1)
    return x * cos + up * sa + dn * sb


def _x_specs(nps, combined):
    off = 0 if combined else nps
    return [pl.BlockSpec((STEP, D_MODEL), lambda i, t: (jnp.minimum(i, nps - 1), 0)),
            pl.BlockSpec((STEP, D_MODEL), lambda i, t: (jnp.maximum(i, nps) - off, 0))]


def _head_slots(x):
    half = SLOT // 2
    low = _lane_iota((x.shape[0], SLOT)) < half
    slots = []
    for c in range(x.shape[1] // SLOT):
        pair = x[:, c * SLOT:(c + 1) * SLOT]
        slots.append(jnp.where(low, pair, 0.0))
        slots.append(jnp.where(low, pltpu.roll(pair, half, 1), 0.0))
    return slots


def _pre_kernel(tbl_ref, xa_ref, xb_ref, mod_ref, g_ref, w_ref, qg_ref, wuq_ref, kvg_ref,
                cd_ref, sad_ref, sbd_ref, cm_ref, sam_ref, sbm_ref,
                ra_ref, qa_ref, ka_ref, dkc_ref, dvc_ref, ckvc_ref, kpec_ref):
    is_context = pl.program_id(0) < tbl_ref[2, 0]
    mod = mod_ref[...]
    shift1 = mod[:, 0:D_MODEL]
    scale1 = mod[:, D_MODEL:2 * D_MODEL]
    for s in range(TS):
        rows = slice(s * TM, (s + 1) * TM)
        x = jnp.where(is_context, xa_ref[rows, :], xb_ref[rows, :])
        ms = jnp.mean(x * x, axis=-1, keepdims=True)
        h = x * lax.rsqrt(ms + NORM_EPS) * g_ref[...]
        h = h * (1.0 + scale1) + shift1
        hb = h.astype(BF16)

        def proj(lo, hi, hb=hb):
            return _dot(hb, w_ref[:, lo:hi])

        ra_ref[rows, C_RQ:C_RK] = proj(C_RQ, C_RK).astype(BF16)
        ra_ref[rows, C_RK:C_RV] = (proj(C_RK, C_RV) * (RET_DIM ** -0.5)).astype(BF16)
        ra_ref[rows, C_RV:C_RG] = proj(C_RV, C_RG).astype(BF16)
        ra_ref[rows, C_RG:C_DQ] = _silu(proj(C_RG, C_DQ)).astype(BF16)

        cd, sad, sbd = cd_ref[rows, :], sad_ref[rows, :], sbd_ref[rows, :]
        cm, sam, sbm = cm_ref[rows, :], sam_ref[rows, :], sbm_ref[rows, :]
        dq_slots = _head_slots(proj(C_DQ, C_DK))
        dk_slots = [_rope_slot(v, cd, sad, sbd) for v in _head_slots(proj(C_DK, C_DV))]
        for hd in range(DIFF_HEADS):
            qa_ref[rows, hd * SLOT:(hd + 1) * SLOT] = _rope_slot(dq_slots[hd], cd, sad, sbd).astype(BF16)
            ka_ref[rows, KA_DK + hd * SLOT:KA_DK + (hd + 1) * SLOT] = dk_slots[hd].astype(BF16)
        dv = proj(C_DV, C_CQ)
        ka_ref[rows, KA_DV:KA_CKV] = dv.astype(BF16)

        cq = proj(C_CQ, C_CKV)
        cqn = cq * lax.rsqrt(jnp.mean(cq * cq, axis=-1, keepdims=True) + NORM_EPS) * qg_ref[...]
        qm = _dot(cqn.astype(BF16), wuq_ref[...])
        for hd in range(MLA_HEADS):
            sl = slice(hd * SLOT, (hd + 1) * SLOT)
            qa_ref[rows, 4 * SLOT + hd * SLOT:4 * SLOT + (hd + 1) * SLOT] = _rope_slot(qm[:, sl], cm, sam, sbm).astype(BF16)

        ckv = proj(C_CKV, C_KPE)
        ckvn = ckv * lax.rsqrt(jnp.mean(ckv * ckv, axis=-1, keepdims=True) + NORM_EPS) * kvg_ref[...]
        kpe = proj(C_KPE, N_PRE)
        kpe_slot = _rope_slot(kpe + pltpu.roll(kpe, SLOT // 2, 1), cm, sam, sbm)
        ka_ref[rows, KA_CKV:KA_KPE] = ckvn.astype(BF16)
        ka_ref[rows, KA_KPE:KA_W] = kpe_slot.astype(BF16)

        @pl.when(is_context)
        def _(s=s, dk_slots=dk_slots, dv=dv, ckvn=ckvn, kpe_slot=kpe_slot):
            dv_slots = _head_slots(dv)
            for hd in range(DIFF_HEADS):
                dkc_ref[s, hd] = dk_slots[hd][:, 0:2 * DIFF_QK_DIM]
                dvc_ref[s, hd] = dv_slots[hd][:, 0:DIFF_V_DIM]
            ckvc_ref[s] = ckvn
            kpec_ref[s] = kpe_slot[:, 0:MLA_ROPE_DIM]


def _layer_spec(layer, rows, cols):
    return pl.BlockSpec((None, rows, cols), lambda *_: (layer, 0, 0))


def _mod_spec(layer):
    return pl.BlockSpec((None, None, 1, 6 * D_MODEL), lambda i, t: (layer, t[0, i], 0, 0))


def _pre_call(layer, tbl, xa, xb, npt, n_tok, mod4, g, w_pre, qg, wuq, kvg, rope_d, rope_m):
    ns = n_tok // STEP
    nps = npt // TS
    tile = lambda i, t: (i, 0)
    rope = lambda i, t: (t[1, i], 0)
    ctx5 = lambda i, t: (jnp.minimum(i, nps - 1), 0, 0, 0)
    ctx4 = lambda i, t: (jnp.minimum(i, nps - 1), 0, 0)
    cache_shapes = [
        jax.ShapeDtypeStruct((npt, DIFF_HEADS, TM, 2 * DIFF_QK_DIM), F32),
        jax.ShapeDtypeStruct((npt, DIFF_HEADS, TM, DIFF_V_DIM), F32),
        jax.ShapeDtypeStruct((npt, TM, MLA_KV_LORA), F32),
        jax.ShapeDtypeStruct((npt, TM, MLA_ROPE_DIM), F32),
    ]
    gs = pltpu.PrefetchScalarGridSpec(
        num_scalar_prefetch=1,
        grid=(ns,),
        in_specs=_x_specs(nps, xa is xb) + [
            _mod_spec(layer),
            _layer_spec(layer, 1, D_MODEL),
            _layer_spec(layer, D_MODEL, N_PRE),
            _layer_spec(layer, 1, MLA_Q_LORA),
            _layer_spec(layer, MLA_Q_LORA, MLA_HEADS * SLOT),
            _layer_spec(layer, 1, MLA_KV_LORA),
        ] + [pl.BlockSpec((STEP, SLOT), rope)] * 6,
        out_specs=[
            pl.BlockSpec((STEP, D_MODEL), tile),
            pl.BlockSpec((STEP, QA_W), tile),
            pl.BlockSpec((STEP, KA_W), tile),
            pl.BlockSpec((TS, DIFF_HEADS, TM, 2 * DIFF_QK_DIM), ctx5),
            pl.BlockSpec((TS, DIFF_HEADS, TM, DIFF_V_DIM), ctx5),
            pl.BlockSpec((TS, TM, MLA_KV_LORA), ctx4),
            pl.BlockSpec((TS, TM, MLA_ROPE_DIM), ctx4),
        ],
    )
    return pl.pallas_call(
        _pre_kernel,
        grid_spec=gs,
        out_shape=[
            jax.ShapeDtypeStruct((n_tok, D_MODEL), BF16),
            jax.ShapeDtypeStruct((n_tok, QA_W), BF16),
            jax.ShapeDtypeStruct((n_tok, KA_W), BF16),
        ] + cache_shapes,
        compiler_params=_cparams(("arbitrary",)),
        name="pre_proj",
    )(tbl, xa, xb, mod4, g, w_pre, qg, wuq, kvg, *rope_d, *rope_m)


def _lane_iota(shape):
    return lax.broadcasted_iota(I32, shape, len(shape) - 1)


def _head_mask(n_rows, width, head, head_w):
    lane = _lane_iota((n_rows, width))
    return (lane >= head * head_w) & (lane < (head + 1) * head_w)


def _seg_mean_sq(o, bd_ones):
    return _split_dot(o * o, bd_ones) * (1.0 / RET_DIM)


def _block_diag_ones(n, blk):
    r = lax.broadcasted_iota(I32, (n, n), 0) // blk
    c = lax.broadcasted_iota(I32, (n, n), 1) // blk
    return r == c


def _retention(ra_ref, seq_len, decf_ref, decb_ref, s0f, s0b):
    C = RET_CHUNK
    nc = seq_len // C
    lgf = -jnp.exp(decf_ref[...])
    lgb = -jnp.exp(decb_ref[...])
    pos = lax.broadcasted_iota(I32, (C, RET_W), 0).astype(F32)
    qdf = jnp.exp((pos + 1.0) * lgf)
    kdf = jnp.exp((C - 1.0 - pos) * lgf)
    cdf = jnp.exp(float(C) * lgf)
    qdb = jnp.exp((C - pos) * lgb)
    kdb = jnp.exp(pos * lgb)
    cdb = jnp.exp(float(C) * lgb)
    ii = lax.broadcasted_iota(I32, (C, C), 0).astype(F32)
    jj = lax.broadcasted_iota(I32, (C, C), 1).astype(F32)
    dist = ii - jj
    dmats = []
    for hd in range(RET_HEADS):
        lf = lgf[:, hd * RET_DIM:hd * RET_DIM + 1]
        lb = lgb[:, hd * RET_DIM:hd * RET_DIM + 1]
        dmats.append(jnp.where(dist >= 0, jnp.exp(dist * lf), jnp.exp(-dist * lb)))
    bd = _block_diag_ones(RET_W, RET_DIM)
    bd_ones = jnp.where(bd, 1.0, 0.0).astype(BF16)

    def chunk(n):
        rows = slice(n * C, (n + 1) * C)
        return (ra_ref[rows, C_RQ:C_RK], ra_ref[rows, C_RK:C_RV], ra_ref[rows, C_RV:C_RG])

    cross = [None] * nc
    sf = s0f
    for n in range(nc):
        q, k, v = chunk(n)
        cross[n] = _dot((q * qdf).astype(BF16), sf.astype(BF16))
        kv = _dot_tn((k * kdf).astype(BF16), v.astype(BF16))
        sf = sf * cdf + jnp.where(bd, kv, 0.0)
    sb = s0b
    for n in range(nc - 1, -1, -1):
        q, k, v = chunk(n)
        cross[n] = cross[n] + _dot((q * qdb).astype(BF16), sb.astype(BF16))
        kv = _dot_tn((k * kdb).astype(BF16), v.astype(BF16))
        sb = sb * cdb + jnp.where(bd, kv, 0.0)

    outs = []
    for n in range(nc):
        q, k, v = chunk(n)
        kb = k.astype(BF16)
        vb = v.astype(BF16)
        o = cross[n]
        for hd in range(RET_HEADS):
            hm = _head_mask(C, RET_W, hd, RET_DIM)
            sc = _dot_nt(jnp.where(hm, q, 0.0).astype(BF16), kb) * dmats[hd]
            o = o + jnp.where(hm, _dot(sc.astype(BF16), vb), 0.0)
        on = o * lax.rsqrt(_seg_mean_sq(o, bd_ones) + NORM_EPS)
        outs.append(on * ra_ref[n * C:(n + 1) * C, C_RG:C_DQ])
    return outs, sf, sb


def _softmax_pv(s_parts, v_parts, scale):
    scale = scale * math.log2(math.e)
    m = None
    for s in s_parts:
        mm = jnp.max(s, axis=-1, keepdims=True)
        m = mm if m is None else jnp.maximum(m, mm)
    m = m * scale
    acc = None
    den = None
    for s, v in zip(s_parts, v_parts):
        e = jnp.exp2(s * scale - m)
        ds = jnp.sum(e, axis=-1, keepdims=True)
        pv = _dot(e.astype(BF16), v)
        acc = pv if acc is None else acc + pv
        den = ds if den is None else den + ds
    return acc / den


def _diff_attention(dq, k_parts, v_parts, lam, subln, lam_init, bd_ones):
    lq = dq.shape[0]
    scale = DIFF_QK_DIM ** -0.5
    lane = _lane_iota((lq, SLOT))
    out = jnp.zeros((lq, DIFF_V_W), F32)
    for hd in range(DIFF_HEADS):
        qh = dq[:, hd * SLOT:(hd + 1) * SLOT]
        q1 = jnp.where(lane < DIFF_QK_DIM, qh, 0.0).astype(BF16)
        q2 = jnp.where(lane >= DIFF_QK_DIM, qh, 0.0).astype(BF16)
        q12 = jnp.concatenate([q1, q2], axis=0)
        s12 = [_dot_nt(q12[:, :kp[hd].shape[1]], kp[hd]) for kp in k_parts]
        o12 = _softmax_pv(s12, v_parts, scale)
        o = o12[:lq] - lam * o12[lq:]
        out = jnp.where(_head_mask(lq, DIFF_V_W, hd, DIFF_V_DIM), o, out)
    on = out * lax.rsqrt(_seg_mean_sq(out, bd_ones) + NORM_EPS) * subln
    return on * (1.0 - lam_init)


def _mla_attention(qm, k_parts, v_parts):
    lq = qm.shape[0]
    scale = (MLA_NOPE_DIM + MLA_ROPE_DIM) ** -0.5
    halves = []
    for g in range(2):
        out = jnp.zeros((lq, 256), F32)
        for hh in range(4):
            hd = 4 * g + hh
            qh = qm[:, hd * SLOT:(hd + 1) * SLOT].astype(BF16)
            s = [_dot_nt(qh, kp[:, hd * SLOT:(hd + 1) * SLOT]) for kp in k_parts]
            o = _softmax_pv(s, [vp[:, 256 * g:256 * (g + 1)] for vp in v_parts], scale)
            out = jnp.where(_head_mask(lq, 256, hh, MLA_V_DIM), o, out)
        halves.append(out)
    return halves


def _mla_keys(ka_val_ckv, kr_slot, wk_ref, wv_ref):
    cb = ka_val_ckv.astype(BF16)
    kn = _dot(cb, wk_ref[...])
    ks = [(kn[:, hd * SLOT:(hd + 1) * SLOT] + kr_slot).astype(BF16) for hd in range(MLA_HEADS)]
    return jnp.concatenate(ks, axis=1), _dot(cb, wv_ref[...]).astype(BF16)


def _kr_only(kpe_slot):
    lane = _lane_iota(kpe_slot.shape)
    return jnp.where((lane >= KR_LO) & (lane < KR_HI), kpe_slot, 0.0)


def _diff_lambda(dl_ref, lam_init):
    dl = dl_ref[...]
    a = jnp.sum(dl[0:1] * dl[1:2], axis=-1, keepdims=True)
    b = jnp.sum(dl[2:3] * dl[3:4], axis=-1, keepdims=True)
    return jnp.exp(a) - jnp.exp(b) + lam_init


def _mix_prompt_kernel(lam_init, qa_ref, ka_ref, ra_ref, decf_ref, decb_ref, dl_ref, subln_ref,
                       wk_ref, wv_ref, mix_ref, sf_ref, sb_ref):
    seq = qa_ref.shape[0]
    zero_state = jnp.zeros((RET_W, RET_W), F32)
    outs, sf, sb = _retention(ra_ref, seq, decf_ref, decb_ref, zero_state, zero_state)
    for n, o in enumerate(outs):
        mix_ref[n * RET_CHUNK:(n + 1) * RET_CHUNK, 0:RET_W] = o.astype(BF16)
    for hd in range(RET_HEADS):
        blk = slice(hd * RET_DIM, (hd + 1) * RET_DIM)
        sf_ref[hd] = sf[blk, blk]
        sb_ref[hd] = sb[blk, blk]

    bd_ones = jnp.where(_block_diag_ones(DIFF_V_W, DIFF_V_DIM), 1.0, 0.0).astype(BF16)
    lam = _diff_lambda(dl_ref, lam_init)
    kd = [ka_ref[:, KA_DK + hd * SLOT:KA_DK + (hd + 1) * SLOT].astype(BF16) for hd in range(DIFF_HEADS)]
    vd = ka_ref[:, KA_DV:KA_CKV].astype(BF16)
    mix_ref[:, RET_W:RET_W + DIFF_V_W] = _diff_attention(
        qa_ref[:, 0:4 * SLOT], [kd], [vd], lam, subln_ref[...], lam_init, bd_ones).astype(BF16)

    km, vm = _mla_keys(ka_ref[:, KA_CKV:KA_KPE], _kr_only(ka_ref[:, KA_KPE:KA_W]), wk_ref, wv_ref)
    halves = _mla_attention(qa_ref[:, 4 * SLOT:QA_W], [km], [vm])
    mix_ref[:, 512:768] = halves[0].astype(BF16)
    mix_ref[:, 768:1024] = halves[1].astype(BF16)


def _mixer_param_specs(layer):
    return [
        _layer_spec(layer, 1, RET_W),
        _layer_spec(layer, 1, RET_W),
        _layer_spec(layer, 4, DIFF_QK_DIM),
        _layer_spec(layer, 1, DIFF_V_W),
        _layer_spec(layer, MLA_KV_LORA, MLA_HEADS * SLOT),
        _layer_spec(layer, MLA_KV_LORA, MLA_V_W),
    ]


def _mix_prompt_call(lam_init, layer, qa, ka, ra, n_seq, seq_len, decf, decb, dl, subln, wk, wv):
    seq = lambda b: (b, 0)
    state_spec = pl.BlockSpec((None, RET_HEADS, RET_DIM, RET_DIM), lambda b: (b, 0, 0, 0))
    state_shape = jax.ShapeDtypeStruct((n_seq, RET_HEADS, RET_DIM, RET_DIM), F32)
    return pl.pallas_call(
        functools.partial(_mix_prompt_kernel, lam_init),
        grid=(n_seq,),
        in_specs=[
            pl.BlockSpec((seq_len, QA_W), seq),
            pl.BlockSpec((seq_len, KA_W), seq),
            pl.BlockSpec((seq_len, D_MODEL), seq),
        ] + _mixer_param_specs(layer),
        out_specs=[pl.BlockSpec((seq_len, MIX_W), seq), state_spec, state_spec],
        out_shape=[jax.ShapeDtypeStruct((n_seq * seq_len, MIX_W), BF16), state_shape, state_shape],
        compiler_params=_cparams(("arbitrary",)),
        name="mix_prompt",
    )(qa, ka, ra, decf, decb, dl, subln, wk, wv)


def _mix_sample_kernel(lam_init, qa_ref, ka_ref, ra_ref, ckd_ref, cvd_ref, cckv_ref, ckpe_ref,
                       s0f_ref, s0b_ref, decf_ref, decb_ref, dl_ref, subln_ref, wk_ref, wv_ref,
                       place_ref, mix_ref,
                       ret_s, kdn_s, vdn_s, kdc_s, vdc_s, kmn_s, vmn_s, kmc_s, vmc_s):
    j = pl.program_id(1)
    seq = ka_ref.shape[0]

    @pl.when(j == 0)
    def _():
        outs, _, _ = _retention(ra_ref, seq, decf_ref, decb_ref, s0f_ref[...], s0b_ref[...])
        for n, o in enumerate(outs):
            ret_s[n * RET_CHUNK:(n + 1) * RET_CHUNK, :] = o
        kdn_s[...] = ka_ref[:, KA_DK:KA_DV].astype(BF16)
        vdn_s[...] = ka_ref[:, KA_DV:KA_CKV].astype(BF16)
        kdc_s[...] = ckd_ref[...].astype(BF16)
        vdc_s[...] = cvd_ref[...].astype(BF16)
        km, vm = _mla_keys(ka_ref[:, KA_CKV:KA_KPE], _kr_only(ka_ref[:, KA_KPE:KA_W]), wk_ref, wv_ref)
        kmn_s[...] = km
        vmn_s[...] = vm
        kr_ctx = _dot(ckpe_ref[...].astype(BF16), place_ref[...])
        km, vm = _mla_keys(cckv_ref[...], kr_ctx, wk_ref, wv_ref)
        kmc_s[...] = km
        vmc_s[...] = vm

    row0 = pl.multiple_of(j * TM, TM)
    mix_ref[:, 0:RET_W] = ret_s[pl.ds(row0, TM), :].astype(BF16)

    bd_ones = jnp.where(_block_diag_ones(DIFF_V_W, DIFF_V_DIM), 1.0, 0.0).astype(BF16)
    lam = _diff_lambda(dl_ref, lam_init)
    kd_ctx = [kdc_s[hd] for hd in range(DIFF_HEADS)]
    kd_new = [kdn_s[:, hd * SLOT:(hd + 1) * SLOT] for hd in range(DIFF_HEADS)]
    mix_ref[:, RET_W:RET_W + DIFF_V_W] = _diff_attention(
        qa_ref[:, 0:4 * SLOT], [kd_ctx, kd_new], [vdc_s[...], vdn_s[...]], lam, subln_ref[...],
        lam_init, bd_ones).astype(BF16)

    halves = _mla_attention(qa_ref[:, 4 * SLOT:QA_W], [kmc_s[...], kmn_s[...]], [vmc_s[...], vmn_s[...]])
    mix_ref[:, 512:768] = halves[0].astype(BF16)
    mix_ref[:, 768:1024] = halves[1].astype(BF16)


def _mix_sample_call(lam_init, layer, qa, ka, ra, tok0, n_seq, seq_len, past_len, cache_dk, cache_dv_t,
                     cache_ckv, cache_kpe, s0f_bd, s0b_bd, decf, decb, dl, subln, wk, wv, place):
    nq = seq_len // TM
    q0 = tok0 // TM
    s0 = tok0 // seq_len
    const = lambda b, j: (0, 0)
    return pl.pallas_call(
        functools.partial(_mix_sample_kernel, lam_init),
        grid=(n_seq, nq),
        in_specs=[
            pl.BlockSpec((TM, QA_W), lambda b, j: (q0 + b * nq + j, 0)),
            pl.BlockSpec((seq_len, KA_W), lambda b, j: (s0 + b, 0)),
            pl.BlockSpec((seq_len, D_MODEL), lambda b, j: (s0 + b, 0)),
            pl.BlockSpec((None, None, DIFF_HEADS, past_len, 2 * DIFF_QK_DIM), lambda b, j: (b, layer, 0, 0, 0)),
            pl.BlockSpec((None, None, past_len, DIFF_V_W), lambda b, j: (b, layer, 0, 0)),
            pl.BlockSpec((None, None, past_len, MLA_KV_LORA), lambda b, j: (b, layer, 0, 0)),
            pl.BlockSpec((None, None, past_len, MLA_ROPE_DIM), lambda b, j: (b, layer, 0, 0)),
            pl.BlockSpec((None, None, RET_W, RET_W), lambda b, j: (b, layer, 0, 0)),
            pl.BlockSpec((None, None, RET_W, RET_W), lambda b, j: (b, layer, 0, 0)),
        ] + _mixer_param_specs(layer) + [
            pl.BlockSpec((MLA_ROPE_DIM, SLOT), const),
        ],
        out_specs=pl.BlockSpec((TM, MIX_W), lambda b, j: (b * nq + j, 0)),
        out_shape=jax.ShapeDtypeStruct((n_seq * seq_len, MIX_W), BF16),
        scratch_shapes=[
            pltpu.VMEM((seq_len, RET_W), F32),
            pltpu.VMEM((seq_len, 4 * SLOT), BF16),
            pltpu.VMEM((seq_len, DIFF_V_W), BF16),
            pltpu.VMEM((DIFF_HEADS, past_len, 2 * DIFF_QK_DIM), BF16),
            pltpu.VMEM((past_len, DIFF_V_W), BF16),
            pltpu.VMEM((seq_len, MLA_HEADS * SLOT), BF16),
            pltpu.VMEM((seq_len, MLA_V_W), BF16),
            pltpu.VMEM((past_len, MLA_HEADS * SLOT), BF16),
            pltpu.VMEM((past_len, MLA_V_W), BF16),
        ],
        compiler_params=_cparams(("arbitrary", "arbitrary")),
        name="mix_sample",
    )(qa, ka, ra, cache_dk, cache_dv_t, cache_ckv, cache_kpe, s0f_bd, s0b_bd,
      decf, decb, dl, subln, wk, wv, place)


def _route(h2, rwt_ref, rb_ref):
    tm = h2.shape[0]
    neg = -jnp.inf
    logits = _split_dot_nt(rwt_ref[...], h2)
    sc = jax.nn.sigmoid(logits)
    sel = sc + rb_ref[...]
    member = lax.broadcasted_iota(I32, (GROUP_SIZE, tm), 0).astype(F32)
    gscore = []
    for g in range(N_GROUPS):
        sg = sel[g * GROUP_SIZE:(g + 1) * GROUP_SIZE, :]
        m1 = jnp.max(sg, axis=0, keepdims=True)
        f1 = jnp.min(jnp.where(sg == m1, member, float(GROUP_SIZE)), axis=0, keepdims=True)
        m2 = jnp.max(jnp.where(member == f1, neg, sg), axis=0, keepdims=True)
        gscore.append(m1 + m2)
    gsel = [jnp.zeros((1, tm), F32) for _ in range(N_GROUPS)]
    for _ in range(TOPK_GROUPS):
        mx = gscore[0]
        for g in range(1, N_GROUPS):
            mx = jnp.maximum(mx, gscore[g])
        fi = jnp.full((1, tm), float(N_GROUPS), F32)
        for g in range(N_GROUPS - 1, -1, -1):
            fi = jnp.where(gscore[g] == mx, float(g), fi)
        for g in range(N_GROUPS):
            hit = fi == float(g)
            gsel[g] = jnp.where(hit, 1.0, gsel[g])
            gscore[g] = jnp.where(hit, neg, gscore[g])
    cand = jnp.concatenate(
        [jnp.where(gsel[g] > 0.0, sel[g * GROUP_SIZE:(g + 1) * GROUP_SIZE, :], neg) for g in range(N_GROUPS)],
        axis=0)
    flat = lax.broadcasted_iota(I32, (N_EXPERTS, tm), 0).astype(F32)
    hits, gts = [], []
    chosen = jnp.zeros((N_EXPERTS, tm), F32)
    for _ in range(TOP_K):
        mx = jnp.max(cand, axis=0, keepdims=True)
        fk = jnp.min(jnp.where(cand == mx, flat, float(N_EXPERTS)), axis=0, keepdims=True)
        hit = flat == fk
        hits.append(hit)
        gts.append(jnp.sum(jnp.where(hit, sc, 0.0), axis=0, keepdims=True))
        chosen = jnp.where(hit, 1.0, chosen)
        cand = jnp.where(hit, neg, cand)
    gsum = gts[0]
    for g in gts[1:]:
        gsum = gsum + g
    gts = [g / gsum * ROUTED_SCALE for g in gts]

    before = (lax.broadcasted_iota(I32, (tm, tm), 0) < lax.broadcasted_iota(I32, (tm, tm), 1))
    rank_in = _dot(chosen.astype(BF16), jnp.where(before, 1.0, 0.0).astype(BF16))
    cnt = jnp.sum(chosen, axis=1, keepdims=True)
    cnt_pad = jnp.floor((cnt + (CH - 1.0)) * (1.0 / CH)) * CH
    below = (lax.broadcasted_iota(I32, (N_EXPERTS, N_EXPERTS), 1) < lax.broadcasted_iota(I32, (N_EXPERTS, N_EXPERTS), 0))
    start = _dot(jnp.where(below, 1.0, 0.0).astype(BF16),
                 jnp.broadcast_to(cnt_pad, (N_EXPERTS, LANES)).astype(BF16))[:, 0:1]
    pos = rank_in + start
    lpos = [jnp.sum(jnp.where(hit, pos, 0.0), axis=0, keepdims=True) for hit in hits]
    return lpos, gts, cnt_pad, start


def _slot_rows(vals, n_rows):
    tm = vals[0].shape[1]
    row = lax.broadcasted_iota(I32, (n_rows, tm), 0)
    out = jnp.zeros((n_rows, tm), F32)
    for k, v in enumerate(vals):
        out = jnp.where(row == k, v, out)
    return out


def _post_kernel(tbl_ref, xa_ref, xb_ref, mp_ref, ms_ref, mod_ref, wout_ref, g2_ref, shg_ref, shu_ref, shd_ref,
                 rwt_ref, rb_ref, base_ref, h2_ref, lpos_ref, ptok_ref, gtok_ref, cnt_ref, start_ref, rel_ref,
                 run_ref):
    i = pl.program_id(0)
    is_context = i < tbl_ref[2, 0]

    @pl.when(i == 0)
    def _():
        cnt_ref[...] = jnp.zeros_like(cnt_ref)
        start_ref[...] = jnp.zeros_like(start_ref)
        rel_ref[...] = jnp.zeros_like(rel_ref)
        run_ref[...] = jnp.zeros_like(run_ref)

    mod = mod_ref[...]
    gate1 = mod[:, 2 * D_MODEL:3 * D_MODEL]
    shift2 = mod[:, 3 * D_MODEL:4 * D_MODEL]
    scale2 = mod[:, 4 * D_MODEL:5 * D_MODEL]
    gate2 = mod[:, 5 * D_MODEL:6 * D_MODEL]
    for s in range(TS):
        rows = slice(s * TM, (s + 1) * TM)
        mix = jnp.where(is_context, mp_ref[rows, :], ms_ref[rows, :])
        x = jnp.where(is_context, xa_ref[rows, :], xb_ref[rows, :])
        x1 = x + gate1 * _dot(mix, wout_ref[...])
        ms = jnp.mean(x1 * x1, axis=-1, keepdims=True)
        h2 = x1 * lax.rsqrt(ms + NORM_EPS) * g2_ref[...]
        h2 = h2 * (1.0 + scale2) + shift2
        hb = h2.astype(BF16)
        h2_ref[rows, :] = hb
        act = _silu(_dot(hb, shg_ref[...])) * _dot(hb, shu_ref[...])
        base_ref[rows, :] = x1 + gate2 * _dot(act.astype(BF16), shd_ref[...])

        lpos, gts, cnt_pad, start = _route(h2, rwt_ref, rb_ref)
        lpos_ref[:, rows] = _slot_rows(lpos, SLOT_ROWS)
        ptok_ref[rows, :] = _slot_rows(lpos, LANES).T
        gtok_ref[rows, :] = _slot_rows(gts, LANES).T
        tile_col = lax.broadcasted_iota(I32, cnt_ref.shape, 1) == i * TS + s
        run = run_ref[...]
        cnt_ref[...] = jnp.where(tile_col, cnt_pad.astype(I32), cnt_ref[...])
        start_ref[...] = jnp.where(tile_col, start.astype(I32), start_ref[...])
        rel_ref[...] = jnp.where(tile_col, run.astype(I32), rel_ref[...])
        run_ref[...] = run + cnt_pad


def _post_call(layer, tbl, xa, xb, mix_p, mix_s, mod4, wout, g2, shg, shu, shd, rwt, rb):
    nps = mix_p.shape[0] // STEP
    n_tok = mix_p.shape[0] + mix_s.shape[0]
    nt = n_tok // TM
    ns = n_tok // STEP
    const = lambda i, t: (0, 0)
    tile = lambda i, t: (i, 0)
    gs = pltpu.PrefetchScalarGridSpec(
        num_scalar_prefetch=1,
        grid=(ns,),
        in_specs=_x_specs(nps, xa is xb) + [
            pl.BlockSpec((STEP, MIX_W), lambda i, t: (jnp.minimum(i, nps - 1), 0)),
            pl.BlockSpec((STEP, MIX_W), lambda i, t: (jnp.maximum(i - nps, 0), 0)),
            _mod_spec(layer),
            _layer_spec(layer, MIX_W, D_MODEL),
            _layer_spec(layer, 1, D_MODEL),
            _layer_spec(layer, D_MODEL, EXPERT_FF),
            _layer_spec(layer, D_MODEL, EXPERT_FF),
            _layer_spec(layer, EXPERT_FF, D_MODEL),
            _layer_spec(layer, N_EXPERTS, D_MODEL),
            _layer_spec(layer, N_EXPERTS, 1),
        ],
        out_specs=[
            pl.BlockSpec((STEP, D_MODEL), tile),
            pl.BlockSpec((STEP, D_MODEL), tile),
            pl.BlockSpec((SLOT_ROWS, STEP), lambda i, t: (0, i)),
            pl.BlockSpec((STEP, LANES), tile),
            pl.BlockSpec((STEP, LANES), tile),
            pl.BlockSpec((N_EXPERTS, LANES), const),
            pl.BlockSpec((N_EXPERTS, LANES), const),
            pl.BlockSpec((N_EXPERTS, LANES), const),
        ],
        scratch_shapes=[pltpu.VMEM((N_EXPERTS, 1), F32)],
    )
    assert nt <= LANES
    return pl.pallas_call(
        _post_kernel,
        grid_spec=gs,
        out_shape=[
            jax.ShapeDtypeStruct((n_tok, D_MODEL), F32),
            jax.ShapeDtypeStruct((n_tok, D_MODEL), BF16),
            jax.ShapeDtypeStruct((SLOT_ROWS, n_tok), F32),
            jax.ShapeDtypeStruct((n_tok, LANES), F32),
            jax.ShapeDtypeStruct((n_tok, LANES), F32),
            jax.ShapeDtypeStruct((N_EXPERTS, LANES), I32),
            jax.ShapeDtypeStruct((N_EXPERTS, LANES), I32),
            jax.ShapeDtypeStruct((N_EXPERTS, LANES), I32),
        ],
        compiler_params=_cparams(("arbitrary",)),
        name="post_route",
    )(tbl, xa, xb, mix_p, mix_s, mod4, wout, g2, shg, shu, shd, rwt, rb)


SEG_ROW0, SEG_NBLK, SEG_PAD0, SEG_NPAD, SEG_NEXT, SEG_USED = range(6)


def _plan_kernel(last_tile, cnt_ref, rel_ref, seg_ref):
    def per_expert(e, start):
        end = start + rel_ref[e, last_tile] + cnt_ref[e, last_tile]
        nb = lax.shift_right_logical(end - start + (EB - 1), LOG_EB)
        nxt = start + lax.shift_left(nb, LOG_EB)
        seg_ref[SEG_ROW0, e] = start
        seg_ref[SEG_NBLK, e] = nb
        seg_ref[SEG_PAD0, e] = end
        seg_ref[SEG_NPAD, e] = lax.shift_right_logical(nxt - end, LOG_CH)
        seg_ref[SEG_USED, e] = 0
        return nxt

    total = lax.fori_loop(0, N_EXPERTS, per_expert, jnp.int32(0))

    def link(k, nxt):
        e = N_EXPERTS - 1 - k
        seg_ref[SEG_NEXT, e] = nxt
        return jnp.where(seg_ref[SEG_NBLK, e] > 0, e, nxt)

    first = lax.fori_loop(0, N_EXPERTS, link, jnp.int32(N_EXPERTS))
    seg_ref[SEG_USED, 0] = lax.shift_right_logical(total, LOG_EB)
    seg_ref[SEG_USED, 1] = first


def _plan_call(cnt, rel, nt):
    smem = pl.BlockSpec(memory_space=pltpu.SMEM)
    return pl.pallas_call(
        functools.partial(_plan_kernel, nt - 1),
        in_specs=[smem, smem],
        out_specs=smem,
        out_shape=jax.ShapeDtypeStruct((6, N_EXPERTS), I32),
        name="moe_plan",
    )(cnt, rel)


def _rows_copy(src_ref, src_row, dst_ref, dst_row, n_rows, sem):
    return pltpu.make_async_copy(src_ref.at[pl.ds(pl.multiple_of(src_row, CH), n_rows)],
                                 dst_ref.at[pl.ds(pl.multiple_of(dst_row, CH), n_rows)], sem)


class _Runs:
    def __init__(self, cnt_ref, start_ref, rel_ref, seg_ref):
        self.cnt, self.start, self.rel, self.seg = cnt_ref, start_ref, rel_ref, seg_ref

    def n_rows(self, i):
        return self.start[N_EXPERTS - 1, i] + self.cnt[N_EXPERTS - 1, i]

    def start_copies(self, i, copy, tot_ref, slot):
        def per_pair(e2, carry):
            n_big, n_small = carry
            for par in range(2):
                e = 2 * e2 + par
                c = self.cnt[e, i]
                a0 = self.start[e, i]
                b0 = self.seg[SEG_ROW0, e] + self.rel[e, i]
                nb = lax.shift_right_logical(c, LOG_CH + 1)
                odd = jnp.bitwise_and(lax.shift_right_logical(c, LOG_CH), 1)

                def big(q, cc, a0=a0, b0=b0, par=par):
                    copy(a0 + q * (2 * CH), b0 + q * (2 * CH), 2 * CH).start(priority=par)
                    return cc

                lax.fori_loop(0, nb, big, 0)

                @pl.when(odd == 1)
                def _(a0=a0, b0=b0, nb=nb, par=par):
                    copy(a0 + nb * (2 * CH), b0 + nb * (2 * CH), CH).start(priority=par)

                n_big, n_small = n_big + nb, n_small + odd
            return n_big, n_small

        n_big, n_small = lax.fori_loop(0, N_EXPERTS // 2, per_pair, (jnp.int32(0), jnp.int32(0)))
        tot_ref[slot, 0] = n_big
        tot_ref[slot, 1] = n_small

    @staticmethod
    def wait_copies(copy, tot_ref, slot):
        def big(q, c):
            copy(0, 0, 2 * CH).wait()
            return c

        lax.fori_loop(0, tot_ref[slot, 0], big, 0)

        def small(q, c):
            copy(0, 0, CH).wait()
            return c

        lax.fori_loop(0, tot_ref[slot, 1], small, 0)


SORT_TIERS = (2048, 2304, SORT_ROWS)


def _for_sorted_rows(n_sorted, body):
    lo = -1
    for hi in SORT_TIERS:
        @pl.when(jnp.logical_and(n_sorted > lo, n_sorted <= hi))
        def _(hi=hi):
            body(hi)
        lo = hi


def _dispatch_kernel(reuse, cnt_ref, start_ref, rel_ref, seg_ref, h_ref, lpos_ref, *rest):
    xb_hbm, sort_s, zero_s, tot_s, sems = rest[1:] if reuse else rest
    i = pl.program_id(0)
    last = pl.num_programs(0) - 1
    slot = lax.rem(i, 2)
    runs = _Runs(cnt_ref, start_ref, rel_ref, seg_ref)
    def sort_rows(n_rows):
        lp = lpos_ref[...]
        hb = h_ref[...]
        blk = TM
        for r in range(n_rows // blk):
            srow = (lax.broadcasted_iota(I32, (blk, TM), 0) + r * blk).astype(F32)
            p = jnp.zeros((blk, TM), F32)
            for k in range(TOP_K):
                p = jnp.where(srow == lp[k:k + 1, :], 1.0, p)
            sort_s[slot, r * blk:(r + 1) * blk, :] = _dot(p.astype(BF16), hb).astype(BF16)

    _for_sorted_rows(runs.n_rows(i), sort_rows)

    def copy_from(sl):
        return lambda s, d, n: _rows_copy(sort_s.at[sl], s, xb_hbm, d, n, sems.at[sl])

    runs.start_copies(i, copy_from(slot), tot_s, slot)

    @pl.when(i > 0)
    def _():
        runs.wait_copies(copy_from(1 - slot), tot_s, 1 - slot)

    @pl.when(i == last)
    def _():
        runs.wait_copies(copy_from(slot), tot_s, slot)
        _zero_fill_unused(seg_ref, xb_hbm, zero_s, sems.at[0], tail=not reuse)


def _zero_fill_unused(seg_ref, buf_hbm, zero_s, sem, tail):
    zero_s[...] = jnp.zeros_like(zero_s)

    def per_expert(e, c):
        first = seg_ref[SEG_PAD0, e]

        def z_issue(r, cc):
            _rows_copy(zero_s, 0, buf_hbm, first + r * CH, CH, sem).start()
            return cc

        lax.fori_loop(0, seg_ref[SEG_NPAD, e], z_issue, 0)

        def z_drain(r, cc):
            _rows_copy(zero_s, 0, buf_hbm, 0, CH, sem).wait()
            return cc

        lax.fori_loop(0, seg_ref[SEG_NPAD, e], z_drain, 0)
        return c

    lax.fori_loop(0, N_EXPERTS, per_expert, 0)
    if tail:
        _zero_fill_tail(seg_ref, buf_hbm, zero_s, sem)


def _zero_fill_tail(seg_ref, buf_hbm, zero_s, sem):
    n_blocks = buf_hbm.shape[0] // EB

    def blk_copy(b):
        return pltpu.make_async_copy(zero_s, buf_hbm.at[pl.ds(pl.multiple_of(b * EB, EB), EB)], sem)

    def t_issue(b, cc):
        blk_copy(b).start()
        return cc

    lax.fori_loop(seg_ref[SEG_USED, 0], n_blocks, t_issue, 0)

    def t_drain(b, cc):
        blk_copy(0).wait()
        return cc

    lax.fori_loop(seg_ref[SEG_USED, 0], n_blocks, t_drain, 0)


def _dispatch_call(cnt, start, rel, seg, h2, lpos, n_rows, prev=None):
    n_tok = h2.shape[0]
    nt = n_tok // TM
    smem = pl.BlockSpec(memory_space=pltpu.SMEM)
    reuse = prev is not None
    return pl.pallas_call(
        functools.partial(_dispatch_kernel, reuse),
        grid=(nt,),
        in_specs=[
            smem, smem, smem, smem,
            pl.BlockSpec((TM, D_MODEL), lambda i: (i, 0)),
            pl.BlockSpec((SLOT_ROWS, TM), lambda i: (0, i)),
        ] + ([pl.BlockSpec(memory_space=pl.ANY)] if reuse else []),
        input_output_aliases={6: 0} if reuse else {},
        out_specs=pl.BlockSpec(memory_space=pl.ANY),
        out_shape=jax.ShapeDtypeStruct((n_rows, D_MODEL), BF16),
        scratch_shapes=[
            pltpu.VMEM((2, SORT_ROWS, D_MODEL), BF16),
            pltpu.VMEM((EB, D_MODEL), BF16),
            pltpu.SMEM((2, 2), I32),
            pltpu.SemaphoreType.DMA((2,)),
        ],
        compiler_params=_cparams(("arbitrary",)),
        name="moe_dispatch",
    )(cnt, start, rel, seg, h2, lpos, *([prev] if reuse else []))


X_SLOTS = 4
Y_SLOTS = 4


def _experts_kernel(layer, seg_ref, wg_hbm, wu_hbm, wd_hbm, xb_hbm, yb_hbm,
                    wgf_s, wuf_s, wdf_s, wg_s, wu_s, wd_s, x_s, y_s, sem_w, sem_x, sem_y):
    n_used = seg_ref[SEG_USED, 0]
    first = seg_ref[SEG_USED, 1]

    def rows(g):
        return pl.ds(pl.multiple_of(g * EB, EB), EB)

    def x_copy(g, slot):
        return pltpu.make_async_copy(xb_hbm.at[rows(g)], x_s.at[slot], sem_x.at[slot])

    def y_copy(g, slot):
        return pltpu.make_async_copy(y_s.at[slot], yb_hbm.at[rows(g)], sem_y.at[slot])

    def w_copies(e, slot):
        return [pltpu.make_async_copy(hbm.at[layer, e], buf.at[slot], sem_w.at[slot, n])
                for n, (hbm, buf) in enumerate(((wg_hbm, wgf_s), (wu_hbm, wuf_s), (wd_hbm, wdf_s)))]

    def fetch_weights(e, slot):
        for c in w_copies(e, slot):
            c.start()

    def take_weights(slot):
        for c in w_copies(0, slot):
            c.wait()
        wg_s[...] = wgf_s[slot].astype(BF16)
        wu_s[...] = wuf_s[slot].astype(BF16)
        wd_s[...] = wdf_s[slot].astype(BF16)

    def next_expert(e):
        return seg_ref[SEG_NEXT, jnp.minimum(e, N_EXPERTS - 1)]

    @pl.when(n_used > 0)
    def _():
        for p in range(X_SLOTS - 1):
            @pl.when(p < n_used)
            def _(p=p):
                x_copy(p, p).start()

        fetch_weights(first, 0)

        @pl.when(next_expert(first) < N_EXPERTS)
        def _():
            fetch_weights(next_expert(first), 1)

        take_weights(0)

        def block(g, carry):
            e, left, wslot = carry
            ahead = g + (X_SLOTS - 1)

            @pl.when(ahead < n_used)
            def _():
                x_copy(ahead, lax.rem(ahead, X_SLOTS)).start()

            xslot = lax.rem(g, X_SLOTS)
            yslot = lax.rem(g, Y_SLOTS)
            x_copy(g, xslot).wait()

            @pl.when(g >= Y_SLOTS)
            def _():
                y_copy(g - Y_SLOTS, yslot).wait()

            xb = x_s[xslot]
            act = _silu(_dot(xb, wg_s[...])) * _dot(xb, wu_s[...])
            y_s[yslot] = _dot(act.astype(BF16), wd_s[...]).astype(BF16)
            y_copy(g, yslot).start()

            switch = jnp.logical_and(left == 1, g + 1 < n_used)
            nxt = next_expert(e)

            @pl.when(switch)
            def _():
                take_weights(1 - wslot)

                @pl.when(next_expert(nxt) < N_EXPERTS)
                def _():
                    fetch_weights(next_expert(nxt), wslot)

            nxt_c = jnp.minimum(nxt, N_EXPERTS - 1)
            return (jnp.where(switch, nxt_c, e), jnp.where(switch, seg_ref[SEG_NBLK, nxt_c], left - 1),
                    jnp.where(switch, 1 - wslot, wslot))

        lax.fori_loop(0, n_used, block,
                      (first, seg_ref[SEG_NBLK, jnp.minimum(first, N_EXPERTS - 1)], jnp.int32(0)))

        for back in range(Y_SLOTS, 0, -1):
            @pl.when(n_used >= back)
            def _(back=back):
                y_copy(n_used - back, lax.rem(n_used - back, Y_SLOTS)).wait()


def _experts_call(seg, xb, layer, wg, wu, wd):
    n_rows = xb.shape[0]
    hbm = pl.BlockSpec(memory_space=pl.ANY)
    return pl.pallas_call(
        functools.partial(_experts_kernel, layer),
        in_specs=[pl.BlockSpec(memory_space=pltpu.SMEM), hbm, hbm, hbm, hbm],
        out_specs=hbm,
        out_shape=jax.ShapeDtypeStruct((n_rows, D_MODEL), BF16),
        input_output_aliases={4: 0},
        scratch_shapes=[
            pltpu.VMEM((2, D_MODEL, EXPERT_FF), F32),
            pltpu.VMEM((2, D_MODEL, EXPERT_FF), F32),
            pltpu.VMEM((2, EXPERT_FF, D_MODEL), F32),
            pltpu.VMEM((D_MODEL, EXPERT_FF), BF16),
            pltpu.VMEM((D_MODEL, EXPERT_FF), BF16),
            pltpu.VMEM((EXPERT_FF, D_MODEL), BF16),
            pltpu.VMEM((X_SLOTS, EB, D_MODEL), BF16),
            pltpu.VMEM((Y_SLOTS, EB, D_MODEL), BF16),
            pltpu.SemaphoreType.DMA((2, 3)),
            pltpu.SemaphoreType.DMA((X_SLOTS,)),
            pltpu.SemaphoreType.DMA((Y_SLOTS,)),
        ],
        compiler_params=pltpu.CompilerParams(vmem_limit_bytes=VMEM_LIMIT),
        name="moe_experts",
    )(seg, wg, wu, wd, xb)


def _combine_kernel(final, tbl_ref, cnt_ref, start_ref, rel_ref, seg_ref, yb_hbm, base_ref, gtok_ref, ptok_ref,
                    mod_ref, gf_ref, *rest):
    *out_refs, sort_s, routed_s, tot_s, sems = rest
    i = pl.program_id(0)
    slot = lax.rem(i, 2)
    runs = _Runs(cnt_ref, start_ref, rel_ref, seg_ref)

    def copy_to(sl):
        return lambda s, d, n: _rows_copy(yb_hbm, d, sort_s.at[sl], s, n, sems.at[sl])

    @pl.when(i == 0)
    def _():
        sort_s[...] = jnp.zeros_like(sort_s)
        runs.start_copies(i, copy_to(slot), tot_s, slot)

    @pl.when(i + 1 < pl.num_programs(0))
    def _():
        runs.start_copies(i + 1, copy_to(1 - slot), tot_s, 1 - slot)

    runs.wait_copies(copy_to(slot), tot_s, slot)

    def unsort(n_rows):
        gt = gtok_ref[...]
        pt = ptok_ref[...]
        col = lax.broadcasted_iota(I32, (TM, n_rows), 1).astype(F32)
        w = jnp.zeros((TM, n_rows), F32)
        for k in range(TOP_K):
            w = jnp.where(col == pt[:, k:k + 1], gt[:, k:k + 1], w)
        routed_s[...] = _dot(w.astype(BF16), sort_s[slot, 0:n_rows, :])

    _for_sorted_rows(runs.n_rows(i), unsort)
    gate2 = mod_ref[...][:, 5 * D_MODEL:6 * D_MODEL]
    y = base_ref[...] + gate2 * routed_s[...]
    if final:
        y = y * lax.rsqrt(jnp.mean(y * y, axis=-1, keepdims=True) + NORM_EPS) * gf_ref[...]
        yp_ref, ys_ref = out_refs
        is_context = i < tbl_ref[2, 0]

        @pl.when(is_context)
        def _():
            yp_ref[...] = y

        @pl.when(jnp.logical_not(is_context))
        def _():
            ys_ref[...] = y
    else:
        out_refs[0][...] = y


def _combine_call(final, layer, npt, tbl, cnt, start, rel, seg, yb, base, gtok, ptok, mod4, gfinal):
    n_tok = base.shape[0]
    nt = n_tok // TM
    tile = lambda i, t: (i, 0)
    smem = pl.BlockSpec(memory_space=pltpu.SMEM)
    if final:
        out_specs = [pl.BlockSpec((TM, D_MODEL), lambda i, t: (jnp.minimum(i, npt - 1), 0)),
                     pl.BlockSpec((TM, D_MODEL), lambda i, t: (jnp.maximum(i - npt, 0), 0))]
        out_shape = [jax.ShapeDtypeStruct((npt * TM, D_MODEL), F32),
                     jax.ShapeDtypeStruct((n_tok - npt * TM, D_MODEL), F32)]
    else:
        out_specs = pl.BlockSpec((TM, D_MODEL), tile)
        out_shape = jax.ShapeDtypeStruct((n_tok, D_MODEL), F32)
    gs = pltpu.PrefetchScalarGridSpec(
        num_scalar_prefetch=1,
        grid=(nt,),
        in_specs=[
            smem, smem, smem, smem,
            pl.BlockSpec(memory_space=pl.ANY),
            pl.BlockSpec((TM, D_MODEL), tile),
            pl.BlockSpec((TM, LANES), tile),
            pl.BlockSpec((TM, LANES), tile),
            _mod_spec(layer),
            pl.BlockSpec((1, D_MODEL), lambda i, t: (0, 0)),
        ],
        out_specs=out_specs,
        scratch_shapes=[
            pltpu.VMEM((2, SORT_ROWS, D_MODEL), BF16),
            pltpu.VMEM((TM, D_MODEL), F32),
            pltpu.SMEM((2, 2), I32),
            pltpu.SemaphoreType.DMA((2,)),
        ],
    )
    return pl.pallas_call(
        functools.partial(_combine_kernel, final),
        grid_spec=gs,
        out_shape=out_shape,
        compiler_params=_cparams(("arbitrary",)),
        name="moe_combine",
    )(tbl, cnt, start, rel, seg, yb, base, gtok, ptok, mod4, gfinal)


def _pad_cols(w, groups, width, slot):
    lead = w.shape[:-1]
    w = w.reshape(*lead, groups, width)
    pad = [(0, 0)] * (len(lead) + 1) + [(0, slot - width)]
    return jnp.pad(w, pad).reshape(*lead, groups * slot)


def _prep_w_in(w):
    assert w.shape[-1] == C_KPE + MLA_ROPE_DIM
    return jnp.pad(w.astype(BF16), ((0, 0), (0, 0), (0, N_PRE - w.shape[-1])))


def _rope_tables(n_pos, dim, lane_offsets):
    f32 = np.float32
    n_rows = n_pos // GRID_W
    row = np.repeat(np.arange(n_rows, dtype=f32), GRID_W)
    col = np.tile(np.arange(GRID_W, dtype=f32), n_rows)
    half = dim // 2
    freqs = f32(ROPE_THETA) ** (-np.arange(0, half, 2, dtype=f32) / f32(half))
    ar = row[:, None] * freqs[None, :]
    ac = col[:, None] * freqs[None, :]
    ang = np.concatenate([ar, ar, ac, ac], axis=-1).astype(f32)
    cos, sin = np.cos(ang), np.sin(ang)
    first = (np.arange(dim) % 16) < 8
    sa = np.where(first[None, :], -sin, f32(0))
    sb = np.where(first[None, :], f32(0), sin)
    c_t = np.ones((STEP + n_pos, SLOT), f32)
    a_t = np.zeros((STEP + n_pos, SLOT), f32)
    b_t = np.zeros((STEP + n_pos, SLOT), f32)
    for off in lane_offsets:
        c_t[STEP:, off:off + dim] = cos
        a_t[STEP:, off:off + dim] = sa
        b_t[STEP:, off:off + dim] = sb
    return tuple(jnp.asarray(t) for t in (c_t, a_t, b_t))


def _block_diag_states(s):
    b, l, h, dk, dv = s.shape
    eye = jnp.eye(h, dtype=s.dtype)
    return jnp.einsum('blhkv,hg->blhkgv', s, eye).reshape(b, l, h * dk, h * dv)


def kernel(x_prompt, x_sample, cache_diff_k, cache_diff_v, cache_mla_ckv, cache_mla_kpe, state_ret_fwd, state_ret_bwd, c, c_ctx, w_ada, b_ada, norm_mix, norm_ffn, norm_final, w_in, ret_decay_fwd, ret_decay_bwd, diff_lambda, diff_subln, mla_q_norm, mla_w_uq, mla_kv_norm, mla_w_ukv, w_out, router_w, router_bias, exp_w_gate, exp_w_up, exp_w_down, sh_w_gate, sh_w_up, sh_w_down):
    n_pb, p_len, _ = x_prompt.shape
    n_sb, s_len, _ = x_sample.shape
    past_len = cache_diff_k.shape[3]
    n_p = n_pb * p_len
    n_s = n_sb * s_len
    n_tok = n_p + n_s
    nt = n_tok // TM
    npt = n_p // TM
    assert p_len == TM and s_len % TM == 0 and n_p % s_len == 0 and past_len % 8 == 0

    def step_tables(rows):
        steps = np.arange(n_tok // rows)
        n_ctx = n_p // rows
        per_seq = s_len // rows
        mod_row = np.where(steps < n_ctx, n_sb, (steps - n_ctx) // per_seq)
        rope_blk = np.where(steps < n_ctx, 0, 1 + (steps - n_ctx) % per_seq)
        return jnp.asarray(np.stack([mod_row, rope_blk, np.full(len(steps), n_ctx)]).astype(np.int32))

    assert n_p % STEP == 0 and s_len % STEP == 0
    tbl = step_tables(TM)
    tbl_step = step_tables(STEP)

    n_cond = 16
    cond = jnp.concatenate([c, c_ctx[None, :], jnp.zeros((n_cond - n_sb - 1, D_MODEL), F32)], axis=0)
    mod_all = _modulation(cond, w_ada, b_ada)

    rope_d = _rope_tables(s_len, DIFF_QK_DIM, (0, DIFF_QK_DIM))
    rope_m = _rope_tables(s_len, MLA_ROPE_DIM, (KR_LO,))
    place = np.zeros((MLA_ROPE_DIM, SLOT), np.float32)
    place[np.arange(MLA_ROPE_DIM), KR_LO + np.arange(MLA_ROPE_DIM)] = 1.0
    place = jnp.asarray(place, BF16)
    cache_dv_t = cache_diff_v.transpose(0, 1, 3, 2, 4).reshape(n_sb, DEPTH, past_len, DIFF_V_W)
    s0f_bd = _block_diag_states(state_ret_fwd)
    s0b_bd = _block_diag_states(state_ret_bwd)

    n_blocks = pl.cdiv(n_tok * TOP_K + nt * N_EXPERTS * (CH - 1) + N_EXPERTS * (EB - CH), EB)
    n_rows = n_blocks * EB

    xa, xb = x_prompt.reshape(n_p, D_MODEL), x_sample.reshape(n_s, D_MODEL)
    gfinal = norm_final.reshape(1, D_MODEL)

    mod4 = mod_all.reshape(DEPTH, n_cond, 1, 6 * D_MODEL)
    w_pre = _prep_w_in(w_in)
    wuq = _pad_cols(mla_w_uq, MLA_HEADS, MLA_NOPE_DIM + MLA_ROPE_DIM, SLOT).astype(BF16)
    ukv = mla_w_ukv.reshape(DEPTH, MLA_KV_LORA, MLA_HEADS, MLA_NOPE_DIM + MLA_V_DIM)
    wk = _pad_cols(ukv[..., :MLA_NOPE_DIM].reshape(DEPTH, MLA_KV_LORA, -1), MLA_HEADS, MLA_NOPE_DIM, SLOT).astype(BF16)
    wv = ukv[..., MLA_NOPE_DIM:].reshape(DEPTH, MLA_KV_LORA, MLA_V_W).astype(BF16)
    decf = jnp.repeat(ret_decay_fwd, RET_DIM, axis=-1).reshape(DEPTH, 1, RET_W)
    decb = jnp.repeat(ret_decay_bwd, RET_DIM, axis=-1).reshape(DEPTH, 1, RET_W)
    subln = jnp.tile(diff_subln, (1, DIFF_HEADS)).reshape(DEPTH, 1, DIFF_V_W)
    vec = lambda p: p.reshape(DEPTH, 1, -1)
    wout_b, shg_b, shu_b, shd_b = (w.astype(BF16) for w in (w_out, sh_w_gate, sh_w_up, sh_w_down))
    rwt = router_w.transpose(0, 2, 1)
    rb = router_bias.reshape(DEPTH, N_EXPERTS, 1)

    new_ctx = []
    sorted_buf = None
    for l in range(DEPTH):
        lam_init = 0.8 - 0.6 * math.exp(-0.3 * l)
        ra, qa, ka, *caches = _pre_call(l, tbl_step, xa, xb, npt, n_tok, mod4, vec(norm_mix), w_pre, vec(mla_q_norm),
                                        wuq, vec(mla_kv_norm), rope_d, rope_m)
        mix_p, *states = _mix_prompt_call(lam_init, l, qa, ka, ra, n_pb, p_len, decf, decb,
                                          diff_lambda, subln, wk, wv)
        new_ctx.append(caches + states)
        mix_s = _mix_sample_call(lam_init, l, qa, ka, ra, n_p, n_sb, s_len, past_len, cache_diff_k,
                                 cache_dv_t, cache_mla_ckv, cache_mla_kpe, s0f_bd, s0b_bd, decf, decb,
                                 diff_lambda, subln, wk, wv, place)
        base, h2, lpos, ptok, gtok, cnt, start, rel = _post_call(
            l, tbl_step, xa, xb, mix_p, mix_s, mod4, wout_b, vec(norm_ffn), shg_b, shu_b, shd_b, rwt, rb)
        seg = _plan_call(cnt, rel, nt)
        sorted_buf = _dispatch_call(cnt, start, rel, seg, h2, lpos, n_rows, prev=sorted_buf)
        sorted_buf = _experts_call(seg, sorted_buf, l, exp_w_gate, exp_w_up, exp_w_down)
        final = l == DEPTH - 1
        out = _combine_call(final, l, npt, tbl, cnt, start, rel, seg, sorted_buf, base, gtok, ptok, mod4, gfinal)
        xa, xb = out if final else (out, out)

    y_prompt = xa.reshape(n_pb, p_len, D_MODEL)
    y_sample = xb.reshape(n_sb, s_len, D_MODEL)
    return (y_prompt, y_sample, *(jnp.stack(per_layer, axis=1) for per_layer in zip(*new_ctx)))
```
